```python
import math
import jax, jax.numpy as jnp
from jax import lax
import numpy as np


D_MODEL = 1024
BATCH = 4
SEQ = 8192
DEPTH = 1

MEM_LEN = 256

RWKV_HEADS = 8
RWKV_HEAD_DIM = 64
RWKV_DIM = RWKV_HEADS * RWKV_HEAD_DIM
DECAY_LORA = 64
AAA_LORA = 64
GATE_LORA = 128
RWKV_GN_EPS = 64e-5

NSA_HEADS = 8
NSA_KV_GROUPS = 2
NSA_HPG = NSA_HEADS // NSA_KV_GROUPS
NSA_HEAD_DIM = 64
NSA_DIM = NSA_HEADS * NSA_HEAD_DIM
NSA_KV_DIM = NSA_KV_GROUPS * NSA_HEAD_DIM
CMP_LEN = 32
CMP_STRIDE = 16
CMP_HIDDEN = 128
SLC_LEN = 64
N_SEL = 16
WINDOW = 512
Q_BLOCK = 128
FORCED_SCORE = 1e4

NUM_BUCKETS = 32
MAX_DISTANCE = 128

XATTN_HEADS = 4
XATTN_HEAD_DIM = D_MODEL // XATTN_HEADS

D_FF = 2816
LN_EPS = 1e-5

DEEPNORM_ALPHA = (2.0 * DEPTH) ** 0.25
DEEPNORM_BETA = (8.0 * DEPTH) ** -0.25

MIX_DIM = RWKV_DIM + NSA_DIM
RWKV_SPLITS = (RWKV_DIM, RWKV_DIM, RWKV_DIM, DECAY_LORA, AAA_LORA, GATE_LORA)
NSA_SPLITS = (NSA_DIM,) + (NSA_KV_DIM,) * 6 + (3 * NSA_HEADS,)
RWKV_IN = sum(RWKV_SPLITS)
NSA_IN = sum(NSA_SPLITS)
IN_DIM = RWKV_IN + NSA_IN

kernel_name = 'hybrid_rwkv7_nsa_macaron_deepnorm'


def _offsets(sizes):
    return [int(s) for s in np.cumsum(sizes)[:-1]]


def layer_norm(x, g, b):
    xf = x.astype(jnp.float32)
    mu = jnp.mean(xf, -1, keepdims=True)
    var = jnp.mean(jnp.square(xf - mu), -1, keepdims=True)
    return ((xf - mu) * lax.rsqrt(var + LN_EPS) * g + b).astype(x.dtype)


def swiglu(x, w_gate, w_up, w_down):
    return (jax.nn.silu(x @ w_gate) * (x @ w_up)) @ w_down


def masked_softmax(logits, mask):
    logits = jnp.where(mask, logits.astype(jnp.float32), -1e30)
    m = jnp.max(logits, -1, keepdims=True)
    e = jnp.where(mask, jnp.exp(logits - m), 0.0)
    return e / jnp.maximum(jnp.sum(e, -1, keepdims=True), 1e-30)


def t5_bucket(dist):
    n = jnp.maximum(dist, 0)
    max_exact = NUM_BUCKETS // 2
    nf = jnp.maximum(n, max_exact).astype(jnp.float32)
    large = max_exact + (jnp.log(nf / max_exact) / math.log(MAX_DISTANCE / max_exact)
                         * (NUM_BUCKETS - max_exact)).astype(jnp.int32)
    large = jnp.minimum(large, NUM_BUCKETS - 1)
    return jnp.where(n < max_exact, n, large)


def token_shift(z, mu):
    prev = jnp.pad(z, ((0, 0), (1, 0), (0, 0)))[:, :-1]
    return z + (prev - z) * mu


def rwkv7_group(z, mu, w0, w_up, a0, a_up, g_up, k_k, k_a, r_k, gn_g, gn_b):
    B, S, _ = z.shape
    heads = lambda t: t.reshape(B, S, RWKV_HEADS, RWKV_HEAD_DIM)
    z = token_shift(z, mu)
    r, k, v, wl, al, gl = jnp.split(z, _offsets(RWKV_SPLITS), axis=-1)
    w = -jax.nn.softplus(-(w0 + jnp.tanh(wl) @ w_up)) - 0.5
    decay = jnp.exp(-jnp.exp(w.astype(jnp.float32)))
    a = jax.nn.sigmoid(a0 + al @ a_up)
    g = jax.nn.sigmoid(gl) @ g_up
    kk = heads(k * k_k).astype(jnp.float32)
    kk = kk * lax.rsqrt(jnp.maximum(jnp.sum(kk * kk, -1, keepdims=True), 1e-24))
    k = k * (1.0 + (a - 1.0) * k_a)
    rh, kh, vh, ah, wh = [heads(t).astype(jnp.float32) for t in (r, k, v, a, decay)]

    def step(state, inp):
        r_t, w_t, k_t, v_t, kk_t, a_t = inp
        sa = jnp.einsum('bhij,bhj->bhi', state, -kk_t)
        state = (state * w_t[:, :, None, :] + sa[..., None] * (kk_t * a_t)[:, :, None, :]
                 + v_t[..., None] * k_t[:, :, None, :])
        return state, jnp.einsum('bhij,bhj->bhi', state, r_t)

    xs = tuple(jnp.moveaxis(t, 1, 0) for t in (rh, wh, kh, vh, kk, ah))
    state0 = jnp.zeros((B, RWKV_HEADS, RWKV_HEAD_DIM, RWKV_HEAD_DIM), jnp.float32)
    _, y = lax.scan(step, state0, xs)
    y = jnp.moveaxis(y, 0, 1)
    mu_y = jnp.mean(y, -1, keepdims=True)
    var_y = jnp.mean(jnp.square(y - mu_y), -1, keepdims=True)
    y = ((y - mu_y) * lax.rsqrt(var_y + RWKV_GN_EPS)).reshape(B, S, RWKV_DIM) * gn_g + gn_b
    r_k_h = r_k.reshape(RWKV_HEADS, RWKV_HEAD_DIM).astype(jnp.float32)
    bonus = jnp.sum(rh * kh * r_k_h, -1, keepdims=True) * vh
    return ((y + bonus.reshape(B, S, RWKV_DIM)) * g).astype(z.dtype)


def compress(kv, pe, w1, w2):
    S = kv.shape[2]
    n_cmp = (S - CMP_LEN) // CMP_STRIDE + 1
    idx = jnp.arange(n_cmp)[:, None] * CMP_STRIDE + jnp.arange(CMP_LEN)[None, :]
    blocks = kv[:, :, idx, :] + pe
    flat = blocks.reshape(*blocks.shape[:3], CMP_LEN * NSA_HEAD_DIM)
    return jax.nn.gelu(flat @ w1) @ w2


def nsa_group(z, pe_k, w1_k, w2_k, pe_v, w1_v, w2_v, rel_bias):
    B, S, _ = z.shape
    G, P, dk = NSA_KV_GROUPS, NSA_HPG, NSA_HEAD_DIM
    q, kc, vc, ks, vs, kw, vw, gate = jnp.split(z, _offsets(NSA_SPLITS), axis=-1)
    q = q.reshape(B, S, G, P, dk).transpose(0, 2, 3, 1, 4)
    kvh = lambda t: t.reshape(B, S, G, dk).transpose(0, 2, 1, 3)
    kc, vc, ks, vs, kw, vw = [kvh(t) for t in (kc, vc, ks, vs, kw, vw)]
    kc = compress(kc, pe_k, w1_k, w2_k)
    vc = compress(vc, pe_v, w1_v, w2_v)
    n_cmp = kc.shape[2]
    cmp_end = jnp.arange(n_cmp) * CMP_STRIDE + CMP_LEN - 1
    cmp_start = cmp_end - CMP_LEN + 1
    n_slc = S // SLC_LEN
    n_sel = min(N_SEL, n_slc)
    slc_start = jnp.arange(n_slc) * SLC_LEN
    overlap = ((cmp_start[:, None] < slc_start[None, :] + SLC_LEN)
               & (cmp_end[:, None] >= slc_start[None, :])).astype(jnp.float32)
    ks_flat = ks.reshape(B * G * n_slc, SLC_LEN, dk)
    vs_flat = vs.reshape(B * G * n_slc, SLC_LEN, dk)
    bg_off = (jnp.arange(B * G) * n_slc).reshape(B, G, 1, 1)
    kw_pad = jnp.pad(kw, ((0, 0), (0, 0), (WINDOW, 0), (0, 0)))
    vw_pad = jnp.pad(vw, ((0, 0), (0, 0), (WINDOW, 0), (0, 0)))
    bias_gp = rel_bias.reshape(NUM_BUCKETS, G, P)
    g_idx = jnp.arange(G).reshape(1, G, 1, 1)
    scale = dk ** -0.5

    def shared_bias(dist):
        return bias_gp[t5_bucket(dist)].transpose(2, 3, 0, 1)

    def block(qi):
        q0 = qi * Q_BLOCK
        qb = lax.dynamic_slice_in_dim(q, q0, Q_BLOCK, axis=3)
        t = q0 + jnp.arange(Q_BLOCK)
        dist_c = t[:, None] - cmp_end[None, :]
        lc = jnp.einsum('bgpqd,bgnd->bgpqn', qb, kc) * scale + shared_bias(dist_c)
        pc = masked_softmax(lc, dist_c >= 0)
        oc = jnp.einsum('bgpqn,bgnd->bgpqd', pc.astype(vc.dtype), vc)
        imp = jnp.einsum('bgpqn,nj->bgqj', pc, overlap)
        blk = jnp.arange(n_slc)[None, :]
        cur = (t // SLC_LEN)[:, None]
        valid = slc_start[None, :] <= t[:, None]
        forced = (blk == 0) | (blk == cur) | (blk == cur - 1)
        score = jnp.where(valid, jnp.where(forced, FORCED_SCORE, imp), -1.0)
        _, sel = lax.top_k(score, n_sel)
        ksel = ks_flat[sel + bg_off].reshape(B, G, Q_BLOCK, n_sel * SLC_LEN, dk)
        vsel = vs_flat[sel + bg_off].reshape(B, G, Q_BLOCK, n_sel * SLC_LEN, dk)
        pos = (sel[..., None] * SLC_LEN + jnp.arange(SLC_LEN)).reshape(B, G, Q_BLOCK, n_sel * SLC_LEN)
        dist_s = t[:, None] - pos
        bias_s = jnp.moveaxis(bias_gp[t5_bucket(dist_s), g_idx], -1, 2)
        ls = jnp.einsum('bgpqd,bgqkd->bgpqk', qb, ksel) * scale + bias_s
        ps = masked_softmax(ls, (dist_s >= 0)[:, :, None])
        osel = jnp.einsum('bgpqk,bgqkd->bgpqd', ps.astype(vs.dtype), vsel)
        kwb = lax.dynamic_slice_in_dim(kw_pad, q0, WINDOW + Q_BLOCK, axis=2)
        vwb = lax.dynamic_slice_in_dim(vw_pad, q0, WINDOW + Q_BLOCK, axis=2)
        pos_w = q0 - WINDOW + jnp.arange(WINDOW + Q_BLOCK)
        dist_w = t[:, None] - pos_w[None, :]
        mask_w = (dist_w >= 0) & (dist_w < WINDOW) & (pos_w[None, :] >= 0)
        lw = jnp.einsum('bgpqd,bgkd->bgpqk', qb, kwb) * scale + shared_bias(dist_w)
        pw = masked_softmax(lw, mask_w)
        ow = jnp.einsum('bgpqk,bgkd->bgpqd', pw.astype(vw.dtype), vwb)
        return oc, osel, ow

    oc, osel, ow = lax.map(block, jnp.arange(S // Q_BLOCK))
    merge = lambda o: o.transpose(1, 0, 4, 2, 3, 5).reshape(B, S, NSA_HEADS, dk)
    gate = jax.nn.sigmoid(gate.reshape(B, S, NSA_HEADS, 3))
    o = gate[..., 0:1] * merge(oc) + gate[..., 1:2] * merge(osel) + gate[..., 2:3] * merge(ow)
    return o.reshape(B, S, NSA_DIM)


def cross_attention(h, mem, wq, wk, wv, wo):
    B, S, _ = h.shape
    M = mem.shape[1]
    q = (h @ wq).reshape(B, S, XATTN_HEADS, XATTN_HEAD_DIM)
    k = (mem @ wk).reshape(B, M, XATTN_HEADS, XATTN_HEAD_DIM)
    v = (mem @ wv).reshape(B, M, XATTN_HEADS, XATTN_HEAD_DIM)
    logits = jnp.einsum('bqhd,bkhd->bhqk', q, k).astype(jnp.float32) * XATTN_HEAD_DIM ** -0.5
    p = jax.nn.softmax(logits, -1).astype(v.dtype)
    o = jnp.einsum('bhqk,bkhd->bqhd', p, v).reshape(B, S, D_MODEL)
    return o @ wo


def setup_inputs(seed: int = 0) -> dict:
    key = jax.random.key(seed)
    keys = iter(jax.random.split(key, 48))
    L = DEPTH
    nrm = lambda shape, s: jax.random.normal(next(keys), shape, jnp.float32) * s
    gain = lambda n: 1.0 + nrm((L, n), 0.02)
    bias = lambda n: nrm((L, n), 0.02)
    D, F = D_MODEL, D_FF
    return {
        'x': nrm((BATCH, SEQ, D), 1.0),
        'mem': nrm((BATCH, MEM_LEN, D), 1.0),
        'ffn1_w_gate': nrm((L, D, F), D ** -0.5),
        'ffn1_w_up': nrm((L, D, F), D ** -0.5),
        'ffn1_w_down': nrm((L, F, D), F ** -0.5 * DEEPNORM_BETA),
        'ln1_g': gain(D), 'ln1_b': bias(D),
        'mix_w_in': nrm((L, D, IN_DIM), D ** -0.5),
        'rwkv_mu': jax.random.uniform(next(keys), (L, RWKV_IN), jnp.float32, 0.0, 1.0),
        'rwkv_w0': jax.random.uniform(next(keys), (L, RWKV_DIM), jnp.float32, -6.0, -1.0),
        'rwkv_w_up': nrm((L, DECAY_LORA, RWKV_DIM), 0.1 * DECAY_LORA ** -0.5),
        'rwkv_a0': nrm((L, RWKV_DIM), 0.1),
        'rwkv_a_up': nrm((L, AAA_LORA, RWKV_DIM), AAA_LORA ** -0.5),
        'rwkv_g_up': nrm((L, GATE_LORA, RWKV_DIM), GATE_LORA ** -0.5),
        'rwkv_k_k': 0.85 + nrm((L, RWKV_DIM), 0.02),
        'rwkv_k_a': 1.0 + nrm((L, RWKV_DIM), 0.02),
        'rwkv_r_k': nrm((L, RWKV_DIM), 0.1),
        'rwkv_gn_g': gain(RWKV_DIM), 'rwkv_gn_b': bias(RWKV_DIM),
        'nsa_pe_k': nrm((L, CMP_LEN, NSA_HEAD_DIM), 0.02),
        'nsa_w1_k': nrm((L, CMP_LEN * NSA_HEAD_DIM, CMP_HIDDEN), (CMP_LEN * NSA_HEAD_DIM) ** -0.5),
        'nsa_w2_k': nrm((L, CMP_HIDDEN, NSA_HEAD_DIM), CMP_HIDDEN ** -0.5),
        'nsa_pe_v': nrm((L, CMP_LEN, NSA_HEAD_DIM), 0.02),
        'nsa_w1_v': nrm((L, CMP_LEN * NSA_HEAD_DIM, CMP_HIDDEN), (CMP_LEN * NSA_HEAD_DIM) ** -0.5),
        'nsa_w2_v': nrm((L, CMP_HIDDEN, NSA_HEAD_DIM), CMP_HIDDEN ** -0.5),
        'mix_w_out': nrm((L, MIX_DIM, D), MIX_DIM ** -0.5 * DEEPNORM_BETA),
        'ln2_g': gain(D), 'ln2_b': bias(D),
        'xattn_wq': nrm((L, D, D), D ** -0.5),
        'xattn_wk': nrm((L, D, D), D ** -0.5),
        'xattn_wv': nrm((L, D, D), D ** -0.5),
        'xattn_wo': nrm((L, D, D), D ** -0.5 * DEEPNORM_BETA),
        'ln3_g': gain(D), 'ln3_b': bias(D),
        'ffn2_w_gate': nrm((L, D, F), D ** -0.5),
        'ffn2_w_up': nrm((L, D, F), D ** -0.5),
        'ffn2_w_down': nrm((L, F, D), F ** -0.5 * DEEPNORM_BETA),
        'ln4_g': gain(D), 'ln4_b': bias(D),
        'rel_bias': nrm((NUM_BUCKETS, NSA_HEADS), 0.1),
    }


def reference(x, mem, ffn1_w_gate, ffn1_w_up, ffn1_w_down, ln1_g, ln1_b, mix_w_in,
              rwkv_mu, rwkv_w0, rwkv_w_up, rwkv_a0, rwkv_a_up, rwkv_g_up, rwkv_k_k, rwkv_k_a,
              rwkv_r_k, rwkv_gn_g, rwkv_gn_b, nsa_pe_k, nsa_w1_k, nsa_w2_k, nsa_pe_v, nsa_w1_v,
              nsa_w2_v, mix_w_out, ln2_g, ln2_b, xattn_wq, xattn_wk, xattn_wv, xattn_wo,
              ln3_g, ln3_b, ffn2_w_gate, ffn2_w_up, ffn2_w_down, ln4_g, ln4_b, rel_bias):
    a = DEEPNORM_ALPHA
    for l in range(DEPTH):
        x = layer_norm(a * x + 0.5 * swiglu(x, ffn1_w_gate[l], ffn1_w_up[l], ffn1_w_down[l]), ln1_g[l], ln1_b[l])
        z = x @ mix_w_in[l]
        y_rwkv = rwkv7_group(z[..., :RWKV_IN], rwkv_mu[l], rwkv_w0[l], rwkv_w_up[l], rwkv_a0[l],
                             rwkv_a_up[l], rwkv_g_up[l], rwkv_k_k[l], rwkv_k_a[l], rwkv_r_k[l],
                             rwkv_gn_g[l], rwkv_gn_b[l])
        y_nsa = nsa_group(z[..., RWKV_IN:], nsa_pe_k[l], nsa_w1_k[l], nsa_w2_k[l],
                          nsa_pe_v[l], nsa_w1_v[l], nsa_w2_v[l], rel_bias)
        mixed = jnp.concatenate([y_rwkv, y_nsa.astype(y_rwkv.dtype)], axis=-1) @ mix_w_out[l]
        x = layer_norm(a * x + mixed, ln2_g[l], ln2_b[l])
        x = layer_norm(a * x + cross_attention(x, mem, xattn_wq[l], xattn_wk[l], xattn_wv[l], xattn_wo[l]),
                       ln3_g[l], ln3_b[l])
        x = layer_norm(a * x + 0.5 * swiglu(x, ffn2_w_gate[l], ffn2_w_up[l], ffn2_w_down[l]), ln4_g[l], ln4_b[l])
    return x
```

```python
import functools
import math

import numpy as np
import jax
import jax.numpy as jnp
from jax import lax
from jax.experimental import pallas as pl
from jax.experimental.pallas import tpu as pltpu

F32 = jnp.float32
BF16 = jnp.bfloat16
HIGHEST = lax.Precision.HIGHEST

D_MODEL = 1024
DEPTH = 1
RWKV_HEADS = 8
HEAD_DIM = 64
RWKV_DIM = RWKV_HEADS * HEAD_DIM
DECAY_LORA = 64
AAA_LORA = 64
GATE_LORA = 128
RWKV_IN = 3 * RWKV_DIM + DECAY_LORA + AAA_LORA + GATE_LORA
RWKV_GN_EPS = 64e-5
NSA_HEADS = 8
NSA_GROUPS = 2
NSA_HPG = NSA_HEADS // NSA_GROUPS
NSA_DIM = NSA_HEADS * HEAD_DIM
CMP_LEN = 32
CMP_STRIDE = 16
CMP_HIDDEN = 128
SLC_LEN = 64
N_SEL = 16
WINDOW = 512
Q_BLOCK = 128
FORCED_SCORE = 1e4
NUM_BUCKETS = 32
MAX_DISTANCE = 128
XATTN_HEADS = 4
XATTN_HEAD_DIM = D_MODEL // XATTN_HEADS
D_FF = 2816
LN_EPS = 1e-5
ALPHA = (2.0 * DEPTH) ** 0.25
NEG = -1e30

V7X_LANES = 128
V7X_SUBLANES = 8
V7X_VMEM_LIMIT_BYTES = 56 * 1024 * 1024

CHUNK = 64
CMP_PAD = 120
CMP_ROWS = 640


def _dot(a, b, prec=None):
    return jnp.dot(a, b, preferred_element_type=F32, precision=prec)


def _dot_nt(a, b, prec=None):
    return lax.dot_general(a, b, (((1,), (1,)), ((), ())), preferred_element_type=F32, precision=prec)


def _dot_tn(a, b, prec=None):
    return lax.dot_general(a, b, (((0,), (0,)), ((), ())), preferred_element_type=F32, precision=prec)


def _sigmoid(x):
    return 1.0 / (1.0 + jnp.exp(-x))


def _layer_norm(y, g, b):
    mu = jnp.mean(y, axis=-1, keepdims=True)
    yc = y - mu
    var = jnp.mean(yc * yc, axis=-1, keepdims=True)
    return yc * lax.rsqrt(var + LN_EPS) * g + b


def _params(sem):
    return pltpu.CompilerParams(dimension_semantics=sem, vmem_limit_bytes=V7X_VMEM_LIMIT_BYTES)


def _const_spec(shape):
    nd = len(shape)
    return pl.BlockSpec(shape, lambda *_: (0,) * nd)


def _bucket_thresholds():
    n = np.arange(0, 4 * MAX_DISTANCE)
    max_exact = NUM_BUCKETS // 2
    nf = np.maximum(n, max_exact).astype(np.float32)
    large = max_exact + (np.log(nf / np.float32(max_exact)) / np.float32(math.log(MAX_DISTANCE / max_exact))
                         * np.float32(NUM_BUCKETS - max_exact)).astype(np.int32)
    large = np.minimum(large, NUM_BUCKETS - 1)
    bucket = np.where(n < max_exact, n, large)
    return [int(np.argmax(bucket >= b)) for b in range(1, NUM_BUCKETS)]


_BUCKET_THR = _bucket_thresholds()


def _bias_tiles_kernel(rb_ref, o_ref):
    h = pl.program_id(0)
    ql = lax.broadcasted_iota(jnp.int32, (Q_BLOCK, Q_BLOCK), 0)
    kl = lax.broadcasted_iota(jnp.int32, (Q_BLOCK, Q_BLOCK), 1)
    dists = (ql - kl, Q_BLOCK + ql - kl, ql - CMP_STRIDE * (kl - CMP_PAD) - (CMP_LEN - 1))
    for kind, dist in enumerate(dists):
        val = jnp.full((Q_BLOCK, Q_BLOCK), rb_ref[0, h], F32)
        for b in range(1, NUM_BUCKETS):
            val = jnp.where(dist >= _BUCKET_THR[b - 1], rb_ref[b, h], val)
        o_ref[0, kind] = jnp.where(dist >= 0, val, NEG)


def _bias_tiles(rel_bias):
    return pl.pallas_call(
        _bias_tiles_kernel,
        grid=(NSA_HEADS,),
        in_specs=[pl.BlockSpec(memory_space=pltpu.SMEM)],
        out_specs=pl.BlockSpec((1, 3, Q_BLOCK, Q_BLOCK), lambda h: (h, 0, 0, 0)),
        out_shape=jax.ShapeDtypeStruct((NSA_HEADS, 3, Q_BLOCK, Q_BLOCK), F32),
        compiler_params=_params(("arbitrary",)),
        name="bias_tiles",
    )(rel_bias)


FFN_TM = 512
FFN_TF = 256


def _ffn_kernel(x_ref, wg_ref, wu_ref, wd_ref, g_ref, b_ref, o_ref, acc_ref, *, nchunk):
    x = x_ref[...]
    xb = x.astype(BF16)
    acc_ref[...] = jnp.zeros_like(acc_ref)

    def body(c, carry):
        hg = _dot(xb, wg_ref[c])
        hu = _dot(xb, wu_ref[c])
        h = hg * _sigmoid(hg) * hu
        acc_ref[...] += _dot(h.astype(BF16), wd_ref[c])
        return carry

    lax.fori_loop(0, nchunk, body, 0)
    o_ref[...] = _layer_norm(ALPHA * x + 0.5 * acc_ref[...], g_ref[...], b_ref[...])


def _ffn_ln(x2d, wg, wu, wd, g, b):
    rows, d = x2d.shape
    f = wg.shape[1]
    nchunk = f // FFN_TF
    wg3 = wg.astype(BF16).reshape(d, nchunk, FFN_TF).transpose(1, 0, 2)
    wu3 = wu.astype(BF16).reshape(d, nchunk, FFN_TF).transpose(1, 0, 2)
    wd3 = wd.astype(BF16).reshape(nchunk, FFN_TF, d)
    tm = min(FFN_TM, rows)
    return pl.pallas_call(
        functools.partial(_ffn_kernel, nchunk=nchunk),
        grid=(rows // tm,),
        in_specs=[
            pl.BlockSpec((tm, d), lambda i: (i, 0)),
            _const_spec((nchunk, d, FFN_TF)),
            _const_spec((nchunk, d, FFN_TF)),
            _const_spec((nchunk, FFN_TF, d)),
            _const_spec((1, d)),
            _const_spec((1, d)),
        ],
        out_specs=pl.BlockSpec((tm, d), lambda i: (i, 0)),
        out_shape=jax.ShapeDtypeStruct((rows, d), F32),
        scratch_shapes=[pltpu.VMEM((tm, d), F32)],
        compiler_params=_params(("parallel",)),
        name="ffn_ln",
    )(x2d, wg3, wu3, wd3, g.reshape(1, d), b.reshape(1, d))


PROJ_TM = 512
_C_RWKV = 0
_C_Q = _C_RWKV + RWKV_IN
_C_KCVC = _C_Q + NSA_DIM
_C_K = _C_KCVC + 256
_C_V = _C_K + 256
_C_GATE = _C_V + 4 * 128
_C_END = _C_GATE + 128


def _inproj_kernel(x_ref, w_ref, zr_ref, q_ref, kcvc_ref, k_ref, v_ref, gate_ref):
    xb = x_ref[0].astype(BF16)
    zr_ref[0] = _dot(xb, w_ref[:, _C_RWKV:_C_Q])
    zq = _dot(xb, w_ref[:, _C_Q:_C_KCVC]) * (HEAD_DIM ** -0.5)
    for h in range(NSA_HEADS):
        q_ref[0, h] = zq[:, h * HEAD_DIM:(h + 1) * HEAD_DIM].astype(BF16)
    zc = _dot(xb, w_ref[:, _C_KCVC:_C_K])
    kcvc_ref[0, 0] = zc[:, 0:128]
    kcvc_ref[0, 1] = zc[:, 128:256]
    zk = _dot(xb, w_ref[:, _C_K:_C_V])
    for j in range(4):
        k_ref[0, j] = zk[:, j * HEAD_DIM:(j + 1) * HEAD_DIM].astype(BF16)
    one_lane = (lax.broadcasted_iota(jnp.int32, (1, V7X_LANES), 1) == HEAD_DIM).astype(F32)
    for j in range(4):
        zv = _dot(xb, w_ref[:, _C_V + j * 128:_C_V + (j + 1) * 128]) + one_lane
        v_ref[0, j] = zv.astype(BF16)
    gate_ref[0] = _dot(xb, w_ref[:, _C_GATE:_C_END])


def _pack_w_in(w_in):
    d = w_in.shape[0]
    o = RWKV_IN
    q = w_in[:, o:o + 512]
    kc = w_in[:, o + 512:o + 640]
    vc = w_in[:, o + 640:o + 768]
    ks = w_in[:, o + 768:o + 896]
    vs = w_in[:, o + 896:o + 1024]
    kw = w_in[:, o + 1024:o + 1152]
    vw = w_in[:, o + 1152:o + 1280]
    gate = w_in[:, o + 1280:o + 1304]
    z64 = jnp.zeros((d, 64), w_in.dtype)
    vpad = [jnp.concatenate([v[:, g * 64:(g + 1) * 64], z64], axis=1) for v in (vs, vw) for g in range(NSA_GROUPS)]
    gate_pad = jnp.concatenate([gate, jnp.zeros((d, 128 - gate.shape[1]), w_in.dtype)], axis=1)
    return jnp.concatenate([w_in[:, :o], q, kc, vc, ks, kw] + vpad + [gate_pad], axis=1).astype(BF16)


def _in_projection(x3d, w_in):
    bsz, s, d = x3d.shape
    tm = min(PROJ_TM, s)
    wp = _pack_w_in(w_in)
    out_shape = (
        jax.ShapeDtypeStruct((bsz, s, RWKV_IN), F32),
        jax.ShapeDtypeStruct((bsz, NSA_HEADS, s, HEAD_DIM), BF16),
        jax.ShapeDtypeStruct((bsz, 2, s, 128), F32),
        jax.ShapeDtypeStruct((bsz, 4, s, HEAD_DIM), BF16),
        jax.ShapeDtypeStruct((bsz, 4, s, 128), BF16),
        jax.ShapeDtypeStruct((bsz, s, 128), F32),
    )
    return pl.pallas_call(
        _inproj_kernel,
        grid=(bsz, s // tm),
        in_specs=[pl.BlockSpec((1, tm, d), lambda b, i: (b, i, 0)), _const_spec((d, _C_END))],
        out_specs=(
            pl.BlockSpec((1, tm, RWKV_IN), lambda b, i: (b, i, 0)),
            pl.BlockSpec((1, NSA_HEADS, tm, HEAD_DIM), lambda b, i: (b, 0, i, 0)),
            pl.BlockSpec((1, 2, tm, 128), lambda b, i: (b, 0, i, 0)),
            pl.BlockSpec((1, 4, tm, HEAD_DIM), lambda b, i: (b, 0, i, 0)),
            pl.BlockSpec((1, 4, tm, 128), lambda b, i: (b, 0, i, 0)),
            pl.BlockSpec((1, tm, 128), lambda b, i: (b, i, 0)),
        ),
        out_shape=out_shape,
        compiler_params=_params(("parallel", "parallel")),
        name="in_projection",
    )(x3d, wp)


RWKV_T = 512
RWKV_PREC = HIGHEST


def _rwkv_consts():
    r = lax.broadcasted_iota(jnp.int32, (128, 128), 0)
    c = lax.broadcasted_iota(jnp.int32, (128, 128), 1)
    same = (r >= CHUNK) == (c >= CHUNK)
    mask_sl = (same & (r > c)).astype(F32)
    mask_l = (same & (r >= c)).astype(F32)
    eye = (r == c).astype(F32)
    head_ones = same.astype(F32)
    rt = lax.broadcasted_iota(jnp.int32, (CHUNK, CHUNK), 0)
    ct = lax.broadcasted_iota(jnp.int32, (CHUNK, CHUNK), 1)
    tri = (rt >= ct).astype(F32)
    lane = lax.broadcasted_iota(jnp.int32, (1, 128), 1)
    m0 = (lane < CHUNK).astype(F32)
    m1 = 1.0 - m0
    return mask_sl, mask_l, eye, head_ones, tri, m0, m1


def _rwkv_chunk_pair(r, lw, k, v, kk, a, h_state, consts):
    mask_sl, mask_l, eye, _, tri, m0, m1 = consts
    p = RWKV_PREC
    cum = _dot(tri, lw, HIGHEST)
    cl = cum[CHUNK - 1:CHUNK, :]
    ka = kk * a
    g_tail = jnp.exp(cl - cum)
    g_inv = jnp.exp(-cum)
    a_t = -(kk * jnp.exp(cum - lw))
    r_t = r * jnp.exp(cum)
    k_t = k * g_inv
    b_t = ka * g_inv
    k_h = k * g_tail
    b_h = ka * g_tail
    g_c = jnp.exp(cl)

    def sm(x):
        return jnp.concatenate([x * m0, x * m1], axis=0)

    def dup(x):
        return jnp.concatenate([x, x], axis=0)

    a_sm = sm(a_t)
    r_sm = sm(r_t)
    v_sm = sm(v)
    pm = _dot_nt(jnp.concatenate([a_sm, r_sm], axis=0), jnp.concatenate([dup(k_t), dup(b_t)], axis=0), p)
    a_ak = pm[0:128, 0:128] * mask_sl
    a_ab = pm[0:128, 128:256] * mask_sl
    a_rk = pm[128:256, 0:128] * mask_l
    a_rb = pm[128:256, 128:256] * mask_l
    x = a_ab
    t_inv = eye + x
    for _ in range(5):
        x = _dot(x, x, p)
        t_inv = t_inv + _dot(t_inv, x, p)
    av = _dot(a_ak, v_sm, p)
    wu = _dot(t_inv, jnp.concatenate([a_sm, av], axis=1), p)
    z = jnp.concatenate([jnp.concatenate([jnp.zeros_like(v_sm), v_sm], axis=1), wu], axis=0)
    qy = _dot(jnp.concatenate([a_rk, a_rb], axis=1), z, p)
    q_hat = r_sm + qy[:, 0:128]
    y_hat = qy[:, 128:256]
    mn = _dot_tn(jnp.concatenate([sm(k_h), sm(b_h)], axis=0), z, p)
    y_sm = _dot(q_hat, h_state, p) + y_hat
    y = y_sm[0:CHUNK] + y_sm[CHUNK:2 * CHUNK]
    h_new = _dot(mn[:, 0:128] + eye * g_c, h_state, p) + mn[:, 128:256]
    return y, h_new


def _rwkv_kernel(z_ref, mu_ref, w0_ref, wup_ref, a0_ref, aup_ref, gup_ref, kk_ref, ka_ref, rk_ref,
                 gng_ref, gnb_ref, o_ref, prev_ref, h_ref, r_s, lw_s, k_s, v_s, kk_s, a_s, y_s, g_s, *, t):
    ti = pl.program_id(1)

    @pl.when(ti == 0)
    def _():
        prev_ref[...] = jnp.zeros_like(prev_ref)
        h_ref[...] = jnp.zeros_like(h_ref)

    consts = _rwkv_consts()
    head_ones = consts[3]

    z = z_ref[0]
    row = lax.broadcasted_iota(jnp.int32, (t, 1), 0)
    prev = jnp.where(row == 0, prev_ref[0:1, :], pltpu.roll(z, 1, 0))
    prev_ref[0:1, :] = z[t - 1:t, :]
    zs = z + (prev - z) * mu_ref[...]

    r = zs[:, 0:512]
    k = zs[:, 512:1024]
    v = zs[:, 1024:1536]
    wa = zs[:, 1536:1664]
    gl = zs[:, 1664:1792]
    u = w0_ref[...] + _dot(jnp.tanh(wa).astype(BF16), wup_ref[...])
    lw = (-math.exp(-0.5)) * _sigmoid(u)
    a = _sigmoid(a0_ref[...] + _dot(wa.astype(BF16), aup_ref[...]))
    g_s[...] = _dot(_sigmoid(gl).astype(BF16), gup_ref[...])
    kkr = k * kk_ref[...]
    k2 = k * (1.0 + (a - 1.0) * ka_ref[...])
    r_s[...] = r
    lw_s[...] = lw
    k_s[...] = k2
    v_s[...] = v
    a_s[...] = a
    for pr in range(4):
        sl = slice(pr * 128, (pr + 1) * 128)
        kp = kkr[:, sl]
        ss = _dot(kp * kp, head_ones, HIGHEST)
        kk_s[:, sl] = kp * lax.rsqrt(jnp.maximum(ss, 1e-24))

    def chunk_body(c, carry):
        rows = pl.ds(pl.multiple_of(c * CHUNK, CHUNK), CHUNK)
        for pr in range(4):
            sl = slice(pr * 128, (pr + 1) * 128)
            y, h_new = _rwkv_chunk_pair(r_s[rows, sl], lw_s[rows, sl], k_s[rows, sl], v_s[rows, sl],
                                        kk_s[rows, sl], a_s[rows, sl], h_ref[pr], consts)
            y_s[rows, sl] = y
            h_ref[pr] = h_new
        return carry

    lax.fori_loop(0, t // CHUNK, chunk_body, 0)

    for pr in range(4):
        sl = slice(pr * 128, (pr + 1) * 128)
        y = y_s[:, sl]
        mean = _dot(y, head_ones, HIGHEST) * (1.0 / HEAD_DIM)
        yc = y - mean
        var = _dot(yc * yc, head_ones, HIGHEST) * (1.0 / HEAD_DIM)
        yn = yc * lax.rsqrt(var + RWKV_GN_EPS) * gng_ref[:, sl] + gnb_ref[:, sl]
        rp = r_s[:, sl]
        bonus = _dot(rp * k_s[:, sl] * rk_ref[:, sl], head_ones, HIGHEST) * v_s[:, sl]
        o_ref[0, :, sl] = ((yn + bonus) * g_s[:, sl]).astype(o_ref.dtype)


def _rwkv_group(zr, mu, w0, w_up, a0, a_up, g_up, k_k, k_a, r_k, gn_g, gn_b):
    bsz, s, _ = zr.shape
    t = min(RWKV_T, s)
    wup_pad = jnp.concatenate([w_up, jnp.zeros_like(a_up)], axis=0).astype(BF16)
    aup_pad = jnp.concatenate([jnp.zeros_like(w_up), a_up], axis=0).astype(BF16)
    row = lambda p: p.reshape(1, -1)
    tile = pltpu.VMEM((t, RWKV_DIM), F32)
    return pl.pallas_call(
        functools.partial(_rwkv_kernel, t=t),
        grid=(bsz, s // t),
        in_specs=[
            pl.BlockSpec((1, t, RWKV_IN), lambda b, i: (b, i, 0)),
            _const_spec((1, RWKV_IN)), _const_spec((1, RWKV_DIM)), _const_spec((128, RWKV_DIM)),
            _const_spec((1, RWKV_DIM)), _const_spec((128, RWKV_DIM)), _const_spec((GATE_LORA, RWKV_DIM)),
            _const_spec((1, RWKV_DIM)), _const_spec((1, RWKV_DIM)), _const_spec((1, RWKV_DIM)),
            _const_spec((1, RWKV_DIM)), _const_spec((1, RWKV_DIM)),
        ],
        out_specs=pl.BlockSpec((1, t, RWKV_DIM), lambda b, i: (b, i, 0)),
        out_shape=jax.ShapeDtypeStruct((bsz, s, RWKV_DIM), BF16),
        scratch_shapes=[pltpu.VMEM((V7X_SUBLANES, RWKV_IN), F32), pltpu.VMEM((4, 128, 128), F32),
                        tile, tile, tile, tile, tile, tile, tile, tile],
        compiler_params=_params(("parallel", "arbitrary")),
        name="rwkv7",
    )(zr, row(mu), row(w0), wup_pad, row(a0), aup_pad, g_up.astype(BF16), row(k_k), row(k_a), row(r_k),
      row(gn_g), row(gn_b))


def _gelu_tanh(x):
    return 0.5 * x * (1.0 + jnp.tanh(math.sqrt(2.0 / math.pi) * (x + 0.044715 * (x * x * x))))


def _compress_kernel(x_ref, pe_ref, w1_ref, w2_ref, kc_ref, vc_ref, *, n_rows):
    half = CMP_LEN // 2
    for kv, o_ref in enumerate((kc_ref, vc_ref)):
        first = jnp.zeros((n_rows, 2 * CMP_HIDDEN), F32)
        second = jnp.zeros((n_rows, 2 * CMP_HIDDEN), F32)
        for p in range(half):
            xa = x_ref[0, kv, pl.ds(p, n_rows, stride=CMP_STRIDE), :]
            first += _dot((xa + pe_ref[kv, p:p + 1, :]).astype(BF16), w1_ref[kv, p])
            second += _dot((xa + pe_ref[kv, half + p:half + p + 1, :]).astype(BF16), w1_ref[kv, half + p])
        pre = first + pltpu.roll(second, n_rows - 1, 0)
        out = _dot(_gelu_tanh(pre).astype(BF16), w2_ref[kv])
        for g in range(NSA_GROUPS):
            o_ref[0, g, 0:CMP_PAD, :] = jnp.zeros((CMP_PAD, HEAD_DIM), o_ref.dtype)
            o_ref[0, g, CMP_PAD:CMP_PAD + n_rows, :] = out[:, g * HEAD_DIM:(g + 1) * HEAD_DIM].astype(o_ref.dtype)
            tail = o_ref.shape[2] - CMP_PAD - n_rows
            o_ref[0, g, CMP_PAD + n_rows:, :] = jnp.zeros((tail, HEAD_DIM), o_ref.dtype)


def _pair_diag(w):
    z = jnp.zeros_like(w)
    return jnp.concatenate([jnp.concatenate([w, z], axis=-1), jnp.concatenate([z, w], axis=-1)], axis=-2)


def _compress(kcvc, pe_k, w1_k, w2_k, pe_v, w1_v, w2_v):
    bsz, _, s, _ = kcvc.shape
    n_rows = s // CMP_STRIDE
    rows_out = CMP_PAD + n_rows + V7X_SUBLANES
    pe = jnp.stack([jnp.concatenate([pe_k, pe_k], axis=1), jnp.concatenate([pe_v, pe_v], axis=1)])
    w1 = jnp.stack([_pair_diag(w1_k.reshape(CMP_LEN, HEAD_DIM, CMP_HIDDEN)),
                    _pair_diag(w1_v.reshape(CMP_LEN, HEAD_DIM, CMP_HIDDEN))]).astype(BF16)
    w2 = jnp.stack([_pair_diag(w2_k), _pair_diag(w2_v)]).astype(BF16)
    shape = jax.ShapeDtypeStruct((bsz, NSA_GROUPS, rows_out, HEAD_DIM), BF16)
    spec = pl.BlockSpec((1, NSA_GROUPS, rows_out, HEAD_DIM), lambda b: (b, 0, 0, 0))
    return pl.pallas_call(
        functools.partial(_compress_kernel, n_rows=n_rows),
        grid=(bsz,),
        in_specs=[pl.BlockSpec((1, 2, s, 128), lambda b: (b, 0, 0, 0)), _const_spec(pe.shape), _const_spec(w1.shape),
                  _const_spec(w2.shape)],
        out_specs=(spec, spec),
        out_shape=(shape, shape),
        compiler_params=_params(("parallel",)),
        name="nsa_compress",
    )(kcvc, pe, w1, w2)


def _cmp_select_kernel(rb_ref, q_ref, kc_ref, vc_ref, tz_ref, gate_ref, oc_ref, sel_ref, *, n_far):
    qi = pl.program_id(1)
    qb = Q_BLOCK
    lane = lax.broadcasted_iota(jnp.int32, (qb, 128), 1)
    rowi = lax.broadcasted_iota(jnp.int32, (qb, 128), 0)
    sg = _sigmoid(gate_ref[0])
    near0 = pl.multiple_of(qi * 8, 8)
    n_near = qi * 8 - CMP_PAD + lane
    for g in range(NSA_GROUPS):
        q4 = q_ref[0, g * NSA_HPG:(g + 1) * NSA_HPG].reshape(NSA_HPG * qb, HEAD_DIM)
        tiles = []
        for j in range(n_far):
            kt = kc_ref[0, g, CMP_PAD + 128 * j:CMP_PAD + 128 * (j + 1), :]
            s = _dot_nt(q4, kt)
            ok = (128 * j + lane) < (qi * 8 - CMP_PAD)
            parts = []
            for p in range(NSA_HPG):
                b31 = rb_ref[NUM_BUCKETS - 1, g * NSA_HPG + p]
                parts.append(jnp.where(ok, s[p * qb:(p + 1) * qb] + b31, NEG))
            tiles.append(parts)
        kt = kc_ref[0, g, pl.ds(near0, 128), :]
        s = _dot_nt(q4, kt)
        parts = []
        for p in range(NSA_HPG):
            sp = s[p * qb:(p + 1) * qb] + tz_ref[g * NSA_HPG + p, 2]
            parts.append(jnp.where(n_near >= 0, sp, NEG))
        tiles.append(parts)

        imp = jnp.zeros((qb, 128), F32)
        psum = [jnp.zeros((qb, 128), F32) for _ in tiles]
        for p in range(NSA_HPG):
            m = tiles[0][p].max(axis=1, keepdims=True)
            for tl in tiles[1:]:
                m = jnp.maximum(m, tl[p].max(axis=1, keepdims=True))
            es = [jnp.where(tl[p] > 0.5 * NEG, jnp.exp(tl[p] - m), 0.0) for tl in tiles]
            den = es[0].sum(axis=1, keepdims=True)
            for e in es[1:]:
                den = den + e.sum(axis=1, keepdims=True)
            inv = 1.0 / jnp.maximum(den, 1e-30)
            o = jnp.zeros((qb, HEAD_DIM), F32)
            for j, e in enumerate(es):
                pc = e * inv
                psum[j] = psum[j] + pc
                if j < n_far:
                    vt = vc_ref[0, g, CMP_PAD + 128 * j:CMP_PAD + 128 * (j + 1), :]
                else:
                    vt = vc_ref[0, g, pl.ds(near0, 128), :]
                o = o + _dot(pc.astype(BF16), vt)
            h = g * NSA_HPG + p
            oc_ref[0, :, h * HEAD_DIM:(h + 1) * HEAD_DIM] = o * sg[:, 3 * h:3 * h + 1]
        for j in range(len(tiles)):
            n_idx = (128 * j + rowi) if j < n_far else (qi * 8 - CMP_PAD + rowi)
            ov = ((n_idx >= 4 * lane - 1) & (n_idx <= 4 * lane + 3)).astype(F32)
            imp = imp + _dot(psum[j], ov, HIGHEST)
        cur = 2 * qi + (rowi >= SLC_LEN).astype(jnp.int32)
        forced = (lane == 0) | (lane == cur) | (lane == cur - 1)
        score = jnp.where(lane <= cur, jnp.where(forced, FORCED_SCORE, imp), -1.0)
        sel = jnp.zeros((qb, 128), F32)
        lane_f = lane.astype(F32)
        for _ in range(N_SEL):
            mx = score.max(axis=1, keepdims=True)
            idx = jnp.where(score == mx, lane_f, 128.0).min(axis=1, keepdims=True)
            pick = lane_f == idx
            sel = jnp.where(pick, 1.0, sel)
            score = jnp.where(pick, -3e38, score)
        sel_ref[0, g] = sel.astype(sel_ref.dtype)


def _cmp_select(rel_bias, q, kc, vc, tz, gate):
    bsz, _, s, _ = q.shape
    nq = s // Q_BLOCK
    n_far = max(0, -(-(8 * (nq - 1) - CMP_PAD) // 128))
    rows_c = kc.shape[2]
    return pl.pallas_call(
        functools.partial(_cmp_select_kernel, n_far=n_far),
        grid=(bsz, nq),
        in_specs=[
            pl.BlockSpec(memory_space=pltpu.SMEM),
            pl.BlockSpec((1, NSA_HEADS, Q_BLOCK, HEAD_DIM), lambda b, i: (b, 0, i, 0)),
            pl.BlockSpec((1, NSA_GROUPS, rows_c, HEAD_DIM), lambda b, i: (b, 0, 0, 0)),
            pl.BlockSpec((1, NSA_GROUPS, rows_c, HEAD_DIM), lambda b, i: (b, 0, 0, 0)),
            _const_spec(tz.shape),
            pl.BlockSpec((1, Q_BLOCK, 128), lambda b, i: (b, i, 0)),
        ],
        out_specs=(
            pl.BlockSpec((1, Q_BLOCK, NSA_DIM), lambda b, i: (b, i, 0)),
            pl.BlockSpec((1, NSA_GROUPS, Q_BLOCK, 128), lambda b, i: (b, 0, i, 0)),
        ),
        out_shape=(jax.ShapeDtypeStruct((bsz, s, NSA_DIM), F32),
                   jax.ShapeDtypeStruct((bsz, NSA_GROUPS, s, 128), BF16)),
        compiler_params=_params(("parallel", "parallel")),
        name="nsa_cmp_select",
    )(rel_bias, q, kc, vc, tz, gate)


def _attn_tile(q4, k, v, bias_fn, mask, m_ref, acc_ref):
    qb = Q_BLOCK
    s = _dot_nt(q4, k)
    parts = []
    for p in range(NSA_HPG):
        sp = s[p * qb:(p + 1) * qb] + bias_fn(p)
        if mask is not None:
            sp = jnp.where(mask, sp, NEG)
        parts.append(sp)
    s = jnp.concatenate(parts, axis=0)
    m_prev = m_ref[...]
    m_next = jnp.maximum(m_prev, s.max(axis=1, keepdims=True))
    alpha = jnp.exp(m_prev - m_next)
    pexp = jnp.exp(s - m_next)
    acc_ref[...] = alpha * acc_ref[...] + _dot(pexp.astype(BF16), v)
    m_ref[...] = m_next


def _sel_win_kernel(rb_ref, q_ref, k_ref, v_ref, sel_ref, tz_ref, gate_ref, oc_ref, o_ref, m_ref, acc_ref):
    qi = pl.program_id(1)
    qb = Q_BLOCK
    rowi = lax.broadcasted_iota(jnp.int32, (qb, 128), 0)
    lane = lax.broadcasted_iota(jnp.int32, (qb, 128), 1)
    sg = _sigmoid(gate_ref[0])
    oc = oc_ref[0]
    win_tiles = WINDOW // qb

    def key_rows(kt):
        return pl.ds(pl.multiple_of(kt * qb, qb), qb)

    def reset():
        m_ref[...] = jnp.full(m_ref.shape, NEG, F32)
        acc_ref[...] = jnp.zeros(acc_ref.shape, F32)

    def result():
        acc = acc_ref[...]
        return acc[:, 0:HEAD_DIM] * (1.0 / acc[:, HEAD_DIM:HEAD_DIM + 1])

    for g in range(NSA_GROUPS):
        q4 = q_ref[0, g * NSA_HPG:(g + 1) * NSA_HPG].reshape(NSA_HPG * qb, HEAD_DIM)
        b31 = [rb_ref[NUM_BUCKETS - 1, g * NSA_HPG + p] for p in range(NSA_HPG)]
        far_bias = lambda p: b31[p]
        sub_bias = lambda p: tz_ref[g * NSA_HPG + p, 1]
        diag_bias = lambda p: tz_ref[g * NSA_HPG + p, 0]
        sel = sel_ref[0, g]

        def sel_mask(kt):
            expand = (rowi == 2 * kt + (lane >= SLC_LEN).astype(jnp.int32)).astype(BF16)
            return _dot(sel, expand) > 0.5

        reset()

        def far_body(kt, carry):
            _attn_tile(q4, k_ref[0, g, key_rows(kt), :], v_ref[0, g, key_rows(kt), :], far_bias, sel_mask(kt),
                       m_ref, acc_ref)
            return carry

        lax.fori_loop(0, jnp.maximum(qi - 1, 0), far_body, 0)

        @pl.when(qi >= 1)
        def _():
            kt = qi - 1
            _attn_tile(q4, k_ref[0, g, key_rows(kt), :], v_ref[0, g, key_rows(kt), :], sub_bias, sel_mask(kt),
                       m_ref, acc_ref)

        _attn_tile(q4, k_ref[0, g, key_rows(qi), :], v_ref[0, g, key_rows(qi), :], diag_bias, sel_mask(qi),
                   m_ref, acc_ref)
        o_sel = result()

        reset()
        kw = NSA_GROUPS + g
        for d in range(win_tiles, 0, -1):
            @pl.when(qi >= d)
            def _(d=d):
                kt = qi - d
                if d == win_tiles:
                    bias, mask = far_bias, lane > rowi
                elif d == 1:
                    bias, mask = sub_bias, None
                else:
                    bias, mask = far_bias, None
                _attn_tile(q4, k_ref[0, kw, key_rows(kt), :], v_ref[0, kw, key_rows(kt), :], bias, mask,
                           m_ref, acc_ref)

        _attn_tile(q4, k_ref[0, kw, key_rows(qi), :], v_ref[0, kw, key_rows(qi), :], diag_bias, None, m_ref, acc_ref)
        o_win = result()

        for p in range(NSA_HPG):
            h = g * NSA_HPG + p
            rows = slice(p * qb, (p + 1) * qb)
            cols = slice(h * HEAD_DIM, (h + 1) * HEAD_DIM)
            y = oc[:, cols] + sg[:, 3 * h + 1:3 * h + 2] * o_sel[rows] + sg[:, 3 * h + 2:3 * h + 3] * o_win[rows]
            o_ref[0, :, cols] = y.astype(o_ref.dtype)


def _sel_win(rel_bias, q, k4, v4, sel, tz, gate, oc):
    bsz, _, s, _ = q.shape
    nq = s // Q_BLOCK
    return pl.pallas_call(
        _sel_win_kernel,
        grid=(bsz, nq),
        in_specs=[
            pl.BlockSpec(memory_space=pltpu.SMEM),
            pl.BlockSpec((1, NSA_HEADS, Q_BLOCK, HEAD_DIM), lambda b, i: (b, 0, i, 0)),
            pl.BlockSpec((1, 4, s, HEAD_DIM), lambda b, i: (b, 0, 0, 0)),
            pl.BlockSpec((1, 4, s, 128), lambda b, i: (b, 0, 0, 0)),
            pl.BlockSpec((1, NSA_GROUPS, Q_BLOCK, 128), lambda b, i: (b, 0, i, 0)),
            _const_spec(tz.shape),
            pl.BlockSpec((1, Q_BLOCK, 128), lambda b, i: (b, i, 0)),
            pl.BlockSpec((1, Q_BLOCK, NSA_DIM), lambda b, i: (b, i, 0)),
        ],
        out_specs=pl.BlockSpec((1, Q_BLOCK, NSA_DIM), lambda b, i: (b, i, 0)),
        out_shape=jax.ShapeDtypeStruct((bsz, s, NSA_DIM), BF16),
        scratch_shapes=[pltpu.VMEM((NSA_HPG * Q_BLOCK, 1), F32), pltpu.VMEM((NSA_HPG * Q_BLOCK, 128), F32)],
        compiler_params=_params(("parallel", "parallel")),
        name="nsa_sel_win",
    )(rel_bias, q, k4, v4, sel, tz, gate, oc)


OUT_TM = 512


def _outproj_kernel(x_ref, yr_ref, yn_ref, w_ref, g_ref, b_ref, o_ref):
    mixed = _dot(yr_ref[...], w_ref[0:RWKV_DIM, :]) + _dot(yn_ref[...], w_ref[RWKV_DIM:, :])
    o_ref[...] = _layer_norm(ALPHA * x_ref[...] + mixed, g_ref[...], b_ref[...])


def _out_projection(x2d, yr, yn, w_out, g, b):
    rows, d = x2d.shape
    tm = min(OUT_TM, rows)
    return pl.pallas_call(
        _outproj_kernel,
        grid=(rows // tm,),
        in_specs=[pl.BlockSpec((tm, d), lambda i: (i, 0)), pl.BlockSpec((tm, RWKV_DIM), lambda i: (i, 0)),
                  pl.BlockSpec((tm, NSA_DIM), lambda i: (i, 0)), _const_spec(w_out.shape),
                  _const_spec((1, d)), _const_spec((1, d))],
        out_specs=pl.BlockSpec((tm, d), lambda i: (i, 0)),
        out_shape=jax.ShapeDtypeStruct((rows, d), F32),
        compiler_params=_params(("parallel",)),
        name="out_projection_ln",
    )(x2d, yr, yn, w_out.astype(BF16), g.reshape(1, d), b.reshape(1, d))


XATTN_TM = 512


def _mem_kv_kernel(mem_ref, wk_ref, wv_ref, k_ref, v_ref):
    mb = mem_ref[0].astype(BF16)
    k_ref[0] = _dot(mb, wk_ref[...]).astype(BF16)
    v_ref[0] = _dot(mb, wv_ref[...]).astype(BF16)


def _mem_kv(mem, wk, wv):
    bsz, m, d = mem.shape
    shape = jax.ShapeDtypeStruct((bsz, m, d), BF16)
    spec = pl.BlockSpec((1, m, d), lambda b: (b, 0, 0))
    return pl.pallas_call(
        _mem_kv_kernel,
        grid=(bsz,),
        in_specs=[spec, _const_spec((d, d)), _const_spec((d, d))],
        out_specs=(spec, spec),
        out_shape=(shape, shape),
        compiler_params=_params(("parallel",)),
        name="xattn_mem_kv",
    )(mem, wk.astype(BF16), wv.astype(BF16))


def _xattn_kernel(x_ref, k_ref, v_ref, wq_ref, wo_ref, g_ref, b_ref, o_ref):
    x = x_ref[0]
    q = (_dot(x.astype(BF16), wq_ref[...]) * (XATTN_HEAD_DIM ** -0.5)).astype(BF16)
    heads = []
    for h in range(XATTN_HEADS):
        cols = slice(h * XATTN_HEAD_DIM, (h + 1) * XATTN_HEAD_DIM)
        s = _dot_nt(q[:, cols], k_ref[0, :, cols])
        e = jnp.exp(s - s.max(axis=1, keepdims=True))
        p = e * (1.0 / e.sum(axis=1, keepdims=True))
        heads.append(_dot(p.astype(BF16), v_ref[0, :, cols]).astype(BF16))
    o = _dot(jnp.concatenate(heads, axis=1), wo_ref[...])
    o_ref[0] = _layer_norm(ALPHA * x + o, g_ref[...], b_ref[...])


def _cross_attention_ln(x3d, k, v, wq, wo, g, b):
    bsz, s, d = x3d.shape
    m = k.shape[1]
    tm = min(XATTN_TM, s)
    return pl.pallas_call(
        _xattn_kernel,
        grid=(bsz, s // tm),
        in_specs=[pl.BlockSpec((1, tm, d), lambda bb, i: (bb, i, 0)),
                  pl.BlockSpec((1, m, d), lambda bb, i: (bb, 0, 0)),
                  pl.BlockSpec((1, m, d), lambda bb, i: (bb, 0, 0)),
                  _const_spec((d, d)), _const_spec((d, d)), _const_spec((1, d)), _const_spec((1, d))],
        out_specs=pl.BlockSpec((1, tm, d), lambda bb, i: (bb, i, 0)),
        out_shape=jax.ShapeDtypeStruct((bsz, s, d), F32),
        compiler_params=_params(("parallel", "parallel")),
        name="xattn_ln",
    )(x3d, k, v, wq.astype(BF16), wo.astype(BF16), g.reshape(1, d), b.reshape(1, d))


def _nsa_group(rel_bias, q, kcvc, k4, v4, gate, pe_k, w1_k, w2_k, pe_v, w1_v, w2_v):
    tz = _bias_tiles(rel_bias)
    kc, vc = _compress(kcvc, pe_k, w1_k, w2_k, pe_v, w1_v, w2_v)
    oc, sel = _cmp_select(rel_bias, q, kc, vc, tz, gate)
    return _sel_win(rel_bias, q, k4, v4, sel, tz, gate, oc)


def kernel(x, mem, ffn1_w_gate, ffn1_w_up, ffn1_w_down, ln1_g, ln1_b, mix_w_in, rwkv_mu, rwkv_w0, rwkv_w_up, rwkv_a0, rwkv_a_up, rwkv_g_up, rwkv_k_k, rwkv_k_a, rwkv_r_k, rwkv_gn_g, rwkv_gn_b, nsa_pe_k, nsa_w1_k, nsa_w2_k, nsa_pe_v, nsa_w1_v, nsa_w2_v, mix_w_out, ln2_g, ln2_b, xattn_wq, xattn_wk, xattn_wv, xattn_wo, ln3_g, ln3_b, ffn2_w_gate, ffn2_w_up, ffn2_w_down, ln4_g, ln4_b, rel_bias):
    bsz, s, d = x.shape
    rows = bsz * s
    for l in range(DEPTH):
        x1 = _ffn_ln(x.reshape(rows, d), ffn1_w_gate[l], ffn1_w_up[l], ffn1_w_down[l], ln1_g[l], ln1_b[l])
        zr, q, kcvc, k4, v4, gate = _in_projection(x1.reshape(bsz, s, d), mix_w_in[l])
        y_rwkv = _rwkv_group(zr, rwkv_mu[l], rwkv_w0[l], rwkv_w_up[l], rwkv_a0[l], rwkv_a_up[l], rwkv_g_up[l],
                             rwkv_k_k[l], rwkv_k_a[l], rwkv_r_k[l], rwkv_gn_g[l], rwkv_gn_b[l])
        y_nsa = _nsa_group(rel_bias, q, kcvc, k4, v4, gate, nsa_pe_k[l], nsa_w1_k[l], nsa_w2_k[l],
                           nsa_pe_v[l], nsa_w1_v[l], nsa_w2_v[l])
        x2 = _out_projection(x1, y_rwkv.reshape(rows, RWKV_DIM), y_nsa.reshape(rows, NSA_DIM), mix_w_out[l],
                             ln2_g[l], ln2_b[l])
        mk, mv = _mem_kv(mem, xattn_wk[l], xattn_wv[l])
        x3 = _cross_attention_ln(x2.reshape(bsz, s, d), mk, mv, xattn_wq[l], xattn_wo[l], ln3_g[l], ln3_b[l])
        x = _ffn_ln(x3.reshape(rows, d), ffn2_w_gate[l], ffn2_w_up[l], ffn2_w_down[l], ln4_g[l], ln4_b[l])
        x = x.reshape(bsz, s, d)
    return x
```

```python
import functools
import math

import numpy as np
import jax
import jax.numpy as jnp
from jax import lax
from jax.experimental import pallas as pl
from jax.experimental.pallas import tpu as pltpu

F32 = jnp.float32
BF16 = jnp.bfloat16
HIGHEST = lax.Precision.HIGHEST

D_MODEL = 1024
DEPTH = 1
RWKV_HEADS = 8
HEAD_DIM = 64
RWKV_DIM = RWKV_HEADS * HEAD_DIM
DECAY_LORA = 64
AAA_LORA = 64
GATE_LORA = 128
RWKV_IN = 3 * RWKV_DIM + DECAY_LORA + AAA_LORA + GATE_LORA
RWKV_GN_EPS = 64e-5
NSA_HEADS = 8
NSA_GROUPS = 2
NSA_HPG = NSA_HEADS // NSA_GROUPS
NSA_DIM = NSA_HEADS * HEAD_DIM
CMP_LEN = 32
CMP_STRIDE = 16
CMP_HIDDEN = 128
SLC_LEN = 64
N_SEL = 16
WINDOW = 512
Q_BLOCK = 128
FORCED_SCORE = 1e4
NUM_BUCKETS = 32
MAX_DISTANCE = 128
XATTN_HEADS = 4
XATTN_HEAD_DIM = D_MODEL // XATTN_HEADS
D_FF = 2816
LN_EPS = 1e-5
ALPHA = (2.0 * DEPTH) ** 0.25
NEG = -1e30

V7X_LANES = 128
V7X_SUBLANES = 8
V7X_VMEM_LIMIT_BYTES = 56 * 1024 * 1024

CHUNK = 64
CMP_PAD = 120
VT_ROWS = 80


def _dot(a, b, prec=None):
    return jnp.dot(a, b, preferred_element_type=F32, precision=prec)


def _dot_nt(a, b, prec=None):
    return lax.dot_general(a, b, (((1,), (1,)), ((), ())), preferred_element_type=F32, precision=prec)


def _dot_tn(a, b, prec=None):
    return lax.dot_general(a, b, (((0,), (0,)), ((), ())), preferred_element_type=F32, precision=prec)


def _sigmoid(x):
    return 1.0 / (1.0 + jnp.exp(-x))


def _layer_norm(y, g, b):
    mu = jnp.mean(y, axis=-1, keepdims=True)
    yc = y - mu
    var = jnp.mean(yc * yc, axis=-1, keepdims=True)
    return yc * lax.rsqrt(var + LN_EPS) * g + b


def _params(sem):
    return pltpu.CompilerParams(dimension_semantics=sem, vmem_limit_bytes=V7X_VMEM_LIMIT_BYTES)


def _const_spec(shape):
    nd = len(shape)
    return pl.BlockSpec(shape, lambda *_: (0,) * nd)


def _bucket_thresholds():
    n = np.arange(0, 4 * MAX_DISTANCE)
    max_exact = NUM_BUCKETS // 2
    nf = np.maximum(n, max_exact).astype(np.float32)
    large = max_exact + (np.log(nf / np.float32(max_exact)) / np.float32(math.log(MAX_DISTANCE / max_exact))
                         * np.float32(NUM_BUCKETS - max_exact)).astype(np.int32)
    large = np.minimum(large, NUM_BUCKETS - 1)
    bucket = np.where(n < max_exact, n, large)
    return [int(np.argmax(bucket >= b)) for b in range(1, NUM_BUCKETS)]


_BUCKET_THR = _bucket_thresholds()


TZ_CMP_NEAR, TZ_DIAG, TZ_SUB, TZ_FAR, TZ_MASKED, TZ_KINDS = 2, 3, 4, 5, 6, 7


def _bias_tiles_kernel(rb_ref, o_ref):
    h = pl.program_id(0)
    r = lax.broadcasted_iota(jnp.int32, (Q_BLOCK, Q_BLOCK), 0)
    c = lax.broadcasted_iota(jnp.int32, (Q_BLOCK, Q_BLOCK), 1)
    dists = (r - c, Q_BLOCK + r - c, r - CMP_STRIDE * (c - CMP_PAD) - (CMP_LEN - 1), c - r, Q_BLOCK + c - r)
    for kind, dist in enumerate(dists):
        val = jnp.full((Q_BLOCK, Q_BLOCK), rb_ref[0, h], F32)
        for b in range(1, NUM_BUCKETS):
            val = jnp.where(dist >= _BUCKET_THR[b - 1], rb_ref[b, h], val)
        o_ref[0, kind] = jnp.where(dist >= 0, val, NEG)
    o_ref[0, TZ_FAR] = jnp.full((Q_BLOCK, Q_BLOCK), rb_ref[NUM_BUCKETS - 1, h], F32)
    o_ref[0, TZ_MASKED] = jnp.full((Q_BLOCK, Q_BLOCK), NEG, F32)


def _bias_tiles(rel_bias):
    return pl.pallas_call(
        _bias_tiles_kernel,
        grid=(NSA_HEADS,),
        in_specs=[pl.BlockSpec(memory_space=pltpu.SMEM)],
        out_specs=pl.BlockSpec((1, TZ_KINDS, Q_BLOCK, Q_BLOCK), lambda h: (h, 0, 0, 0)),
        out_shape=jax.ShapeDtypeStruct((NSA_HEADS, TZ_KINDS, Q_BLOCK, Q_BLOCK), F32),
        compiler_params=_params(("arbitrary",)),
        name="bias_tiles",
    )(rel_bias)


FFN_TM = 512
FFN_TF = 256


def _ffn_kernel(x_ref, wg_ref, wu_ref, wd_ref, g_ref, b_ref, o_ref, acc_ref, *, nchunk):
    x = x_ref[...]
    xb = x.astype(BF16)
    acc_ref[...] = jnp.zeros_like(acc_ref)

    def body(c, carry):
        hg = _dot(xb, wg_ref[c])
        hu = _dot(xb, wu_ref[c])
        h = hg * _sigmoid(hg) * hu
        acc_ref[...] += _dot(h.astype(BF16), wd_ref[c])
        return carry

    lax.fori_loop(0, nchunk, body, 0)
    o_ref[...] = _layer_norm(ALPHA * x + 0.5 * acc_ref[...], g_ref[...], b_ref[...])


def _ffn_ln(x2d, wg, wu, wd, g, b):
    rows, d = x2d.shape
    f = wg.shape[1]
    nchunk = f // FFN_TF
    wg3 = wg.astype(BF16).reshape(d, nchunk, FFN_TF).transpose(1, 0, 2)
    wu3 = wu.astype(BF16).reshape(d, nchunk, FFN_TF).transpose(1, 0, 2)
    wd3 = wd.astype(BF16).reshape(nchunk, FFN_TF, d)
    tm = min(FFN_TM, rows)
    return pl.pallas_call(
        functools.partial(_ffn_kernel, nchunk=nchunk),
        grid=(rows // tm,),
        in_specs=[
            pl.BlockSpec((tm, d), lambda i: (i, 0)),
            _const_spec((nchunk, d, FFN_TF)),
            _const_spec((nchunk, d, FFN_TF)),
            _const_spec((nchunk, FFN_TF, d)),
            _const_spec((1, d)),
            _const_spec((1, d)),
        ],
        out_specs=pl.BlockSpec((tm, d), lambda i: (i, 0)),
        out_shape=jax.ShapeDtypeStruct((rows, d), F32),
        scratch_shapes=[pltpu.VMEM((tm, d), F32)],
        compiler_params=_params(("parallel",)),
        name="ffn_ln",
    )(x2d, wg3, wu3, wd3, g.reshape(1, d), b.reshape(1, d))


PROJ_TM = 512
_C_RWKV = 0
_C_Q = _C_RWKV + RWKV_IN
_C_KCVC = _C_Q + NSA_DIM
_C_K = _C_KCVC + 256
_C_V = _C_K + 256
_C_GATE = _C_V + 4 * 128
_C_END = _C_GATE + 128


def _inproj_kernel(x_ref, w_ref, zr_ref, q_ref, kcvc_ref, k_ref, v_ref, gate_ref):
    xb = x_ref[0].astype(BF16)
    zr_ref[0] = _dot(xb, w_ref[:, _C_RWKV:_C_Q])
    zq = _dot(xb, w_ref[:, _C_Q:_C_KCVC]) * (HEAD_DIM ** -0.5)
    for h in range(NSA_HEADS):
        q_ref[0, h] = zq[:, h * HEAD_DIM:(h + 1) * HEAD_DIM].astype(BF16)
    zc = _dot(xb, w_ref[:, _C_KCVC:_C_K])
    kcvc_ref[0, 0] = zc[:, 0:128]
    kcvc_ref[0, 1] = zc[:, 128:256]
    zk = _dot(xb, w_ref[:, _C_K:_C_V])
    for j in range(4):
        k_ref[0, j] = zk[:, j * HEAD_DIM:(j + 1) * HEAD_DIM].astype(BF16)
    one_lane = (lax.broadcasted_iota(jnp.int32, (1, V7X_LANES), 1) == HEAD_DIM).astype(F32)
    for j in range(4):
        zv = _dot(xb, w_ref[:, _C_V + j * 128:_C_V + (j + 1) * 128]) + one_lane
        for i in range(zv.shape[0] // Q_BLOCK):
            v_ref[0, j, i] = zv[i * Q_BLOCK:(i + 1) * Q_BLOCK].T[0:VT_ROWS].astype(BF16)
    gate_ref[0] = _dot(xb, w_ref[:, _C_GATE:_C_END])


def _pack_w_in(w_in):
    d = w_in.shape[0]
    o = RWKV_IN
    q = w_in[:, o:o + 512]
    kc = w_in[:, o + 512:o + 640]
    vc = w_in[:, o + 640:o + 768]
    ks = w_in[:, o + 768:o + 896]
    vs = w_in[:, o + 896:o + 1024]
    kw = w_in[:, o + 1024:o + 1152]
    vw = w_in[:, o + 1152:o + 1280]
    gate = w_in[:, o + 1280:o + 1304]
    z64 = jnp.zeros((d, 64), w_in.dtype)
    vpad = [jnp.concatenate([v[:, g * 64:(g + 1) * 64], z64], axis=1) for v in (vs, vw) for g in range(NSA_GROUPS)]
    gate_pad = jnp.concatenate([gate, jnp.zeros((d, 128 - gate.shape[1]), w_in.dtype)], axis=1)
    return jnp.concatenate([w_in[:, :o], q, kc, vc, ks, kw] + vpad + [gate_pad], axis=1).astype(BF16)


def _in_projection(x3d, w_in):
    bsz, s, d = x3d.shape
    tm = min(PROJ_TM, s)
    wp = _pack_w_in(w_in)
    out_shape = (
        jax.ShapeDtypeStruct((bsz, s, RWKV_IN), F32),
        jax.ShapeDtypeStruct((bsz, NSA_HEADS, s, HEAD_DIM), BF16),
        jax.ShapeDtypeStruct((bsz, 2, s, 128), F32),
        jax.ShapeDtypeStruct((bsz, 4, s, HEAD_DIM), BF16),
        jax.ShapeDtypeStruct((bsz, 4, s // Q_BLOCK, VT_ROWS, Q_BLOCK), BF16),
        jax.ShapeDtypeStruct((bsz, s, 128), F32),
    )
    return pl.pallas_call(
        _inproj_kernel,
        grid=(bsz, s // tm),
        in_specs=[pl.BlockSpec((1, tm, d), lambda b, i: (b, i, 0)), _const_spec((d, _C_END))],
        out_specs=(
            pl.BlockSpec((1, tm, RWKV_IN), lambda b, i: (b, i, 0)),
            pl.BlockSpec((1, NSA_HEADS, tm, HEAD_DIM), lambda b, i: (b, 0, i, 0)),
            pl.BlockSpec((1, 2, tm, 128), lambda b, i: (b, 0, i, 0)),
            pl.BlockSpec((1, 4, tm, HEAD_DIM), lambda b, i: (b, 0, i, 0)),
            pl.BlockSpec((1, 4, tm // Q_BLOCK, VT_ROWS, Q_BLOCK), lambda b, i: (b, 0, i, 0, 0)),
            pl.BlockSpec((1, tm, 128), lambda b, i: (b, i, 0)),
        ),
        out_shape=out_shape,
        compiler_params=_params(("parallel", "parallel")),
        name="in_projection",
    )(x3d, wp)


RWKV_T = 512
RWKV_PREC = HIGHEST


def _rwkv_consts():
    r = lax.broadcasted_iota(jnp.int32, (128, 128), 0)
    c = lax.broadcasted_iota(jnp.int32, (128, 128), 1)
    same = (r >= CHUNK) == (c >= CHUNK)
    mask_sl = (same & (r > c)).astype(F32)
    mask_l = (same & (r >= c)).astype(F32)
    eye = (r == c).astype(F32)
    head_ones = same.astype(F32)
    rt = lax.broadcasted_iota(jnp.int32, (CHUNK, CHUNK), 0)
    ct = lax.broadcasted_iota(jnp.int32, (CHUNK, CHUNK), 1)
    tri = (rt >= ct).astype(F32)
    lane = lax.broadcasted_iota(jnp.int32, (1, 128), 1)
    m0 = (lane < CHUNK).astype(F32)
    m1 = 1.0 - m0
    return mask_sl, mask_l, eye, head_ones, tri, m0, m1


def _rwkv_chunk_pair(r, lw, k, v, kk, a, h_state, consts):
    mask_sl, mask_l, eye, _, tri, m0, m1 = consts
    p = RWKV_PREC
    cum = _dot(tri, lw, HIGHEST)
    cl = cum[CHUNK - 1:CHUNK, :]
    ka = kk * a
    g_tail = jnp.exp(cl - cum)
    g_inv = jnp.exp(-cum)
    a_t = -(kk * jnp.exp(cum - lw))
    r_t = r * jnp.exp(cum)
    k_t = k * g_inv
    b_t = ka * g_inv
    k_h = k * g_tail
    b_h = ka * g_tail
    g_c = jnp.exp(cl)

    def sm(x):
        return jnp.concatenate([x * m0, x * m1], axis=0)

    def dup(x):
        return jnp.concatenate([x, x], axis=0)

    a_sm = sm(a_t)
    r_sm = sm(r_t)
    v_sm = sm(v)
    pm = _dot_nt(jnp.concatenate([a_sm, r_sm], axis=0), jnp.concatenate([dup(k_t), dup(b_t)], axis=0), p)
    a_ak = pm[0:128, 0:128] * mask_sl
    a_ab = pm[0:128, 128:256] * mask_sl
    a_rk = pm[128:256, 0:128] * mask_l
    a_rb = pm[128:256, 128:256] * mask_l
    x = a_ab
    t_inv = eye + x
    for _ in range(5):
        x = _dot(x, x, p)
        t_inv = t_inv + _dot(t_inv, x, p)
    av = _dot(a_ak, v_sm, p)
    wu = _dot(t_inv, jnp.concatenate([a_sm, av], axis=1), p)
    z = jnp.concatenate([jnp.concatenate([jnp.zeros_like(v_sm), v_sm], axis=1), wu], axis=0)
    qy = _dot(jnp.concatenate([a_rk, a_rb], axis=1), z, p)
    q_hat = r_sm + qy[:, 0:128]
    y_hat = qy[:, 128:256]
    mn = _dot_tn(jnp.concatenate([sm(k_h), sm(b_h)], axis=0), z, p)
    y_sm = _dot(q_hat, h_state, p) + y_hat
    y = y_sm[0:CHUNK] + y_sm[CHUNK:2 * CHUNK]
    h_new = _dot(mn[:, 0:128] + eye * g_c, h_state, p) + mn[:, 128:256]
    return y, h_new


def _rwkv_kernel(z_ref, mu_ref, w0_ref, wup_ref, a0_ref, aup_ref, gup_ref, kk_ref, ka_ref, rk_ref,
                 gng_ref, gnb_ref, o_ref, prev_ref, h_ref, r_s, lw_s, k_s, v_s, kk_s, a_s, y_s, g_s, *, t):
    ti = pl.program_id(1)

    @pl.when(ti == 0)
    def _():
        prev_ref[...] = jnp.zeros_like(prev_ref)
        h_ref[...] = jnp.zeros_like(h_ref)

    consts = _rwkv_consts()
    head_ones = consts[3]

    z = z_ref[0]
    row = lax.broadcasted_iota(jnp.int32, (t, 1), 0)
    prev = jnp.where(row == 0, prev_ref[0:1, :], pltpu.roll(z, 1, 0))
    prev_ref[0:1, :] = z[t - 1:t, :]
    zs = z + (prev - z) * mu_ref[...]

    r = zs[:, 0:512]
    k = zs[:, 512:1024]
    v = zs[:, 1024:1536]
    wa = zs[:, 1536:1664]
    gl = zs[:, 1664:1792]
    u = w0_ref[...] + _dot(jnp.tanh(wa).astype(BF16), wup_ref[...])
    lw = (-math.exp(-0.5)) * _sigmoid(u)
    a = _sigmoid(a0_ref[...] + _dot(wa.astype(BF16), aup_ref[...]))
    g_s[...] = _dot(_sigmoid(gl).astype(BF16), gup_ref[...])
    kkr = k * kk_ref[...]
    k2 = k * (1.0 + (a - 1.0) * ka_ref[...])
    r_s[...] = r
    lw_s[...] = lw
    k_s[...] = k2
    v_s[...] = v
    a_s[...] = a
    for pr in range(4):
        sl = slice(pr * 128, (pr + 1) * 128)
        kp = kkr[:, sl]
        ss = _dot(kp * kp, head_ones, HIGHEST)
        kk_s[:, sl] = kp * lax.rsqrt(jnp.maximum(ss, 1e-24))

    def chunk_body(c, carry):
        rows = pl.ds(pl.multiple_of(c * CHUNK, CHUNK), CHUNK)
        for pr in range(4):
            sl = slice(pr * 128, (pr + 1) * 128)
            y, h_new = _rwkv_chunk_pair(r_s[rows, sl], lw_s[rows, sl], k_s[rows, sl], v_s[rows, sl],
                                        kk_s[rows, sl], a_s[rows, sl], h_ref[pr], consts)
            y_s[rows, sl] = y
            h_ref[pr] = h_new
        return carry

    lax.fori_loop(0, t // CHUNK, chunk_body, 0)

    for pr in range(4):
        sl = slice(pr * 128, (pr + 1) * 128)
        y = y_s[:, sl]
        mean = _dot(y, head_ones, HIGHEST) * (1.0 / HEAD_DIM)
        yc = y - mean
        var = _dot(yc * yc, head_ones, HIGHEST) * (1.0 / HEAD_DIM)
        yn = yc * lax.rsqrt(var + RWKV_GN_EPS) * gng_ref[:, sl] + gnb_ref[:, sl]
        rp = r_s[:, sl]
        bonus = _dot(rp * k_s[:, sl] * rk_ref[:, sl], head_ones, HIGHEST) * v_s[:, sl]
        o_ref[0, :, sl] = ((yn + bonus) * g_s[:, sl]).astype(o_ref.dtype)


def _rwkv_group(zr, mu, w0, w_up, a0, a_up, g_up, k_k, k_a, r_k, gn_g, gn_b):
    bsz, s, _ = zr.shape
    t = min(RWKV_T, s)
    wup_pad = jnp.concatenate([w_up, jnp.zeros_like(a_up)], axis=0).astype(BF16)
    aup_pad = jnp.concatenate([jnp.zeros_like(w_up), a_up], axis=0).astype(BF16)
    row = lambda p: p.reshape(1, -1)
    tile = pltpu.VMEM((t, RWKV_DIM), F32)
    return pl.pallas_call(
        functools.partial(_rwkv_kernel, t=t),
        grid=(bsz, s // t),
        in_specs=[
            pl.BlockSpec((1, t, RWKV_IN), lambda b, i: (b, i, 0)),
            _const_spec((1, RWKV_IN)), _const_spec((1, RWKV_DIM)), _const_spec((128, RWKV_DIM)),
            _const_spec((1, RWKV_DIM)), _const_spec((128, RWKV_DIM)), _const_spec((GATE_LORA, RWKV_DIM)),
            _const_spec((1, RWKV_DIM)), _const_spec((1, RWKV_DIM)), _const_spec((1, RWKV_DIM)),
            _const_spec((1, RWKV_DIM)), _const_spec((1, RWKV_DIM)),
        ],
        out_specs=pl.BlockSpec((1, t, RWKV_DIM), lambda b, i: (b, i, 0)),
        out_shape=jax.ShapeDtypeStruct((bsz, s, RWKV_DIM), BF16),
        scratch_shapes=[pltpu.VMEM((V7X_SUBLANES, RWKV_IN), F32), pltpu.VMEM((4, 128, 128), F32),
                        tile, tile, tile, tile, tile, tile, tile, tile],
        compiler_params=_params(("parallel", "arbitrary")),
        name="rwkv7",
    )(zr, row(mu), row(w0), wup_pad, row(a0), aup_pad, g_up.astype(BF16), row(k_k), row(k_a), row(r_k),
      row(gn_g), row(gn_b))


def _gelu_tanh(x):
    return 0.5 * x * (1.0 + jnp.tanh(math.sqrt(2.0 / math.pi) * (x + 0.044715 * (x * x * x))))


def _compress_kernel(x_ref, pe_ref, w1_ref, w2_ref, kc_ref, vc_ref, *, n_rows):
    half = CMP_LEN // 2
    for kv, o_ref in enumerate((kc_ref, vc_ref)):
        first = jnp.zeros((n_rows, 2 * CMP_HIDDEN), F32)
        second = jnp.zeros((n_rows, 2 * CMP_HIDDEN), F32)
        for p in range(half):
            xa = x_ref[0, kv, pl.ds(p, n_rows, stride=CMP_STRIDE), :]
            first += _dot((xa + pe_ref[kv, p:p + 1, :]).astype(BF16), w1_ref[kv, p])
            second += _dot((xa + pe_ref[kv, half + p:half + p + 1, :]).astype(BF16), w1_ref[kv, half + p])
        pre = first + pltpu.roll(second, n_rows - 1, 0)
        out = _dot(_gelu_tanh(pre).astype(BF16), w2_ref[kv])
        for g in range(NSA_GROUPS):
            o_ref[0, g, 0:CMP_PAD, :] = jnp.zeros((CMP_PAD, HEAD_DIM), o_ref.dtype)
            o_ref[0, g, CMP_PAD:CMP_PAD + n_rows, :] = out[:, g * HEAD_DIM:(g + 1) * HEAD_DIM].astype(o_ref.dtype)
            tail = o_ref.shape[2] - CMP_PAD - n_rows
            o_ref[0, g, CMP_PAD + n_rows:, :] = jnp.zeros((tail, HEAD_DIM), o_ref.dtype)


def _pair_diag(w):
    z = jnp.zeros_like(w)
    return jnp.concatenate([jnp.concatenate([w, z], axis=-1), jnp.concatenate([z, w], axis=-1)], axis=-2)


def _compress(kcvc, pe_k, w1_k, w2_k, pe_v, w1_v, w2_v):
    bsz, _, s, _ = kcvc.shape
    n_rows = s // CMP_STRIDE
    rows_out = CMP_PAD + n_rows + V7X_SUBLANES
    pe = jnp.stack([jnp.concatenate([pe_k, pe_k], axis=1), jnp.concatenate([pe_v, pe_v], axis=1)])
    w1 = jnp.stack([_pair_diag(w1_k.reshape(CMP_LEN, HEAD_DIM, CMP_HIDDEN)),
                    _pair_diag(w1_v.reshape(CMP_LEN, HEAD_DIM, CMP_HIDDEN))]).astype(BF16)
    w2 = jnp.stack([_pair_diag(w2_k), _pair_diag(w2_v)]).astype(BF16)
    shape = jax.ShapeDtypeStruct((bsz, NSA_GROUPS, rows_out, HEAD_DIM), BF16)
    spec = pl.BlockSpec((1, NSA_GROUPS, rows_out, HEAD_DIM), lambda b: (b, 0, 0, 0))
    return pl.pallas_call(
        functools.partial(_compress_kernel, n_rows=n_rows),
        grid=(bsz,),
        in_specs=[pl.BlockSpec((1, 2, s, 128), lambda b: (b, 0, 0, 0)), _const_spec(pe.shape), _const_spec(w1.shape),
                  _const_spec(w2.shape)],
        out_specs=(spec, spec),
        out_shape=(shape, shape),
        compiler_params=_params(("parallel",)),
        name="nsa_compress",
    )(kcvc, pe, w1, w2)


def _cmp_select_kernel(rb_ref, q_ref, kc_ref, vc_ref, tz_ref, gate_ref, oc_ref, sel_ref, *, n_far):
    qi = pl.program_id(1)
    qb = Q_BLOCK
    lane = lax.broadcasted_iota(jnp.int32, (qb, 128), 1)
    rowi = lax.broadcasted_iota(jnp.int32, (qb, 128), 0)
    sg = _sigmoid(gate_ref[0])
    near0 = pl.multiple_of(qi * 8, 8)
    n_near = qi * 8 - CMP_PAD + lane
    for g in range(NSA_GROUPS):
        q4 = q_ref[0, g * NSA_HPG:(g + 1) * NSA_HPG].reshape(NSA_HPG * qb, HEAD_DIM)
        tiles = []
        for j in range(n_far):
            kt = kc_ref[0, g, CMP_PAD + 128 * j:CMP_PAD + 128 * (j + 1), :]
            s = _dot_nt(q4, kt)
            ok = (128 * j + lane) < (qi * 8 - CMP_PAD)
            parts = []
            for p in range(NSA_HPG):
                b31 = rb_ref[NUM_BUCKETS - 1, g * NSA_HPG + p]
                parts.append(jnp.where(ok, s[p * qb:(p + 1) * qb] + b31, NEG))
            tiles.append(parts)
        kt = kc_ref[0, g, pl.ds(near0, 128), :]
        s = _dot_nt(q4, kt)
        parts = []
        for p in range(NSA_HPG):
            sp = s[p * qb:(p + 1) * qb] + tz_ref[g * NSA_HPG + p, 2]
            parts.append(jnp.where(n_near >= 0, sp, NEG))
        tiles.append(parts)

        imp = jnp.zeros((qb, 128), F32)
        psum = [jnp.zeros((qb, 128), F32) for _ in tiles]
        for p in range(NSA_HPG):
            m = tiles[0][p].max(axis=1, keepdims=True)
            for tl in tiles[1:]:
                m = jnp.maximum(m, tl[p].max(axis=1, keepdims=True))
            es = [jnp.where(tl[p] > 0.5 * NEG, jnp.exp(tl[p] - m), 0.0) for tl in tiles]
            den = es[0].sum(axis=1, keepdims=True)
            for e in es[1:]:
                den = den + e.sum(axis=1, keepdims=True)
            inv = 1.0 / jnp.maximum(den, 1e-30)
            o = jnp.zeros((qb, HEAD_DIM), F32)
            for j, e in enumerate(es):
                pc = e * inv
                psum[j] = psum[j] + pc
                if j < n_far:
                    vt = vc_ref[0, g, CMP_PAD + 128 * j:CMP_PAD + 128 * (j + 1), :]
                else:
                    vt = vc_ref[0, g, pl.ds(near0, 128), :]
                o = o + _dot(pc.astype(BF16), vt)
            h = g * NSA_HPG + p
            oc_ref[0, :, h * HEAD_DIM:(h + 1) * HEAD_DIM] = o * sg[:, 3 * h:3 * h + 1]
        for j in range(len(tiles)):
            n_idx = (128 * j + rowi) if j < n_far else (qi * 8 - CMP_PAD + rowi)
            ov = ((n_idx >= 4 * lane - 1) & (n_idx <= 4 * lane + 3)).astype(F32)
            imp = imp + _dot(psum[j], ov, HIGHEST)
        cur = 2 * qi + (rowi >= SLC_LEN).astype(jnp.int32)
        forced = (lane == 0) | (lane == cur) | (lane == cur - 1)
        score = jnp.where(lane <= cur, jnp.where(forced, FORCED_SCORE, imp), -1.0)
        sel = jnp.zeros((qb, 128), F32)
        lane_f = lane.astype(F32)
        for _ in range(N_SEL):
            mx = score.max(axis=1, keepdims=True)
            idx = jnp.where(score == mx, lane_f, 128.0).min(axis=1, keepdims=True)
            pick = lane_f == idx
            sel = jnp.where(pick, 1.0, sel)
            score = jnp.where(pick, -3e38, score)
        sel_ref[0, g] = sel.T


def _cmp_select(rel_bias, q, kc, vc, tz, gate):
    bsz, _, s, _ = q.shape
    nq = s // Q_BLOCK
    n_far = max(0, -(-(8 * (nq - 1) - CMP_PAD) // 128))
    rows_c = kc.shape[2]
    return pl.pallas_call(
        functools.partial(_cmp_select_kernel, n_far=n_far),
        grid=(bsz, nq),
        in_specs=[
            pl.BlockSpec(memory_space=pltpu.SMEM),
            pl.BlockSpec((1, NSA_HEADS, Q_BLOCK, HEAD_DIM), lambda b, i: (b, 0, i, 0)),
            pl.BlockSpec((1, NSA_GROUPS, rows_c, HEAD_DIM), lambda b, i: (b, 0, 0, 0)),
            pl.BlockSpec((1, NSA_GROUPS, rows_c, HEAD_DIM), lambda b, i: (b, 0, 0, 0)),
            _const_spec(tz.shape),
            pl.BlockSpec((1, Q_BLOCK, 128), lambda b, i: (b, i, 0)),
        ],
        out_specs=(
            pl.BlockSpec((1, Q_BLOCK, NSA_DIM), lambda b, i: (b, i, 0)),
            pl.BlockSpec((1, NSA_GROUPS, 128, Q_BLOCK), lambda b, i: (b, 0, 0, i)),
        ),
        out_shape=(jax.ShapeDtypeStruct((bsz, s, NSA_DIM), F32),
                   jax.ShapeDtypeStruct((bsz, NSA_GROUPS, 128, s), F32)),
        compiler_params=_params(("parallel", "parallel")),
        name="nsa_cmp_select",
    )(rel_bias, q, kc, vc, tz, gate)


SEL_TILES = 4


def _attn_scores(q4, ks, bias_fns, masks, m_prev):
    qb = Q_BLOCK
    s = _dot_nt(ks, q4)
    rows = []
    for i in range(len(bias_fns)):
        parts = []
        for p in range(NSA_HPG):
            sp = s[i * qb:(i + 1) * qb, p * qb:(p + 1) * qb] + bias_fns[i](p)
            if masks[i] is not None:
                sp = jnp.where(masks[i], sp, NEG)
            parts.append(sp)
        rows.append(jnp.concatenate(parts, axis=1))
    s = jnp.concatenate(rows, axis=0)
    return s, jnp.maximum(m_prev, s.max(axis=0, keepdims=True))


def _attn_accumulate(s, vts, m_prev, m_next, m_ref, acc_ref):
    pexp = jnp.exp(s - m_next)
    acc_ref[...] = jnp.exp(m_prev - m_next) * acc_ref[...] + _dot(jnp.concatenate(vts, axis=1), pexp.astype(BF16))
    m_ref[...] = m_next


def _sel_win_kernel(q_ref, k_ref, vt_ref, sel_ref, tz_ref, gate_ref, oc_ref, o_ref, m_ref, acc_ref):
    qi = pl.program_id(1)
    qb = Q_BLOCK
    rowi = lax.broadcasted_iota(jnp.int32, (qb, 128), 0)
    lane = lax.broadcasted_iota(jnp.int32, (qb, 128), 1)
    win_tiles = WINDOW // qb
    m_ref[...] = jnp.full(m_ref.shape, NEG, F32)
    acc_ref[...] = jnp.zeros(acc_ref.shape, F32)
    q4 = [q_ref[0, g * NSA_HPG:(g + 1) * NSA_HPG].reshape(NSA_HPG * qb, HEAD_DIM) for g in range(NSA_GROUPS)]

    def key_rows(kt, n=1):
        return pl.ds(pl.multiple_of(kt * qb, qb), n * qb)

    def bias_of(g, kind):
        return lambda p: tz_ref[g * NSA_HPG + p, kind]

    def sel_mask(g, kt):
        lo = sel_ref[0, g, pl.ds(2 * kt, 1), :]
        hi = sel_ref[0, g, pl.ds(2 * kt + 1, 1), :]
        return jnp.where(rowi < SLC_LEN, lo, hi) > 0.5

    def sel_body(j, carry):
        kt0 = j * SEL_TILES
        scored = []
        for g in range(NSA_GROUPS):
            vts, biases, masks = [], [], []
            for i in range(SEL_TILES):
                kt = kt0 + i
                kind = jnp.where(kt == qi, TZ_DIAG,
                                 jnp.where(kt == qi - 1, TZ_SUB, jnp.where(kt < qi, TZ_FAR, TZ_MASKED)))
                vts.append(vt_ref[0, g, kt])
                biases.append(bias_of(g, kind))
                masks.append(sel_mask(g, kt))
            m_prev = m_ref[g]
            s, m_next = _attn_scores(q4[g], k_ref[0, g, key_rows(kt0, SEL_TILES), :], biases, masks, m_prev)
            scored.append((s, vts, m_prev, m_next))
        for g in range(NSA_GROUPS):
            _attn_accumulate(*scored[g], m_ref.at[g], acc_ref.at[g])
        return carry

    lax.fori_loop(0, (qi + SEL_TILES) // SEL_TILES, sel_body, 0)

    scored = []
    for g in range(NSA_GROUPS):
        kw = NSA_GROUPS + g
        ks, vts, biases, masks = [], [], [], []
        for d in range(win_tiles, -1, -1):
            kt = jnp.maximum(qi - d, 0)
            kind = TZ_DIAG if d == 0 else jnp.where(qi >= d, TZ_SUB if d == 1 else TZ_FAR, TZ_MASKED)
            ks.append(k_ref[0, kw, key_rows(kt), :])
            vts.append(vt_ref[0, kw, kt])
            biases.append(bias_of(g, kind))
            masks.append(rowi > lane if d == win_tiles else None)
        m_prev = m_ref[kw]
        s, m_next = _attn_scores(q4[g], jnp.concatenate(ks, axis=0), biases, masks, m_prev)
        scored.append((s, vts, m_prev, m_next))
    for g in range(NSA_GROUPS):
        _attn_accumulate(*scored[g], m_ref.at[NSA_GROUPS + g], acc_ref.at[NSA_GROUPS + g])

    sg = _sigmoid(gate_ref[0])
    oc = oc_ref[0]
    for g in range(NSA_GROUPS):
        outs = []
        for stream in (g, NSA_GROUPS + g):
            acc = acc_ref[stream]
            outs.append(acc[0:HEAD_DIM] * (1.0 / acc[HEAD_DIM:HEAD_DIM + 1]))
        for p in range(NSA_HPG):
            h = g * NSA_HPG + p
            qcols = slice(p * qb, (p + 1) * qb)
            cols = slice(h * HEAD_DIM, (h + 1) * HEAD_DIM)
            y = (oc[:, cols] + sg[:, 3 * h + 1:3 * h + 2] * outs[0][:, qcols].T
                 + sg[:, 3 * h + 2:3 * h + 3] * outs[1][:, qcols].T)
            o_ref[0, :, cols] = y.astype(o_ref.dtype)


def _sel_win(q, k4, v4, sel, tz, gate, oc):
    bsz, _, s, _ = q.shape
    nq = s // Q_BLOCK
    assert nq % SEL_TILES == 0
    return pl.pallas_call(
        _sel_win_kernel,
        grid=(bsz, nq),
        in_specs=[
            pl.BlockSpec((1, NSA_HEADS, Q_BLOCK, HEAD_DIM), lambda b, i: (b, 0, i, 0)),
            pl.BlockSpec((1, 4, s, HEAD_DIM), lambda b, i: (b, 0, 0, 0)),
            pl.BlockSpec((1, 4, nq, VT_ROWS, Q_BLOCK), lambda b, i: (b, 0, 0, 0, 0)),
            pl.BlockSpec((1, NSA_GROUPS, 128, Q_BLOCK), lambda b, i: (b, 0, 0, i)),
            _const_spec(tz.shape),
            pl.BlockSpec((1, Q_BLOCK, 128), lambda b, i: (b, i, 0)),
            pl.BlockSpec((1, Q_BLOCK, NSA_DIM), lambda b, i: (b, i, 0)),
        ],
        out_specs=pl.BlockSpec((1, Q_BLOCK, NSA_DIM), lambda b, i: (b, i, 0)),
        out_shape=jax.ShapeDtypeStruct((bsz, s, NSA_DIM), BF16),
        scratch_shapes=[pltpu.VMEM((2 * NSA_GROUPS, 1, NSA_HPG * Q_BLOCK), F32),
                        pltpu.VMEM((2 * NSA_GROUPS, VT_ROWS, NSA_HPG * Q_BLOCK), F32)],
        compiler_params=_params(("parallel", "parallel")),
        name="nsa_sel_win",
    )(q, k4, v4, sel, tz, gate, oc)


OUT_TM = 512


def _outproj_kernel(x_ref, yr_ref, yn_ref, w_ref, g_ref, b_ref, o_ref):
    mixed = _dot(yr_ref[...], w_ref[0:RWKV_DIM, :]) + _dot(yn_ref[...], w_ref[RWKV_DIM:, :])
    o_ref[...] = _layer_norm(ALPHA * x_ref[...] + mixed, g_ref[...], b_ref[...])


def _out_projection(x2d, yr, yn, w_out, g, b):
    rows, d = x2d.shape
    tm = min(OUT_TM, rows)
    return pl.pallas_call(
        _outproj_kernel,
        grid=(rows // tm,),
        in_specs=[pl.BlockSpec((tm, d), lambda i: (i, 0)), pl.BlockSpec((tm, RWKV_DIM), lambda i: (i, 0)),
                  pl.BlockSpec((tm, NSA_DIM), lambda i: (i, 0)), _const_spec(w_out.shape),
                  _const_spec((1, d)), _const_spec((1, d))],
        out_specs=pl.BlockSpec((tm, d), lambda i: (i, 0)),
        out_shape=jax.ShapeDtypeStruct((rows, d), F32),
        compiler_params=_params(("parallel",)),
        name="out_projection_ln",
    )(x2d, yr, yn, w_out.astype(BF16), g.reshape(1, d), b.reshape(1, d))


XATTN_TM = 512


def _mem_kv_kernel(mem_ref, wk_ref, wv_ref, k_ref, v_ref):
    mb = mem_ref[0].astype(BF16)
    k_ref[0] = _dot(mb, wk_ref[...]).astype(BF16)
    v_ref[0] = _dot(mb, wv_ref[...]).astype(BF16)


def _mem_kv(mem, wk, wv):
    bsz, m, d = mem.shape
    shape = jax.ShapeDtypeStruct((bsz, m, d), BF16)
    spec = pl.BlockSpec((1, m, d), lambda b: (b, 0, 0))
    return pl.pallas_call(
        _mem_kv_kernel,
        grid=(bsz,),
        in_specs=[spec, _const_spec((d, d)), _const_spec((d, d))],
        out_specs=(spec, spec),
        out_shape=(shape, shape),
        compiler_params=_params(("parallel",)),
        name="xattn_mem_kv",
    )(mem, wk.astype(BF16), wv.astype(BF16))


def _xattn_kernel(x_ref, k_ref, v_ref, wq_ref, wo_ref, g_ref, b_ref, o_ref):
    x = x_ref[0]
    q = (_dot(x.astype(BF16), wq_ref[...]) * (XATTN_HEAD_DIM ** -0.5)).astype(BF16)
    heads = []
    for h in range(XATTN_HEADS):
        cols = slice(h * XATTN_HEAD_DIM, (h + 1) * XATTN_HEAD_DIM)
        s = _dot_nt(q[:, cols], k_ref[0, :, cols])
        e = jnp.exp(s - s.max(axis=1, keepdims=True))
        p = e * (1.0 / e.sum(axis=1, keepdims=True))
        heads.append(_dot(p.astype(BF16), v_ref[0, :, cols]).astype(BF16))
    o = _dot(jnp.concatenate(heads, axis=1), wo_ref[...])
    o_ref[0] = _layer_norm(ALPHA * x + o, g_ref[...], b_ref[...])


def _cross_attention_ln(x3d, k, v, wq, wo, g, b):
    bsz, s, d = x3d.shape
    m = k.shape[1]
    tm = min(XATTN_TM, s)
    return pl.pallas_call(
        _xattn_kernel,
        grid=(bsz, s // tm),
        in_specs=[pl.BlockSpec((1, tm, d), lambda bb, i: (bb, i, 0)),
                  pl.BlockSpec((1, m, d), lambda bb, i: (bb, 0, 0)),
                  pl.BlockSpec((1, m, d), lambda bb, i: (bb, 0, 0)),
                  _const_spec((d, d)), _const_spec((d, d)), _const_spec((1, d)), _const_spec((1, d))],
        out_specs=pl.BlockSpec((1, tm, d), lambda bb, i: (bb, i, 0)),
        out_shape=jax.ShapeDtypeStruct((bsz, s, d), F32),
        compiler_params=_params(("parallel", "parallel")),
        name="xattn_ln",
    )(x3d, k, v, wq.astype(BF16), wo.astype(BF16), g.reshape(1, d), b.reshape(1, d))


def _nsa_group(rel_bias, q, kcvc, k4, v4, gate, pe_k, w1_k, w2_k, pe_v, w1_v, w2_v):
    tz = _bias_tiles(rel_bias)
    kc, vc = _compress(kcvc, pe_k, w1_k, w2_k, pe_v, w1_v, w2_v)
    oc, sel = _cmp_select(rel_bias, q, kc, vc, tz, gate)
    return _sel_win(q, k4, v4, sel, tz, gate, oc)


def kernel(x, mem, ffn1_w_gate, ffn1_w_up, ffn1_w_down, ln1_g, ln1_b, mix_w_in, rwkv_mu, rwkv_w0, rwkv_w_up, rwkv_a0, rwkv_a_up, rwkv_g_up, rwkv_k_k, rwkv_k_a, rwkv_r_k, rwkv_gn_g, rwkv_gn_b, nsa_pe_k, nsa_w1_k, nsa_w2_k, nsa_pe_v, nsa_w1_v, nsa_w2_v, mix_w_out, ln2_g, ln2_b, xattn_wq, xattn_wk, xattn_wv, xattn_wo, ln3_g, ln3_b, ffn2_w_gate, ffn2_w_up, ffn2_w_down, ln4_g, ln4_b, rel_bias):
    bsz, s, d = x.shape
    rows = bsz * s
    for l in range(DEPTH):
        x1 = _ffn_ln(x.reshape(rows, d), ffn1_w_gate[l], ffn1_w_up[l], ffn1_w_down[l], ln1_g[l], ln1_b[l])
        zr, q, kcvc, k4, v4, gate = _in_projection(x1.reshape(bsz, s, d), mix_w_in[l])
        y_rwkv = _rwkv_group(zr, rwkv_mu[l], rwkv_w0[l], rwkv_w_up[l], rwkv_a0[l], rwkv_a_up[l], rwkv_g_up[l],
                             rwkv_k_k[l], rwkv_k_a[l], rwkv_r_k[l], rwkv_gn_g[l], rwkv_gn_b[l])
        y_nsa = _nsa_group(rel_bias, q, kcvc, k4, v4, gate, nsa_pe_k[l], nsa_w1_k[l], nsa_w2_k[l],
                           nsa_pe_v[l], nsa_w1_v[l], nsa_w2_v[l])
        x2 = _out_projection(x1, y_rwkv.reshape(rows, RWKV_DIM), y_nsa.reshape(rows, NSA_DIM), mix_w_out[l],
                             ln2_g[l], ln2_b[l])
        mk, mv = _mem_kv(mem, xattn_wk[l], xattn_wv[l])
        x3 = _cross_attention_ln(x2.reshape(bsz, s, d), mk, mv, xattn_wq[l], xattn_wo[l], ln3_g[l], ln3_b[l])
        x = _ffn_ln(x3.reshape(rows, d), ffn2_w_gate[l], ffn2_w_up[l], ffn2_w_down[l], ln4_g[l], ln4_b[l])
        x = x.reshape(bsz, s, d)
    return x
```

```python
import functools
import math

import numpy as np
import jax
import jax.numpy as jnp
from jax import lax
from jax.experimental import pallas as pl
from jax.experimental.pallas import tpu as pltpu

F32 = jnp.float32
BF16 = jnp.bfloat16
HIGHEST = lax.Precision.HIGHEST

D_MODEL = 1024
DEPTH = 1
RWKV_HEADS = 8
HEAD_DIM = 64
RWKV_DIM = RWKV_HEADS * HEAD_DIM
DECAY_LORA = 64
AAA_LORA = 64
GATE_LORA = 128
RWKV_IN = 3 * RWKV_DIM + DECAY_LORA + AAA_LORA + GATE_LORA
RWKV_GN_EPS = 64e-5
NSA_HEADS = 8
NSA_GROUPS = 2
NSA_HPG = NSA_HEADS // NSA_GROUPS
NSA_DIM = NSA_HEADS * HEAD_DIM
CMP_LEN = 32
CMP_STRIDE = 16
CMP_HIDDEN = 128
SLC_LEN = 64
N_SEL = 16
WINDOW = 512
Q_BLOCK = 128
FORCED_SCORE = 1e4
NUM_BUCKETS = 32
MAX_DISTANCE = 128
XATTN_HEADS = 4
XATTN_HEAD_DIM = D_MODEL // XATTN_HEADS
D_FF = 2816
LN_EPS = 1e-5
ALPHA = (2.0 * DEPTH) ** 0.25
NEG = -1e30

V7X_LANES = 128
V7X_SUBLANES = 8
V7X_VMEM_LIMIT_BYTES = 56 * 1024 * 1024

CHUNK = 64
CMP_PAD = 120
VT_ROWS = 80


def _dot(a, b, prec=None):
    return jnp.dot(a, b, preferred_element_type=F32, precision=prec)


def _dot_nt(a, b, prec=None):
    return lax.dot_general(a, b, (((1,), (1,)), ((), ())), preferred_element_type=F32, precision=prec)


def _dot_tn(a, b, prec=None):
    return lax.dot_general(a, b, (((0,), (0,)), ((), ())), preferred_element_type=F32, precision=prec)


def _sigmoid(x):
    return 1.0 / (1.0 + jnp.exp(-x))


def _layer_norm(y, g, b):
    mu = jnp.mean(y, axis=-1, keepdims=True)
    yc = y - mu
    var = jnp.mean(yc * yc, axis=-1, keepdims=True)
    return yc * lax.rsqrt(var + LN_EPS) * g + b


def _params(sem):
    return pltpu.CompilerParams(dimension_semantics=sem, vmem_limit_bytes=V7X_VMEM_LIMIT_BYTES)


def _const_spec(shape):
    nd = len(shape)
    return pl.BlockSpec(shape, lambda *_: (0,) * nd)


def _bucket_thresholds():
    n = np.arange(0, 4 * MAX_DISTANCE)
    max_exact = NUM_BUCKETS // 2
    nf = np.maximum(n, max_exact).astype(np.float32)
    large = max_exact + (np.log(nf / np.float32(max_exact)) / np.float32(math.log(MAX_DISTANCE / max_exact))
                         * np.float32(NUM_BUCKETS - max_exact)).astype(np.int32)
    large = np.minimum(large, NUM_BUCKETS - 1)
    bucket = np.where(n < max_exact, n, large)
    return [int(np.argmax(bucket >= b)) for b in range(1, NUM_BUCKETS)]


_BUCKET_THR = _bucket_thresholds()


TZ_CMP_NEAR, TZ_DIAG, TZ_SUB, TZ_FAR, TZ_MASKED, TZ_KINDS = 0, 1, 2, 3, 4, 5


def _bias_tiles_kernel(rb_ref, o_ref):
    h = pl.program_id(0)
    r = lax.broadcasted_iota(jnp.int32, (Q_BLOCK, Q_BLOCK), 0)
    c = lax.broadcasted_iota(jnp.int32, (Q_BLOCK, Q_BLOCK), 1)
    dists = (c - CMP_STRIDE * (r - CMP_PAD) - (CMP_LEN - 1), c - r, Q_BLOCK + c - r)
    for kind, dist in enumerate(dists):
        val = jnp.full((Q_BLOCK, Q_BLOCK), rb_ref[0, h], F32)
        for b in range(1, NUM_BUCKETS):
            val = jnp.where(dist >= _BUCKET_THR[b - 1], rb_ref[b, h], val)
        o_ref[0, kind] = jnp.where(dist >= 0, val, NEG)
    o_ref[0, TZ_FAR] = jnp.full((Q_BLOCK, Q_BLOCK), rb_ref[NUM_BUCKETS - 1, h], F32)
    o_ref[0, TZ_MASKED] = jnp.full((Q_BLOCK, Q_BLOCK), NEG, F32)


def _bias_tiles(rel_bias):
    return pl.pallas_call(
        _bias_tiles_kernel,
        grid=(NSA_HEADS,),
        in_specs=[pl.BlockSpec(memory_space=pltpu.SMEM)],
        out_specs=pl.BlockSpec((1, TZ_KINDS, Q_BLOCK, Q_BLOCK), lambda h: (h, 0, 0, 0)),
        out_shape=jax.ShapeDtypeStruct((NSA_HEADS, TZ_KINDS, Q_BLOCK, Q_BLOCK), F32),
        compiler_params=_params(("arbitrary",)),
        name="bias_tiles",
    )(rel_bias)


FFN_TM = 512
FFN_TF = 256


def _ffn_kernel(x_ref, wg_ref, wu_ref, wd_ref, g_ref, b_ref, o_ref, acc_ref, *, nchunk):
    x = x_ref[...]
    xb = x.astype(BF16)
    acc_ref[...] = jnp.zeros_like(acc_ref)

    def body(c, carry):
        hg = _dot(xb, wg_ref[c])
        hu = _dot(xb, wu_ref[c])
        h = hg * _sigmoid(hg) * hu
        acc_ref[...] += _dot(h.astype(BF16), wd_ref[c])
        return carry

    lax.fori_loop(0, nchunk, body, 0)
    o_ref[...] = _layer_norm(ALPHA * x + 0.5 * acc_ref[...], g_ref[...], b_ref[...])


def _ffn_ln(x2d, wg, wu, wd, g, b):
    rows, d = x2d.shape
    f = wg.shape[1]
    nchunk = f // FFN_TF
    wg3 = wg.astype(BF16).reshape(d, nchunk, FFN_TF).transpose(1, 0, 2)
    wu3 = wu.astype(BF16).reshape(d, nchunk, FFN_TF).transpose(1, 0, 2)
    wd3 = wd.astype(BF16).reshape(nchunk, FFN_TF, d)
    tm = min(FFN_TM, rows)
    return pl.pallas_call(
        functools.partial(_ffn_kernel, nchunk=nchunk),
        grid=(rows // tm,),
        in_specs=[
            pl.BlockSpec((tm, d), lambda i: (i, 0)),
            _const_spec((nchunk, d, FFN_TF)),
            _const_spec((nchunk, d, FFN_TF)),
            _const_spec((nchunk, FFN_TF, d)),
            _const_spec((1, d)),
            _const_spec((1, d)),
        ],
        out_specs=pl.BlockSpec((tm, d), lambda i: (i, 0)),
        out_shape=jax.ShapeDtypeStruct((rows, d), F32),
        scratch_shapes=[pltpu.VMEM((tm, d), F32)],
        compiler_params=_params(("parallel",)),
        name="ffn_ln",
    )(x2d, wg3, wu3, wd3, g.reshape(1, d), b.reshape(1, d))


PROJ_TM = 512
_C_RWKV = 0
_C_Q = _C_RWKV + RWKV_IN
_C_KCVC = _C_Q + NSA_DIM
_C_K = _C_KCVC + 256
_C_V = _C_K + 256
_C_GATE = _C_V + 4 * 128
_C_END = _C_GATE + 128


def _inproj_kernel(x_ref, w_ref, zr_ref, q_ref, kcvc_ref, k_ref, v_ref, gate_ref):
    xb = x_ref[0].astype(BF16)
    zr_ref[0] = _dot(xb, w_ref[:, _C_RWKV:_C_Q])
    zq = _dot(xb, w_ref[:, _C_Q:_C_KCVC]) * (HEAD_DIM ** -0.5)
    for h in range(NSA_HEADS):
        q_ref[0, h] = zq[:, h * HEAD_DIM:(h + 1) * HEAD_DIM].astype(BF16)
    zc = _dot(xb, w_ref[:, _C_KCVC:_C_K])
    kcvc_ref[0, 0] = zc[:, 0:128]
    kcvc_ref[0, 1] = zc[:, 128:256]
    zk = _dot(xb, w_ref[:, _C_K:_C_V])
    for j in range(4):
        k_ref[0, j] = zk[:, j * HEAD_DIM:(j + 1) * HEAD_DIM].astype(BF16)
    one_lane = (lax.broadcasted_iota(jnp.int32, (1, V7X_LANES), 1) == HEAD_DIM).astype(F32)
    for j in range(4):
        zv = _dot(xb, w_ref[:, _C_V + j * 128:_C_V + (j + 1) * 128]) + one_lane
        for i in range(zv.shape[0] // Q_BLOCK):
            v_ref[0, j, i] = zv[i * Q_BLOCK:(i + 1) * Q_BLOCK].T[0:VT_ROWS].astype(BF16)
    gate_ref[0] = _dot(xb, w_ref[:, _C_GATE:_C_END])


def _pack_w_in(w_in):
    d = w_in.shape[0]
    o = RWKV_IN
    q = w_in[:, o:o + 512]
    kc = w_in[:, o + 512:o + 640]
    vc = w_in[:, o + 640:o + 768]
    ks = w_in[:, o + 768:o + 896]
    vs = w_in[:, o + 896:o + 1024]
    kw = w_in[:, o + 1024:o + 1152]
    vw = w_in[:, o + 1152:o + 1280]
    gate = w_in[:, o + 1280:o + 1304]
    z64 = jnp.zeros((d, 64), w_in.dtype)
    vpad = [jnp.concatenate([v[:, g * 64:(g + 1) * 64], z64], axis=1) for v in (vs, vw) for g in range(NSA_GROUPS)]
    gate_pad = jnp.concatenate([gate, jnp.zeros((d, 128 - gate.shape[1]), w_in.dtype)], axis=1)
    return jnp.concatenate([w_in[:, :o], q, kc, vc, ks, kw] + vpad + [gate_pad], axis=1).astype(BF16)


def _in_projection(x3d, w_in):
    bsz, s, d = x3d.shape
    tm = min(PROJ_TM, s)
    wp = _pack_w_in(w_in)
    out_shape = (
        jax.ShapeDtypeStruct((bsz, s, RWKV_IN), F32),
        jax.ShapeDtypeStruct((bsz, NSA_HEADS, s, HEAD_DIM), BF16),
        jax.ShapeDtypeStruct((bsz, 2, s, 128), F32),
        jax.ShapeDtypeStruct((bsz, 4, s, HEAD_DIM), BF16),
        jax.ShapeDtypeStruct((bsz, 4, s // Q_BLOCK, VT_ROWS, Q_BLOCK), BF16),
        jax.ShapeDtypeStruct((bsz, s, 128), F32),
    )
    return pl.pallas_call(
        _inproj_kernel,
        grid=(bsz, s // tm),
        in_specs=[pl.BlockSpec((1, tm, d), lambda b, i: (b, i, 0)), _const_spec((d, _C_END))],
        out_specs=(
            pl.BlockSpec((1, tm, RWKV_IN), lambda b, i: (b, i, 0)),
            pl.BlockSpec((1, NSA_HEADS, tm, HEAD_DIM), lambda b, i: (b, 0, i, 0)),
            pl.BlockSpec((1, 2, tm, 128), lambda b, i: (b, 0, i, 0)),
            pl.BlockSpec((1, 4, tm, HEAD_DIM), lambda b, i: (b, 0, i, 0)),
            pl.BlockSpec((1, 4, tm // Q_BLOCK, VT_ROWS, Q_BLOCK), lambda b, i: (b, 0, i, 0, 0)),
            pl.BlockSpec((1, tm, 128), lambda b, i: (b, i, 0)),
        ),
        out_shape=out_shape,
        compiler_params=_params(("parallel", "parallel")),
        name="in_projection",
    )(x3d, wp)


RWKV_T = 512


def _rwkv_consts():
    r = lax.broadcasted_iota(jnp.int32, (128, 128), 0)
    c = lax.broadcasted_iota(jnp.int32, (128, 128), 1)
    same = (r >= CHUNK) == (c >= CHUNK)
    mask_sl = (same & (r > c)).astype(F32)
    mask_l = (same & (r >= c)).astype(F32)
    eye = (r == c).astype(F32)
    head_ones = same.astype(F32)
    rt = lax.broadcasted_iota(jnp.int32, (CHUNK, CHUNK), 0)
    ct = lax.broadcasted_iota(jnp.int32, (CHUNK, CHUNK), 1)
    tri = (rt >= ct).astype(F32)
    lane = lax.broadcasted_iota(jnp.int32, (1, 128), 1)
    m0 = (lane < CHUNK).astype(F32)
    m1 = 1.0 - m0
    return mask_sl, mask_l, eye, head_ones, tri, m0, m1


def _split2(x):
    hi = x.astype(BF16)
    return hi, (x - hi.astype(F32)).astype(BF16)


def _mm(a, b, mode, dot=_dot):
    if mode == "bf16":
        return dot(a.astype(BF16), b.astype(BF16))
    if mode == "bf16x3":
        ah, al = _split2(a)
        bh, bl = _split2(b)
        return dot(ah, bh) + (dot(ah, bl) + dot(al, bh))
    if mode in ("exact_a", "exact_b"):
        x = b if mode == "exact_a" else a
        hi, rest = x.astype(BF16), None
        rest = x - hi.astype(F32)
        mid = rest.astype(BF16)
        lo = (rest - mid.astype(F32)).astype(BF16)
        if mode == "exact_a":
            ab = a.astype(BF16)
            return dot(ab, hi) + (dot(ab, mid) + dot(ab, lo))
        bb = b.astype(BF16)
        return dot(hi, bb) + (dot(mid, bb) + dot(lo, bb))
    raise ValueError(mode)


RWKV_MODES = dict(p="bf16", inv="bf16", av="bf16", wu="bf16", qy="bf16", mn="bf16", y="bf16", h="bf16")


def _rwkv_chunk(rs, lws, ks, vs, kks, as_, hs, consts):
    mask_sl, mask_l, eye, _, tri, m0, m1 = consts
    md = RWKV_MODES
    n = len(rs)
    each = range(n)

    def sm(x):
        return jnp.concatenate([x * m0, x * m1], axis=0)

    def dup(x):
        return jnp.concatenate([x, x], axis=0)

    cums = [_mm(tri, lws[i], "exact_a") for i in each]
    a_sm, r_sm, v_sm, kb, kbh, g_c = [], [], [], [], [], []
    for i in each:
        cum, lw, kk, k = cums[i], lws[i], kks[i], ks[i]
        cl = cum[CHUNK - 1:CHUNK, :]
        ka = kk * as_[i]
        g_tail = jnp.exp(cl - cum)
        g_inv = jnp.exp(-cum)
        a_sm.append(sm(-(kk * jnp.exp(cum - lw))))
        r_sm.append(sm(rs[i] * jnp.exp(cum)))
        v_sm.append(sm(vs[i]))
        kb.append(jnp.concatenate([dup(k * g_inv), dup(ka * g_inv)], axis=0))
        kbh.append(jnp.concatenate([sm(k * g_tail), sm(ka * g_tail)], axis=0))
        g_c.append(jnp.exp(cl))
    pm = [_mm(jnp.concatenate([a_sm[i], r_sm[i]], axis=0), kb[i], md["p"], _dot_nt) for i in each]
    a_ak = [pm[i][0:128, 0:128] * mask_sl for i in each]
    a_rr = [jnp.concatenate([pm[i][128:256, 0:128] * mask_l, pm[i][128:256, 128:256] * mask_l], axis=1) for i in each]
    x = [pm[i][0:128, 128:256] * mask_sl for i in each]
    t_inv = [eye + x[i] for i in each]
    for _ in range(5):
        x = [_mm(x[i], x[i], md["inv"]) for i in each]
        t_inv = [t_inv[i] + _mm(t_inv[i], x[i], md["inv"]) for i in each]
    av = [_mm(a_ak[i], v_sm[i], md["av"]) for i in each]
    wu = [_mm(t_inv[i], jnp.concatenate([a_sm[i], av[i]], axis=1), md["wu"]) for i in each]
    z = [jnp.concatenate([jnp.concatenate([jnp.zeros_like(v_sm[i]), v_sm[i]], axis=1), wu[i]], axis=0) for i in each]
    qy = [_mm(a_rr[i], z[i], md["qy"]) for i in each]
    mn = [_mm(kbh[i], z[i], md["mn"], _dot_tn) for i in each]
    y_sm = [_mm(r_sm[i] + qy[i][:, 0:128], hs[i], md["y"]) + qy[i][:, 128:256] for i in each]
    h_new = [_mm(mn[i][:, 0:128] + eye * g_c[i], hs[i], md["h"]) + mn[i][:, 128:256] for i in each]
    return [y_sm[i][0:CHUNK] + y_sm[i][CHUNK:2 * CHUNK] for i in each], h_new


def _rwkv_kernel(z_ref, mu_ref, w0_ref, wup_ref, a0_ref, aup_ref, gup_ref, kk_ref, ka_ref, rk_ref,
                 gng_ref, gnb_ref, o_ref, prev_ref, h_ref, r_s, lw_s, k_s, v_s, kk_s, a_s, y_s, g_s, *, t):
    ti = pl.program_id(1)

    @pl.when(ti == 0)
    def _():
        prev_ref[...] = jnp.zeros_like(prev_ref)
        h_ref[...] = jnp.zeros_like(h_ref)

    consts = _rwkv_consts()
    head_ones = consts[3]

    z = z_ref[0]
    row = lax.broadcasted_iota(jnp.int32, (t, 1), 0)
    prev = jnp.where(row == 0, prev_ref[0:1, :], pltpu.roll(z, 1, 0))
    prev_ref[0:1, :] = z[t - 1:t, :]
    zs = z + (prev - z) * mu_ref[...]

    r = zs[:, 0:512]
    k = zs[:, 512:1024]
    v = zs[:, 1024:1536]
    wa = zs[:, 1536:1664]
    gl = zs[:, 1664:1792]
    u = w0_ref[...] + _dot(jnp.tanh(wa).astype(BF16), wup_ref[...])
    lw = (-math.exp(-0.5)) * _sigmoid(u)
    a = _sigmoid(a0_ref[...] + _dot(wa.astype(BF16), aup_ref[...]))
    g_s[...] = _dot(_sigmoid(gl).astype(BF16), gup_ref[...])
    kkr = k * kk_ref[...]
    k2 = k * (1.0 + (a - 1.0) * ka_ref[...])
    r_s[...] = r
    lw_s[...] = lw
    k_s[...] = k2
    v_s[...] = v
    a_s[...] = a
    for pr in range(4):
        sl = slice(pr * 128, (pr + 1) * 128)
        kp = kkr[:, sl]
        ss = _mm(kp * kp, head_ones, "exact_b")
        kk_s[:, sl] = kp * lax.rsqrt(jnp.maximum(ss, 1e-24))

    pairs = [slice(pr * 128, (pr + 1) * 128) for pr in range(4)]

    def chunk_body(c, carry):
        rows = pl.ds(pl.multiple_of(c * CHUNK, CHUNK), CHUNK)
        ys, hs = _rwkv_chunk(*[[ref[rows, sl] for sl in pairs] for ref in (r_s, lw_s, k_s, v_s, kk_s, a_s)],
                             [h_ref[pr] for pr in range(4)], consts)
        for pr, sl in enumerate(pairs):
            y_s[rows, sl] = ys[pr]
            h_ref[pr] = hs[pr]
        return carry

    lax.fori_loop(0, t // CHUNK, chunk_body, 0)

    for pr in range(4):
        sl = slice(pr * 128, (pr + 1) * 128)
        y = y_s[:, sl]
        mean = _mm(y, head_ones, "exact_b") * (1.0 / HEAD_DIM)
        yc = y - mean
        var = _mm(yc * yc, head_ones, "exact_b") * (1.0 / HEAD_DIM)
        yn = yc * lax.rsqrt(var + RWKV_GN_EPS) * gng_ref[:, sl] + gnb_ref[:, sl]
        rp = r_s[:, sl]
        bonus = _mm(rp * k_s[:, sl] * rk_ref[:, sl], head_ones, "exact_b") * v_s[:, sl]
        o_ref[0, :, sl] = ((yn + bonus) * g_s[:, sl]).astype(o_ref.dtype)


def _rwkv_group(zr, mu, w0, w_up, a0, a_up, g_up, k_k, k_a, r_k, gn_g, gn_b):
    bsz, s, _ = zr.shape
    t = min(RWKV_T, s)
    wup_pad = jnp.concatenate([w_up, jnp.zeros_like(a_up)], axis=0).astype(BF16)
    aup_pad = jnp.concatenate([jnp.zeros_like(w_up), a_up], axis=0).astype(BF16)
    row = lambda p: p.reshape(1, -1)
    tile = pltpu.VMEM((t, RWKV_DIM), F32)
    return pl.pallas_call(
        functools.partial(_rwkv_kernel, t=t),
        grid=(bsz, s // t),
        in_specs=[
            pl.BlockSpec((1, t, RWKV_IN), lambda b, i: (b, i, 0)),
            _const_spec((1, RWKV_IN)), _const_spec((1, RWKV_DIM)), _const_spec((128, RWKV_DIM)),
            _const_spec((1, RWKV_DIM)), _const_spec((128, RWKV_DIM)), _const_spec((GATE_LORA, RWKV_DIM)),
            _const_spec((1, RWKV_DIM)), _const_spec((1, RWKV_DIM)), _const_spec((1, RWKV_DIM)),
            _const_spec((1, RWKV_DIM)), _const_spec((1, RWKV_DIM)),
        ],
        out_specs=pl.BlockSpec((1, t, RWKV_DIM), lambda b, i: (b, i, 0)),
        out_shape=jax.ShapeDtypeStruct((bsz, s, RWKV_DIM), BF16),
        scratch_shapes=[pltpu.VMEM((V7X_SUBLANES, RWKV_IN), F32), pltpu.VMEM((4, 128, 128), F32),
                        tile, tile, tile, tile, tile, tile, tile, tile],
        compiler_params=_params(("parallel", "arbitrary")),
        name="rwkv7",
    )(zr, row(mu), row(w0), wup_pad, row(a0), aup_pad, g_up.astype(BF16), row(k_k), row(k_a), row(r_k),
      row(gn_g), row(gn_b))


def _gelu_tanh(x):
    return 0.5 * x * (1.0 + jnp.tanh(math.sqrt(2.0 / math.pi) * (x + 0.044715 * (x * x * x))))


def _compress_kernel(x_ref, pe_ref, w1_ref, w2_ref, kc_ref, vc_ref, *, n_rows):
    half = CMP_LEN // 2
    for kv, o_ref in enumerate((kc_ref, vc_ref)):
        first = jnp.zeros((n_rows, 2 * CMP_HIDDEN), F32)
        second = jnp.zeros((n_rows, 2 * CMP_HIDDEN), F32)
        for p in range(half):
            xa = x_ref[0, kv, pl.ds(p, n_rows, stride=CMP_STRIDE), :]
            first += _dot((xa + pe_ref[kv, p:p + 1, :]).astype(BF16), w1_ref[kv, p])
            second += _dot((xa + pe_ref[kv, half + p:half + p + 1, :]).astype(BF16), w1_ref[kv, half + p])
        pre = first + pltpu.roll(second, n_rows - 1, 0)
        out = _dot(_gelu_tanh(pre).astype(BF16), w2_ref[kv])
        for g in range(NSA_GROUPS):
            o_ref[0, g, 0:CMP_PAD, :] = jnp.zeros((CMP_PAD, HEAD_DIM), o_ref.dtype)
            o_ref[0, g, CMP_PAD:CMP_PAD + n_rows, :] = out[:, g * HEAD_DIM:(g + 1) * HEAD_DIM].astype(o_ref.dtype)
            tail = o_ref.shape[2] - CMP_PAD - n_rows
            o_ref[0, g, CMP_PAD + n_rows:, :] = jnp.zeros((tail, HEAD_DIM), o_ref.dtype)


def _pair_diag(w):
    z = jnp.zeros_like(w)
    return jnp.concatenate([jnp.concatenate([w, z], axis=-1), jnp.concatenate([z, w], axis=-1)], axis=-2)


def _compress(kcvc, pe_k, w1_k, w2_k, pe_v, w1_v, w2_v):
    bsz, _, s, _ = kcvc.shape
    n_rows = s // CMP_STRIDE
    rows_out = CMP_PAD + n_rows + V7X_SUBLANES
    pe = jnp.stack([jnp.concatenate([pe_k, pe_k], axis=1), jnp.concatenate([pe_v, pe_v], axis=1)])
    w1 = jnp.stack([_pair_diag(w1_k.reshape(CMP_LEN, HEAD_DIM, CMP_HIDDEN)),
                    _pair_diag(w1_v.reshape(CMP_LEN, HEAD_DIM, CMP_HIDDEN))]).astype(BF16)
    w2 = jnp.stack([_pair_diag(w2_k), _pair_diag(w2_v)]).astype(BF16)
    shape = jax.ShapeDtypeStruct((bsz, NSA_GROUPS, rows_out, HEAD_DIM), BF16)
    spec = pl.BlockSpec((1, NSA_GROUPS, rows_out, HEAD_DIM), lambda b: (b, 0, 0, 0))
    return pl.pallas_call(
        functools.partial(_compress_kernel, n_rows=n_rows),
        grid=(bsz,),
        in_specs=[pl.BlockSpec((1, 2, s, 128), lambda b: (b, 0, 0, 0)), _const_spec(pe.shape), _const_spec(w1.shape),
                  _const_spec(w2.shape)],
        out_specs=(spec, spec),
        out_shape=(shape, shape),
        compiler_params=_params(("parallel",)),
        name="nsa_compress",
    )(kcvc, pe, w1, w2)


def _cmp_select_kernel(rb_ref, q_ref, kc_ref, vc_ref, tz_ref, gate_ref, oc_ref, sel_ref, *, n_far):
    qi = pl.program_id(1)
    qb = Q_BLOCK
    rowi = lax.broadcasted_iota(jnp.int32, (qb, 128), 0)
    lane = lax.broadcasted_iota(jnp.int32, (qb, 128), 1)
    row_f = rowi.astype(F32)
    sg = _sigmoid(gate_ref[0])
    near0 = pl.multiple_of(qi * 8, 8)
    n_first = qi * 8 - CMP_PAD
    for g in range(NSA_GROUPS):
        q4 = q_ref[0, g * NSA_HPG:(g + 1) * NSA_HPG].reshape(NSA_HPG * qb, HEAD_DIM)
        heads = [g * NSA_HPG + p for p in range(NSA_HPG)]
        cols = [slice(p * qb, (p + 1) * qb) for p in range(NSA_HPG)]
        tiles, values, n_of_lane = [], [], []
        for j in range(n_far):
            rows = slice(CMP_PAD + 128 * j, CMP_PAD + 128 * (j + 1))
            s = _dot_nt(kc_ref[0, g, rows, :], q4)
            ok = (128 * j + rowi) < n_first
            tiles.append(jnp.concatenate(
                [jnp.where(ok, s[:, cols[p]] + rb_ref[NUM_BUCKETS - 1, heads[p]], NEG) for p in range(NSA_HPG)], axis=1))
            values.append(vc_ref[0, g, rows, :])
            n_of_lane.append(128 * j + lane)
        s = _dot_nt(kc_ref[0, g, pl.ds(near0, 128), :], q4)
        ok = (n_first + rowi) >= 0
        tiles.append(jnp.concatenate(
            [jnp.where(ok, s[:, cols[p]] + tz_ref[heads[p], TZ_CMP_NEAR], NEG) for p in range(NSA_HPG)], axis=1))
        values.append(vc_ref[0, g, pl.ds(near0, 128), :])
        n_of_lane.append(n_first + lane)

        m = tiles[0].max(axis=0, keepdims=True)
        for tl in tiles[1:]:
            m = jnp.maximum(m, tl.max(axis=0, keepdims=True))
        es = [jnp.where(tl > 0.5 * NEG, jnp.exp(tl - m), 0.0) for tl in tiles]
        den = es[0].sum(axis=0, keepdims=True)
        for e in es[1:]:
            den = den + e.sum(axis=0, keepdims=True)
        inv = 1.0 / jnp.maximum(den, 1e-30)
        o_t = jnp.zeros((HEAD_DIM, NSA_HPG * qb), F32)
        imp = jnp.zeros((qb, 128), F32)
        for j, e in enumerate(es):
            pc = e * inv
            o_t = o_t + _dot_tn(values[j], pc.astype(BF16))
            psum = pc[:, cols[0]] + pc[:, cols[1]] + pc[:, cols[2]] + pc[:, cols[3]]
            ov = ((n_of_lane[j] >= 4 * rowi - 1) & (n_of_lane[j] <= 4 * rowi + 3)).astype(F32)
            imp = imp + _mm(ov, psum, "exact_a")
        for p, h in enumerate(heads):
            oc_ref[0, :, h * HEAD_DIM:(h + 1) * HEAD_DIM] = o_t[:, cols[p]].T * sg[:, 3 * h:3 * h + 1]
        cur = 2 * qi + (lane >= SLC_LEN).astype(jnp.int32)
        forced = (rowi == 0) | (rowi == cur) | (rowi == cur - 1)
        score = jnp.where(rowi <= cur, jnp.where(forced, FORCED_SCORE, imp), -1.0)
        sel = jnp.zeros((qb, 128), F32)
        for _ in range(N_SEL):
            mx = score.max(axis=0, keepdims=True)
            idx = jnp.where(score == mx, row_f, 128.0).min(axis=0, keepdims=True)
            pick = row_f == idx
            sel = jnp.where(pick, 1.0, sel)
            score = jnp.where(pick, -3e38, score)
        sel_ref[0, g] = sel


def _cmp_select(rel_bias, q, kc, vc, tz, gate):
    bsz, _, s, _ = q.shape
    nq = s // Q_BLOCK
    n_far = max(0, -(-(8 * (nq - 1) - CMP_PAD) // 128))
    rows_c = kc.shape[2]
    return pl.pallas_call(
        functools.partial(_cmp_select_kernel, n_far=n_far),
        grid=(bsz, nq),
        in_specs=[
            pl.BlockSpec(memory_space=pltpu.SMEM),
            pl.BlockSpec((1, NSA_HEADS, Q_BLOCK, HEAD_DIM), lambda b, i: (b, 0, i, 0)),
            pl.BlockSpec((1, NSA_GROUPS, rows_c, HEAD_DIM), lambda b, i: (b, 0, 0, 0)),
            pl.BlockSpec((1, NSA_GROUPS, rows_c, HEAD_DIM), lambda b, i: (b, 0, 0, 0)),
            _const_spec(tz.shape),
            pl.BlockSpec((1, Q_BLOCK, 128), lambda b, i: (b, i, 0)),
        ],
        out_specs=(
            pl.BlockSpec((1, Q_BLOCK, NSA_DIM), lambda b, i: (b, i, 0)),
            pl.BlockSpec((1, NSA_GROUPS, 128, Q_BLOCK), lambda b, i: (b, 0, 0, i)),
        ),
        out_shape=(jax.ShapeDtypeStruct((bsz, s, NSA_DIM), F32),
                   jax.ShapeDtypeStruct((bsz, NSA_GROUPS, 128, s), F32)),
        compiler_params=_params(("parallel", "parallel")),
        name="nsa_cmp_select",
    )(rel_bias, q, kc, vc, tz, gate)


SEL_TILES = 4


def _attn_scores(q4, ks, bias_fns, masks, m_prev):
    qb = Q_BLOCK
    s = _dot_nt(ks, q4)
    rows = []
    for i in range(len(bias_fns)):
        parts = []
        for p in range(NSA_HPG):
            sp = s[i * qb:(i + 1) * qb, p * qb:(p + 1) * qb] + bias_fns[i](p)
            if masks[i] is not None:
                sp = jnp.where(masks[i], sp, NEG)
            parts.append(sp)
        rows.append(jnp.concatenate(parts, axis=1))
    s = jnp.concatenate(rows, axis=0)
    return s, jnp.maximum(m_prev, s.max(axis=0, keepdims=True))


def _attn_accumulate(s, vts, m_prev, m_next, m_ref, acc_ref):
    pexp = jnp.exp(s - m_next)
    acc_ref[...] = jnp.exp(m_prev - m_next) * acc_ref[...] + _dot(jnp.concatenate(vts, axis=1), pexp.astype(BF16))
    m_ref[...] = m_next


def _sel_win_kernel(q_ref, k_ref, vt_ref, sel_ref, tz_ref, gate_ref, oc_ref, o_ref, m_ref, acc_ref):
    qi = pl.program_id(1)
    qb = Q_BLOCK
    rowi = lax.broadcasted_iota(jnp.int32, (qb, 128), 0)
    lane = lax.broadcasted_iota(jnp.int32, (qb, 128), 1)
    win_tiles = WINDOW // qb
    m_ref[...] = jnp.full(m_ref.shape, NEG, F32)
    acc_ref[...] = jnp.zeros(acc_ref.shape, F32)
    q4 = [q_ref[0, g * NSA_HPG:(g + 1) * NSA_HPG].reshape(NSA_HPG * qb, HEAD_DIM) for g in range(NSA_GROUPS)]

    def key_rows(kt, n=1):
        return pl.ds(pl.multiple_of(kt * qb, qb), n * qb)

    def bias_of(g, kind):
        return lambda p: tz_ref[g * NSA_HPG + p, kind]

    def sel_mask(g, kt):
        lo = sel_ref[0, g, pl.ds(2 * kt, 1), :]
        hi = sel_ref[0, g, pl.ds(2 * kt + 1, 1), :]
        return jnp.where(rowi < SLC_LEN, lo, hi) > 0.5

    def sel_body(j, carry):
        kt0 = j * SEL_TILES
        scored = []
        for g in range(NSA_GROUPS):
            vts, biases, masks = [], [], []
            for i in range(SEL_TILES):
                kt = kt0 + i
                kind = jnp.where(kt == qi, TZ_DIAG,
                                 jnp.where(kt == qi - 1, TZ_SUB, jnp.where(kt < qi, TZ_FAR, TZ_MASKED)))
                vts.append(vt_ref[0, g, kt])
                biases.append(bias_of(g, kind))
                masks.append(sel_mask(g, kt))
            m_prev = m_ref[g]
            s, m_next = _attn_scores(q4[g], k_ref[0, g, key_rows(kt0, SEL_TILES), :], biases, masks, m_prev)
            scored.append((s, vts, m_prev, m_next))
        for g in range(NSA_GROUPS):
            _attn_accumulate(*scored[g], m_ref.at[g], acc_ref.at[g])
        return carry

    lax.fori_loop(0, (qi + SEL_TILES) // SEL_TILES, sel_body, 0)

    scored = []
    for g in range(NSA_GROUPS):
        kw = NSA_GROUPS + g
        ks, vts, biases, masks = [], [], [], []
        for d in range(win_tiles, -1, -1):
            kt = jnp.maximum(qi - d, 0)
            kind = TZ_DIAG if d == 0 else jnp.where(qi >= d, TZ_SUB if d == 1 else TZ_FAR, TZ_MASKED)
            ks.append(k_ref[0, kw, key_rows(kt), :])
            vts.append(vt_ref[0, kw, kt])
            biases.append(bias_of(g, kind))
            masks.append(rowi > lane if d == win_tiles else None)
        m_prev = m_ref[kw]
        s, m_next = _attn_scores(q4[g], jnp.concatenate(ks, axis=0), biases, masks, m_prev)
        scored.append((s, vts, m_prev, m_next))
    for g in range(NSA_GROUPS):
        _attn_accumulate(*scored[g], m_ref.at[NSA_GROUPS + g], acc_ref.at[NSA_GROUPS + g])

    sg = _sigmoid(gate_ref[0])
    oc = oc_ref[0]
    for g in range(NSA_GROUPS):
        outs = []
        for stream in (g, NSA_GROUPS + g):
            acc = acc_ref[stream]
            outs.append(acc[0:HEAD_DIM] * (1.0 / acc[HEAD_DIM:HEAD_DIM + 1]))
        for p in range(NSA_HPG):
            h = g * NSA_HPG + p
            qcols = slice(p * qb, (p + 1) * qb)
            cols = slice(h * HEAD_DIM, (h + 1) * HEAD_DIM)
            y = (oc[:, cols] + sg[:, 3 * h + 1:3 * h + 2] * outs[0][:, qcols].T
                 + sg[:, 3 * h + 2:3 * h + 3] * outs[1][:, qcols].T)
            o_ref[0, :, cols] = y.astype(o_ref.dtype)


def _sel_win(q, k4, v4, sel, tz, gate, oc):
    bsz, _, s, _ = q.shape
    nq = s // Q_BLOCK
    assert nq % SEL_TILES == 0
    return pl.pallas_call(
        _sel_win_kernel,
        grid=(bsz, nq),
        in_specs=[
            pl.BlockSpec((1, NSA_HEADS, Q_BLOCK, HEAD_DIM), lambda b, i: (b, 0, i, 0)),
            pl.BlockSpec((1, 4, s, HEAD_DIM), lambda b, i: (b, 0, 0, 0)),
            pl.BlockSpec((1, 4, nq, VT_ROWS, Q_BLOCK), lambda b, i: (b, 0, 0, 0, 0)),
            pl.BlockSpec((1, NSA_GROUPS, 128, Q_BLOCK), lambda b, i: (b, 0, 0, i)),
            _const_spec(tz.shape),
            pl.BlockSpec((1, Q_BLOCK, 128), lambda b, i: (b, i, 0)),
            pl.BlockSpec((1, Q_BLOCK, NSA_DIM), lambda b, i: (b, i, 0)),
        ],
        out_specs=pl.BlockSpec((1, Q_BLOCK, NSA_DIM), lambda b, i: (b, i, 0)),
        out_shape=jax.ShapeDtypeStruct((bsz, s, NSA_DIM), BF16),
        scratch_shapes=[pltpu.VMEM((2 * NSA_GROUPS, 1, NSA_HPG * Q_BLOCK), F32),
                        pltpu.VMEM((2 * NSA_GROUPS, VT_ROWS, NSA_HPG * Q_BLOCK), F32)],
        compiler_params=_params(("parallel", "parallel")),
        name="nsa_sel_win",
    )(q, k4, v4, sel, tz, gate, oc)


OUT_TM = 512


def _outproj_kernel(x_ref, yr_ref, yn_ref, w_ref, g_ref, b_ref, o_ref):
    mixed = _dot(yr_ref[...], w_ref[0:RWKV_DIM, :]) + _dot(yn_ref[...], w_ref[RWKV_DIM:, :])
    o_ref[...] = _layer_norm(ALPHA * x_ref[...] + mixed, g_ref[...], b_ref[...])


def _out_projection(x2d, yr, yn, w_out, g, b):
    rows, d = x2d.shape
    tm = min(OUT_TM, rows)
    return pl.pallas_call(
        _outproj_kernel,
        grid=(rows // tm,),
        in_specs=[pl.BlockSpec((tm, d), lambda i: (i, 0)), pl.BlockSpec((tm, RWKV_DIM), lambda i: (i, 0)),
                  pl.BlockSpec((tm, NSA_DIM), lambda i: (i, 0)), _const_spec(w_out.shape),
                  _const_spec((1, d)), _const_spec((1, d))],
        out_specs=pl.BlockSpec((tm, d), lambda i: (i, 0)),
        out_shape=jax.ShapeDtypeStruct((rows, d), F32),
        compiler_params=_params(("parallel",)),
        name="out_projection_ln",
    )(x2d, yr, yn, w_out.astype(BF16), g.reshape(1, d), b.reshape(1, d))


XATTN_TM = 512


def _mem_kv_kernel(mem_ref, wk_ref, wv_ref, k_ref, v_ref):
    mb = mem_ref[0].astype(BF16)
    k_ref[0] = _dot(mb, wk_ref[...]).astype(BF16)
    v_ref[0] = _dot(mb, wv_ref[...]).astype(BF16)


def _mem_kv(mem, wk, wv):
    bsz, m, d = mem.shape
    shape = jax.ShapeDtypeStruct((bsz, m, d), BF16)
    spec = pl.BlockSpec((1, m, d), lambda b: (b, 0, 0))
    return pl.pallas_call(
        _mem_kv_kernel,
        grid=(bsz,),
        in_specs=[spec, _const_spec((d, d)), _const_spec((d, d))],
        out_specs=(spec, spec),
        out_shape=(shape, shape),
        compiler_params=_params(("parallel",)),
        name="xattn_mem_kv",
    )(mem, wk.astype(BF16), wv.astype(BF16))


def _xattn_kernel(x_ref, k_ref, v_ref, wq_ref, wo_ref, g_ref, b_ref, o_ref):
    x = x_ref[0]
    q = (_dot(x.astype(BF16), wq_ref[...]) * (XATTN_HEAD_DIM ** -0.5)).astype(BF16)
    heads = []
    for h in range(XATTN_HEADS):
        cols = slice(h * XATTN_HEAD_DIM, (h + 1) * XATTN_HEAD_DIM)
        s = _dot_nt(q[:, cols], k_ref[0, :, cols])
        e = jnp.exp(s - s.max(axis=1, keepdims=True))
        p = e * (1.0 / e.sum(axis=1, keepdims=True))
        heads.append(_dot(p.astype(BF16), v_ref[0, :, cols]).astype(BF16))
    o = _dot(jnp.concatenate(heads, axis=1), wo_ref[...])
    o_ref[0] = _layer_norm(ALPHA * x + o, g_ref[...], b_ref[...])


def _cross_attention_ln(x3d, k, v, wq, wo, g, b):
    bsz, s, d = x3d.shape
    m = k.shape[1]
    tm = min(XATTN_TM, s)
    return pl.pallas_call(
        _xattn_kernel,
        grid=(bsz, s // tm),
        in_specs=[pl.BlockSpec((1, tm, d), lambda bb, i: (bb, i, 0)),
                  pl.BlockSpec((1, m, d), lambda bb, i: (bb, 0, 0)),
                  pl.BlockSpec((1, m, d), lambda bb, i: (bb, 0, 0)),
                  _const_spec((d, d)), _const_spec((d, d)), _const_spec((1, d)), _const_spec((1, d))],
        out_specs=pl.BlockSpec((1, tm, d), lambda bb, i: (bb, i, 0)),
        out_shape=jax.ShapeDtypeStruct((bsz, s, d), F32),
        compiler_params=_params(("parallel", "parallel")),
        name="xattn_ln",
    )(x3d, k, v, wq.astype(BF16), wo.astype(BF16), g.reshape(1, d), b.reshape(1, d))


def _nsa_group(rel_bias, q, kcvc, k4, v4, gate, pe_k, w1_k, w2_k, pe_v, w1_v, w2_v):
    tz = _bias_tiles(rel_bias)
    kc, vc = _compress(kcvc, pe_k, w1_k, w2_k, pe_v, w1_v, w2_v)
    oc, sel = _cmp_select(rel_bias, q, kc, vc, tz, gate)
    return _sel_win(q, k4, v4, sel, tz, gate, oc)


def kernel(x, mem, ffn1_w_gate, ffn1_w_up, ffn1_w_down, ln1_g, ln1_b, mix_w_in, rwkv_mu, rwkv_w0, rwkv_w_up, rwkv_a0, rwkv_a_up, rwkv_g_up, rwkv_k_k, rwkv_k_a, rwkv_r_k, rwkv_gn_g, rwkv_gn_b, nsa_pe_k, nsa_w1_k, nsa_w2_k, nsa_pe_v, nsa_w1_v, nsa_w2_v, mix_w_out, ln2_g, ln2_b, xattn_wq, xattn_wk, xattn_wv, xattn_wo, ln3_g, ln3_b, ffn2_w_gate, ffn2_w_up, ffn2_w_down, ln4_g, ln4_b, rel_bias):
    bsz, s, d = x.shape
    rows = bsz * s
    for l in range(DEPTH):
        x1 = _ffn_ln(x.reshape(rows, d), ffn1_w_gate[l], ffn1_w_up[l], ffn1_w_down[l], ln1_g[l], ln1_b[l])
        zr, q, kcvc, k4, v4, gate = _in_projection(x1.reshape(bsz, s, d), mix_w_in[l])
        y_rwkv = _rwkv_group(zr, rwkv_mu[l], rwkv_w0[l], rwkv_w_up[l], rwkv_a0[l], rwkv_a_up[l], rwkv_g_up[l],
                             rwkv_k_k[l], rwkv_k_a[l], rwkv_r_k[l], rwkv_gn_g[l], rwkv_gn_b[l])
        y_nsa = _nsa_group(rel_bias, q, kcvc, k4, v4, gate, nsa_pe_k[l], nsa_w1_k[l], nsa_w2_k[l],
                           nsa_pe_v[l], nsa_w1_v[l], nsa_w2_v[l])
        x2 = _out_projection(x1, y_rwkv.reshape(rows, RWKV_DIM), y_nsa.reshape(rows, NSA_DIM), mix_w_out[l],
                             ln2_g[l], ln2_b[l])
        mk, mv = _mem_kv(mem, xattn_wk[l], xattn_wv[l])
        x3 = _cross_attention_ln(x2.reshape(bsz, s, d), mk, mv, xattn_wq[l], xattn_wo[l], ln3_g[l], ln3_b[l])
        x = _ffn_ln(x3.reshape(rows, d), ffn2_w_gate[l], ffn2_w_up[l], ffn2_w_down[l], ln4_g[l], ln4_b[l])
        x = x.reshape(bsz, s, d)
    return x
```

```python
import functools
import math

import numpy as np
import jax
import jax.numpy as jnp
from jax import lax
from jax.experimental import pallas as pl
from jax.experimental.pallas import tpu as pltpu

F32 = jnp.float32
BF16 = jnp.bfloat16
HIGHEST = lax.Precision.HIGHEST

D_MODEL = 1024
DEPTH = 1
RWKV_HEADS = 8
HEAD_DIM = 64
RWKV_DIM = RWKV_HEADS * HEAD_DIM
DECAY_LORA = 64
AAA_LORA = 64
GATE_LORA = 128
RWKV_IN = 3 * RWKV_DIM + DECAY_LORA + AAA_LORA + GATE_LORA
RWKV_GN_EPS = 64e-5
NSA_HEADS = 8
NSA_GROUPS = 2
NSA_HPG = NSA_HEADS // NSA_GROUPS
NSA_DIM = NSA_HEADS * HEAD_DIM
CMP_LEN = 32
CMP_STRIDE = 16
CMP_HIDDEN = 128
SLC_LEN = 64
N_SEL = 16
WINDOW = 512
Q_BLOCK = 128
FORCED_SCORE = 1e4
NUM_BUCKETS = 32
MAX_DISTANCE = 128
XATTN_HEADS = 4
XATTN_HEAD_DIM = D_MODEL // XATTN_HEADS
D_FF = 2816
LN_EPS = 1e-5
ALPHA = (2.0 * DEPTH) ** 0.25
NEG = -1e30

V7X_LANES = 128
V7X_SUBLANES = 8
V7X_VMEM_LIMIT_BYTES = 56 * 1024 * 1024

CHUNK = 64
CMP_PAD = 120
VT_ROWS = 80


def _dot(a, b, prec=None):
    return jnp.dot(a, b, preferred_element_type=F32, precision=prec)


def _dot_nt(a, b, prec=None):
    return lax.dot_general(a, b, (((1,), (1,)), ((), ())), preferred_element_type=F32, precision=prec)


def _dot_tn(a, b, prec=None):
    return lax.dot_general(a, b, (((0,), (0,)), ((), ())), preferred_element_type=F32, precision=prec)


def _sigmoid(x):
    return 1.0 / (1.0 + jnp.exp(-x))


def _layer_norm(y, g, b):
    mu = jnp.mean(y, axis=-1, keepdims=True)
    yc = y - mu
    var = jnp.mean(yc * yc, axis=-1, keepdims=True)
    return yc * lax.rsqrt(var + LN_EPS) * g + b


def _params(sem):
    return pltpu.CompilerParams(dimension_semantics=sem, vmem_limit_bytes=V7X_VMEM_LIMIT_BYTES)


def _const_spec(shape, single_buffer=False):
    nd = len(shape)
    if single_buffer:
        return pl.BlockSpec(shape, lambda *_: (0,) * nd, pipeline_mode=pl.Buffered(1))
    return pl.BlockSpec(shape, lambda *_: (0,) * nd)


def _bucket_thresholds():
    n = np.arange(0, 4 * MAX_DISTANCE)
    max_exact = NUM_BUCKETS // 2
    nf = np.maximum(n, max_exact).astype(np.float32)
    large = max_exact + (np.log(nf / np.float32(max_exact)) / np.float32(math.log(MAX_DISTANCE / max_exact))
                         * np.float32(NUM_BUCKETS - max_exact)).astype(np.int32)
    large = np.minimum(large, NUM_BUCKETS - 1)
    bucket = np.where(n < max_exact, n, large)
    return [int(np.argmax(bucket >= b)) for b in range(1, NUM_BUCKETS)]


_BUCKET_THR = _bucket_thresholds()


TZ_CMP_NEAR, TZ_DIAG, TZ_SUB, TZ_ZERO, TZ_MASKED, TZ_WIN_OLD, TZ_KINDS = 0, 1, 2, 3, 4, 5, 6


def _bias_tiles_kernel(rb_ref, o_ref):
    h = pl.program_id(0)
    r = lax.broadcasted_iota(jnp.int32, (Q_BLOCK, Q_BLOCK), 0)
    c = lax.broadcasted_iota(jnp.int32, (Q_BLOCK, Q_BLOCK), 1)
    far = jnp.full((Q_BLOCK, Q_BLOCK), rb_ref[NUM_BUCKETS - 1, h], F32)
    far_hi = far.astype(BF16).astype(F32)
    far_added = far_hi + (far - far_hi).astype(BF16).astype(F32)
    dists = (c - CMP_STRIDE * (r - CMP_PAD) - (CMP_LEN - 1), c - r, Q_BLOCK + c - r)
    for kind, dist in enumerate(dists):
        val = jnp.full((Q_BLOCK, Q_BLOCK), rb_ref[0, h], F32)
        for b in range(1, NUM_BUCKETS):
            val = jnp.where(dist >= _BUCKET_THR[b - 1], rb_ref[b, h], val)
        if kind != TZ_CMP_NEAR:
            val = val - far_added
        o_ref[0, kind] = jnp.where(dist >= 0, val, NEG)
    o_ref[0, TZ_ZERO] = far - far_added
    o_ref[0, TZ_MASKED] = jnp.full((Q_BLOCK, Q_BLOCK), NEG, F32)
    o_ref[0, TZ_WIN_OLD] = jnp.where(r > c, far - far_added, NEG)


def _bias_tiles(rel_bias):
    return pl.pallas_call(
        _bias_tiles_kernel,
        grid=(NSA_HEADS,),
        in_specs=[pl.BlockSpec(memory_space=pltpu.SMEM)],
        out_specs=pl.BlockSpec((1, TZ_KINDS, Q_BLOCK, Q_BLOCK), lambda h: (h, 0, 0, 0)),
        out_shape=jax.ShapeDtypeStruct((NSA_HEADS, TZ_KINDS, Q_BLOCK, Q_BLOCK), F32),
        compiler_params=_params(("arbitrary",)),
        name="bias_tiles",
    )(rel_bias)


FFN_TM = 1024
FFN_TF = 256


def _ffn_kernel(x_ref, wg_ref, wu_ref, wd_ref, g_ref, b_ref, o_ref, acc_ref, *, nchunk):
    x = x_ref[...]
    xb = x.astype(BF16)
    acc_ref[...] = jnp.zeros_like(acc_ref)

    def body(c, carry):
        hg = _dot(xb, wg_ref[c])
        hu = _dot(xb, wu_ref[c])
        h = hg * _sigmoid(hg) * hu
        acc_ref[...] += _dot(h.astype(BF16), wd_ref[c])
        return carry

    lax.fori_loop(0, nchunk, body, 0)
    o_ref[...] = _layer_norm(ALPHA * x + 0.5 * acc_ref[...], g_ref[...], b_ref[...])


def _ffn_ln(x2d, wg, wu, wd, g, b):
    rows, d = x2d.shape
    f = wg.shape[1]
    nchunk = f // FFN_TF
    wg3 = wg.astype(BF16).reshape(d, nchunk, FFN_TF).transpose(1, 0, 2)
    wu3 = wu.astype(BF16).reshape(d, nchunk, FFN_TF).transpose(1, 0, 2)
    wd3 = wd.astype(BF16).reshape(nchunk, FFN_TF, d)
    tm = min(FFN_TM, rows)
    return pl.pallas_call(
        functools.partial(_ffn_kernel, nchunk=nchunk),
        grid=(rows // tm,),
        in_specs=[
            pl.BlockSpec((tm, d), lambda i: (i, 0)),
            _const_spec((nchunk, d, FFN_TF), single_buffer=True),
            _const_spec((nchunk, d, FFN_TF), single_buffer=True),
            _const_spec((nchunk, FFN_TF, d), single_buffer=True),
            _const_spec((1, d)),
            _const_spec((1, d)),
        ],
        out_specs=pl.BlockSpec((tm, d), lambda i: (i, 0)),
        out_shape=jax.ShapeDtypeStruct((rows, d), F32),
        scratch_shapes=[pltpu.VMEM((tm, d), F32)],
        compiler_params=_params(("parallel",)),
        name="ffn_ln",
    )(x2d, wg3, wu3, wd3, g.reshape(1, d), b.reshape(1, d))


PROJ_TM = 512
_C_RWKV = 0
_C_Q = _C_RWKV + RWKV_IN
_C_KCVC = _C_Q + NSA_HEADS * 128
_C_K = _C_KCVC + 256
_C_V = _C_K + 4 * 128
_C_GATE = _C_V + 4 * 128
_C_END = _C_GATE + 128
BIAS_LANES = (HEAD_DIM, HEAD_DIM + 1)


def _inproj_kernel(x_ref, w_ref, qx_ref, zr_ref, q_ref, kcvc_ref, k_ref, v_ref, gate_ref):
    xb = x_ref[0].astype(BF16)
    lane = lax.broadcasted_iota(jnp.int32, (1, V7X_LANES), 1)
    zr_ref[0] = _dot(xb, w_ref[:, _C_RWKV:_C_Q])
    for h in range(NSA_HEADS):
        zq = _dot(xb, w_ref[:, _C_Q + h * 128:_C_Q + (h + 1) * 128]) * (HEAD_DIM ** -0.5)
        q_ref[0, h] = (zq + qx_ref[h:h + 1, :]).astype(BF16)
    zc = _dot(xb, w_ref[:, _C_KCVC:_C_K])
    kcvc_ref[0, 0] = zc[:, 0:128]
    kcvc_ref[0, 1] = zc[:, 128:256]
    bias_ones = ((lane == BIAS_LANES[0]) | (lane == BIAS_LANES[1])).astype(F32)
    for j in range(4):
        zk = _dot(xb, w_ref[:, _C_K + j * 128:_C_K + (j + 1) * 128]) + bias_ones
        k_ref[0, j] = zk.astype(BF16)
    one_lane = (lane == HEAD_DIM).astype(F32)
    for j in range(4):
        zv = _dot(xb, w_ref[:, _C_V + j * 128:_C_V + (j + 1) * 128]) + one_lane
        for i in range(zv.shape[0] // Q_BLOCK):
            v_ref[0, j, i] = zv[i * Q_BLOCK:(i + 1) * Q_BLOCK].T[0:VT_ROWS].astype(BF16)
    gate_ref[0] = _dot(xb, w_ref[:, _C_GATE:_C_END])


def _pack_w_in(w_in):
    d = w_in.shape[0]
    o = RWKV_IN
    q = w_in[:, o:o + 512]
    kc = w_in[:, o + 512:o + 640]
    vc = w_in[:, o + 640:o + 768]
    ks = w_in[:, o + 768:o + 896]
    vs = w_in[:, o + 896:o + 1024]
    kw = w_in[:, o + 1024:o + 1152]
    vw = w_in[:, o + 1152:o + 1280]
    gate = w_in[:, o + 1280:o + 1304]
    z64 = jnp.zeros((d, 64), w_in.dtype)
    pad = lambda m, n: [jnp.concatenate([m[:, i * 64:(i + 1) * 64], z64], axis=1) for i in range(n)]
    gate_pad = jnp.concatenate([gate, jnp.zeros((d, 128 - gate.shape[1]), w_in.dtype)], axis=1)
    cols = ([w_in[:, :o]] + pad(q, NSA_HEADS) + [kc, vc] + pad(ks, NSA_GROUPS) + pad(kw, NSA_GROUPS)
            + pad(vs, NSA_GROUPS) + pad(vw, NSA_GROUPS) + [gate_pad])
    return jnp.concatenate(cols, axis=1).astype(BF16)


def _far_bias_lanes(rel_bias):
    far = rel_bias[NUM_BUCKETS - 1, :]
    hi = far.astype(BF16).astype(F32)
    lane = jnp.arange(V7X_LANES)[None, :]
    return jnp.where(lane == BIAS_LANES[0], hi[:, None], jnp.where(lane == BIAS_LANES[1], (far - hi)[:, None], 0.0))


def _in_projection(x3d, w_in, rel_bias):
    bsz, s, d = x3d.shape
    tm = min(PROJ_TM, s)
    wp = _pack_w_in(w_in)
    out_shape = (
        jax.ShapeDtypeStruct((bsz, s, RWKV_IN), F32),
        jax.ShapeDtypeStruct((bsz, NSA_HEADS, s, 128), BF16),
        jax.ShapeDtypeStruct((bsz, 2, s, 128), F32),
        jax.ShapeDtypeStruct((bsz, 4, s, 128), BF16),
        jax.ShapeDtypeStruct((bsz, 4, s // Q_BLOCK, VT_ROWS, Q_BLOCK), BF16),
        jax.ShapeDtypeStruct((bsz, s, 128), F32),
    )
    return pl.pallas_call(
        _inproj_kernel,
        grid=(bsz, s // tm),
        in_specs=[pl.BlockSpec((1, tm, d), lambda b, i: (b, i, 0)), _const_spec((d, _C_END)),
                  _const_spec((NSA_HEADS, 128))],
        out_specs=(
            pl.BlockSpec((1, tm, RWKV_IN), lambda b, i: (b, i, 0)),
            pl.BlockSpec((1, NSA_HEADS, tm, 128), lambda b, i: (b, 0, i, 0)),
            pl.BlockSpec((1, 2, tm, 128), lambda b, i: (b, 0, i, 0)),
            pl.BlockSpec((1, 4, tm, 128), lambda b, i: (b, 0, i, 0)),
            pl.BlockSpec((1, 4, tm // Q_BLOCK, VT_ROWS, Q_BLOCK), lambda b, i: (b, 0, i, 0, 0)),
            pl.BlockSpec((1, tm, 128), lambda b, i: (b, i, 0)),
        ),
        out_shape=out_shape,
        compiler_params=_params(("parallel", "parallel")),
        name="in_projection",
    )(x3d, wp, _far_bias_lanes(rel_bias))


RWKV_T = 512


def _rwkv_consts():
    r = lax.broadcasted_iota(jnp.int32, (128, 128), 0)
    c = lax.broadcasted_iota(jnp.int32, (128, 128), 1)
    same = (r >= CHUNK) == (c >= CHUNK)
    mask_sl = (same & (r > c)).astype(F32)
    mask_l = (same & (r >= c)).astype(F32)
    eye = (r == c).astype(F32)
    head_ones = same.astype(F32)
    rt = lax.broadcasted_iota(jnp.int32, (CHUNK, CHUNK), 0)
    ct = lax.broadcasted_iota(jnp.int32, (CHUNK, CHUNK), 1)
    tri = (rt >= ct).astype(F32)
    lane = lax.broadcasted_iota(jnp.int32, (1, 128), 1)
    m0 = (lane < CHUNK).astype(F32)
    m1 = 1.0 - m0
    return mask_sl, mask_l, eye, head_ones, tri, m0, m1


def _split2(x):
    hi = x.astype(BF16)
    return hi, (x - hi.astype(F32)).astype(BF16)


def _mm(a, b, mode, dot=_dot):
    if mode == "bf16":
        return dot(a.astype(BF16), b.astype(BF16))
    if mode == "bf16x3":
        ah, al = _split2(a)
        bh, bl = _split2(b)
        return dot(ah, bh) + (dot(ah, bl) + dot(al, bh))
    if mode in ("exact_a", "exact_b"):
        x = b if mode == "exact_a" else a
        hi, rest = x.astype(BF16), None
        rest = x - hi.astype(F32)
        mid = rest.astype(BF16)
        lo = (rest - mid.astype(F32)).astype(BF16)
        if mode == "exact_a":
            ab = a.astype(BF16)
            return dot(ab, hi) + (dot(ab, mid) + dot(ab, lo))
        bb = b.astype(BF16)
        return dot(hi, bb) + (dot(mid, bb) + dot(lo, bb))
    raise ValueError(mode)


RWKV_MODES = dict(p="bf16", inv="bf16", av="bf16", wu="bf16", qy="bf16", mn="bf16", y="bf16", h="bf16")


def _rwkv_chunk(rs, lws, ks, vs, kks, as_, hs, consts):
    mask_sl, mask_l, eye, _, tri, m0, m1 = consts
    md = RWKV_MODES
    n = len(rs)
    each = range(n)

    def sm(x):
        return jnp.concatenate([x * m0, x * m1], axis=0)

    def dup(x):
        return jnp.concatenate([x, x], axis=0)

    cums = [_mm(tri, lws[i], "exact_a") for i in each]
    a_sm, r_sm, v_sm, kb, kbh, g_c = [], [], [], [], [], []
    for i in each:
        cum, lw, kk, k = cums[i], lws[i], kks[i], ks[i]
        cl = cum[CHUNK - 1:CHUNK, :]
        ka = kk * as_[i]
        g_tail = jnp.exp(cl - cum)
        g_inv = jnp.exp(-cum)
        a_sm.append(sm(-(kk * jnp.exp(cum - lw))))
        r_sm.append(sm(rs[i] * jnp.exp(cum)))
        v_sm.append(sm(vs[i]))
        kb.append(jnp.concatenate([dup(k * g_inv), dup(ka * g_inv)], axis=0))
        kbh.append(jnp.concatenate([sm(k * g_tail), sm(ka * g_tail)], axis=0))
        g_c.append(jnp.exp(cl))
    pm = [_mm(jnp.concatenate([a_sm[i], r_sm[i]], axis=0), kb[i], md["p"], _dot_nt) for i in each]
    a_ak = [pm[i][0:128, 0:128] * mask_sl for i in each]
    a_rr = [jnp.concatenate([pm[i][128:256, 0:128] * mask_l, pm[i][128:256, 128:256] * mask_l], axis=1) for i in each]
    x = [pm[i][0:128, 128:256] * mask_sl for i in each]
    t_inv = [eye + x[i] for i in each]
    for _ in range(5):
        x = [_mm(x[i], x[i], md["inv"]) for i in each]
        t_inv = [t_inv[i] + _mm(t_inv[i], x[i], md["inv"]) for i in each]
    av = [_mm(a_ak[i], v_sm[i], md["av"]) for i in each]
    wu = [_mm(t_inv[i], jnp.concatenate([a_sm[i], av[i]], axis=1), md["wu"]) for i in each]
    z = [jnp.concatenate([jnp.concatenate([jnp.zeros_like(v_sm[i]), v_sm[i]], axis=1), wu[i]], axis=0) for i in each]
    qy = [_mm(a_rr[i], z[i], md["qy"]) for i in each]
    mn = [_mm(kbh[i], z[i], md["mn"], _dot_tn) for i in each]
    y_sm = [_mm(r_sm[i] + qy[i][:, 0:128], hs[i], md["y"]) + qy[i][:, 128:256] for i in each]
    h_new = [_mm(mn[i][:, 0:128] + eye * g_c[i], hs[i], md["h"]) + mn[i][:, 128:256] for i in each]
    return [y_sm[i][0:CHUNK] + y_sm[i][CHUNK:2 * CHUNK] for i in each], h_new


def _rwkv_kernel(z_ref, mu_ref, w0_ref, wup_ref, a0_ref, aup_ref, gup_ref, kk_ref, ka_ref, rk_ref,
                 gng_ref, gnb_ref, o_ref, prev_ref, h_ref, r_s, lw_s, k_s, v_s, kk_s, a_s, y_s, g_s, *, t):
    ti = pl.program_id(1)

    @pl.when(ti == 0)
    def _():
        prev_ref[...] = jnp.zeros_like(prev_ref)
        h_ref[...] = jnp.zeros_like(h_ref)

    consts = _rwkv_consts()
    head_ones = consts[3]

    z = z_ref[0]
    row = lax.broadcasted_iota(jnp.int32, (t, 1), 0)
    prev = jnp.where(row == 0, prev_ref[0:1, :], pltpu.roll(z, 1, 0))
    prev_ref[0:1, :] = z[t - 1:t, :]
    zs = z + (prev - z) * mu_ref[...]

    r = zs[:, 0:512]
    k = zs[:, 512:1024]
    v = zs[:, 1024:1536]
    wa = zs[:, 1536:1664]
    gl = zs[:, 1664:1792]
    u = w0_ref[...] + _dot(jnp.tanh(wa).astype(BF16), wup_ref[...])
    lw = (-math.exp(-0.5)) * _sigmoid(u)
    a = _sigmoid(a0_ref[...] + _dot(wa.astype(BF16), aup_ref[...]))
    g_s[...] = _dot(_sigmoid(gl).astype(BF16), gup_ref[...])
    kkr = k * kk_ref[...]
    k2 = k * (1.0 + (a - 1.0) * ka_ref[...])
    r_s[...] = r
    lw_s[...] = lw
    k_s[...] = k2
    v_s[...] = v
    a_s[...] = a
    for pr in range(4):
        sl = slice(pr * 128, (pr + 1) * 128)
        kp = kkr[:, sl]
        ss = _mm(kp * kp, head_ones, "exact_b")
        kk_s[:, sl] = kp * lax.rsqrt(jnp.maximum(ss, 1e-24))

    pairs = [slice(pr * 128, (pr + 1) * 128) for pr in range(4)]

    def chunk_body(c, carry):
        rows = pl.ds(pl.multiple_of(c * CHUNK, CHUNK), CHUNK)
        ys, hs = _rwkv_chunk(*[[ref[rows, sl] for sl in pairs] for ref in (r_s, lw_s, k_s, v_s, kk_s, a_s)],
                             [h_ref[pr] for pr in range(4)], consts)
        for pr, sl in enumerate(pairs):
            y_s[rows, sl] = ys[pr]
            h_ref[pr] = hs[pr]
        return carry

    lax.fori_loop(0, t // CHUNK, chunk_body, 0)

    for pr in range(4):
        sl = slice(pr * 128, (pr + 1) * 128)
        y = y_s[:, sl]
        mean = _mm(y, head_ones, "exact_b") * (1.0 / HEAD_DIM)
        yc = y - mean
        var = _mm(yc * yc, head_ones, "exact_b") * (1.0 / HEAD_DIM)
        yn = yc * lax.rsqrt(var + RWKV_GN_EPS) * gng_ref[:, sl] + gnb_ref[:, sl]
        rp = r_s[:, sl]
        bonus = _mm(rp * k_s[:, sl] * rk_ref[:, sl], head_ones, "exact_b") * v_s[:, sl]
        o_ref[0, :, sl] = ((yn + bonus) * g_s[:, sl]).astype(o_ref.dtype)


def _rwkv_group(zr, mu, w0, w_up, a0, a_up, g_up, k_k, k_a, r_k, gn_g, gn_b):
    bsz, s, _ = zr.shape
    t = min(RWKV_T, s)
    wup_pad = jnp.concatenate([w_up, jnp.zeros_like(a_up)], axis=0).astype(BF16)
    aup_pad = jnp.concatenate([jnp.zeros_like(w_up), a_up], axis=0).astype(BF16)
    row = lambda p: p.reshape(1, -1)
    tile = pltpu.VMEM((t, RWKV_DIM), F32)
    return pl.pallas_call(
        functools.partial(_rwkv_kernel, t=t),
        grid=(bsz, s // t),
        in_specs=[
            pl.BlockSpec((1, t, RWKV_IN), lambda b, i: (b, i, 0)),
            _const_spec((1, RWKV_IN)), _const_spec((1, RWKV_DIM)), _const_spec((128, RWKV_DIM)),
            _const_spec((1, RWKV_DIM)), _const_spec((128, RWKV_DIM)), _const_spec((GATE_LORA, RWKV_DIM)),
            _const_spec((1, RWKV_DIM)), _const_spec((1, RWKV_DIM)), _const_spec((1, RWKV_DIM)),
            _const_spec((1, RWKV_DIM)), _const_spec((1, RWKV_DIM)),
        ],
        out_specs=pl.BlockSpec((1, t, RWKV_DIM), lambda b, i: (b, i, 0)),
        out_shape=jax.ShapeDtypeStruct((bsz, s, RWKV_DIM), BF16),
        scratch_shapes=[pltpu.VMEM((V7X_SUBLANES, RWKV_IN), F32), pltpu.VMEM((4, 128, 128), F32),
                        tile, tile, tile, tile, tile, tile, tile, tile],
        compiler_params=_params(("parallel", "arbitrary")),
        name="rwkv7",
    )(zr, row(mu), row(w0), wup_pad, row(a0), aup_pad, g_up.astype(BF16), row(k_k), row(k_a), row(r_k),
      row(gn_g), row(gn_b))


def _gelu_tanh(x):
    return 0.5 * x * (1.0 + jnp.tanh(math.sqrt(2.0 / math.pi) * (x + 0.044715 * (x * x * x))))


def _compress_kernel(x_ref, pe_ref, w1_ref, w2_ref, kc_ref, vc_ref, *, n_rows):
    half = CMP_LEN // 2
    for kv, o_ref in enumerate((kc_ref, vc_ref)):
        first = jnp.zeros((n_rows, 2 * CMP_HIDDEN), F32)
        second = jnp.zeros((n_rows, 2 * CMP_HIDDEN), F32)
        for p in range(half):
            xa = x_ref[0, kv, pl.ds(p, n_rows, stride=CMP_STRIDE), :]
            first += _dot((xa + pe_ref[kv, p:p + 1, :]).astype(BF16), w1_ref[kv, p])
            second += _dot((xa + pe_ref[kv, half + p:half + p + 1, :]).astype(BF16), w1_ref[kv, half + p])
        pre = first + pltpu.roll(second, n_rows - 1, 0)
        out = _dot(_gelu_tanh(pre).astype(BF16), w2_ref[kv])
        for g in range(NSA_GROUPS):
            o_ref[0, g, 0:CMP_PAD, :] = jnp.zeros((CMP_PAD, HEAD_DIM), o_ref.dtype)
            o_ref[0, g, CMP_PAD:CMP_PAD + n_rows, :] = out[:, g * HEAD_DIM:(g + 1) * HEAD_DIM].astype(o_ref.dtype)
            tail = o_ref.shape[2] - CMP_PAD - n_rows
            o_ref[0, g, CMP_PAD + n_rows:, :] = jnp.zeros((tail, HEAD_DIM), o_ref.dtype)


def _pair_diag(w):
    z = jnp.zeros_like(w)
    return jnp.concatenate([jnp.concatenate([w, z], axis=-1), jnp.concatenate([z, w], axis=-1)], axis=-2)


def _compress(kcvc, pe_k, w1_k, w2_k, pe_v, w1_v, w2_v):
    bsz, _, s, _ = kcvc.shape
    n_rows = s // CMP_STRIDE
    rows_out = CMP_PAD + n_rows + V7X_SUBLANES
    pe = jnp.stack([jnp.concatenate([pe_k, pe_k], axis=1), jnp.concatenate([pe_v, pe_v], axis=1)])
    w1 = jnp.stack([_pair_diag(w1_k.reshape(CMP_LEN, HEAD_DIM, CMP_HIDDEN)),
                    _pair_diag(w1_v.reshape(CMP_LEN, HEAD_DIM, CMP_HIDDEN))]).astype(BF16)
    w2 = jnp.stack([_pair_diag(w2_k), _pair_diag(w2_v)]).astype(BF16)
    shape = jax.ShapeDtypeStruct((bsz, NSA_GROUPS, rows_out, HEAD_DIM), BF16)
    spec = pl.BlockSpec((1, NSA_GROUPS, rows_out, HEAD_DIM), lambda b: (b, 0, 0, 0))
    return pl.pallas_call(
        functools.partial(_compress_kernel, n_rows=n_rows),
        grid=(bsz,),
        in_specs=[pl.BlockSpec((1, 2, s, 128), lambda b: (b, 0, 0, 0)), _const_spec(pe.shape), _const_spec(w1.shape),
                  _const_spec(w2.shape)],
        out_specs=(spec, spec),
        out_shape=(shape, shape),
        compiler_params=_params(("parallel",)),
        name="nsa_compress",
    )(kcvc, pe, w1, w2)


def _cmp_select_kernel(rb_ref, q_ref, kc_ref, vc_ref, tz_ref, gate_ref, oc_ref, sel_ref, *, n_far):
    qi = pl.program_id(1)
    qb = Q_BLOCK
    rowi = lax.broadcasted_iota(jnp.int32, (qb, 128), 0)
    lane = lax.broadcasted_iota(jnp.int32, (qb, 128), 1)
    row_f = rowi.astype(F32)
    sg = _sigmoid(gate_ref[0])
    near0 = pl.multiple_of(qi * 8, 8)
    n_first = qi * 8 - CMP_PAD
    for g in range(NSA_GROUPS):
        q4 = q_ref[0, g * NSA_HPG:(g + 1) * NSA_HPG, :, 0:HEAD_DIM].reshape(NSA_HPG * qb, HEAD_DIM)
        heads = [g * NSA_HPG + p for p in range(NSA_HPG)]
        cols = [slice(p * qb, (p + 1) * qb) for p in range(NSA_HPG)]
        tiles, values, n_of_lane = [], [], []
        for j in range(n_far):
            rows = slice(CMP_PAD + 128 * j, CMP_PAD + 128 * (j + 1))
            s = _dot_nt(kc_ref[0, g, rows, :], q4)
            ok = (128 * j + rowi) < n_first
            tiles.append(jnp.concatenate(
                [jnp.where(ok, s[:, cols[p]] + rb_ref[NUM_BUCKETS - 1, heads[p]], NEG) for p in range(NSA_HPG)], axis=1))
            values.append(vc_ref[0, g, rows, :])
            n_of_lane.append(128 * j + lane)
        s = _dot_nt(kc_ref[0, g, pl.ds(near0, 128), :], q4)
        ok = (n_first + rowi) >= 0
        tiles.append(jnp.concatenate(
            [jnp.where(ok, s[:, cols[p]] + tz_ref[heads[p], TZ_CMP_NEAR], NEG) for p in range(NSA_HPG)], axis=1))
        values.append(vc_ref[0, g, pl.ds(near0, 128), :])
        n_of_lane.append(n_first + lane)

        m = tiles[0].max(axis=0, keepdims=True)
        for tl in tiles[1:]:
            m = jnp.maximum(m, tl.max(axis=0, keepdims=True))
        es = [jnp.where(tl > 0.5 * NEG, jnp.exp(tl - m), 0.0) for tl in tiles]
        den = es[0].sum(axis=0, keepdims=True)
        for e in es[1:]:
            den = den + e.sum(axis=0, keepdims=True)
        inv = 1.0 / jnp.maximum(den, 1e-30)
        o_t = jnp.zeros((HEAD_DIM, NSA_HPG * qb), F32)
        imp = jnp.zeros((qb, 128), F32)
        for j, e in enumerate(es):
            pc = e * inv
            o_t = o_t + _dot_tn(values[j], pc.astype(BF16))
            psum = pc[:, cols[0]] + pc[:, cols[1]] + pc[:, cols[2]] + pc[:, cols[3]]
            ov = ((n_of_lane[j] >= 4 * rowi - 1) & (n_of_lane[j] <= 4 * rowi + 3)).astype(F32)
            imp = imp + _mm(ov, psum, "exact_a")
        for p, h in enumerate(heads):
            oc_ref[0, :, h * HEAD_DIM:(h + 1) * HEAD_DIM] = o_t[:, cols[p]].T * sg[:, 3 * h:3 * h + 1]
        cur = 2 * qi + (lane >= SLC_LEN).astype(jnp.int32)
        forced = (rowi == 0) | (rowi == cur) | (rowi == cur - 1)
        score = jnp.where(rowi <= cur, jnp.where(forced, FORCED_SCORE, imp), -1.0)
        sel = jnp.zeros((qb, 128), F32)
        for _ in range(N_SEL):
            mx = score.max(axis=0, keepdims=True)
            idx = jnp.where(score == mx, row_f, 128.0).min(axis=0, keepdims=True)
            pick = row_f == idx
            sel = jnp.where(pick, 1.0, sel)
            score = jnp.where(pick, -3e38, score)
        sel_ref[0, g] = jnp.where(sel.T > 0.5, 0.0, NEG).astype(sel_ref.dtype)


def _cmp_select(rel_bias, q, kc, vc, tz, gate):
    bsz, _, s, _ = q.shape
    nq = s // Q_BLOCK
    n_far = max(0, -(-(8 * (nq - 1) - CMP_PAD) // 128))
    rows_c = kc.shape[2]
    return pl.pallas_call(
        functools.partial(_cmp_select_kernel, n_far=n_far),
        grid=(bsz, nq),
        in_specs=[
            pl.BlockSpec(memory_space=pltpu.SMEM),
            pl.BlockSpec((1, NSA_HEADS, Q_BLOCK, 128), lambda b, i: (b, 0, i, 0)),
            pl.BlockSpec((1, NSA_GROUPS, rows_c, HEAD_DIM), lambda b, i: (b, 0, 0, 0)),
            pl.BlockSpec((1, NSA_GROUPS, rows_c, HEAD_DIM), lambda b, i: (b, 0, 0, 0)),
            _const_spec(tz.shape),
            pl.BlockSpec((1, Q_BLOCK, 128), lambda b, i: (b, i, 0)),
        ],
        out_specs=(
            pl.BlockSpec((1, Q_BLOCK, NSA_DIM), lambda b, i: (b, i, 0)),
            pl.BlockSpec((1, NSA_GROUPS, Q_BLOCK, 128), lambda b, i: (b, 0, i, 0)),
        ),
        out_shape=(jax.ShapeDtypeStruct((bsz, s, NSA_DIM), F32),
                   jax.ShapeDtypeStruct((bsz, NSA_GROUPS, s, 128), BF16)),
        compiler_params=_params(("parallel", "parallel")),
        name="nsa_cmp_select",
    )(rel_bias, q, kc, vc, tz, gate)


SEL_TILES = 4


def _attn_scores(qa, ka, bias_fns, m_prev):
    qb = Q_BLOCK
    s = _dot_nt(ka, qa)
    if bias_fns is not None:
        s = jnp.concatenate(
            [jnp.concatenate([s[i * qb:(i + 1) * qb, p * qb:(p + 1) * qb] + fn(p) for p in range(NSA_HPG)], axis=1)
             for i, fn in enumerate(bias_fns)], axis=0)
    return s, jnp.maximum(m_prev, s.max(axis=0, keepdims=True))


def _attn_accumulate(s, vts, m_prev, m_next, acc_ref):
    pexp = jnp.exp(s - m_next)
    acc_ref[...] = jnp.exp(m_prev - m_next) * acc_ref[...] + _dot(jnp.concatenate(vts, axis=1), pexp.astype(BF16))


def _sel_win_kernel(q_ref, k_ref, vt_ref, e_ref, sel_ref, tz_ref, gate_ref, oc_ref, o_ref, m_ref, acc_ref,
                    s0_ref, s1_ref, mm0_ref, mm1_ref):
    qi = pl.program_id(1)
    qb = Q_BLOCK
    win_tiles = WINDOW // qb
    m_ref[...] = jnp.full(m_ref.shape, NEG, F32)
    acc_ref[...] = jnp.zeros(acc_ref.shape, F32)
    q4 = [q_ref[0, g * NSA_HPG:(g + 1) * NSA_HPG].reshape(NSA_HPG * qb, 128) for g in range(NSA_GROUPS)]
    selm = [jnp.concatenate([sel_ref[0, g]] * NSA_HPG, axis=0) for g in range(NSA_GROUPS)]

    def key_rows(kt, n=1):
        return pl.ds(pl.multiple_of(kt * qb, qb), n * qb)

    def bias_of(g, kind):
        return lambda p: tz_ref[g * NSA_HPG + p, kind]

    def keys_sel(g, kt, n=1):
        return jnp.concatenate([k_ref[0, g, key_rows(kt, n), :], e_ref[key_rows(kt, n), :]], axis=1)

    block = lax.broadcasted_iota(jnp.int32, (NSA_HPG * qb, 128), 1)

    n_far = (jnp.maximum(qi - 1, 0) + SEL_TILES - 1) // SEL_TILES
    n_run = n_far | 1
    last_tile0 = k_ref.shape[2] // qb - SEL_TILES

    def tile0(step):
        return jnp.minimum(step * SEL_TILES, last_tile0)

    def score_far(step, s_ref, mm_ref):
        for g in range(NSA_GROUPS):
            qa = jnp.concatenate(
                [q4[g], jnp.where((block >= 2 * (qi - 1)) | (step >= n_far), jnp.asarray(NEG, BF16), selm[g])], axis=1)
            m_prev = m_ref[g]
            s, m_next = _attn_scores(qa, keys_sel(g, tile0(step), SEL_TILES), None, m_prev)
            s_ref[g] = s
            mm_ref[g, 0:1, :] = m_prev
            mm_ref[g, 1:2, :] = m_next
            m_ref[g] = m_next

    def accumulate_far(step, s_ref, mm_ref):
        for g in range(NSA_GROUPS):
            _attn_accumulate(s_ref[g], [vt_ref[0, g, tile0(step) + i] for i in range(SEL_TILES)],
                             mm_ref[g, 0:1, :], mm_ref[g, 1:2, :], acc_ref.at[g])

    score_far(0, s0_ref, mm0_ref)

    def far_body(i, carry):
        score_far(2 * i + 1, s1_ref, mm1_ref)
        accumulate_far(2 * i, s0_ref, mm0_ref)
        score_far(2 * i + 2, s0_ref, mm0_ref)
        accumulate_far(2 * i + 1, s1_ref, mm1_ref)
        return carry

    lax.fori_loop(0, n_run // 2, far_body, 0)

    scored = []
    for g in range(NSA_GROUPS):
        kt_sub = jnp.maximum(qi - 1, 0)
        kinds = (jnp.where(qi >= 1, TZ_SUB, TZ_MASKED), TZ_DIAG)
        m_prev = m_ref[g]
        s, m_next = _attn_scores(jnp.concatenate([q4[g], selm[g]], axis=1),
                                 jnp.concatenate([keys_sel(g, kt_sub), keys_sel(g, qi)], axis=0),
                                 [bias_of(g, kind) for kind in kinds], m_prev)
        scored.append((g, s, [vt_ref[0, g, kt_sub], vt_ref[0, g, qi]], m_prev, m_next))
    for g in range(NSA_GROUPS):
        kw = NSA_GROUPS + g
        ks, vts, biases = [], [], []
        for d in range(win_tiles, -1, -1):
            kt = jnp.maximum(qi - d, 0)
            kind = TZ_DIAG if d == 0 else TZ_SUB if d == 1 else TZ_WIN_OLD if d == win_tiles else TZ_ZERO
            if d > 0:
                kind = jnp.where(qi >= d, kind, TZ_MASKED)
            ks.append(k_ref[0, kw, key_rows(kt), :])
            vts.append(vt_ref[0, kw, kt])
            biases.append(bias_of(g, kind))
        m_prev = m_ref[kw]
        s, m_next = _attn_scores(q4[g], jnp.concatenate(ks, axis=0), biases, m_prev)
        scored.append((kw, s, vts, m_prev, m_next))
    accumulate_far(n_run - 1, s0_ref, mm0_ref)
    for stream, s, vts, m_prev, m_next in scored:
        _attn_accumulate(s, vts, m_prev, m_next, acc_ref.at[stream])

    sg = _sigmoid(gate_ref[0])
    oc = oc_ref[0]
    for g in range(NSA_GROUPS):
        outs = []
        for stream in (g, NSA_GROUPS + g):
            acc = acc_ref[stream]
            outs.append(acc[0:HEAD_DIM] * (1.0 / acc[HEAD_DIM:HEAD_DIM + 1]))
        for p in range(NSA_HPG):
            h = g * NSA_HPG + p
            qcols = slice(p * qb, (p + 1) * qb)
            cols = slice(h * HEAD_DIM, (h + 1) * HEAD_DIM)
            y = (oc[:, cols] + sg[:, 3 * h + 1:3 * h + 2] * outs[0][:, qcols].T
                 + sg[:, 3 * h + 2:3 * h + 3] * outs[1][:, qcols].T)
            o_ref[0, :, cols] = y.astype(o_ref.dtype)


def _sel_win(q, k4, v4, sel, tz, gate, oc):
    bsz, _, s, _ = q.shape
    nq = s // Q_BLOCK
    assert nq % SEL_TILES == 0
    member = (jnp.arange(s)[:, None] // SLC_LEN == jnp.arange(128)[None, :]).astype(BF16)
    return pl.pallas_call(
        _sel_win_kernel,
        grid=(bsz, nq),
        in_specs=[
            pl.BlockSpec((1, NSA_HEADS, Q_BLOCK, 128), lambda b, i: (b, 0, i, 0)),
            pl.BlockSpec((1, 4, s, 128), lambda b, i: (b, 0, 0, 0)),
            pl.BlockSpec((1, 4, nq, VT_ROWS, Q_BLOCK), lambda b, i: (b, 0, 0, 0, 0)),
            _const_spec((s, 128)),
            pl.BlockSpec((1, NSA_GROUPS, Q_BLOCK, 128), lambda b, i: (b, 0, i, 0)),
            _const_spec(tz.shape),
            pl.BlockSpec((1, Q_BLOCK, 128), lambda b, i: (b, i, 0)),
            pl.BlockSpec((1, Q_BLOCK, NSA_DIM), lambda b, i: (b, i, 0)),
        ],
        out_specs=pl.BlockSpec((1, Q_BLOCK, NSA_DIM), lambda b, i: (b, i, 0)),
        out_shape=jax.ShapeDtypeStruct((bsz, s, NSA_DIM), BF16),
        scratch_shapes=[pltpu.VMEM((2 * NSA_GROUPS, 1, NSA_HPG * Q_BLOCK), F32),
                        pltpu.VMEM((2 * NSA_GROUPS, VT_ROWS, NSA_HPG * Q_BLOCK), F32),
                        pltpu.VMEM((NSA_GROUPS, SEL_TILES * Q_BLOCK, NSA_HPG * Q_BLOCK), F32),
                        pltpu.VMEM((NSA_GROUPS, SEL_TILES * Q_BLOCK, NSA_HPG * Q_BLOCK), F32),
                        pltpu.VMEM((NSA_GROUPS, 2, NSA_HPG * Q_BLOCK), F32),
                        pltpu.VMEM((NSA_GROUPS, 2, NSA_HPG * Q_BLOCK), F32)],
        compiler_params=_params(("parallel", "parallel")),
        name="nsa_sel_win",
    )(q, k4, v4, member, sel, tz, gate, oc)


OUT_TM = 512


def _outproj_kernel(x_ref, yr_ref, yn_ref, w_ref, g_ref, b_ref, o_ref):
    mixed = _dot(yr_ref[...], w_ref[0:RWKV_DIM, :]) + _dot(yn_ref[...], w_ref[RWKV_DIM:, :])
    o_ref[...] = _layer_norm(ALPHA * x_ref[...] + mixed, g_ref[...], b_ref[...])


def _out_projection(x2d, yr, yn, w_out, g, b):
    rows, d = x2d.shape
    tm = min(OUT_TM, rows)
    return pl.pallas_call(
        _outproj_kernel,
        grid=(rows // tm,),
        in_specs=[pl.BlockSpec((tm, d), lambda i: (i, 0)), pl.BlockSpec((tm, RWKV_DIM), lambda i: (i, 0)),
                  pl.BlockSpec((tm, NSA_DIM), lambda i: (i, 0)), _const_spec(w_out.shape),
                  _const_spec((1, d)), _const_spec((1, d))],
        out_specs=pl.BlockSpec((tm, d), lambda i: (i, 0)),
        out_shape=jax.ShapeDtypeStruct((rows, d), F32),
        compiler_params=_params(("parallel",)),
        name="out_projection_ln",
    )(x2d, yr, yn, w_out.astype(BF16), g.reshape(1, d), b.reshape(1, d))


XATTN_TM = 512


def _mem_kv_kernel(mem_ref, wk_ref, wv_ref, k_ref, v_ref):
    mb = mem_ref[0].astype(BF16)
    k_ref[0] = _dot(mb, wk_ref[...]).astype(BF16)
    v_ref[0] = _dot(mb, wv_ref[...]).astype(BF16)


def _mem_kv(mem, wk, wv):
    bsz, m, d = mem.shape
    shape = jax.ShapeDtypeStruct((bsz, m, d), BF16)
    spec = pl.BlockSpec((1, m, d), lambda b: (b, 0, 0))
    return pl.pallas_call(
        _mem_kv_kernel,
        grid=(bsz,),
        in_specs=[spec, _const_spec((d, d)), _const_spec((d, d))],
        out_specs=(spec, spec),
        out_shape=(shape, shape),
        compiler_params=_params(("parallel",)),
        name="xattn_mem_kv",
    )(mem, wk.astype(BF16), wv.astype(BF16))


def _xattn_kernel(x_ref, k_ref, v_ref, wq_ref, wo_ref, g_ref, b_ref, o_ref):
    x = x_ref[0]
    q = (_dot(x.astype(BF16), wq_ref[...]) * (XATTN_HEAD_DIM ** -0.5)).astype(BF16)
    heads = []
    for h in range(XATTN_HEADS):
        cols = slice(h * XATTN_HEAD_DIM, (h + 1) * XATTN_HEAD_DIM)
        s = _dot_nt(q[:, cols], k_ref[0, :, cols])
        e = jnp.exp(s - s.max(axis=1, keepdims=True))
        p = e * (1.0 / e.sum(axis=1, keepdims=True))
        heads.append(_dot(p.astype(BF16), v_ref[0, :, cols]).astype(BF16))
    o = _dot(jnp.concatenate(heads, axis=1), wo_ref[...])
    o_ref[0] = _layer_norm(ALPHA * x + o, g_ref[...], b_ref[...])


def _cross_attention_ln(x3d, k, v, wq, wo, g, b):
    bsz, s, d = x3d.shape
    m = k.shape[1]
    tm = min(XATTN_TM, s)
    return pl.pallas_call(
        _xattn_kernel,
        grid=(bsz, s // tm),
        in_specs=[pl.BlockSpec((1, tm, d), lambda bb, i: (bb, i, 0)),
                  pl.BlockSpec((1, m, d), lambda bb, i: (bb, 0, 0)),
                  pl.BlockSpec((1, m, d), lambda bb, i: (bb, 0, 0)),
                  _const_spec((d, d)), _const_spec((d, d)), _const_spec((1, d)), _const_spec((1, d))],
        out_specs=pl.BlockSpec((1, tm, d), lambda bb, i: (bb, i, 0)),
        out_shape=jax.ShapeDtypeStruct((bsz, s, d), F32),
        compiler_params=_params(("parallel", "parallel")),
        name="xattn_ln",
    )(x3d, k, v, wq.astype(BF16), wo.astype(BF16), g.reshape(1, d), b.reshape(1, d))


def _nsa_group(rel_bias, q, kcvc, k4, v4, gate, pe_k, w1_k, w2_k, pe_v, w1_v, w2_v):
    tz = _bias_tiles(rel_bias)
    kc, vc = _compress(kcvc, pe_k, w1_k, w2_k, pe_v, w1_v, w2_v)
    oc, sel = _cmp_select(rel_bias, q, kc, vc, tz, gate)
    return _sel_win(q, k4, v4, sel, tz, gate, oc)


def kernel(x, mem, ffn1_w_gate, ffn1_w_up, ffn1_w_down, ln1_g, ln1_b, mix_w_in, rwkv_mu, rwkv_w0, rwkv_w_up, rwkv_a0, rwkv_a_up, rwkv_g_up, rwkv_k_k, rwkv_k_a, rwkv_r_k, rwkv_gn_g, rwkv_gn_b, nsa_pe_k, nsa_w1_k, nsa_w2_k, nsa_pe_v, nsa_w1_v, nsa_w2_v, mix_w_out, ln2_g, ln2_b, xattn_wq, xattn_wk, xattn_wv, xattn_wo, ln3_g, ln3_b, ffn2_w_gate, ffn2_w_up, ffn2_w_down, ln4_g, ln4_b, rel_bias):
    bsz, s, d = x.shape
    rows = bsz * s
    for l in range(DEPTH):
        x1 = _ffn_ln(x.reshape(rows, d), ffn1_w_gate[l], ffn1_w_up[l], ffn1_w_down[l], ln1_g[l], ln1_b[l])
        zr, q, kcvc, k4, v4, gate = _in_projection(x1.reshape(bsz, s, d), mix_w_in[l], rel_bias)
        y_rwkv = _rwkv_group(zr, rwkv_mu[l], rwkv_w0[l], rwkv_w_up[l], rwkv_a0[l], rwkv_a_up[l], rwkv_g_up[l],
                             rwkv_k_k[l], rwkv_k_a[l], rwkv_r_k[l], rwkv_gn_g[l], rwkv_gn_b[l])
        y_nsa = _nsa_group(rel_bias, q, kcvc, k4, v4, gate, nsa_pe_k[l], nsa_w1_k[l], nsa_w2_k[l],
                           nsa_pe_v[l], nsa_w1_v[l], nsa_w2_v[l])
        x2 = _out_projection(x1, y_rwkv.reshape(rows, RWKV_DIM), y_nsa.reshape(rows, NSA_DIM), mix_w_out[l],
                             ln2_g[l], ln2_b[l])
        mk, mv = _mem_kv(mem, xattn_wk[l], xattn_wv[l])
        x3 = _cross_attention_ln(x2.reshape(bsz, s, d), mk, mv, xattn_wq[l], xattn_wo[l], ln3_g[l], ln3_b[l])
        x = _ffn_ln(x3.reshape(rows, d), ffn2_w_gate[l], ffn2_w_up[l], ffn2_w_down[l], ln4_g[l], ln4_b[l])
        x = x.reshape(bsz, s, d)
    return x
```

```python
import functools
import math

import numpy as np
import jax
import jax.numpy as jnp
from jax import lax
from jax.experimental import pallas as pl
from jax.experimental.pallas import tpu as pltpu

F32 = jnp.float32
BF16 = jnp.bfloat16
HIGHEST = lax.Precision.HIGHEST

D_MODEL = 1024
DEPTH = 1
RWKV_HEADS = 8
HEAD_DIM = 64
RWKV_DIM = RWKV_HEADS * HEAD_DIM
DECAY_LORA = 64
AAA_LORA = 64
GATE_LORA = 128
RWKV_IN = 3 * RWKV_DIM + DECAY_LORA + AAA_LORA + GATE_LORA
RWKV_GN_EPS = 64e-5
NSA_HEADS = 8
NSA_GROUPS = 2
NSA_HPG = NSA_HEADS // NSA_GROUPS
NSA_DIM = NSA_HEADS * HEAD_DIM
CMP_LEN = 32
CMP_STRIDE = 16
CMP_HIDDEN = 128
SLC_LEN = 64
N_SEL = 16
WINDOW = 512
Q_BLOCK = 128
FORCED_SCORE = 1e4
NUM_BUCKETS = 32
MAX_DISTANCE = 128
XATTN_HEADS = 4
XATTN_HEAD_DIM = D_MODEL // XATTN_HEADS
D_FF = 2816
LN_EPS = 1e-5
ALPHA = (2.0 * DEPTH) ** 0.25
NEG = -1e30

V7X_LANES = 128
V7X_SUBLANES = 8
V7X_VMEM_LIMIT_BYTES = 56 * 1024 * 1024

CHUNK = 64
CMP_PAD = 120
VT_ROWS = 80


def _dot(a, b, prec=None):
    return jnp.dot(a, b, preferred_element_type=F32, precision=prec)


def _dot_nt(a, b, prec=None):
    return lax.dot_general(a, b, (((1,), (1,)), ((), ())), preferred_element_type=F32, precision=prec)


def _dot_tn(a, b, prec=None):
    return lax.dot_general(a, b, (((0,), (0,)), ((), ())), preferred_element_type=F32, precision=prec)


def _sigmoid(x):
    return 1.0 / (1.0 + jnp.exp(-x))


def _layer_norm(y, g, b):
    mu = jnp.mean(y, axis=-1, keepdims=True)
    yc = y - mu
    var = jnp.mean(yc * yc, axis=-1, keepdims=True)
    return yc * lax.rsqrt(var + LN_EPS) * g + b


def _params(sem):
    return pltpu.CompilerParams(dimension_semantics=sem, vmem_limit_bytes=V7X_VMEM_LIMIT_BYTES)


def _const_spec(shape, single_buffer=False):
    nd = len(shape)
    if single_buffer:
        return pl.BlockSpec(shape, lambda *_: (0,) * nd, pipeline_mode=pl.Buffered(1))
    return pl.BlockSpec(shape, lambda *_: (0,) * nd)


def _bucket_thresholds():
    n = np.arange(0, 4 * MAX_DISTANCE)
    max_exact = NUM_BUCKETS // 2
    nf = np.maximum(n, max_exact).astype(np.float32)
    large = max_exact + (np.log(nf / np.float32(max_exact)) / np.float32(math.log(MAX_DISTANCE / max_exact))
                         * np.float32(NUM_BUCKETS - max_exact)).astype(np.int32)
    large = np.minimum(large, NUM_BUCKETS - 1)
    bucket = np.where(n < max_exact, n, large)
    return [int(np.argmax(bucket >= b)) for b in range(1, NUM_BUCKETS)]


_BUCKET_THR = _bucket_thresholds()


TZ_CMP_NEAR, TZ_DIAG, TZ_SUB, TZ_ZERO, TZ_MASKED, TZ_WIN_OLD, TZ_KINDS = 0, 1, 2, 3, 4, 5, 6


def _bias_tiles_kernel(rb_ref, o_ref):
    h = pl.program_id(0)
    r = lax.broadcasted_iota(jnp.int32, (Q_BLOCK, Q_BLOCK), 0)
    c = lax.broadcasted_iota(jnp.int32, (Q_BLOCK, Q_BLOCK), 1)
    far = jnp.full((Q_BLOCK, Q_BLOCK), rb_ref[NUM_BUCKETS - 1, h], F32)
    far_hi = far.astype(BF16).astype(F32)
    far_added = far_hi + (far - far_hi).astype(BF16).astype(F32)
    dists = (c - CMP_STRIDE * (r - CMP_PAD) - (CMP_LEN - 1), c - r, Q_BLOCK + c - r)
    for kind, dist in enumerate(dists):
        val = jnp.full((Q_BLOCK, Q_BLOCK), rb_ref[0, h], F32)
        for b in range(1, NUM_BUCKETS):
            val = jnp.where(dist >= _BUCKET_THR[b - 1], rb_ref[b, h], val)
        if kind != TZ_CMP_NEAR:
            val = val - far_added
        o_ref[0, kind] = jnp.where(dist >= 0, val, NEG)
    o_ref[0, TZ_ZERO] = far - far_added
    o_ref[0, TZ_MASKED] = jnp.full((Q_BLOCK, Q_BLOCK), NEG, F32)
    o_ref[0, TZ_WIN_OLD] = jnp.where(r > c, far - far_added, NEG)


def _bias_tiles(rel_bias):
    return pl.pallas_call(
        _bias_tiles_kernel,
        grid=(NSA_HEADS,),
        in_specs=[pl.BlockSpec(memory_space=pltpu.SMEM)],
        out_specs=pl.BlockSpec((1, TZ_KINDS, Q_BLOCK, Q_BLOCK), lambda h: (h, 0, 0, 0)),
        out_shape=jax.ShapeDtypeStruct((NSA_HEADS, TZ_KINDS, Q_BLOCK, Q_BLOCK), F32),
        compiler_params=_params(("arbitrary",)),
        name="bias_tiles",
    )(rel_bias)


FFN_TM = 1024
FFN_TF = 256


def _ffn_kernel(x_ref, wg_ref, wu_ref, wd_ref, g_ref, b_ref, o_ref, acc_ref, *, nchunk):
    x = x_ref[...]
    xb = x.astype(BF16)
    acc_ref[...] = jnp.zeros_like(acc_ref)

    def body(c, carry):
        hg = _dot(xb, wg_ref[c])
        hu = _dot(xb, wu_ref[c])
        h = hg * _sigmoid(hg) * hu
        acc_ref[...] += _dot(h.astype(BF16), wd_ref[c])
        return carry

    lax.fori_loop(0, nchunk, body, 0)
    o_ref[...] = _layer_norm(ALPHA * x + 0.5 * acc_ref[...], g_ref[...], b_ref[...])


def _ffn_ln(x2d, wg, wu, wd, g, b):
    rows, d = x2d.shape
    f = wg.shape[1]
    nchunk = f // FFN_TF
    wg3 = wg.astype(BF16).reshape(d, nchunk, FFN_TF).transpose(1, 0, 2)
    wu3 = wu.astype(BF16).reshape(d, nchunk, FFN_TF).transpose(1, 0, 2)
    wd3 = wd.astype(BF16).reshape(nchunk, FFN_TF, d)
    tm = min(FFN_TM, rows)
    return pl.pallas_call(
        functools.partial(_ffn_kernel, nchunk=nchunk),
        grid=(rows // tm,),
        in_specs=[
            pl.BlockSpec((tm, d), lambda i: (i, 0)),
            _const_spec((nchunk, d, FFN_TF), single_buffer=True),
            _const_spec((nchunk, d, FFN_TF), single_buffer=True),
            _const_spec((nchunk, FFN_TF, d), single_buffer=True),
            _const_spec((1, d)),
            _const_spec((1, d)),
        ],
        out_specs=pl.BlockSpec((tm, d), lambda i: (i, 0)),
        out_shape=jax.ShapeDtypeStruct((rows, d), F32),
        scratch_shapes=[pltpu.VMEM((tm, d), F32)],
        compiler_params=_params(("parallel",)),
        name="ffn_ln",
    )(x2d, wg3, wu3, wd3, g.reshape(1, d), b.reshape(1, d))


PROJ_TM = 512
_C_RWKV = 0
_C_Q = _C_RWKV + RWKV_IN
_C_KCVC = _C_Q + NSA_HEADS * 128
_C_K = _C_KCVC + 256
_C_V = _C_K + 4 * 128
_C_GATE = _C_V + 4 * 128
_C_END = _C_GATE + 128
BIAS_LANES = (HEAD_DIM, HEAD_DIM + 1)


def _inproj_kernel(x_ref, w_ref, qx_ref, zr_ref, q_ref, kcvc_ref, k_ref, v_ref, gate_ref):
    xb = x_ref[0].astype(BF16)
    lane = lax.broadcasted_iota(jnp.int32, (1, V7X_LANES), 1)
    zr_ref[0] = _dot(xb, w_ref[:, _C_RWKV:_C_Q])
    zq = _dot(xb, w_ref[:, _C_Q:_C_KCVC]) * (HEAD_DIM ** -0.5)
    for h in range(NSA_HEADS):
        q_ref[0, h] = (zq[:, h * 128:(h + 1) * 128] + qx_ref[h:h + 1, :]).astype(BF16)
    zc = _dot(xb, w_ref[:, _C_KCVC:_C_K])
    kcvc_ref[0, 0] = zc[:, 0:128]
    kcvc_ref[0, 1] = zc[:, 128:256]
    bias_ones = ((lane == BIAS_LANES[0]) | (lane == BIAS_LANES[1])).astype(F32)
    zk = _dot(xb, w_ref[:, _C_K:_C_V])
    for j in range(4):
        k_ref[0, j] = (zk[:, j * 128:(j + 1) * 128] + bias_ones).astype(BF16)
    one_lane = (lane == HEAD_DIM).astype(F32)
    zvg = _dot(xb, w_ref[:, _C_V:_C_END])
    for j in range(4):
        zv = zvg[:, j * 128:(j + 1) * 128] + one_lane
        for i in range(zv.shape[0] // Q_BLOCK):
            v_ref[0, j, i] = zv[i * Q_BLOCK:(i + 1) * Q_BLOCK].T[0:VT_ROWS].astype(BF16)
    gate_ref[0] = zvg[:, 4 * 128:5 * 128]


def _pack_w_in(w_in):
    d = w_in.shape[0]
    o = RWKV_IN
    q = w_in[:, o:o + 512]
    kc = w_in[:, o + 512:o + 640]
    vc = w_in[:, o + 640:o + 768]
    ks = w_in[:, o + 768:o + 896]
    vs = w_in[:, o + 896:o + 1024]
    kw = w_in[:, o + 1024:o + 1152]
    vw = w_in[:, o + 1152:o + 1280]
    gate = w_in[:, o + 1280:o + 1304]
    z64 = jnp.zeros((d, 64), w_in.dtype)
    pad = lambda m, n: [jnp.concatenate([m[:, i * 64:(i + 1) * 64], z64], axis=1) for i in range(n)]
    gate_pad = jnp.concatenate([gate, jnp.zeros((d, 128 - gate.shape[1]), w_in.dtype)], axis=1)
    cols = ([w_in[:, :o]] + pad(q, NSA_HEADS) + [kc, vc] + pad(ks, NSA_GROUPS) + pad(kw, NSA_GROUPS)
            + pad(vs, NSA_GROUPS) + pad(vw, NSA_GROUPS) + [gate_pad])
    return jnp.concatenate(cols, axis=1).astype(BF16)


def _far_bias_lanes(rel_bias):
    far = rel_bias[NUM_BUCKETS - 1, :]
    hi = far.astype(BF16).astype(F32)
    lane = jnp.arange(V7X_LANES)[None, :]
    return jnp.where(lane == BIAS_LANES[0], hi[:, None], jnp.where(lane == BIAS_LANES[1], (far - hi)[:, None], 0.0))


def _in_projection(x3d, w_in, rel_bias):
    bsz, s, d = x3d.shape
    tm = min(PROJ_TM, s)
    wp = _pack_w_in(w_in)
    out_shape = (
        jax.ShapeDtypeStruct((bsz, s, RWKV_IN), F32),
        jax.ShapeDtypeStruct((bsz, NSA_HEADS, s, 128), BF16),
        jax.ShapeDtypeStruct((bsz, 2, s, 128), F32),
        jax.ShapeDtypeStruct((bsz, 4, s, 128), BF16),
        jax.ShapeDtypeStruct((bsz, 4, s // Q_BLOCK, VT_ROWS, Q_BLOCK), BF16),
        jax.ShapeDtypeStruct((bsz, s, 128), F32),
    )
    return pl.pallas_call(
        _inproj_kernel,
        grid=(bsz, s // tm),
        in_specs=[pl.BlockSpec((1, tm, d), lambda b, i: (b, i, 0)), _const_spec((d, _C_END)),
                  _const_spec((NSA_HEADS, 128))],
        out_specs=(
            pl.BlockSpec((1, tm, RWKV_IN), lambda b, i: (b, i, 0)),
            pl.BlockSpec((1, NSA_HEADS, tm, 128), lambda b, i: (b, 0, i, 0)),
            pl.BlockSpec((1, 2, tm, 128), lambda b, i: (b, 0, i, 0)),
            pl.BlockSpec((1, 4, tm, 128), lambda b, i: (b, 0, i, 0)),
            pl.BlockSpec((1, 4, tm // Q_BLOCK, VT_ROWS, Q_BLOCK), lambda b, i: (b, 0, i, 0, 0)),
            pl.BlockSpec((1, tm, 128), lambda b, i: (b, i, 0)),
        ),
        out_shape=out_shape,
        compiler_params=_params(("parallel", "parallel")),
        name="in_projection",
    )(x3d, wp, _far_bias_lanes(rel_bias))


RWKV_T = 512
RWKV_CHUNKS_PER_STEP = 4


def _rwkv_consts():
    r = lax.broadcasted_iota(jnp.int32, (128, 128), 0)
    c = lax.broadcasted_iota(jnp.int32, (128, 128), 1)
    same = (r >= CHUNK) == (c >= CHUNK)
    mask_sl = (same & (r > c)).astype(F32)
    mask_l = (same & (r >= c)).astype(F32)
    eye = (r == c).astype(F32)
    head_ones = same.astype(F32)
    rt = lax.broadcasted_iota(jnp.int32, (CHUNK, CHUNK), 0)
    ct = lax.broadcasted_iota(jnp.int32, (CHUNK, CHUNK), 1)
    tri = (rt >= ct).astype(F32)
    lane = lax.broadcasted_iota(jnp.int32, (1, 128), 1)
    m0 = (lane < CHUNK).astype(F32)
    m1 = 1.0 - m0
    return mask_sl, mask_l, eye, head_ones, tri, m0, m1


def _split2(x):
    hi = x.astype(BF16)
    return hi, (x - hi.astype(F32)).astype(BF16)


def _mm(a, b, mode, dot=_dot):
    if mode == "bf16":
        return dot(a.astype(BF16), b.astype(BF16))
    if mode == "bf16x3":
        ah, al = _split2(a)
        bh, bl = _split2(b)
        return dot(ah, bh) + (dot(ah, bl) + dot(al, bh))
    if mode in ("exact_a", "exact_b"):
        x = b if mode == "exact_a" else a
        hi, rest = x.astype(BF16), None
        rest = x - hi.astype(F32)
        mid = rest.astype(BF16)
        lo = (rest - mid.astype(F32)).astype(BF16)
        if mode == "exact_a":
            ab = a.astype(BF16)
            return dot(ab, hi) + (dot(ab, mid) + dot(ab, lo))
        bb = b.astype(BF16)
        return dot(hi, bb) + (dot(mid, bb) + dot(lo, bb))
    raise ValueError(mode)


RWKV_MODES = dict(p="bf16", inv="bf16", av="bf16", wu="bf16", qy="bf16", mn="bf16", y="bf16", h="bf16")


def _rwkv_chunk(rs, lws, ks, vs, kks, as_, hs, consts):
    mask_sl, mask_l, eye, _, tri, m0, m1 = consts
    md = RWKV_MODES
    n = len(rs)
    each = range(n)

    def sm(x):
        return jnp.concatenate([x * m0, x * m1], axis=0)

    def dup(x):
        return jnp.concatenate([x, x], axis=0)

    cums = [_mm(tri, lws[i], "exact_a") for i in each]
    a_sm, r_sm, v_sm, kb, kbh, g_c = [], [], [], [], [], []
    for i in each:
        cum, lw, kk, k = cums[i], lws[i], kks[i], ks[i]
        cl = cum[CHUNK - 1:CHUNK, :]
        ka = kk * as_[i]
        g_tail = jnp.exp(cl - cum)
        g_inv = jnp.exp(-cum)
        a_sm.append(sm(-(kk * jnp.exp(cum - lw))))
        r_sm.append(sm(rs[i] * jnp.exp(cum)))
        v_sm.append(sm(vs[i]))
        kb.append(jnp.concatenate([dup(k * g_inv), dup(ka * g_inv)], axis=0))
        kbh.append(jnp.concatenate([sm(k * g_tail), sm(ka * g_tail)], axis=0))
        g_c.append(jnp.exp(cl))
    pm = [_mm(jnp.concatenate([a_sm[i], r_sm[i]], axis=0), kb[i], md["p"], _dot_nt) for i in each]
    a_ak = [pm[i][0:128, 0:128] * mask_sl for i in each]
    a_rr = [jnp.concatenate([pm[i][128:256, 0:128] * mask_l, pm[i][128:256, 128:256] * mask_l], axis=1) for i in each]
    x = [pm[i][0:128, 128:256] * mask_sl for i in each]
    t_inv = [eye + x[i] for i in each]
    for _ in range(5):
        x = [_mm(x[i], x[i], md["inv"]) for i in each]
        t_inv = [t_inv[i] + _mm(t_inv[i], x[i], md["inv"]) for i in each]
    av = [_mm(a_ak[i], v_sm[i], md["av"]) for i in each]
    wu = [_mm(t_inv[i], jnp.concatenate([a_sm[i], av[i]], axis=1), md["wu"]) for i in each]
    z = [jnp.concatenate([jnp.concatenate([jnp.zeros_like(v_sm[i]), v_sm[i]], axis=1), wu[i]], axis=0) for i in each]
    qy = [_mm(a_rr[i], z[i], md["qy"]) for i in each]
    mn = [_mm(kbh[i], z[i], md["mn"], _dot_tn) for i in each]
    n_pairs = len(hs)
    ys = []
    for c0 in range(0, n, n_pairs):
        idx = range(c0, c0 + n_pairs)
        y_sm = [_mm(r_sm[i] + qy[i][:, 0:128], hs[i - c0], md["y"]) + qy[i][:, 128:256] for i in idx]
        hs = [_mm(mn[i][:, 0:128] + eye * g_c[i], hs[i - c0], md["h"]) + mn[i][:, 128:256] for i in idx]
        ys += [y[0:CHUNK] + y[CHUNK:2 * CHUNK] for y in y_sm]
    return ys, hs


def _rwkv_kernel(z_ref, mu_ref, w0_ref, wup_ref, a0_ref, aup_ref, gup_ref, kk_ref, ka_ref, rk_ref,
                 gng_ref, gnb_ref, o_ref, prev_ref, h_ref, r_s, lw_s, k_s, v_s, kk_s, a_s, y_s, g_s, *, t):
    ti = pl.program_id(1)

    @pl.when(ti == 0)
    def _():
        prev_ref[...] = jnp.zeros_like(prev_ref)
        h_ref[...] = jnp.zeros_like(h_ref)

    consts = _rwkv_consts()
    head_ones = consts[3]

    z = z_ref[0]
    row = lax.broadcasted_iota(jnp.int32, (t, 1), 0)
    prev = jnp.where(row == 0, prev_ref[0:1, :], pltpu.roll(z, 1, 0))
    prev_ref[0:1, :] = z[t - 1:t, :]
    zs = z + (prev - z) * mu_ref[...]

    r = zs[:, 0:512]
    k = zs[:, 512:1024]
    v = zs[:, 1024:1536]
    wa = zs[:, 1536:1664]
    gl = zs[:, 1664:1792]
    u = w0_ref[...] + _dot(jnp.tanh(wa).astype(BF16), wup_ref[...])
    lw = (-math.exp(-0.5)) * _sigmoid(u)
    a = _sigmoid(a0_ref[...] + _dot(wa.astype(BF16), aup_ref[...]))
    g_s[...] = _dot(_sigmoid(gl).astype(BF16), gup_ref[...])
    kkr = k * kk_ref[...]
    k2 = k * (1.0 + (a - 1.0) * ka_ref[...])
    r_s[...] = r
    lw_s[...] = lw
    k_s[...] = k2
    v_s[...] = v
    a_s[...] = a
    for pr in range(4):
        sl = slice(pr * 128, (pr + 1) * 128)
        kp = kkr[:, sl]
        ss = _mm(kp * kp, head_ones, "exact_b")
        kk_s[:, sl] = kp * lax.rsqrt(jnp.maximum(ss, 1e-24))

    pairs = [slice(pr * 128, (pr + 1) * 128) for pr in range(4)]

    def chunk_body(c, carry):
        rows = [pl.ds(pl.multiple_of((c * RWKV_CHUNKS_PER_STEP + i) * CHUNK, CHUNK), CHUNK)
                for i in range(RWKV_CHUNKS_PER_STEP)]
        ys, hs = _rwkv_chunk(*[[ref[rw, sl] for rw in rows for sl in pairs]
                               for ref in (r_s, lw_s, k_s, v_s, kk_s, a_s)],
                             [h_ref[pr] for pr in range(4)], consts)
        for i, rw in enumerate(rows):
            for pr, sl in enumerate(pairs):
                y_s[rw, sl] = ys[i * 4 + pr]
        for pr in range(4):
            h_ref[pr] = hs[pr]
        return carry

    lax.fori_loop(0, t // (CHUNK * RWKV_CHUNKS_PER_STEP), chunk_body, 0)

    for pr in range(4):
        sl = slice(pr * 128, (pr + 1) * 128)
        y = y_s[:, sl]
        mean = _mm(y, head_ones, "exact_b") * (1.0 / HEAD_DIM)
        yc = y - mean
        var = _mm(yc * yc, head_ones, "exact_b") * (1.0 / HEAD_DIM)
        yn = yc * lax.rsqrt(var + RWKV_GN_EPS) * gng_ref[:, sl] + gnb_ref[:, sl]
        rp = r_s[:, sl]
        bonus = _mm(rp * k_s[:, sl] * rk_ref[:, sl], head_ones, "exact_b") * v_s[:, sl]
        o_ref[0, :, sl] = ((yn + bonus) * g_s[:, sl]).astype(o_ref.dtype)


def _rwkv_group(zr, mu, w0, w_up, a0, a_up, g_up, k_k, k_a, r_k, gn_g, gn_b):
    bsz, s, _ = zr.shape
    t = min(RWKV_T, s)
    wup_pad = jnp.concatenate([w_up, jnp.zeros_like(a_up)], axis=0).astype(BF16)
    aup_pad = jnp.concatenate([jnp.zeros_like(w_up), a_up], axis=0).astype(BF16)
    row = lambda p: p.reshape(1, -1)
    tile = pltpu.VMEM((t, RWKV_DIM), F32)
    return pl.pallas_call(
        functools.partial(_rwkv_kernel, t=t),
        grid=(bsz, s // t),
        in_specs=[
            pl.BlockSpec((1, t, RWKV_IN), lambda b, i: (b, i, 0)),
            _const_spec((1, RWKV_IN)), _const_spec((1, RWKV_DIM)), _const_spec((128, RWKV_DIM)),
            _const_spec((1, RWKV_DIM)), _const_spec((128, RWKV_DIM)), _const_spec((GATE_LORA, RWKV_DIM)),
            _const_spec((1, RWKV_DIM)), _const_spec((1, RWKV_DIM)), _const_spec((1, RWKV_DIM)),
            _const_spec((1, RWKV_DIM)), _const_spec((1, RWKV_DIM)),
        ],
        out_specs=pl.BlockSpec((1, t, RWKV_DIM), lambda b, i: (b, i, 0)),
        out_shape=jax.ShapeDtypeStruct((bsz, s, RWKV_DIM), BF16),
        scratch_shapes=[pltpu.VMEM((V7X_SUBLANES, RWKV_IN), F32), pltpu.VMEM((4, 128, 128), F32),
                        tile, tile, tile, tile, tile, tile, tile, tile],
        compiler_params=_params(("parallel", "arbitrary")),
        name="rwkv7",
    )(zr, row(mu), row(w0), wup_pad, row(a0), aup_pad, g_up.astype(BF16), row(k_k), row(k_a), row(r_k),
      row(gn_g), row(gn_b))


def _gelu_tanh(x):
    return 0.5 * x * (1.0 + jnp.tanh(math.sqrt(2.0 / math.pi) * (x + 0.044715 * (x * x * x))))


def _compress_kernel(x_ref, pe_ref, w1_ref, w2_ref, kc_ref, vc_ref, *, n_rows):
    half = CMP_LEN // 2
    for kv, o_ref in enumerate((kc_ref, vc_ref)):
        first = jnp.zeros((n_rows, 2 * CMP_HIDDEN), F32)
        second = jnp.zeros((n_rows, 2 * CMP_HIDDEN), F32)
        for p in range(half):
            xa = x_ref[0, kv, pl.ds(p, n_rows, stride=CMP_STRIDE), :]
            first += _dot((xa + pe_ref[kv, p:p + 1, :]).astype(BF16), w1_ref[kv, p])
            second += _dot((xa + pe_ref[kv, half + p:half + p + 1, :]).astype(BF16), w1_ref[kv, half + p])
        pre = first + pltpu.roll(second, n_rows - 1, 0)
        out = _dot(_gelu_tanh(pre).astype(BF16), w2_ref[kv])
        for g in range(NSA_GROUPS):
            o_ref[0, g, 0:CMP_PAD, :] = jnp.zeros((CMP_PAD, HEAD_DIM), o_ref.dtype)
            o_ref[0, g, CMP_PAD:CMP_PAD + n_rows, :] = out[:, g * HEAD_DIM:(g + 1) * HEAD_DIM].astype(o_ref.dtype)
            tail = o_ref.shape[2] - CMP_PAD - n_rows
            o_ref[0, g, CMP_PAD + n_rows:, :] = jnp.zeros((tail, HEAD_DIM), o_ref.dtype)


def _pair_diag(w):
    z = jnp.zeros_like(w)
    return jnp.concatenate([jnp.concatenate([w, z], axis=-1), jnp.concatenate([z, w], axis=-1)], axis=-2)


def _compress(kcvc, pe_k, w1_k, w2_k, pe_v, w1_v, w2_v):
    bsz, _, s, _ = kcvc.shape
    n_rows = s // CMP_STRIDE
    rows_out = CMP_PAD + n_rows + V7X_SUBLANES
    pe = jnp.stack([jnp.concatenate([pe_k, pe_k], axis=1), jnp.concatenate([pe_v, pe_v], axis=1)])
    w1 = jnp.stack([_pair_diag(w1_k.reshape(CMP_LEN, HEAD_DIM, CMP_HIDDEN)),
                    _pair_diag(w1_v.reshape(CMP_LEN, HEAD_DIM, CMP_HIDDEN))]).astype(BF16)
    w2 = jnp.stack([_pair_diag(w2_k), _pair_diag(w2_v)]).astype(BF16)
    shape = jax.ShapeDtypeStruct((bsz, NSA_GROUPS, rows_out, HEAD_DIM), BF16)
    spec = pl.BlockSpec((1, NSA_GROUPS, rows_out, HEAD_DIM), lambda b: (b, 0, 0, 0))
    return pl.pallas_call(
        functools.partial(_compress_kernel, n_rows=n_rows),
        grid=(bsz,),
        in_specs=[pl.BlockSpec((1, 2, s, 128), lambda b: (b, 0, 0, 0)), _const_spec(pe.shape), _const_spec(w1.shape),
                  _const_spec(w2.shape)],
        out_specs=(spec, spec),
        out_shape=(shape, shape),
        compiler_params=_params(("parallel",)),
        name="nsa_compress",
    )(kcvc, pe, w1, w2)


def _cmp_select_kernel(rb_ref, q_ref, kc_ref, vc_ref, tz_ref, gate_ref, oc_ref, sel_ref, *, n_far):
    qi = pl.program_id(1)
    qb = Q_BLOCK
    rowi = lax.broadcasted_iota(jnp.int32, (qb, 128), 0)
    lane = lax.broadcasted_iota(jnp.int32, (qb, 128), 1)
    row_f = rowi.astype(F32)
    sg = _sigmoid(gate_ref[0])
    near0 = pl.multiple_of(qi * 8, 8)
    n_first = qi * 8 - CMP_PAD
    for g in range(NSA_GROUPS):
        q4 = q_ref[0, g * NSA_HPG:(g + 1) * NSA_HPG, :, 0:HEAD_DIM].reshape(NSA_HPG * qb, HEAD_DIM)
        heads = [g * NSA_HPG + p for p in range(NSA_HPG)]
        cols = [slice(p * qb, (p + 1) * qb) for p in range(NSA_HPG)]
        tiles, values, n_of_lane = [], [], []
        for j in range(n_far):
            rows = slice(CMP_PAD + 128 * j, CMP_PAD + 128 * (j + 1))
            s = _dot_nt(kc_ref[0, g, rows, :], q4)
            ok = (128 * j + rowi) < n_first
            tiles.append(jnp.concatenate(
                [jnp.where(ok, s[:, cols[p]] + rb_ref[NUM_BUCKETS - 1, heads[p]], NEG) for p in range(NSA_HPG)], axis=1))
            values.append(vc_ref[0, g, rows, :])
            n_of_lane.append(128 * j + lane)
        s = _dot_nt(kc_ref[0, g, pl.ds(near0, 128), :], q4)
        ok = (n_first + rowi) >= 0
        tiles.append(jnp.concatenate(
            [jnp.where(ok, s[:, cols[p]] + tz_ref[heads[p], TZ_CMP_NEAR], NEG) for p in range(NSA_HPG)], axis=1))
        values.append(vc_ref[0, g, pl.ds(near0, 128), :])
        n_of_lane.append(n_first + lane)

        m = tiles[0].max(axis=0, keepdims=True)
        for tl in tiles[1:]:
            m = jnp.maximum(m, tl.max(axis=0, keepdims=True))
        es = [jnp.where(tl > 0.5 * NEG, jnp.exp(tl - m), 0.0) for tl in tiles]
        den = es[0].sum(axis=0, keepdims=True)
        for e in es[1:]:
            den = den + e.sum(axis=0, keepdims=True)
        inv = 1.0 / jnp.maximum(den, 1e-30)
        o_t = jnp.zeros((HEAD_DIM, NSA_HPG * qb), F32)
        imp = jnp.zeros((qb, 128), F32)
        for j, e in enumerate(es):
            pc = e * inv
            o_t = o_t + _dot_tn(values[j], pc.astype(BF16))
            psum = pc[:, cols[0]] + pc[:, cols[1]] + pc[:, cols[2]] + pc[:, cols[3]]
            ov = ((n_of_lane[j] >= 4 * rowi - 1) & (n_of_lane[j] <= 4 * rowi + 3)).astype(F32)
            imp = imp + _mm(ov, psum, "exact_a")
        for p, h in enumerate(heads):
            oc_ref[0, :, h * HEAD_DIM:(h + 1) * HEAD_DIM] = o_t[:, cols[p]].T * sg[:, 3 * h:3 * h + 1]
        cur = 2 * qi + (lane >= SLC_LEN).astype(jnp.int32)
        forced = (rowi == 0) | (rowi == cur) | (rowi == cur - 1)
        score = jnp.where(rowi <= cur, jnp.where(forced, -3e38, imp), -1.0)
        sel = jnp.where(forced & (rowi <= cur), 1.0, 0.0)
        for _ in range(N_SEL - 3):
            mx = score.max(axis=0, keepdims=True)
            idx = jnp.where(score == mx, row_f, 128.0).min(axis=0, keepdims=True)
            pick = row_f == idx
            sel = jnp.where(pick, 1.0, sel)
            score = jnp.where(pick, -3e38, score)
        sel_ref[0, g] = jnp.where(sel.T > 0.5, 0.0, NEG).astype(sel_ref.dtype)


def _cmp_select(rel_bias, q, kc, vc, tz, gate):
    bsz, _, s, _ = q.shape
    nq = s // Q_BLOCK
    n_far = max(0, -(-(8 * (nq - 1) - CMP_PAD) // 128))
    rows_c = kc.shape[2]
    return pl.pallas_call(
        functools.partial(_cmp_select_kernel, n_far=n_far),
        grid=(bsz, nq),
        in_specs=[
            pl.BlockSpec(memory_space=pltpu.SMEM),
            pl.BlockSpec((1, NSA_HEADS, Q_BLOCK, 128), lambda b, i: (b, 0, i, 0)),
            pl.BlockSpec((1, NSA_GROUPS, rows_c, HEAD_DIM), lambda b, i: (b, 0, 0, 0)),
            pl.BlockSpec((1, NSA_GROUPS, rows_c, HEAD_DIM), lambda b, i: (b, 0, 0, 0)),
            _const_spec(tz.shape),
            pl.BlockSpec((1, Q_BLOCK, 128), lambda b, i: (b, i, 0)),
        ],
        out_specs=(
            pl.BlockSpec((1, Q_BLOCK, NSA_DIM), lambda b, i: (b, i, 0)),
            pl.BlockSpec((1, NSA_GROUPS, Q_BLOCK, 128), lambda b, i: (b, 0, i, 0)),
        ),
        out_shape=(jax.ShapeDtypeStruct((bsz, s, NSA_DIM), F32),
                   jax.ShapeDtypeStruct((bsz, NSA_GROUPS, s, 128), BF16)),
        compiler_params=_params(("parallel", "parallel")),
        name="nsa_cmp_select",
    )(rel_bias, q, kc, vc, tz, gate)


SEL_TILES = 2


def _attn_scores(qa, ka, bias_fns, m_prev):
    qb = Q_BLOCK
    s = _dot_nt(ka, qa)
    if bias_fns is not None:
        s = jnp.concatenate(
            [jnp.concatenate([s[i * qb:(i + 1) * qb, p * qb:(p + 1) * qb] + fn(p) for p in range(NSA_HPG)], axis=1)
             for i, fn in enumerate(bias_fns)], axis=0)
    return s, jnp.maximum(m_prev, s.max(axis=0, keepdims=True))


def _attn_accumulate(s, vts, m_prev, m_next, acc_ref):
    pexp = jnp.exp((s - m_next).astype(BF16))
    acc_ref[...] = jnp.exp(m_prev - m_next) * acc_ref[...] + _dot(jnp.concatenate(vts, axis=1), pexp)


def _sel_win_kernel(q_ref, k_ref, vt_ref, e_ref, sel_ref, tz_ref, gate_ref, oc_ref, o_ref, m_ref, acc_ref,
                    s0_ref, s1_ref, mm0_ref, mm1_ref):
    qi = pl.program_id(1)
    qb = Q_BLOCK
    win_tiles = WINDOW // qb
    m_ref[...] = jnp.full(m_ref.shape, NEG, F32)
    acc_ref[...] = jnp.zeros(acc_ref.shape, F32)
    q4 = [q_ref[0, g * NSA_HPG:(g + 1) * NSA_HPG].reshape(NSA_HPG * qb, 128) for g in range(NSA_GROUPS)]
    selm = [jnp.concatenate([sel_ref[0, g]] * NSA_HPG, axis=0) for g in range(NSA_GROUPS)]

    def key_rows(kt, n=1):
        return pl.ds(pl.multiple_of(kt * qb, qb), n * qb)

    def bias_of(g, kind):
        return lambda p: tz_ref[g * NSA_HPG + p, kind]

    def keys_sel(g, kt, n=1):
        return jnp.concatenate([k_ref[0, g, key_rows(kt, n), :], e_ref[key_rows(kt, n), :]], axis=1)

    block = lax.broadcasted_iota(jnp.int32, (NSA_HPG * qb, 128), 1)

    n_far = (jnp.maximum(qi - 1, 0) + SEL_TILES - 1) // SEL_TILES
    n_run = n_far | 1
    last_tile0 = k_ref.shape[2] // qb - SEL_TILES

    def tile0(step):
        return jnp.minimum(step * SEL_TILES, last_tile0)

    def score_far(step, s_ref, mm_ref):
        for g in range(NSA_GROUPS):
            qa = jnp.concatenate(
                [q4[g], jnp.where((block >= 2 * (qi - 1)) | (step >= n_far), jnp.asarray(NEG, BF16), selm[g])], axis=1)
            m_prev = m_ref[g]
            s, m_next = _attn_scores(qa, keys_sel(g, tile0(step), SEL_TILES), None, m_prev)
            s_ref[g] = s
            mm_ref[g, 0:1, :] = m_prev
            mm_ref[g, 1:2, :] = m_next
            m_ref[g] = m_next

    def accumulate_far(step, s_ref, mm_ref):
        for g in range(NSA_GROUPS):
            _attn_accumulate(s_ref[g], [vt_ref[0, g, tile0(step) + i] for i in range(SEL_TILES)],
                             mm_ref[g, 0:1, :], mm_ref[g, 1:2, :], acc_ref.at[g])

    score_far(0, s0_ref, mm0_ref)

    def far_body(i, carry):
        score_far(2 * i + 1, s1_ref, mm1_ref)
        accumulate_far(2 * i, s0_ref, mm0_ref)
        score_far(2 * i + 2, s0_ref, mm0_ref)
        accumulate_far(2 * i + 1, s1_ref, mm1_ref)
        return carry

    lax.fori_loop(0, n_run // 2, far_body, 0)

    scored = []
    for g in range(NSA_GROUPS):
        kt_sub = jnp.maximum(qi - 1, 0)
        kinds = (jnp.where(qi >= 1, TZ_SUB, TZ_MASKED), TZ_DIAG)
        m_prev = m_ref[g]
        s, m_next = _attn_scores(jnp.concatenate([q4[g], selm[g]], axis=1),
                                 jnp.concatenate([keys_sel(g, kt_sub), keys_sel(g, qi)], axis=0),
                                 [bias_of(g, kind) for kind in kinds], m_prev)
        scored.append((g, s, [vt_ref[0, g, kt_sub], vt_ref[0, g, qi]], m_prev, m_next))
    for g in range(NSA_GROUPS):
        kw = NSA_GROUPS + g
        ks, vts, biases = [], [], []
        for d in range(win_tiles, -1, -1):
            kt = jnp.maximum(qi - d, 0)
            kind = TZ_DIAG if d == 0 else TZ_SUB if d == 1 else TZ_WIN_OLD if d == win_tiles else TZ_ZERO
            if d > 0:
                kind = jnp.where(qi >= d, kind, TZ_MASKED)
            ks.append(k_ref[0, kw, key_rows(kt), :])
            vts.append(vt_ref[0, kw, kt])
            biases.append(bias_of(g, kind))
        m_prev = m_ref[kw]
        s, m_next = _attn_scores(q4[g], jnp.concatenate(ks, axis=0), biases, m_prev)
        scored.append((kw, s, vts, m_prev, m_next))
    accumulate_far(n_run - 1, s0_ref, mm0_ref)
    for stream, s, vts, m_prev, m_next in scored:
        _attn_accumulate(s, vts, m_prev, m_next, acc_ref.at[stream])

    sg = _sigmoid(gate_ref[0])
    oc = oc_ref[0]
    for g in range(NSA_GROUPS):
        outs = []
        for stream in (g, NSA_GROUPS + g):
            acc = acc_ref[stream]
            outs.append(acc[0:HEAD_DIM] * (1.0 / acc[HEAD_DIM:HEAD_DIM + 1]))
        for p in range(NSA_HPG):
            h = g * NSA_HPG + p
            qcols = slice(p * qb, (p + 1) * qb)
            cols = slice(h * HEAD_DIM, (h + 1) * HEAD_DIM)
            y = (oc[:, cols] + sg[:, 3 * h + 1:3 * h + 2] * outs[0][:, qcols].T
                 + sg[:, 3 * h + 2:3 * h + 3] * outs[1][:, qcols].T)
            o_ref[0, :, cols] = y.astype(o_ref.dtype)


def _sel_win(q, k4, v4, sel, tz, gate, oc):
    bsz, _, s, _ = q.shape
    nq = s // Q_BLOCK
    assert nq % SEL_TILES == 0
    member = (jnp.arange(s)[:, None] // SLC_LEN == jnp.arange(128)[None, :]).astype(BF16)
    return pl.pallas_call(
        _sel_win_kernel,
        grid=(bsz, nq),
        in_specs=[
            pl.BlockSpec((1, NSA_HEADS, Q_BLOCK, 128), lambda b, i: (b, 0, i, 0)),
            pl.BlockSpec((1, 4, s, 128), lambda b, i: (b, 0, 0, 0)),
            pl.BlockSpec((1, 4, nq, VT_ROWS, Q_BLOCK), lambda b, i: (b, 0, 0, 0, 0)),
            _const_spec((s, 128)),
            pl.BlockSpec((1, NSA_GROUPS, Q_BLOCK, 128), lambda b, i: (b, 0, i, 0)),
            _const_spec(tz.shape),
            pl.BlockSpec((1, Q_BLOCK, 128), lambda b, i: (b, i, 0)),
            pl.BlockSpec((1, Q_BLOCK, NSA_DIM), lambda b, i: (b, i, 0)),
        ],
        out_specs=pl.BlockSpec((1, Q_BLOCK, NSA_DIM), lambda b, i: (b, i, 0)),
        out_shape=jax.ShapeDtypeStruct((bsz, s, NSA_DIM), BF16),
        scratch_shapes=[pltpu.VMEM((2 * NSA_GROUPS, 1, NSA_HPG * Q_BLOCK), F32),
                        pltpu.VMEM((2 * NSA_GROUPS, VT_ROWS, NSA_HPG * Q_BLOCK), F32),
                        pltpu.VMEM((NSA_GROUPS, SEL_TILES * Q_BLOCK, NSA_HPG * Q_BLOCK), F32),
                        pltpu.VMEM((NSA_GROUPS, SEL_TILES * Q_BLOCK, NSA_HPG * Q_BLOCK), F32),
                        pltpu.VMEM((NSA_GROUPS, 2, NSA_HPG * Q_BLOCK), F32),
                        pltpu.VMEM((NSA_GROUPS, 2, NSA_HPG * Q_BLOCK), F32)],
        compiler_params=_params(("parallel", "parallel")),
        name="nsa_sel_win",
    )(q, k4, v4, member, sel, tz, gate, oc)


OUT_TM = 512


def _outproj_kernel(x_ref, yr_ref, yn_ref, w_ref, g_ref, b_ref, o_ref):
    mixed = _dot(yr_ref[...], w_ref[0:RWKV_DIM, :]) + _dot(yn_ref[...], w_ref[RWKV_DIM:, :])
    o_ref[...] = _layer_norm(ALPHA * x_ref[...] + mixed, g_ref[...], b_ref[...])


def _out_projection(x2d, yr, yn, w_out, g, b):
    rows, d = x2d.shape
    tm = min(OUT_TM, rows)
    return pl.pallas_call(
        _outproj_kernel,
        grid=(rows // tm,),
        in_specs=[pl.BlockSpec((tm, d), lambda i: (i, 0)), pl.BlockSpec((tm, RWKV_DIM), lambda i: (i, 0)),
                  pl.BlockSpec((tm, NSA_DIM), lambda i: (i, 0)), _const_spec(w_out.shape),
                  _const_spec((1, d)), _const_spec((1, d))],
        out_specs=pl.BlockSpec((tm, d), lambda i: (i, 0)),
        out_shape=jax.ShapeDtypeStruct((rows, d), F32),
        compiler_params=_params(("parallel",)),
        name="out_projection_ln",
    )(x2d, yr, yn, w_out.astype(BF16), g.reshape(1, d), b.reshape(1, d))


XATTN_TM = 512


def _mem_kv_kernel(mem_ref, wk_ref, wv_ref, k_ref, v_ref):
    mb = mem_ref[0].astype(BF16)
    k_ref[0] = _dot(mb, wk_ref[...]).astype(BF16)
    v_ref[0] = _dot(mb, wv_ref[...]).astype(BF16)


def _mem_kv(mem, wk, wv):
    bsz, m, d = mem.shape
    shape = jax.ShapeDtypeStruct((bsz, m, d), BF16)
    spec = pl.BlockSpec((1, m, d), lambda b: (b, 0, 0))
    return pl.pallas_call(
        _mem_kv_kernel,
        grid=(bsz,),
        in_specs=[spec, _const_spec((d, d)), _const_spec((d, d))],
        out_specs=(spec, spec),
        out_shape=(shape, shape),
        compiler_params=_params(("parallel",)),
        name="xattn_mem_kv",
    )(mem, wk.astype(BF16), wv.astype(BF16))


def _xattn_kernel(x_ref, k_ref, v_ref, wq_ref, wo_ref, g_ref, b_ref, o_ref):
    x = x_ref[0]
    q = (_dot(x.astype(BF16), wq_ref[...]) * (XATTN_HEAD_DIM ** -0.5)).astype(BF16)
    heads = []
    for h in range(XATTN_HEADS):
        cols = slice(h * XATTN_HEAD_DIM, (h + 1) * XATTN_HEAD_DIM)
        s = _dot_nt(q[:, cols], k_ref[0, :, cols])
        e = jnp.exp(s - s.max(axis=1, keepdims=True))
        p = e * (1.0 / e.sum(axis=1, keepdims=True))
        heads.append(_dot(p.astype(BF16), v_ref[0, :, cols]).astype(BF16))
    o = _dot(jnp.concatenate(heads, axis=1), wo_ref[...])
    o_ref[0] = _layer_norm(ALPHA * x + o, g_ref[...], b_ref[...])


def _cross_attention_ln(x3d, k, v, wq, wo, g, b):
    bsz, s, d = x3d.shape
    m = k.shape[1]
    tm = min(XATTN_TM, s)
    return pl.pallas_call(
        _xattn_kernel,
        grid=(bsz, s // tm),
        in_specs=[pl.BlockSpec((1, tm, d), lambda bb, i: (bb, i, 0)),
                  pl.BlockSpec((1, m, d), lambda bb, i: (bb, 0, 0)),
                  pl.BlockSpec((1, m, d), lambda bb, i: (bb, 0, 0)),
                  _const_spec((d, d)), _const_spec((d, d)), _const_spec((1, d)), _const_spec((1, d))],
        out_specs=pl.BlockSpec((1, tm, d), lambda bb, i: (bb, i, 0)),
        out_shape=jax.ShapeDtypeStruct((bsz, s, d), F32),
        compiler_params=_params(("parallel", "parallel")),
        name="xattn_ln",
    )(x3d, k, v, wq.astype(BF16), wo.astype(BF16), g.reshape(1, d), b.reshape(1, d))


def _nsa_group(rel_bias, q, kcvc, k4, v4, gate, pe_k, w1_k, w2_k, pe_v, w1_v, w2_v):
    tz = _bias_tiles(rel_bias)
    kc, vc = _compress(kcvc, pe_k, w1_k, w2_k, pe_v, w1_v, w2_v)
    oc, sel = _cmp_select(rel_bias, q, kc, vc, tz, gate)
    return _sel_win(q, k4, v4, sel, tz, gate, oc)


def kernel(x, mem, ffn1_w_gate, ffn1_w_up, ffn1_w_down, ln1_g, ln1_b, mix_w_in, rwkv_mu, rwkv_w0, rwkv_w_up, rwkv_a0, rwkv_a_up, rwkv_g_up, rwkv_k_k, rwkv_k_a, rwkv_r_k, rwkv_gn_g, rwkv_gn_b, nsa_pe_k, nsa_w1_k, nsa_w2_k, nsa_pe_v, nsa_w1_v, nsa_w2_v, mix_w_out, ln2_g, ln2_b, xattn_wq, xattn_wk, xattn_wv, xattn_wo, ln3_g, ln3_b, ffn2_w_gate, ffn2_w_up, ffn2_w_down, ln4_g, ln4_b, rel_bias):
    bsz, s, d = x.shape
    rows = bsz * s
    for l in range(DEPTH):
        x1 = _ffn_ln(x.reshape(rows, d), ffn1_w_gate[l], ffn1_w_up[l], ffn1_w_down[l], ln1_g[l], ln1_b[l])
        zr, q, kcvc, k4, v4, gate = _in_projection(x1.reshape(bsz, s, d), mix_w_in[l], rel_bias)
        y_rwkv = _rwkv_group(zr, rwkv_mu[l], rwkv_w0[l], rwkv_w_up[l], rwkv_a0[l], rwkv_a_up[l], rwkv_g_up[l],
                             rwkv_k_k[l], rwkv_k_a[l], rwkv_r_k[l], rwkv_gn_g[l], rwkv_gn_b[l])
        y_nsa = _nsa_group(rel_bias, q, kcvc, k4, v4, gate, nsa_pe_k[l], nsa_w1_k[l], nsa_w2_k[l],
                           nsa_pe_v[l], nsa_w1_v[l], nsa_w2_v[l])
        x2 = _out_projection(x1, y_rwkv.reshape(rows, RWKV_DIM), y_nsa.reshape(rows, NSA_DIM), mix_w_out[l],
                             ln2_g[l], ln2_b[l])
        mk, mv = _mem_kv(mem, xattn_wk[l], xattn_wv[l])
        x3 = _cross_attention_ln(x2.reshape(bsz, s, d), mk, mv, xattn_wq[l], xattn_wo[l], ln3_g[l], ln3_b[l])
        x = _ffn_ln(x3.reshape(rows, d), ffn2_w_gate[l], ffn2_w_up[l], ffn2_w_down[l], ln4_g[l], ln4_b[l])
        x = x.reshape(bsz, s, d)
    return x
```

```python
import functools
import math

import numpy as np
import jax
import jax.numpy as jnp
from jax import lax
from jax.experimental import pallas as pl
from jax.experimental.pallas import tpu as pltpu

F32 = jnp.float32
BF16 = jnp.bfloat16
HIGHEST = lax.Precision.HIGHEST

D_MODEL = 1024
DEPTH = 1
RWKV_HEADS = 8
HEAD_DIM = 64
RWKV_DIM = RWKV_HEADS * HEAD_DIM
DECAY_LORA = 64
AAA_LORA = 64
GATE_LORA = 128
RWKV_IN = 3 * RWKV_DIM + DECAY_LORA + AAA_LORA + GATE_LORA
RWKV_GN_EPS = 64e-5
NSA_HEADS = 8
NSA_GROUPS = 2
NSA_HPG = NSA_HEADS // NSA_GROUPS
NSA_DIM = NSA_HEADS * HEAD_DIM
CMP_LEN = 32
CMP_STRIDE = 16
CMP_HIDDEN = 128
SLC_LEN = 64
N_SEL = 16
WINDOW = 512
Q_BLOCK = 128
FORCED_SCORE = 1e4
NUM_BUCKETS = 32
MAX_DISTANCE = 128
XATTN_HEADS = 4
XATTN_HEAD_DIM = D_MODEL // XATTN_HEADS
D_FF = 2816
LN_EPS = 1e-5
ALPHA = (2.0 * DEPTH) ** 0.25
NEG = -1e30

V7X_LANES = 128
V7X_SUBLANES = 8
V7X_VMEM_LIMIT_BYTES = 56 * 1024 * 1024

CHUNK = 64
CMP_PAD = 120
VT_ROWS = 80


def _dot(a, b, prec=None):
    return jnp.dot(a, b, preferred_element_type=F32, precision=prec)


def _dot_nt(a, b, prec=None):
    return lax.dot_general(a, b, (((1,), (1,)), ((), ())), preferred_element_type=F32, precision=prec)


def _dot_tn(a, b, prec=None):
    return lax.dot_general(a, b, (((0,), (0,)), ((), ())), preferred_element_type=F32, precision=prec)


def _sigmoid(x):
    return 1.0 / (1.0 + jnp.exp(-x))


def _layer_norm(y, g, b):
    mu = jnp.mean(y, axis=-1, keepdims=True)
    yc = y - mu
    var = jnp.mean(yc * yc, axis=-1, keepdims=True)
    return yc * lax.rsqrt(var + LN_EPS) * g + b


def _params(sem):
    return pltpu.CompilerParams(dimension_semantics=sem, vmem_limit_bytes=V7X_VMEM_LIMIT_BYTES)


def _const_spec(shape, single_buffer=False):
    nd = len(shape)
    if single_buffer:
        return pl.BlockSpec(shape, lambda *_: (0,) * nd, pipeline_mode=pl.Buffered(1))
    return pl.BlockSpec(shape, lambda *_: (0,) * nd)


def _bucket_thresholds():
    n = np.arange(0, 4 * MAX_DISTANCE)
    max_exact = NUM_BUCKETS // 2
    nf = np.maximum(n, max_exact).astype(np.float32)
    large = max_exact + (np.log(nf / np.float32(max_exact)) / np.float32(math.log(MAX_DISTANCE / max_exact))
                         * np.float32(NUM_BUCKETS - max_exact)).astype(np.int32)
    large = np.minimum(large, NUM_BUCKETS - 1)
    bucket = np.where(n < max_exact, n, large)
    return [int(np.argmax(bucket >= b)) for b in range(1, NUM_BUCKETS)]


_BUCKET_THR = _bucket_thresholds()


TZ_CMP_NEAR, TZ_DIAG, TZ_SUB, TZ_ZERO, TZ_MASKED, TZ_WIN_OLD, TZ_KINDS = 0, 1, 2, 3, 4, 5, 6


def _bias_tiles_kernel(rb_ref, o_ref):
    h = pl.program_id(0)
    r = lax.broadcasted_iota(jnp.int32, (Q_BLOCK, Q_BLOCK), 0)
    c = lax.broadcasted_iota(jnp.int32, (Q_BLOCK, Q_BLOCK), 1)
    far = jnp.full((Q_BLOCK, Q_BLOCK), rb_ref[NUM_BUCKETS - 1, h], F32)
    far_hi = far.astype(BF16).astype(F32)
    far_added = far_hi + (far - far_hi).astype(BF16).astype(F32)
    dists = (c - CMP_STRIDE * (r - CMP_PAD) - (CMP_LEN - 1), c - r, Q_BLOCK + c - r)
    for kind, dist in enumerate(dists):
        val = jnp.full((Q_BLOCK, Q_BLOCK), rb_ref[0, h], F32)
        for b in range(1, NUM_BUCKETS):
            val = jnp.where(dist >= _BUCKET_THR[b - 1], rb_ref[b, h], val)
        if kind != TZ_CMP_NEAR:
            val = val - far_added
        o_ref[0, kind] = jnp.where(dist >= 0, val, NEG)
    o_ref[0, TZ_ZERO] = far - far_added
    o_ref[0, TZ_MASKED] = jnp.full((Q_BLOCK, Q_BLOCK), NEG, F32)
    o_ref[0, TZ_WIN_OLD] = jnp.where(r > c, far - far_added, NEG)


def _bias_tiles(rel_bias):
    return pl.pallas_call(
        _bias_tiles_kernel,
        grid=(NSA_HEADS,),
        in_specs=[pl.BlockSpec(memory_space=pltpu.SMEM)],
        out_specs=pl.BlockSpec((1, TZ_KINDS, Q_BLOCK, Q_BLOCK), lambda h: (h, 0, 0, 0)),
        out_shape=jax.ShapeDtypeStruct((NSA_HEADS, TZ_KINDS, Q_BLOCK, Q_BLOCK), F32),
        compiler_params=_params(("arbitrary",)),
        name="bias_tiles",
    )(rel_bias)


FFN_TM = 1024
FFN_TF = 256


def _ffn_kernel(x_ref, wg_ref, wu_ref, wd_ref, g_ref, b_ref, o_ref, acc_ref):
    x = x_ref[...]
    xb = x.astype(BF16)
    for c in range(wg_ref.shape[1] // FFN_TF):
        cols = slice(c * FFN_TF, (c + 1) * FFN_TF)
        hg = _dot(xb, wg_ref[:, cols])
        hu = _dot(xb, wu_ref[:, cols])
        h = hg * _sigmoid(hg) * hu
        part = _dot(h.astype(BF16), wd_ref[cols, :])
        if c == 0:
            acc_ref[...] = part
        else:
            acc_ref[...] += part
    o_ref[...] = _layer_norm(ALPHA * x + 0.5 * acc_ref[...], g_ref[...], b_ref[...])


def _ffn_ln(x2d, wg, wu, wd, g, b):
    rows, d = x2d.shape
    f = wg.shape[1]
    assert f % FFN_TF == 0
    tm = min(FFN_TM, rows)
    return pl.pallas_call(
        _ffn_kernel,
        grid=(rows // tm,),
        in_specs=[
            pl.BlockSpec((tm, d), lambda i: (i, 0)),
            _const_spec((d, f), single_buffer=True),
            _const_spec((d, f), single_buffer=True),
            _const_spec((f, d), single_buffer=True),
            _const_spec((1, d)),
            _const_spec((1, d)),
        ],
        out_specs=pl.BlockSpec((tm, d), lambda i: (i, 0)),
        out_shape=jax.ShapeDtypeStruct((rows, d), F32),
        scratch_shapes=[pltpu.VMEM((tm, d), F32)],
        compiler_params=_params(("parallel",)),
        name="ffn_ln",
    )(x2d, wg.astype(BF16), wu.astype(BF16), wd.astype(BF16), g.reshape(1, d), b.reshape(1, d))


PROJ_TM = 512
_C_RWKV = 0
_C_Q = _C_RWKV + RWKV_IN
_C_KCVC = _C_Q + NSA_HEADS * 128
_C_K = _C_KCVC + 256
_C_V = _C_K + 4 * 128
_C_GATE = _C_V + 4 * 128
_C_END = _C_GATE + 128
BIAS_LANES = (HEAD_DIM, HEAD_DIM + 1)


def _inproj_kernel(x_ref, w_ref, qx_ref, zr_ref, q_ref, kcvc_ref, k_ref, v_ref, gate_ref):
    xb = x_ref[0].astype(BF16)
    lane = lax.broadcasted_iota(jnp.int32, (1, V7X_LANES), 1)
    zr_ref[0] = _dot(xb, w_ref[:, _C_RWKV:_C_Q])
    zq = _dot(xb, w_ref[:, _C_Q:_C_KCVC]) * (HEAD_DIM ** -0.5)
    for h in range(NSA_HEADS):
        q_ref[0, h] = (zq[:, h * 128:(h + 1) * 128] + qx_ref[h:h + 1, :]).astype(BF16)
    zc = _dot(xb, w_ref[:, _C_KCVC:_C_K])
    kcvc_ref[0, 0] = zc[:, 0:128]
    kcvc_ref[0, 1] = zc[:, 128:256]
    bias_ones = ((lane == BIAS_LANES[0]) | (lane == BIAS_LANES[1])).astype(F32)
    zk = _dot(xb, w_ref[:, _C_K:_C_V])
    for j in range(4):
        k_ref[0, j] = (zk[:, j * 128:(j + 1) * 128] + bias_ones).astype(BF16)
    one_lane = (lane == HEAD_DIM).astype(F32)
    zvg = _dot(xb, w_ref[:, _C_V:_C_END])
    for j in range(4):
        zv = zvg[:, j * 128:(j + 1) * 128] + one_lane
        for i in range(zv.shape[0] // Q_BLOCK):
            v_ref[0, j, i] = zv[i * Q_BLOCK:(i + 1) * Q_BLOCK].T[0:VT_ROWS].astype(BF16)
    gate_ref[0] = zvg[:, 4 * 128:5 * 128]


def _pack_w_in(w_in):
    d = w_in.shape[0]
    o = RWKV_IN
    q = w_in[:, o:o + 512]
    kc = w_in[:, o + 512:o + 640]
    vc = w_in[:, o + 640:o + 768]
    ks = w_in[:, o + 768:o + 896]
    vs = w_in[:, o + 896:o + 1024]
    kw = w_in[:, o + 1024:o + 1152]
    vw = w_in[:, o + 1152:o + 1280]
    gate = w_in[:, o + 1280:o + 1304]
    z64 = jnp.zeros((d, 64), w_in.dtype)
    pad = lambda m, n: [jnp.concatenate([m[:, i * 64:(i + 1) * 64], z64], axis=1) for i in range(n)]
    gate_pad = jnp.concatenate([gate, jnp.zeros((d, 128 - gate.shape[1]), w_in.dtype)], axis=1)
    cols = ([w_in[:, :o]] + pad(q, NSA_HEADS) + [kc, vc] + pad(ks, NSA_GROUPS) + pad(kw, NSA_GROUPS)
            + pad(vs, NSA_GROUPS) + pad(vw, NSA_GROUPS) + [gate_pad])
    return jnp.concatenate(cols, axis=1).astype(BF16)


def _far_bias_lanes(rel_bias):
    far = rel_bias[NUM_BUCKETS - 1, :]
    hi = far.astype(BF16).astype(F32)
    lane = jnp.arange(V7X_LANES)[None, :]
    return jnp.where(lane == BIAS_LANES[0], hi[:, None], jnp.where(lane == BIAS_LANES[1], (far - hi)[:, None], 0.0))


def _in_projection(x3d, w_in, rel_bias):
    bsz, s, d = x3d.shape
    tm = min(PROJ_TM, s)
    wp = _pack_w_in(w_in)
    out_shape = (
        jax.ShapeDtypeStruct((bsz, s, RWKV_IN), F32),
        jax.ShapeDtypeStruct((bsz, NSA_HEADS, s, 128), BF16),
        jax.ShapeDtypeStruct((bsz, 2, s, 128), F32),
        jax.ShapeDtypeStruct((bsz, 4, s, 128), BF16),
        jax.ShapeDtypeStruct((bsz, 4, s // Q_BLOCK, VT_ROWS, Q_BLOCK), BF16),
        jax.ShapeDtypeStruct((bsz, s, 128), F32),
    )
    return pl.pallas_call(
        _inproj_kernel,
        grid=(bsz, s // tm),
        in_specs=[pl.BlockSpec((1, tm, d), lambda b, i: (b, i, 0)), _const_spec((d, _C_END)),
                  _const_spec((NSA_HEADS, 128))],
        out_specs=(
            pl.BlockSpec((1, tm, RWKV_IN), lambda b, i: (b, i, 0)),
            pl.BlockSpec((1, NSA_HEADS, tm, 128), lambda b, i: (b, 0, i, 0)),
            pl.BlockSpec((1, 2, tm, 128), lambda b, i: (b, 0, i, 0)),
            pl.BlockSpec((1, 4, tm, 128), lambda b, i: (b, 0, i, 0)),
            pl.BlockSpec((1, 4, tm // Q_BLOCK, VT_ROWS, Q_BLOCK), lambda b, i: (b, 0, i, 0, 0)),
            pl.BlockSpec((1, tm, 128), lambda b, i: (b, i, 0)),
        ),
        out_shape=out_shape,
        compiler_params=_params(("parallel", "parallel")),
        name="in_projection",
    )(x3d, wp, _far_bias_lanes(rel_bias))


RWKV_T = 512
RWKV_CHUNKS_PER_STEP = 4


def _rwkv_consts():
    r = lax.broadcasted_iota(jnp.int32, (128, 128), 0)
    c = lax.broadcasted_iota(jnp.int32, (128, 128), 1)
    same = (r >= CHUNK) == (c >= CHUNK)
    mask_sl = (same & (r > c)).astype(F32)
    mask_l = (same & (r >= c)).astype(F32)
    eye = (r == c).astype(F32)
    head_ones = same.astype(F32)
    rt = lax.broadcasted_iota(jnp.int32, (CHUNK, CHUNK), 0)
    ct = lax.broadcasted_iota(jnp.int32, (CHUNK, CHUNK), 1)
    tri = (rt >= ct).astype(F32)
    lane = lax.broadcasted_iota(jnp.int32, (1, 128), 1)
    m0 = (lane < CHUNK).astype(F32)
    m1 = 1.0 - m0
    return mask_sl, mask_l, eye, head_ones, tri, m0, m1


def _split2(x):
    hi = x.astype(BF16)
    return hi, (x - hi.astype(F32)).astype(BF16)


def _mm(a, b, mode, dot=_dot):
    if mode == "bf16":
        return dot(a.astype(BF16), b.astype(BF16))
    if mode == "bf16x3":
        ah, al = _split2(a)
        bh, bl = _split2(b)
        return dot(ah, bh) + (dot(ah, bl) + dot(al, bh))
    if mode in ("exact_a", "exact_b"):
        x = b if mode == "exact_a" else a
        hi, rest = x.astype(BF16), None
        rest = x - hi.astype(F32)
        mid = rest.astype(BF16)
        lo = (rest - mid.astype(F32)).astype(BF16)
        if mode == "exact_a":
            ab = a.astype(BF16)
            return dot(ab, hi) + (dot(ab, mid) + dot(ab, lo))
        bb = b.astype(BF16)
        return dot(hi, bb) + (dot(mid, bb) + dot(lo, bb))
    raise ValueError(mode)


RWKV_MODES = dict(p="bf16", inv="bf16", av="bf16", wu="bf16", qy="bf16", mn="bf16", y="bf16", h="bf16")


def _rwkv_chunk(rs, lws, ks, vs, kks, as_, hs, consts):
    mask_sl, mask_l, eye, _, tri, m0, m1 = consts
    md = RWKV_MODES
    n = len(rs)
    each = range(n)

    def sm(x):
        return jnp.concatenate([x * m0, x * m1], axis=0)

    def dup(x):
        return jnp.concatenate([x, x], axis=0)

    cums = [_mm(tri, lws[i], "exact_a") for i in each]
    a_sm, r_sm, v_sm, kb, kbh, g_c = [], [], [], [], [], []
    for i in each:
        cum, lw, kk, k = cums[i], lws[i], kks[i], ks[i]
        cl = cum[CHUNK - 1:CHUNK, :]
        ka = kk * as_[i]
        g_tail = jnp.exp(cl - cum)
        g_inv = jnp.exp(-cum)
        a_sm.append(sm(-(kk * jnp.exp(cum - lw))))
        r_sm.append(sm(rs[i] * jnp.exp(cum)))
        v_sm.append(sm(vs[i]))
        kb.append(jnp.concatenate([dup(k * g_inv), dup(ka * g_inv)], axis=0))
        kbh.append(jnp.concatenate([sm(k * g_tail), sm(ka * g_tail)], axis=0))
        g_c.append(jnp.exp(cl))
    pm = [_mm(jnp.concatenate([a_sm[i], r_sm[i]], axis=0), kb[i], md["p"], _dot_nt) for i in each]
    a_ak = [pm[i][0:128, 0:128] * mask_sl for i in each]
    a_rr = [jnp.concatenate([pm[i][128:256, 0:128] * mask_l, pm[i][128:256, 128:256] * mask_l], axis=1) for i in each]
    x = [pm[i][0:128, 128:256] * mask_sl for i in each]
    t_inv = [eye + x[i] for i in each]
    for _ in range(5):
        x = [_mm(x[i], x[i], md["inv"]) for i in each]
        t_inv = [t_inv[i] + _mm(t_inv[i], x[i], md["inv"]) for i in each]
    av = [_mm(a_ak[i], v_sm[i], md["av"]) for i in each]
    wu = [_mm(t_inv[i], jnp.concatenate([a_sm[i], av[i]], axis=1), md["wu"]) for i in each]
    z = [jnp.concatenate([jnp.concatenate([jnp.zeros_like(v_sm[i]), v_sm[i]], axis=1), wu[i]], axis=0) for i in each]
    qy = [_mm(a_rr[i], z[i], md["qy"]) for i in each]
    mn = [_mm(kbh[i], z[i], md["mn"], _dot_tn) for i in each]
    n_pairs = len(hs)
    ys = []
    for c0 in range(0, n, n_pairs):
        idx = range(c0, c0 + n_pairs)
        y_sm = [_mm(r_sm[i] + qy[i][:, 0:128], hs[i - c0], md["y"]) + qy[i][:, 128:256] for i in idx]
        hs = [_mm(mn[i][:, 0:128] + eye * g_c[i], hs[i - c0], md["h"]) + mn[i][:, 128:256] for i in idx]
        ys += [y[0:CHUNK] + y[CHUNK:2 * CHUNK] for y in y_sm]
    return ys, hs


def _rwkv_kernel(z_ref, mu_ref, w0_ref, wup_ref, a0_ref, aup_ref, gup_ref, kk_ref, ka_ref, rk_ref,
                 gng_ref, gnb_ref, o_ref, prev_ref, h_ref, r_s, lw_s, k_s, v_s, kk_s, a_s, y_s, g_s, *, t):
    ti = pl.program_id(1)

    @pl.when(ti == 0)
    def _():
        prev_ref[...] = jnp.zeros_like(prev_ref)
        h_ref[...] = jnp.zeros_like(h_ref)

    consts = _rwkv_consts()
    head_ones = consts[3]

    z = z_ref[0]
    row = lax.broadcasted_iota(jnp.int32, (t, 1), 0)
    prev = jnp.where(row == 0, prev_ref[0:1, :], pltpu.roll(z, 1, 0))
    prev_ref[0:1, :] = z[t - 1:t, :]
    zs = z + (prev - z) * mu_ref[...]

    r = zs[:, 0:512]
    k = zs[:, 512:1024]
    v = zs[:, 1024:1536]
    wa = zs[:, 1536:1664]
    gl = zs[:, 1664:1792]
    u = w0_ref[...] + _dot(jnp.tanh(wa).astype(BF16), wup_ref[...])
    lw = (-math.exp(-0.5)) * _sigmoid(u)
    a = _sigmoid(a0_ref[...] + _dot(wa.astype(BF16), aup_ref[...]))
    g_s[...] = _dot(_sigmoid(gl).astype(BF16), gup_ref[...])
    kkr = k * kk_ref[...]
    k2 = k * (1.0 + (a - 1.0) * ka_ref[...])
    r_s[...] = r
    lw_s[...] = lw
    k_s[...] = k2
    v_s[...] = v
    a_s[...] = a
    for pr in range(4):
        sl = slice(pr * 128, (pr + 1) * 128)
        kp = kkr[:, sl]
        ss = _mm(kp * kp, head_ones, "exact_b")
        kk_s[:, sl] = kp * lax.rsqrt(jnp.maximum(ss, 1e-24))

    pairs = [slice(pr * 128, (pr + 1) * 128) for pr in range(4)]

    def chunk_body(c, carry):
        rows = [pl.ds(pl.multiple_of((c * RWKV_CHUNKS_PER_STEP + i) * CHUNK, CHUNK), CHUNK)
                for i in range(RWKV_CHUNKS_PER_STEP)]
        ys, hs = _rwkv_chunk(*[[ref[rw, sl] for rw in rows for sl in pairs]
                               for ref in (r_s, lw_s, k_s, v_s, kk_s, a_s)],
                             [h_ref[pr] for pr in range(4)], consts)
        for i, rw in enumerate(rows):
            for pr, sl in enumerate(pairs):
                y_s[rw, sl] = ys[i * 4 + pr]
        for pr in range(4):
            h_ref[pr] = hs[pr]
        return carry

    lax.fori_loop(0, t // (CHUNK * RWKV_CHUNKS_PER_STEP), chunk_body, 0)

    for pr in range(4):
        sl = slice(pr * 128, (pr + 1) * 128)
        y = y_s[:, sl]
        mean = _mm(y, head_ones, "exact_b") * (1.0 / HEAD_DIM)
        yc = y - mean
        var = _mm(yc * yc, head_ones, "exact_b") * (1.0 / HEAD_DIM)
        yn = yc * lax.rsqrt(var + RWKV_GN_EPS) * gng_ref[:, sl] + gnb_ref[:, sl]
        rp = r_s[:, sl]
        bonus = _mm(rp * k_s[:, sl] * rk_ref[:, sl], head_ones, "exact_b") * v_s[:, sl]
        o_ref[0, :, sl] = ((yn + bonus) * g_s[:, sl]).astype(o_ref.dtype)


def _rwkv_group(zr, mu, w0, w_up, a0, a_up, g_up, k_k, k_a, r_k, gn_g, gn_b):
    bsz, s, _ = zr.shape
    t = min(RWKV_T, s)
    wup_pad = jnp.concatenate([w_up, jnp.zeros_like(a_up)], axis=0).astype(BF16)
    aup_pad = jnp.concatenate([jnp.zeros_like(w_up), a_up], axis=0).astype(BF16)
    row = lambda p: p.reshape(1, -1)
    tile = pltpu.VMEM((t, RWKV_DIM), F32)
    return pl.pallas_call(
        functools.partial(_rwkv_kernel, t=t),
        grid=(bsz, s // t),
        in_specs=[
            pl.BlockSpec((1, t, RWKV_IN), lambda b, i: (b, i, 0)),
            _const_spec((1, RWKV_IN)), _const_spec((1, RWKV_DIM)), _const_spec((128, RWKV_DIM)),
            _const_spec((1, RWKV_DIM)), _const_spec((128, RWKV_DIM)), _const_spec((GATE_LORA, RWKV_DIM)),
            _const_spec((1, RWKV_DIM)), _const_spec((1, RWKV_DIM)), _const_spec((1, RWKV_DIM)),
            _const_spec((1, RWKV_DIM)), _const_spec((1, RWKV_DIM)),
        ],
        out_specs=pl.BlockSpec((1, t, RWKV_DIM), lambda b, i: (b, i, 0)),
        out_shape=jax.ShapeDtypeStruct((bsz, s, RWKV_DIM), BF16),
        scratch_shapes=[pltpu.VMEM((V7X_SUBLANES, RWKV_IN), F32), pltpu.VMEM((4, 128, 128), F32),
                        tile, tile, tile, tile, tile, tile, tile, tile],
        compiler_params=_params(("parallel", "arbitrary")),
        name="rwkv7",
    )(zr, row(mu), row(w0), wup_pad, row(a0), aup_pad, g_up.astype(BF16), row(k_k), row(k_a), row(r_k),
      row(gn_g), row(gn_b))


def _gelu_tanh(x):
    return 0.5 * x * (1.0 + jnp.tanh(math.sqrt(2.0 / math.pi) * (x + 0.044715 * (x * x * x))))


def _compress_kernel(x_ref, pe_ref, w1_ref, w2_ref, kc_ref, vc_ref, *, n_rows):
    half = CMP_LEN // 2
    for kv, o_ref in enumerate((kc_ref, vc_ref)):
        first = jnp.zeros((n_rows, 2 * CMP_HIDDEN), F32)
        second = jnp.zeros((n_rows, 2 * CMP_HIDDEN), F32)
        for p in range(half):
            xa = x_ref[0, kv, pl.ds(p, n_rows, stride=CMP_STRIDE), :]
            first += _dot((xa + pe_ref[kv, p:p + 1, :]).astype(BF16), w1_ref[kv, p])
            second += _dot((xa + pe_ref[kv, half + p:half + p + 1, :]).astype(BF16), w1_ref[kv, half + p])
        pre = first + pltpu.roll(second, n_rows - 1, 0)
        out = _dot(_gelu_tanh(pre).astype(BF16), w2_ref[kv])
        for g in range(NSA_GROUPS):
            o_ref[0, g, 0:CMP_PAD, :] = jnp.zeros((CMP_PAD, HEAD_DIM), o_ref.dtype)
            o_ref[0, g, CMP_PAD:CMP_PAD + n_rows, :] = out[:, g * HEAD_DIM:(g + 1) * HEAD_DIM].astype(o_ref.dtype)
            tail = o_ref.shape[2] - CMP_PAD - n_rows
            o_ref[0, g, CMP_PAD + n_rows:, :] = jnp.zeros((tail, HEAD_DIM), o_ref.dtype)


def _pair_diag(w):
    z = jnp.zeros_like(w)
    return jnp.concatenate([jnp.concatenate([w, z], axis=-1), jnp.concatenate([z, w], axis=-1)], axis=-2)


def _compress(kcvc, pe_k, w1_k, w2_k, pe_v, w1_v, w2_v):
    bsz, _, s, _ = kcvc.shape
    n_rows = s // CMP_STRIDE
    rows_out = CMP_PAD + n_rows + V7X_SUBLANES
    pe = jnp.stack([jnp.concatenate([pe_k, pe_k], axis=1), jnp.concatenate([pe_v, pe_v], axis=1)])
    w1 = jnp.stack([_pair_diag(w1_k.reshape(CMP_LEN, HEAD_DIM, CMP_HIDDEN)),
                    _pair_diag(w1_v.reshape(CMP_LEN, HEAD_DIM, CMP_HIDDEN))]).astype(BF16)
    w2 = jnp.stack([_pair_diag(w2_k), _pair_diag(w2_v)]).astype(BF16)
    shape = jax.ShapeDtypeStruct((bsz, NSA_GROUPS, rows_out, HEAD_DIM), BF16)
    spec = pl.BlockSpec((1, NSA_GROUPS, rows_out, HEAD_DIM), lambda b: (b, 0, 0, 0))
    return pl.pallas_call(
        functools.partial(_compress_kernel, n_rows=n_rows),
        grid=(bsz,),
        in_specs=[pl.BlockSpec((1, 2, s, 128), lambda b: (b, 0, 0, 0)), _const_spec(pe.shape), _const_spec(w1.shape),
                  _const_spec(w2.shape)],
        out_specs=(spec, spec),
        out_shape=(shape, shape),
        compiler_params=_params(("parallel",)),
        name="nsa_compress",
    )(kcvc, pe, w1, w2)


def _cmp_select_kernel(rb_ref, q_ref, kc_ref, vc_ref, tz_ref, gate_ref, oc_ref, sel_ref, *, n_far):
    qi = pl.program_id(1)
    qb = Q_BLOCK
    rowi = lax.broadcasted_iota(jnp.int32, (qb, 128), 0)
    lane = lax.broadcasted_iota(jnp.int32, (qb, 128), 1)
    row_f = rowi.astype(F32)
    sg = _sigmoid(gate_ref[0])
    near0 = pl.multiple_of(qi * 8, 8)
    n_first = qi * 8 - CMP_PAD
    for g in range(NSA_GROUPS):
        q4 = q_ref[0, g * NSA_HPG:(g + 1) * NSA_HPG, :, 0:HEAD_DIM].reshape(NSA_HPG * qb, HEAD_DIM)
        heads = [g * NSA_HPG + p for p in range(NSA_HPG)]
        cols = [slice(p * qb, (p + 1) * qb) for p in range(NSA_HPG)]
        tiles, values, n_of_lane = [], [], []
        for j in range(n_far):
            rows = slice(CMP_PAD + 128 * j, CMP_PAD + 128 * (j + 1))
            s = _dot_nt(kc_ref[0, g, rows, :], q4)
            ok = (128 * j + rowi) < n_first
            tiles.append(jnp.concatenate(
                [jnp.where(ok, s[:, cols[p]] + rb_ref[NUM_BUCKETS - 1, heads[p]], NEG) for p in range(NSA_HPG)], axis=1))
            values.append(vc_ref[0, g, rows, :])
            n_of_lane.append(128 * j + lane)
        s = _dot_nt(kc_ref[0, g, pl.ds(near0, 128), :], q4)
        ok = (n_first + rowi) >= 0
        tiles.append(jnp.concatenate(
            [jnp.where(ok, s[:, cols[p]] + tz_ref[heads[p], TZ_CMP_NEAR], NEG) for p in range(NSA_HPG)], axis=1))
        values.append(vc_ref[0, g, pl.ds(near0, 128), :])
        n_of_lane.append(n_first + lane)

        m = tiles[0].max(axis=0, keepdims=True)
        for tl in tiles[1:]:
            m = jnp.maximum(m, tl.max(axis=0, keepdims=True))
        es = [jnp.where(tl > 0.5 * NEG, jnp.exp(tl - m), 0.0) for tl in tiles]
        den = es[0].sum(axis=0, keepdims=True)
        for e in es[1:]:
            den = den + e.sum(axis=0, keepdims=True)
        inv = 1.0 / jnp.maximum(den, 1e-30)
        o_t = jnp.zeros((HEAD_DIM, NSA_HPG * qb), F32)
        imp = jnp.zeros((qb, 128), F32)
        for j, e in enumerate(es):
            pc = e * inv
            o_t = o_t + _dot_tn(values[j], pc.astype(BF16))
            psum = pc[:, cols[0]] + pc[:, cols[1]] + pc[:, cols[2]] + pc[:, cols[3]]
            ov = ((n_of_lane[j] >= 4 * rowi - 1) & (n_of_lane[j] <= 4 * rowi + 3)).astype(F32)
            imp = imp + _mm(ov, psum, "exact_a")
        for p, h in enumerate(heads):
            oc_ref[0, :, h * HEAD_DIM:(h + 1) * HEAD_DIM] = o_t[:, cols[p]].T * sg[:, 3 * h:3 * h + 1]
        cur = 2 * qi + (lane >= SLC_LEN).astype(jnp.int32)
        forced = (rowi == 0) | (rowi == cur) | (rowi == cur - 1)
        score = jnp.where(rowi <= cur, jnp.where(forced, -3e38, imp), -1.0)
        sel = jnp.where(forced & (rowi <= cur), 1.0, 0.0)
        for _ in range(N_SEL - 3):
            mx = score.max(axis=0, keepdims=True)
            idx = jnp.where(score == mx, row_f, 128.0).min(axis=0, keepdims=True)
            pick = row_f == idx
            sel = jnp.where(pick, 1.0, sel)
            score = jnp.where(pick, -3e38, score)
        sel_ref[0, g] = jnp.where(sel.T > 0.5, 0.0, NEG).astype(sel_ref.dtype)


def _cmp_select(rel_bias, q, kc, vc, tz, gate):
    bsz, _, s, _ = q.shape
    nq = s // Q_BLOCK
    n_far = max(0, -(-(8 * (nq - 1) - CMP_PAD) // 128))
    rows_c = kc.shape[2]
    return pl.pallas_call(
        functools.partial(_cmp_select_kernel, n_far=n_far),
        grid=(bsz, nq),
        in_specs=[
            pl.BlockSpec(memory_space=pltpu.SMEM),
            pl.BlockSpec((1, NSA_HEADS, Q_BLOCK, 128), lambda b, i: (b, 0, i, 0)),
            pl.BlockSpec((1, NSA_GROUPS, rows_c, HEAD_DIM), lambda b, i: (b, 0, 0, 0)),
            pl.BlockSpec((1, NSA_GROUPS, rows_c, HEAD_DIM), lambda b, i: (b, 0, 0, 0)),
            _const_spec(tz.shape),
            pl.BlockSpec((1, Q_BLOCK, 128), lambda b, i: (b, i, 0)),
        ],
        out_specs=(
            pl.BlockSpec((1, Q_BLOCK, NSA_DIM), lambda b, i: (b, i, 0)),
            pl.BlockSpec((1, NSA_GROUPS, Q_BLOCK, 128), lambda b, i: (b, 0, i, 0)),
        ),
        out_shape=(jax.ShapeDtypeStruct((bsz, s, NSA_DIM), F32),
                   jax.ShapeDtypeStruct((bsz, NSA_GROUPS, s, 128), BF16)),
        compiler_params=_params(("parallel", "parallel")),
        name="nsa_cmp_select",
    )(rel_bias, q, kc, vc, tz, gate)


SEL_TILES = 4


def _attn_scores(qa, ka, bias_fns, m_prev):
    qb = Q_BLOCK
    s = _dot_nt(ka, qa)
    if bias_fns is not None:
        s = jnp.concatenate(
            [jnp.concatenate([s[i * qb:(i + 1) * qb, p * qb:(p + 1) * qb] + fn(p) for p in range(NSA_HPG)], axis=1)
             for i, fn in enumerate(bias_fns)], axis=0)
    return s, jnp.maximum(m_prev, s.max(axis=0, keepdims=True))


def _attn_accumulate(s, vts, m_prev, m_next, acc_ref):
    pexp = jnp.exp(s - m_next)
    acc_ref[...] = jnp.exp(m_prev - m_next) * acc_ref[...] + _dot(jnp.concatenate(vts, axis=1), pexp.astype(BF16))


def _sel_win_kernel(q_ref, k_ref, vt_ref, e_ref, sel_ref, tz_ref, gate_ref, oc_ref, o_ref, m_ref, acc_ref,
                    s0_ref, s1_ref, mm0_ref, mm1_ref):
    qi = pl.program_id(1)
    qb = Q_BLOCK
    win_tiles = WINDOW // qb
    m_ref[...] = jnp.full(m_ref.shape, NEG, F32)
    acc_ref[...] = jnp.zeros(acc_ref.shape, F32)
    q4 = [q_ref[0, g * NSA_HPG:(g + 1) * NSA_HPG].reshape(NSA_HPG * qb, 128) for g in range(NSA_GROUPS)]
    selm = [jnp.concatenate([sel_ref[0, g]] * NSA_HPG, axis=0) for g in range(NSA_GROUPS)]

    def key_rows(kt, n=1):
        return pl.ds(pl.multiple_of(kt * qb, qb), n * qb)

    def bias_of(g, kind):
        return lambda p: tz_ref[g * NSA_HPG + p, kind]

    def keys_sel(g, kt, n=1):
        return jnp.concatenate([k_ref[0, g, key_rows(kt, n), :], e_ref[key_rows(kt, n), :]], axis=1)

    block = lax.broadcasted_iota(jnp.int32, (NSA_HPG * qb, 128), 1)

    n_far = (jnp.maximum(qi - 1, 0) + SEL_TILES - 1) // SEL_TILES
    n_run = n_far | 1
    last_tile0 = k_ref.shape[2] // qb - SEL_TILES

    def tile0(step):
        return jnp.minimum(step * SEL_TILES, last_tile0)

    def score_far(step, s_ref, mm_ref):
        for g in range(NSA_GROUPS):
            qa = jnp.concatenate(
                [q4[g], jnp.where((block >= 2 * (qi - 1)) | (step >= n_far), jnp.asarray(NEG, BF16), selm[g])], axis=1)
            m_prev = m_ref[g]
            s, m_next = _attn_scores(qa, keys_sel(g, tile0(step), SEL_TILES), None, m_prev)
            s_ref[g] = s
            mm_ref[g, 0:1, :] = m_prev
            mm_ref[g, 1:2, :] = m_next
            m_ref[g] = m_next

    def accumulate_far(step, s_ref, mm_ref):
        for g in range(NSA_GROUPS):
            _attn_accumulate(s_ref[g], [vt_ref[0, g, tile0(step) + i] for i in range(SEL_TILES)],
                             mm_ref[g, 0:1, :], mm_ref[g, 1:2, :], acc_ref.at[g])

    score_far(0, s0_ref, mm0_ref)

    def far_body(i, carry):
        score_far(2 * i + 1, s1_ref, mm1_ref)
        accumulate_far(2 * i, s0_ref, mm0_ref)
        score_far(2 * i + 2, s0_ref, mm0_ref)
        accumulate_far(2 * i + 1, s1_ref, mm1_ref)
        return carry

    lax.fori_loop(0, n_run // 2, far_body, 0)

    scored = []
    for g in range(NSA_GROUPS):
        kt_sub = jnp.maximum(qi - 1, 0)
        kinds = (jnp.where(qi >= 1, TZ_SUB, TZ_MASKED), TZ_DIAG)
        m_prev = m_ref[g]
        s, m_next = _attn_scores(jnp.concatenate([q4[g], selm[g]], axis=1),
                                 jnp.concatenate([keys_sel(g, kt_sub), keys_sel(g, qi)], axis=0),
                                 [bias_of(g, kind) for kind in kinds], m_prev)
        scored.append((g, s, [vt_ref[0, g, kt_sub], vt_ref[0, g, qi]], m_prev, m_next))
    for g in range(NSA_GROUPS):
        kw = NSA_GROUPS + g
        ks, vts, biases = [], [], []
        for d in range(win_tiles, -1, -1):
            kt = jnp.maximum(qi - d, 0)
            kind = TZ_DIAG if d == 0 else TZ_SUB if d == 1 else TZ_WIN_OLD if d == win_tiles else TZ_ZERO
            if d > 0:
                kind = jnp.where(qi >= d, kind, TZ_MASKED)
            ks.append(k_ref[0, kw, key_rows(kt), :])
            vts.append(vt_ref[0, kw, kt])
            biases.append(bias_of(g, kind))
        m_prev = m_ref[kw]
        s, m_next = _attn_scores(q4[g], jnp.concatenate(ks, axis=0), biases, m_prev)
        scored.append((kw, s, vts, m_prev, m_next))
    accumulate_far(n_run - 1, s0_ref, mm0_ref)
    for stream, s, vts, m_prev, m_next in scored:
        _attn_accumulate(s, vts, m_prev, m_next, acc_ref.at[stream])

    sg = _sigmoid(gate_ref[0])
    oc = oc_ref[0]
    for g in range(NSA_GROUPS):
        outs = []
        for stream in (g, NSA_GROUPS + g):
            acc = acc_ref[stream]
            outs.append(acc[0:HEAD_DIM] * (1.0 / acc[HEAD_DIM:HEAD_DIM + 1]))
        for p in range(NSA_HPG):
            h = g * NSA_HPG + p
            qcols = slice(p * qb, (p + 1) * qb)
            cols = slice(h * HEAD_DIM, (h + 1) * HEAD_DIM)
            y = (oc[:, cols] + sg[:, 3 * h + 1:3 * h + 2] * outs[0][:, qcols].T
                 + sg[:, 3 * h + 2:3 * h + 3] * outs[1][:, qcols].T)
            o_ref[0, :, cols] = y.astype(o_ref.dtype)


def _sel_win(q, k4, v4, sel, tz, gate, oc):
    bsz, _, s, _ = q.shape
    nq = s // Q_BLOCK
    assert nq % SEL_TILES == 0
    member = (jnp.arange(s)[:, None] // SLC_LEN == jnp.arange(128)[None, :]).astype(BF16)
    return pl.pallas_call(
        _sel_win_kernel,
        grid=(bsz, nq),
        in_specs=[
            pl.BlockSpec((1, NSA_HEADS, Q_BLOCK, 128), lambda b, i: (b, 0, i, 0)),
            pl.BlockSpec((1, 4, s, 128), lambda b, i: (b, 0, 0, 0)),
            pl.BlockSpec((1, 4, nq, VT_ROWS, Q_BLOCK), lambda b, i: (b, 0, 0, 0, 0)),
            _const_spec((s, 128)),
            pl.BlockSpec((1, NSA_GROUPS, Q_BLOCK, 128), lambda b, i: (b, 0, i, 0)),
            _const_spec(tz.shape),
            pl.BlockSpec((1, Q_BLOCK, 128), lambda b, i: (b, i, 0)),
            pl.BlockSpec((1, Q_BLOCK, NSA_DIM), lambda b, i: (b, i, 0)),
        ],
        out_specs=pl.BlockSpec((1, Q_BLOCK, NSA_DIM), lambda b, i: (b, i, 0)),
        out_shape=jax.ShapeDtypeStruct((bsz, s, NSA_DIM), BF16),
        scratch_shapes=[pltpu.VMEM((2 * NSA_GROUPS, 1, NSA_HPG * Q_BLOCK), F32),
                        pltpu.VMEM((2 * NSA_GROUPS, VT_ROWS, NSA_HPG * Q_BLOCK), F32),
                        pltpu.VMEM((NSA_GROUPS, SEL_TILES * Q_BLOCK, NSA_HPG * Q_BLOCK), F32),
                        pltpu.VMEM((NSA_GROUPS, SEL_TILES * Q_BLOCK, NSA_HPG * Q_BLOCK), F32),
                        pltpu.VMEM((NSA_GROUPS, 2, NSA_HPG * Q_BLOCK), F32),
                        pltpu.VMEM((NSA_GROUPS, 2, NSA_HPG * Q_BLOCK), F32)],
        compiler_params=_params(("parallel", "parallel")),
        name="nsa_sel_win",
    )(q, k4, v4, member, sel, tz, gate, oc)


OUT_TM = 1024


def _outproj_kernel(x_ref, yr_ref, yn_ref, w_ref, g_ref, b_ref, o_ref):
    mixed = _dot(yr_ref[...], w_ref[0:RWKV_DIM, :]) + _dot(yn_ref[...], w_ref[RWKV_DIM:, :])
    o_ref[...] = _layer_norm(ALPHA * x_ref[...] + mixed, g_ref[...], b_ref[...])


def _out_projection(x2d, yr, yn, w_out, g, b):
    rows, d = x2d.shape
    tm = min(OUT_TM, rows)
    return pl.pallas_call(
        _outproj_kernel,
        grid=(rows // tm,),
        in_specs=[pl.BlockSpec((tm, d), lambda i: (i, 0)), pl.BlockSpec((tm, RWKV_DIM), lambda i: (i, 0)),
                  pl.BlockSpec((tm, NSA_DIM), lambda i: (i, 0)), _const_spec(w_out.shape),
                  _const_spec((1, d)), _const_spec((1, d))],
        out_specs=pl.BlockSpec((tm, d), lambda i: (i, 0)),
        out_shape=jax.ShapeDtypeStruct((rows, d), F32),
        compiler_params=_params(("parallel",)),
        name="out_projection_ln",
    )(x2d, yr, yn, w_out.astype(BF16), g.reshape(1, d), b.reshape(1, d))


XATTN_TM = 1024


def _mem_kv_kernel(mem_ref, wk_ref, wv_ref, k_ref, v_ref):
    mb = mem_ref[0].astype(BF16)
    k_ref[0] = _dot(mb, wk_ref[...]).astype(BF16)
    v_ref[0] = _dot(mb, wv_ref[...]).astype(BF16)


def _mem_kv(mem, wk, wv):
    bsz, m, d = mem.shape
    shape = jax.ShapeDtypeStruct((bsz, m, d), BF16)
    spec = pl.BlockSpec((1, m, d), lambda b: (b, 0, 0))
    return pl.pallas_call(
        _mem_kv_kernel,
        grid=(bsz,),
        in_specs=[spec, _const_spec((d, d)), _const_spec((d, d))],
        out_specs=(spec, spec),
        out_shape=(shape, shape),
        compiler_params=_params(("parallel",)),
        name="xattn_mem_kv",
    )(mem, wk.astype(BF16), wv.astype(BF16))


def _xattn_kernel(x_ref, k_ref, v_ref, wq_ref, wo_ref, g_ref, b_ref, o_ref):
    x = x_ref[0]
    q = (_dot(x.astype(BF16), wq_ref[...]) * (XATTN_HEAD_DIM ** -0.5)).astype(BF16)
    heads = []
    for h in range(XATTN_HEADS):
        cols = slice(h * XATTN_HEAD_DIM, (h + 1) * XATTN_HEAD_DIM)
        s = _dot_nt(q[:, cols], k_ref[0, :, cols])
        e = jnp.exp(s - s.max(axis=1, keepdims=True))
        p = e * (1.0 / e.sum(axis=1, keepdims=True))
        heads.append(_dot(p.astype(BF16), v_ref[0, :, cols]).astype(BF16))
    o = _dot(jnp.concatenate(heads, axis=1), wo_ref[...])
    o_ref[0] = _layer_norm(ALPHA * x + o, g_ref[...], b_ref[...])


def _cross_attention_ln(x3d, k, v, wq, wo, g, b):
    bsz, s, d = x3d.shape
    m = k.shape[1]
    tm = min(XATTN_TM, s)
    return pl.pallas_call(
        _xattn_kernel,
        grid=(bsz, s // tm),
        in_specs=[pl.BlockSpec((1, tm, d), lambda bb, i: (bb, i, 0)),
                  pl.BlockSpec((1, m, d), lambda bb, i: (bb, 0, 0)),
                  pl.BlockSpec((1, m, d), lambda bb, i: (bb, 0, 0)),
                  _const_spec((d, d)), _const_spec((d, d)), _const_spec((1, d)), _const_spec((1, d))],
        out_specs=pl.BlockSpec((1, tm, d), lambda bb, i: (bb, i, 0)),
        out_shape=jax.ShapeDtypeStruct((bsz, s, d), F32),
        compiler_params=_params(("parallel", "parallel")),
        name="xattn_ln",
    )(x3d, k, v, wq.astype(BF16), wo.astype(BF16), g.reshape(1, d), b.reshape(1, d))


def _nsa_group(rel_bias, q, kcvc, k4, v4, gate, pe_k, w1_k, w2_k, pe_v, w1_v, w2_v):
    tz = _bias_tiles(rel_bias)
    kc, vc = _compress(kcvc, pe_k, w1_k, w2_k, pe_v, w1_v, w2_v)
    oc, sel = _cmp_select(rel_bias, q, kc, vc, tz, gate)
    return _sel_win(q, k4, v4, sel, tz, gate, oc)


def kernel(x, mem, ffn1_w_gate, ffn1_w_up, ffn1_w_down, ln1_g, ln1_b, mix_w_in, rwkv_mu, rwkv_w0, rwkv_w_up, rwkv_a0, rwkv_a_up, rwkv_g_up, rwkv_k_k, rwkv_k_a, rwkv_r_k, rwkv_gn_g, rwkv_gn_b, nsa_pe_k, nsa_w1_k, nsa_w2_k, nsa_pe_v, nsa_w1_v, nsa_w2_v, mix_w_out, ln2_g, ln2_b, xattn_wq, xattn_wk, xattn_wv, xattn_wo, ln3_g, ln3_b, ffn2_w_gate, ffn2_w_up, ffn2_w_down, ln4_g, ln4_b, rel_bias):
    bsz, s, d = x.shape
    rows = bsz * s
    for l in range(DEPTH):
        x1 = _ffn_ln(x.reshape(rows, d), ffn1_w_gate[l], ffn1_w_up[l], ffn1_w_down[l], ln1_g[l], ln1_b[l])
        zr, q, kcvc, k4, v4, gate = _in_projection(x1.reshape(bsz, s, d), mix_w_in[l], rel_bias)
        y_rwkv = _rwkv_group(zr, rwkv_mu[l], rwkv_w0[l], rwkv_w_up[l], rwkv_a0[l], rwkv_a_up[l], rwkv_g_up[l],
                             rwkv_k_k[l], rwkv_k_a[l], rwkv_r_k[l], rwkv_gn_g[l], rwkv_gn_b[l])
        y_nsa = _nsa_group(rel_bias, q, kcvc, k4, v4, gate, nsa_pe_k[l], nsa_w1_k[l], nsa_w2_k[l],
                           nsa_pe_v[l], nsa_w1_v[l], nsa_w2_v[l])
        x2 = _out_projection(x1, y_rwkv.reshape(rows, RWKV_DIM), y_nsa.reshape(rows, NSA_DIM), mix_w_out[l],
                             ln2_g[l], ln2_b[l])
        mk, mv = _mem_kv(mem, xattn_wk[l], xattn_wv[l])
        x3 = _cross_attention_ln(x2.reshape(bsz, s, d), mk, mv, xattn_wq[l], xattn_wo[l], ln3_g[l], ln3_b[l])
        x = _ffn_ln(x3.reshape(rows, d), ffn2_w_gate[l], ffn2_w_up[l], ffn2_w_down[l], ln4_g[l], ln4_b[l])
        x = x.reshape(bsz, s, d)
    return x
```

```python
import functools
import math

import numpy as np
import jax
import jax.numpy as jnp
from jax import lax
from jax.experimental import pallas as pl
from jax.experimental.pallas import tpu as pltpu

F32 = jnp.float32
BF16 = jnp.bfloat16
HIGHEST = lax.Precision.HIGHEST

D_MODEL = 1024
DEPTH = 1
RWKV_HEADS = 8
HEAD_DIM = 64
RWKV_DIM = RWKV_HEADS * HEAD_DIM
DECAY_LORA = 64
AAA_LORA = 64
GATE_LORA = 128
RWKV_IN = 3 * RWKV_DIM + DECAY_LORA + AAA_LORA + GATE_LORA
RWKV_GN_EPS = 64e-5
NSA_HEADS = 8
NSA_GROUPS = 2
NSA_HPG = NSA_HEADS // NSA_GROUPS
NSA_DIM = NSA_HEADS * HEAD_DIM
CMP_LEN = 32
CMP_STRIDE = 16
CMP_HIDDEN = 128
SLC_LEN = 64
N_SEL = 16
WINDOW = 512
Q_BLOCK = 128
FORCED_SCORE = 1e4
NUM_BUCKETS = 32
MAX_DISTANCE = 128
XATTN_HEADS = 4
XATTN_HEAD_DIM = D_MODEL // XATTN_HEADS
D_FF = 2816
LN_EPS = 1e-5
ALPHA = (2.0 * DEPTH) ** 0.25
NEG = -1e30

V7X_LANES = 128
V7X_SUBLANES = 8
V7X_VMEM_LIMIT_BYTES = 56 * 1024 * 1024

CHUNK = 64
CMP_PAD = 120
VT_ROWS = 80


def _dot(a, b, prec=None):
    return jnp.dot(a, b, preferred_element_type=F32, precision=prec)


def _dot_nt(a, b, prec=None):
    return lax.dot_general(a, b, (((1,), (1,)), ((), ())), preferred_element_type=F32, precision=prec)


def _dot_tn(a, b, prec=None):
    return lax.dot_general(a, b, (((0,), (0,)), ((), ())), preferred_element_type=F32, precision=prec)


def _sigmoid(x):
    return 1.0 / (1.0 + jnp.exp(-x))


def _layer_norm(y, g, b):
    mu = jnp.mean(y, axis=-1, keepdims=True)
    yc = y - mu
    var = jnp.mean(yc * yc, axis=-1, keepdims=True)
    return yc * lax.rsqrt(var + LN_EPS) * g + b


def _params(sem):
    return pltpu.CompilerParams(dimension_semantics=sem, vmem_limit_bytes=V7X_VMEM_LIMIT_BYTES)


def _const_spec(shape, single_buffer=False):
    nd = len(shape)
    if single_buffer:
        return pl.BlockSpec(shape, lambda *_: (0,) * nd, pipeline_mode=pl.Buffered(1))
    return pl.BlockSpec(shape, lambda *_: (0,) * nd)


def _bucket_thresholds():
    n = np.arange(0, 4 * MAX_DISTANCE)
    max_exact = NUM_BUCKETS // 2
    nf = np.maximum(n, max_exact).astype(np.float32)
    large = max_exact + (np.log(nf / np.float32(max_exact)) / np.float32(math.log(MAX_DISTANCE / max_exact))
                         * np.float32(NUM_BUCKETS - max_exact)).astype(np.int32)
    large = np.minimum(large, NUM_BUCKETS - 1)
    bucket = np.where(n < max_exact, n, large)
    return [int(np.argmax(bucket >= b)) for b in range(1, NUM_BUCKETS)]


_BUCKET_THR = _bucket_thresholds()


TZ_CMP_NEAR, TZ_DIAG, TZ_SUB, TZ_ZERO, TZ_MASKED, TZ_WIN_OLD, TZ_KINDS = 0, 1, 2, 3, 4, 5, 6


def _bias_tiles_kernel(rb_ref, o_ref):
    h = pl.program_id(0)
    r = lax.broadcasted_iota(jnp.int32, (Q_BLOCK, Q_BLOCK), 0)
    c = lax.broadcasted_iota(jnp.int32, (Q_BLOCK, Q_BLOCK), 1)
    far = jnp.full((Q_BLOCK, Q_BLOCK), rb_ref[NUM_BUCKETS - 1, h], F32)
    far_hi = far.astype(BF16).astype(F32)
    far_added = far_hi + (far - far_hi).astype(BF16).astype(F32)
    dists = (c - CMP_STRIDE * (r - CMP_PAD) - (CMP_LEN - 1), c - r, Q_BLOCK + c - r)
    for kind, dist in enumerate(dists):
        val = jnp.full((Q_BLOCK, Q_BLOCK), rb_ref[0, h], F32)
        for b in range(1, NUM_BUCKETS):
            val = jnp.where(dist >= _BUCKET_THR[b - 1], rb_ref[b, h], val)
        o_ref[0, kind] = jnp.where(dist >= 0, val - far_added, NEG)
    o_ref[0, TZ_ZERO] = far - far_added
    o_ref[0, TZ_MASKED] = jnp.full((Q_BLOCK, Q_BLOCK), NEG, F32)
    o_ref[0, TZ_WIN_OLD] = jnp.where(r > c, far - far_added, NEG)


def _bias_tiles(rel_bias):
    return pl.pallas_call(
        _bias_tiles_kernel,
        grid=(NSA_HEADS,),
        in_specs=[pl.BlockSpec(memory_space=pltpu.SMEM)],
        out_specs=pl.BlockSpec((1, TZ_KINDS, Q_BLOCK, Q_BLOCK), lambda h: (h, 0, 0, 0)),
        out_shape=jax.ShapeDtypeStruct((NSA_HEADS, TZ_KINDS, Q_BLOCK, Q_BLOCK), F32),
        compiler_params=_params(("arbitrary",)),
        name="bias_tiles",
    )(rel_bias)


FFN_TM = 1024
FFN_TF = 256


def _ffn_kernel(x_ref, wg_ref, wu_ref, wd_ref, g_ref, b_ref, o_ref, acc_ref):
    x = x_ref[...]
    xb = x.astype(BF16)
    for c in range(wg_ref.shape[1] // FFN_TF):
        cols = slice(c * FFN_TF, (c + 1) * FFN_TF)
        hg = _dot(xb, wg_ref[:, cols])
        hu = _dot(xb, wu_ref[:, cols])
        h = hg * _sigmoid(hg) * hu
        part = _dot(h.astype(BF16), wd_ref[cols, :])
        if c == 0:
            acc_ref[...] = part
        else:
            acc_ref[...] += part
    o_ref[...] = _layer_norm(ALPHA * x + 0.5 * acc_ref[...], g_ref[...], b_ref[...])


def _ffn_ln(x2d, wg, wu, wd, g, b):
    rows, d = x2d.shape
    f = wg.shape[1]
    assert f % FFN_TF == 0
    tm = min(FFN_TM, rows)
    return pl.pallas_call(
        _ffn_kernel,
        grid=(rows // tm,),
        in_specs=[
            pl.BlockSpec((tm, d), lambda i: (i, 0)),
            _const_spec((d, f), single_buffer=True),
            _const_spec((d, f), single_buffer=True),
            _const_spec((f, d), single_buffer=True),
            _const_spec((1, d)),
            _const_spec((1, d)),
        ],
        out_specs=pl.BlockSpec((tm, d), lambda i: (i, 0)),
        out_shape=jax.ShapeDtypeStruct((rows, d), F32),
        scratch_shapes=[pltpu.VMEM((tm, d), F32)],
        compiler_params=_params(("parallel",)),
        name="ffn_ln",
    )(x2d, wg.astype(BF16), wu.astype(BF16), wd.astype(BF16), g.reshape(1, d), b.reshape(1, d))


PROJ_TM = 512
_C_RWKV = 0
_C_Q = _C_RWKV + RWKV_IN
_C_KCVC = _C_Q + NSA_HEADS * 128
_C_K = _C_KCVC + 256
_C_V = _C_K + 4 * 128
_C_GATE = _C_V + 4 * 128
_C_END = _C_GATE + 128
BIAS_LANES = (HEAD_DIM, HEAD_DIM + 1)
MASK_LANE = HEAD_DIM + 2


def _inproj_kernel(x_ref, w_ref, qx_ref, zr_ref, q_ref, kcvc_ref, k_ref, v_ref, gate_ref):
    xb = x_ref[0].astype(BF16)
    lane = lax.broadcasted_iota(jnp.int32, (1, V7X_LANES), 1)
    zr_ref[0] = _dot(xb, w_ref[:, _C_RWKV:_C_Q])
    zq = _dot(xb, w_ref[:, _C_Q:_C_KCVC]) * (HEAD_DIM ** -0.5)
    for h in range(NSA_HEADS):
        q_ref[0, h] = (zq[:, h * 128:(h + 1) * 128] + qx_ref[h:h + 1, :]).astype(BF16)
    zc = _dot(xb, w_ref[:, _C_KCVC:_C_K])
    kcvc_ref[0, 0] = zc[:, 0:128]
    kcvc_ref[0, 1] = zc[:, 128:256]
    bias_ones = ((lane == BIAS_LANES[0]) | (lane == BIAS_LANES[1])).astype(F32)
    zk = _dot(xb, w_ref[:, _C_K:_C_V])
    for j in range(4):
        k_ref[0, j] = (zk[:, j * 128:(j + 1) * 128] + bias_ones).astype(BF16)
    one_lane = (lane == HEAD_DIM).astype(F32)
    zvg = _dot(xb, w_ref[:, _C_V:_C_END])
    for j in range(4):
        zv = zvg[:, j * 128:(j + 1) * 128] + one_lane
        for i in range(zv.shape[0] // Q_BLOCK):
            v_ref[0, j, i] = zv[i * Q_BLOCK:(i + 1) * Q_BLOCK].T[0:VT_ROWS].astype(BF16)
    gate_ref[0] = zvg[:, 4 * 128:5 * 128]


def _pack_w_in(w_in):
    d = w_in.shape[0]
    o = RWKV_IN
    q = w_in[:, o:o + 512]
    kc = w_in[:, o + 512:o + 640]
    vc = w_in[:, o + 640:o + 768]
    ks = w_in[:, o + 768:o + 896]
    vs = w_in[:, o + 896:o + 1024]
    kw = w_in[:, o + 1024:o + 1152]
    vw = w_in[:, o + 1152:o + 1280]
    gate = w_in[:, o + 1280:o + 1304]
    z64 = jnp.zeros((d, 64), w_in.dtype)
    pad = lambda m, n: [jnp.concatenate([m[:, i * 64:(i + 1) * 64], z64], axis=1) for i in range(n)]
    gate_pad = jnp.concatenate([gate, jnp.zeros((d, 128 - gate.shape[1]), w_in.dtype)], axis=1)
    cols = ([w_in[:, :o]] + pad(q, NSA_HEADS) + [kc, vc] + pad(ks, NSA_GROUPS) + pad(kw, NSA_GROUPS)
            + pad(vs, NSA_GROUPS) + pad(vw, NSA_GROUPS) + [gate_pad])
    return jnp.concatenate(cols, axis=1).astype(BF16)


def _far_bias_lanes(rel_bias):
    far = rel_bias[NUM_BUCKETS - 1, :]
    hi = far.astype(BF16).astype(F32)
    lane = jnp.arange(V7X_LANES)[None, :]
    return jnp.where(lane == BIAS_LANES[0], hi[:, None],
                     jnp.where(lane == BIAS_LANES[1], (far - hi)[:, None], jnp.where(lane == MASK_LANE, 1.0, 0.0)))


def _in_projection(x3d, w_in, rel_bias):
    bsz, s, d = x3d.shape
    tm = min(PROJ_TM, s)
    wp = _pack_w_in(w_in)
    out_shape = (
        jax.ShapeDtypeStruct((bsz, s, RWKV_IN), F32),
        jax.ShapeDtypeStruct((bsz, NSA_HEADS, s, 128), BF16),
        jax.ShapeDtypeStruct((bsz, 2, s, 128), F32),
        jax.ShapeDtypeStruct((bsz, 4, s, 128), BF16),
        jax.ShapeDtypeStruct((bsz, 4, s // Q_BLOCK, VT_ROWS, Q_BLOCK), BF16),
        jax.ShapeDtypeStruct((bsz, s, 128), F32),
    )
    return pl.pallas_call(
        _inproj_kernel,
        grid=(bsz, s // tm),
        in_specs=[pl.BlockSpec((1, tm, d), lambda b, i: (b, i, 0)), _const_spec((d, _C_END)),
                  _const_spec((NSA_HEADS, 128))],
        out_specs=(
            pl.BlockSpec((1, tm, RWKV_IN), lambda b, i: (b, i, 0)),
            pl.BlockSpec((1, NSA_HEADS, tm, 128), lambda b, i: (b, 0, i, 0)),
            pl.BlockSpec((1, 2, tm, 128), lambda b, i: (b, 0, i, 0)),
            pl.BlockSpec((1, 4, tm, 128), lambda b, i: (b, 0, i, 0)),
            pl.BlockSpec((1, 4, tm // Q_BLOCK, VT_ROWS, Q_BLOCK), lambda b, i: (b, 0, i, 0, 0)),
            pl.BlockSpec((1, tm, 128), lambda b, i: (b, i, 0)),
        ),
        out_shape=out_shape,
        compiler_params=_params(("parallel", "parallel")),
        name="in_projection",
    )(x3d, wp, _far_bias_lanes(rel_bias))


RWKV_T = 512
RWKV_CHUNKS_PER_STEP = 4


def _rwkv_consts():
    r = lax.broadcasted_iota(jnp.int32, (128, 128), 0)
    c = lax.broadcasted_iota(jnp.int32, (128, 128), 1)
    same = (r >= CHUNK) == (c >= CHUNK)
    mask_sl = (same & (r > c)).astype(F32)
    mask_l = (same & (r >= c)).astype(F32)
    eye = (r == c).astype(F32)
    head_ones = same.astype(F32)
    rt = lax.broadcasted_iota(jnp.int32, (CHUNK, CHUNK), 0)
    ct = lax.broadcasted_iota(jnp.int32, (CHUNK, CHUNK), 1)
    tri = (rt >= ct).astype(F32)
    lane = lax.broadcasted_iota(jnp.int32, (1, 128), 1)
    m0 = (lane < CHUNK).astype(F32)
    m1 = 1.0 - m0
    return mask_sl, mask_l, eye, head_ones, tri, m0, m1


def _split2(x):
    hi = x.astype(BF16)
    return hi, (x - hi.astype(F32)).astype(BF16)


def _mm(a, b, mode, dot=_dot):
    if mode == "bf16":
        return dot(a.astype(BF16), b.astype(BF16))
    if mode == "bf16x3":
        ah, al = _split2(a)
        bh, bl = _split2(b)
        return dot(ah, bh) + (dot(ah, bl) + dot(al, bh))
    if mode in ("exact_a", "exact_b"):
        x = b if mode == "exact_a" else a
        hi, rest = x.astype(BF16), None
        rest = x - hi.astype(F32)
        mid = rest.astype(BF16)
        lo = (rest - mid.astype(F32)).astype(BF16)
        if mode == "exact_a":
            ab = a.astype(BF16)
            return dot(ab, hi) + (dot(ab, mid) + dot(ab, lo))
        bb = b.astype(BF16)
        return dot(hi, bb) + (dot(mid, bb) + dot(lo, bb))
    raise ValueError(mode)


RWKV_MODES = dict(p="bf16", inv="bf16", av="bf16", wu="bf16", qy="bf16", mn="bf16", y="bf16", h="bf16")


def _rwkv_chunk(rs, lws, ks, vs, kks, as_, hs, consts):
    mask_sl, mask_l, eye, _, tri, m0, m1 = consts
    md = RWKV_MODES
    n = len(rs)
    each = range(n)

    def sm(x):
        return jnp.concatenate([x * m0, x * m1], axis=0)

    def dup(x):
        return jnp.concatenate([x, x], axis=0)

    cums = [_mm(tri, lws[i], "exact_a") for i in each]
    a_sm, r_sm, v_sm, kb, kbh, g_c = [], [], [], [], [], []
    for i in each:
        cum, lw, kk, k = cums[i], lws[i], kks[i], ks[i]
        cl = cum[CHUNK - 1:CHUNK, :]
        ka = kk * as_[i]
        g_tail = jnp.exp(cl - cum)
        g_inv = jnp.exp(-cum)
        a_sm.append(sm(-(kk * jnp.exp(cum - lw))))
        r_sm.append(sm(rs[i] * jnp.exp(cum)))
        v_sm.append(sm(vs[i]))
        kb.append(jnp.concatenate([dup(k * g_inv), dup(ka * g_inv)], axis=0))
        kbh.append(jnp.concatenate([sm(k * g_tail), sm(ka * g_tail)], axis=0))
        g_c.append(jnp.exp(cl))
    pm = [_mm(jnp.concatenate([a_sm[i], r_sm[i]], axis=0), kb[i], md["p"], _dot_nt) for i in each]
    a_ak = [pm[i][0:128, 0:128] * mask_sl for i in each]
    a_rr = [jnp.concatenate([pm[i][128:256, 0:128] * mask_l, pm[i][128:256, 128:256] * mask_l], axis=1) for i in each]
    x = [pm[i][0:128, 128:256] * mask_sl for i in each]
    t_inv = [eye + x[i] for i in each]
    for _ in range(5):
        x = [_mm(x[i], x[i], md["inv"]) for i in each]
        t_inv = [t_inv[i] + _mm(t_inv[i], x[i], md["inv"]) for i in each]
    av = [_mm(a_ak[i], v_sm[i], md["av"]) for i in each]
    wu = [_mm(t_inv[i], jnp.concatenate([a_sm[i], av[i]], axis=1), md["wu"]) for i in each]
    z = [jnp.concatenate([jnp.concatenate([jnp.zeros_like(v_sm[i]), v_sm[i]], axis=1), wu[i]], axis=0) for i in each]
    qy = [_mm(a_rr[i], z[i], md["qy"]) for i in each]
    mn = [_mm(kbh[i], z[i], md["mn"], _dot_tn) for i in each]
    n_pairs = len(hs)
    ys = []
    for c0 in range(0, n, n_pairs):
        idx = range(c0, c0 + n_pairs)
        y_sm = [_mm(r_sm[i] + qy[i][:, 0:128], hs[i - c0], md["y"]) + qy[i][:, 128:256] for i in idx]
        hs = [_mm(mn[i][:, 0:128] + eye * g_c[i], hs[i - c0], md["h"]) + mn[i][:, 128:256] for i in idx]
        ys += [y[0:CHUNK] + y[CHUNK:2 * CHUNK] for y in y_sm]
    return ys, hs


def _rwkv_kernel(z_ref, mu_ref, w0_ref, wup_ref, a0_ref, aup_ref, gup_ref, kk_ref, ka_ref, rk_ref,
                 gng_ref, gnb_ref, o_ref, prev_ref, h_ref, r_s, lw_s, k_s, v_s, kk_s, a_s, y_s, g_s, *, t):
    ti = pl.program_id(1)

    @pl.when(ti == 0)
    def _():
        prev_ref[...] = jnp.zeros_like(prev_ref)
        h_ref[...] = jnp.zeros_like(h_ref)

    consts = _rwkv_consts()
    head_ones = consts[3]

    z = z_ref[0]
    row = lax.broadcasted_iota(jnp.int32, (t, 1), 0)
    prev = jnp.where(row == 0, prev_ref[0:1, :], pltpu.roll(z, 1, 0))
    prev_ref[0:1, :] = z[t - 1:t, :]
    zs = z + (prev - z) * mu_ref[...]

    r = zs[:, 0:512]
    k = zs[:, 512:1024]
    v = zs[:, 1024:1536]
    wa = zs[:, 1536:1664]
    gl = zs[:, 1664:1792]
    u = w0_ref[...] + _dot(jnp.tanh(wa).astype(BF16), wup_ref[...])
    lw = (-math.exp(-0.5)) * _sigmoid(u)
    a = _sigmoid(a0_ref[...] + _dot(wa.astype(BF16), aup_ref[...]))
    g_s[...] = _dot(_sigmoid(gl).astype(BF16), gup_ref[...])
    kkr = k * kk_ref[...]
    k2 = k * (1.0 + (a - 1.0) * ka_ref[...])
    r_s[...] = r
    lw_s[...] = lw
    k_s[...] = k2
    v_s[...] = v
    a_s[...] = a
    for pr in range(4):
        sl = slice(pr * 128, (pr + 1) * 128)
        kp = kkr[:, sl]
        ss = _mm(kp * kp, head_ones, "exact_b")
        kk_s[:, sl] = kp * lax.rsqrt(jnp.maximum(ss, 1e-24))

    pairs = [slice(pr * 128, (pr + 1) * 128) for pr in range(4)]

    def chunk_body(c, carry):
        rows = [pl.ds(pl.multiple_of((c * RWKV_CHUNKS_PER_STEP + i) * CHUNK, CHUNK), CHUNK)
                for i in range(RWKV_CHUNKS_PER_STEP)]
        ys, hs = _rwkv_chunk(*[[ref[rw, sl] for rw in rows for sl in pairs]
                               for ref in (r_s, lw_s, k_s, v_s, kk_s, a_s)],
                             [h_ref[pr] for pr in range(4)], consts)
        for i, rw in enumerate(rows):
            for pr, sl in enumerate(pairs):
                y_s[rw, sl] = ys[i * 4 + pr]
        for pr in range(4):
            h_ref[pr] = hs[pr]
        return carry

    lax.fori_loop(0, t // (CHUNK * RWKV_CHUNKS_PER_STEP), chunk_body, 0)

    for pr in range(4):
        sl = slice(pr * 128, (pr + 1) * 128)
        y = y_s[:, sl]
        mean = _mm(y, head_ones, "exact_b") * (1.0 / HEAD_DIM)
        yc = y - mean
        var = _mm(yc * yc, head_ones, "exact_b") * (1.0 / HEAD_DIM)
        yn = yc * lax.rsqrt(var + RWKV_GN_EPS) * gng_ref[:, sl] + gnb_ref[:, sl]
        rp = r_s[:, sl]
        bonus = _mm(rp * k_s[:, sl] * rk_ref[:, sl], head_ones, "exact_b") * v_s[:, sl]
        o_ref[0, :, sl] = ((yn + bonus) * g_s[:, sl]).astype(o_ref.dtype)


def _rwkv_group(zr, mu, w0, w_up, a0, a_up, g_up, k_k, k_a, r_k, gn_g, gn_b):
    bsz, s, _ = zr.shape
    t = min(RWKV_T, s)
    wup_pad = jnp.concatenate([w_up, jnp.zeros_like(a_up)], axis=0).astype(BF16)
    aup_pad = jnp.concatenate([jnp.zeros_like(w_up), a_up], axis=0).astype(BF16)
    row = lambda p: p.reshape(1, -1)
    tile = pltpu.VMEM((t, RWKV_DIM), F32)
    return pl.pallas_call(
        functools.partial(_rwkv_kernel, t=t),
        grid=(bsz, s // t),
        in_specs=[
            pl.BlockSpec((1, t, RWKV_IN), lambda b, i: (b, i, 0)),
            _const_spec((1, RWKV_IN)), _const_spec((1, RWKV_DIM)), _const_spec((128, RWKV_DIM)),
            _const_spec((1, RWKV_DIM)), _const_spec((128, RWKV_DIM)), _const_spec((GATE_LORA, RWKV_DIM)),
            _const_spec((1, RWKV_DIM)), _const_spec((1, RWKV_DIM)), _const_spec((1, RWKV_DIM)),
            _const_spec((1, RWKV_DIM)), _const_spec((1, RWKV_DIM)),
        ],
        out_specs=pl.BlockSpec((1, t, RWKV_DIM), lambda b, i: (b, i, 0)),
        out_shape=jax.ShapeDtypeStruct((bsz, s, RWKV_DIM), BF16),
        scratch_shapes=[pltpu.VMEM((V7X_SUBLANES, RWKV_IN), F32), pltpu.VMEM((4, 128, 128), F32),
                        tile, tile, tile, tile, tile, tile, tile, tile],
        compiler_params=_params(("parallel", "arbitrary")),
        name="rwkv7",
    )(zr, row(mu), row(w0), wup_pad, row(a0), aup_pad, g_up.astype(BF16), row(k_k), row(k_a), row(r_k),
      row(gn_g), row(gn_b))


def _gelu_tanh(x):
    return 0.5 * x * (1.0 + jnp.tanh(math.sqrt(2.0 / math.pi) * (x + 0.044715 * (x * x * x))))


def _compress_kernel(x_ref, pe_ref, w1_ref, w2_ref, kc_ref, vc_ref, *, n_rows):
    half = CMP_LEN // 2
    for kv, o_ref in enumerate((kc_ref, vc_ref)):
        first = jnp.zeros((n_rows, 2 * CMP_HIDDEN), F32)
        second = jnp.zeros((n_rows, 2 * CMP_HIDDEN), F32)
        for p in range(half):
            xa = x_ref[0, kv, pl.ds(p, n_rows, stride=CMP_STRIDE), :]
            first += _dot((xa + pe_ref[kv, p:p + 1, :]).astype(BF16), w1_ref[kv, p])
            second += _dot((xa + pe_ref[kv, half + p:half + p + 1, :]).astype(BF16), w1_ref[kv, half + p])
        pre = first + pltpu.roll(second, n_rows - 1, 0)
        hid = _gelu_tanh(pre).astype(BF16)
        width = o_ref.shape[3]
        if kv == 0:
            lane = lax.broadcasted_iota(jnp.int32, (1, V7X_LANES), 1)
            bias_ones = ((lane == BIAS_LANES[0]) | (lane == BIAS_LANES[1])).astype(F32)
            outs = [_dot(hid, w2_ref[g]) + bias_ones for g in range(NSA_GROUPS)]
        else:
            out = _dot(hid, w2_ref[NSA_GROUPS])
            outs = [out[:, g * HEAD_DIM:(g + 1) * HEAD_DIM] for g in range(NSA_GROUPS)]
        for g in range(NSA_GROUPS):
            o_ref[0, g, 0:CMP_PAD, :] = jnp.zeros((CMP_PAD, width), o_ref.dtype)
            o_ref[0, g, CMP_PAD:CMP_PAD + n_rows, :] = outs[g].astype(o_ref.dtype)
            tail = o_ref.shape[2] - CMP_PAD - n_rows
            o_ref[0, g, CMP_PAD + n_rows:, :] = jnp.zeros((tail, width), o_ref.dtype)


def _pair_diag(w):
    z = jnp.zeros_like(w)
    return jnp.concatenate([jnp.concatenate([w, z], axis=-1), jnp.concatenate([z, w], axis=-1)], axis=-2)


def _compress(kcvc, pe_k, w1_k, w2_k, pe_v, w1_v, w2_v):
    bsz, _, s, _ = kcvc.shape
    n_rows = s // CMP_STRIDE
    rows_out = CMP_PAD + n_rows + V7X_SUBLANES
    pe = jnp.stack([jnp.concatenate([pe_k, pe_k], axis=1), jnp.concatenate([pe_v, pe_v], axis=1)])
    w1 = jnp.stack([_pair_diag(w1_k.reshape(CMP_LEN, HEAD_DIM, CMP_HIDDEN)),
                    _pair_diag(w1_v.reshape(CMP_LEN, HEAD_DIM, CMP_HIDDEN))]).astype(BF16)
    zk = jnp.zeros_like(w2_k)
    w2_pad = lambda g: jnp.concatenate([jnp.concatenate([w2_k if i == g else zk, zk], axis=1)
                                        for i in range(NSA_GROUPS)], axis=0)
    w2 = jnp.stack([w2_pad(0), w2_pad(1), _pair_diag(w2_v)]).astype(BF16)
    shapes = tuple(jax.ShapeDtypeStruct((bsz, NSA_GROUPS, rows_out, w), BF16) for w in (128, HEAD_DIM))
    specs = tuple(pl.BlockSpec((1, NSA_GROUPS, rows_out, w), lambda b: (b, 0, 0, 0)) for w in (128, HEAD_DIM))
    return pl.pallas_call(
        functools.partial(_compress_kernel, n_rows=n_rows),
        grid=(bsz,),
        in_specs=[pl.BlockSpec((1, 2, s, 128), lambda b: (b, 0, 0, 0)), _const_spec(pe.shape), _const_spec(w1.shape),
                  _const_spec(w2.shape)],
        out_specs=specs,
        out_shape=shapes,
        compiler_params=_params(("parallel",)),
        name="nsa_compress",
    )(kcvc, pe, w1, w2)


def _cmp_select_kernel(q_ref, kc_ref, vc_ref, tz_ref, gate_ref, oc_ref, sel_ref, *, n_far):
    qi = pl.program_id(1)
    qb = Q_BLOCK
    rowi = lax.broadcasted_iota(jnp.int32, (qb, 128), 0)
    lane = lax.broadcasted_iota(jnp.int32, (qb, 128), 1)
    row_f = rowi.astype(F32)
    sg = _sigmoid(gate_ref[0])
    near0 = pl.multiple_of(qi * 8, 8)
    n_first = qi * 8 - CMP_PAD
    for g in range(NSA_GROUPS):
        q4 = q_ref[0, g * NSA_HPG:(g + 1) * NSA_HPG].reshape(NSA_HPG * qb, 128)
        heads = [g * NSA_HPG + p for p in range(NSA_HPG)]
        cols = [slice(p * qb, (p + 1) * qb) for p in range(NSA_HPG)]

        def masked_keys(kt, ok):
            return jnp.where((lane == MASK_LANE) & jnp.logical_not(ok), jnp.asarray(NEG, BF16), kt)

        tiles, values, n_of_lane = [], [], []
        for j in range(n_far):
            rows = slice(CMP_PAD + 128 * j, CMP_PAD + 128 * (j + 1))
            tiles.append(_dot_nt(masked_keys(kc_ref[0, g, rows, :], (128 * j + rowi) < n_first), q4))
            values.append(vc_ref[0, g, rows, :])
            n_of_lane.append(128 * j + lane)
        s = _dot_nt(masked_keys(kc_ref[0, g, pl.ds(near0, 128), :], (n_first + rowi) >= 0), q4)
        tiles.append(jnp.concatenate([s[:, cols[p]] + tz_ref[heads[p], TZ_CMP_NEAR] for p in range(NSA_HPG)], axis=1))
        values.append(vc_ref[0, g, pl.ds(near0, 128), :])
        n_of_lane.append(n_first + lane)

        m = tiles[0].max(axis=0, keepdims=True)
        for tl in tiles[1:]:
            m = jnp.maximum(m, tl.max(axis=0, keepdims=True))
        es = [jnp.exp(tl - m) for tl in tiles]
        den = es[0].sum(axis=0, keepdims=True)
        for e in es[1:]:
            den = den + e.sum(axis=0, keepdims=True)
        inv = jnp.where(m > 0.5 * NEG, 1.0 / jnp.maximum(den, 1e-30), 0.0)
        o_t = jnp.zeros((HEAD_DIM, NSA_HPG * qb), F32)
        imp = jnp.zeros((qb, 128), F32)
        for j, e in enumerate(es):
            pc = e * inv
            o_t = o_t + _dot_tn(values[j], pc.astype(BF16))
            psum = pc[:, cols[0]] + pc[:, cols[1]] + pc[:, cols[2]] + pc[:, cols[3]]
            ov = ((n_of_lane[j] >= 4 * rowi - 1) & (n_of_lane[j] <= 4 * rowi + 3)).astype(F32)
            imp = imp + _mm(ov, psum, "exact_a")
        for p, h in enumerate(heads):
            oc_ref[0, :, h * HEAD_DIM:(h + 1) * HEAD_DIM] = o_t[:, cols[p]].T * sg[:, 3 * h:3 * h + 1]
        cur = 2 * qi + (lane >= SLC_LEN).astype(jnp.int32)
        forced = (rowi == 0) | (rowi == cur) | (rowi == cur - 1)
        score = jnp.where(rowi <= cur, jnp.where(forced, -3e38, imp), -1.0)
        sel = jnp.where(forced & (rowi <= cur), 1.0, 0.0)
        for _ in range(N_SEL - 3):
            mx = score.max(axis=0, keepdims=True)
            idx = jnp.where(score == mx, row_f, 128.0).min(axis=0, keepdims=True)
            pick = row_f == idx
            sel = jnp.where(pick, 1.0, sel)
            score = jnp.where(pick, -3e38, score)
        sel_ref[0, g] = jnp.where(sel.T > 0.5, 0.0, NEG).astype(sel_ref.dtype)


def _cmp_select(q, kc, vc, tz, gate):
    bsz, _, s, _ = q.shape
    nq = s // Q_BLOCK
    n_far = max(0, -(-(8 * (nq - 1) - CMP_PAD) // 128))
    rows_c = kc.shape[2]
    return pl.pallas_call(
        functools.partial(_cmp_select_kernel, n_far=n_far),
        grid=(bsz, nq),
        in_specs=[
            pl.BlockSpec((1, NSA_HEADS, Q_BLOCK, 128), lambda b, i: (b, 0, i, 0)),
            pl.BlockSpec((1, NSA_GROUPS, rows_c, 128), lambda b, i: (b, 0, 0, 0)),
            pl.BlockSpec((1, NSA_GROUPS, rows_c, HEAD_DIM), lambda b, i: (b, 0, 0, 0)),
            _const_spec(tz.shape),
            pl.BlockSpec((1, Q_BLOCK, 128), lambda b, i: (b, i, 0)),
        ],
        out_specs=(
            pl.BlockSpec((1, Q_BLOCK, NSA_DIM), lambda b, i: (b, i, 0)),
            pl.BlockSpec((1, NSA_GROUPS, Q_BLOCK, 128), lambda b, i: (b, 0, i, 0)),
        ),
        out_shape=(jax.ShapeDtypeStruct((bsz, s, NSA_DIM), F32),
                   jax.ShapeDtypeStruct((bsz, NSA_GROUPS, s, 128), BF16)),
        compiler_params=_params(("parallel", "parallel")),
        name="nsa_cmp_select",
    )(q, kc, vc, tz, gate)


SEL_TILES = 4


def _attn_scores(qa, ka, bias_fns, m_prev):
    qb = Q_BLOCK
    s = _dot_nt(ka, qa)
    if bias_fns is not None:
        s = jnp.concatenate(
            [jnp.concatenate([s[i * qb:(i + 1) * qb, p * qb:(p + 1) * qb] + fn(p) for p in range(NSA_HPG)], axis=1)
             for i, fn in enumerate(bias_fns)], axis=0)
    return s, jnp.maximum(m_prev, s.max(axis=0, keepdims=True))


def _attn_accumulate(s, vts, m_prev, m_next, acc_ref):
    pexp = jnp.exp(s - m_next)
    acc_ref[...] = jnp.exp(m_prev - m_next) * acc_ref[...] + _dot(jnp.concatenate(vts, axis=1), pexp.astype(BF16))


def _sel_win_kernel(q_ref, k_ref, vt_ref, e_ref, sel_ref, tz_ref, gate_ref, oc_ref, o_ref, m_ref, acc_ref,
                    s0_ref, s1_ref, mm0_ref, mm1_ref):
    qi = pl.program_id(1)
    qb = Q_BLOCK
    win_tiles = WINDOW // qb
    m_ref[...] = jnp.full(m_ref.shape, NEG, F32)
    acc_ref[...] = jnp.zeros(acc_ref.shape, F32)
    q4 = [q_ref[0, g * NSA_HPG:(g + 1) * NSA_HPG].reshape(NSA_HPG * qb, 128) for g in range(NSA_GROUPS)]
    selm = [jnp.concatenate([sel_ref[0, g]] * NSA_HPG, axis=0) for g in range(NSA_GROUPS)]

    def key_rows(kt, n=1):
        return pl.ds(pl.multiple_of(kt * qb, qb), n * qb)

    def bias_of(g, kind):
        return lambda p: tz_ref[g * NSA_HPG + p, kind]

    def keys_sel(g, kt, n=1):
        return jnp.concatenate([k_ref[0, g, key_rows(kt, n), :], e_ref[key_rows(kt, n), :]], axis=1)

    block = lax.broadcasted_iota(jnp.int32, (NSA_HPG * qb, 128), 1)

    n_far = (jnp.maximum(qi - 1, 0) + SEL_TILES - 1) // SEL_TILES
    n_run = n_far | 1
    last_tile0 = k_ref.shape[2] // qb - SEL_TILES

    def tile0(step):
        return jnp.minimum(step * SEL_TILES, last_tile0)

    def score_far(step, s_ref, mm_ref):
        for g in range(NSA_GROUPS):
            qa = jnp.concatenate(
                [q4[g], jnp.where((block >= 2 * (qi - 1)) | (step >= n_far), jnp.asarray(NEG, BF16), selm[g])], axis=1)
            m_prev = m_ref[g]
            s, m_next = _attn_scores(qa, keys_sel(g, tile0(step), SEL_TILES), None, m_prev)
            s_ref[g] = s
            mm_ref[g, 0:1, :] = m_prev
            mm_ref[g, 1:2, :] = m_next
            m_ref[g] = m_next

    def accumulate_far(step, s_ref, mm_ref):
        for g in range(NSA_GROUPS):
            _attn_accumulate(s_ref[g], [vt_ref[0, g, tile0(step) + i] for i in range(SEL_TILES)],
                             mm_ref[g, 0:1, :], mm_ref[g, 1:2, :], acc_ref.at[g])

    score_far(0, s0_ref, mm0_ref)

    def far_body(i, carry):
        score_far(2 * i + 1, s1_ref, mm1_ref)
        accumulate_far(2 * i, s0_ref, mm0_ref)
        score_far(2 * i + 2, s0_ref, mm0_ref)
        accumulate_far(2 * i + 1, s1_ref, mm1_ref)
        return carry

    lax.fori_loop(0, n_run // 2, far_body, 0)

    scored = []
    for g in range(NSA_GROUPS):
        kt_sub = jnp.maximum(qi - 1, 0)
        kinds = (jnp.where(qi >= 1, TZ_SUB, TZ_MASKED), TZ_DIAG)
        m_prev = m_ref[g]
        s, m_next = _attn_scores(jnp.concatenate([q4[g], selm[g]], axis=1),
                                 jnp.concatenate([keys_sel(g, kt_sub), keys_sel(g, qi)], axis=0),
                                 [bias_of(g, kind) for kind in kinds], m_prev)
        scored.append((g, s, [vt_ref[0, g, kt_sub], vt_ref[0, g, qi]], m_prev, m_next))
    for g in range(NSA_GROUPS):
        kw = NSA_GROUPS + g
        ks, vts, biases = [], [], []
        for d in range(win_tiles, -1, -1):
            kt = jnp.maximum(qi - d, 0)
            kind = TZ_DIAG if d == 0 else TZ_SUB if d == 1 else TZ_WIN_OLD if d == win_tiles else TZ_ZERO
            if d > 0:
                kind = jnp.where(qi >= d, kind, TZ_MASKED)
            ks.append(k_ref[0, kw, key_rows(kt), :])
            vts.append(vt_ref[0, kw, kt])
            biases.append(bias_of(g, kind))
        m_prev = m_ref[kw]
        s, m_next = _attn_scores(q4[g], jnp.concatenate(ks, axis=0), biases, m_prev)
        scored.append((kw, s, vts, m_prev, m_next))
    accumulate_far(n_run - 1, s0_ref, mm0_ref)
    for stream, s, vts, m_prev, m_next in scored:
        _attn_accumulate(s, vts, m_prev, m_next, acc_ref.at[stream])

    sg = _sigmoid(gate_ref[0])
    oc = oc_ref[0]
    for g in range(NSA_GROUPS):
        outs = []
        for stream in (g, NSA_GROUPS + g):
            acc = acc_ref[stream]
            outs.append(acc[0:HEAD_DIM] * (1.0 / acc[HEAD_DIM:HEAD_DIM + 1]))
        for p in range(NSA_HPG):
            h = g * NSA_HPG + p
            qcols = slice(p * qb, (p + 1) * qb)
            cols = slice(h * HEAD_DIM, (h + 1) * HEAD_DIM)
            y = (oc[:, cols] + sg[:, 3 * h + 1:3 * h + 2] * outs[0][:, qcols].T
                 + sg[:, 3 * h + 2:3 * h + 3] * outs[1][:, qcols].T)
            o_ref[0, :, cols] = y.astype(o_ref.dtype)


def _sel_win(q, k4, v4, sel, tz, gate, oc):
    bsz, _, s, _ = q.shape
    nq = s // Q_BLOCK
    assert nq % SEL_TILES == 0
    member = (jnp.arange(s)[:, None] // SLC_LEN == jnp.arange(128)[None, :]).astype(BF16)
    return pl.pallas_call(
        _sel_win_kernel,
        grid=(bsz, nq),
        in_specs=[
            pl.BlockSpec((1, NSA_HEADS, Q_BLOCK, 128), lambda b, i: (b, 0, i, 0)),
            pl.BlockSpec((1, 4, s, 128), lambda b, i: (b, 0, 0, 0)),
            pl.BlockSpec((1, 4, nq, VT_ROWS, Q_BLOCK), lambda b, i: (b, 0, 0, 0, 0)),
            _const_spec((s, 128)),
            pl.BlockSpec((1, NSA_GROUPS, Q_BLOCK, 128), lambda b, i: (b, 0, i, 0)),
            _const_spec(tz.shape),
            pl.BlockSpec((1, Q_BLOCK, 128), lambda b, i: (b, i, 0)),
            pl.BlockSpec((1, Q_BLOCK, NSA_DIM), lambda b, i: (b, i, 0)),
        ],
        out_specs=pl.BlockSpec((1, Q_BLOCK, NSA_DIM), lambda b, i: (b, i, 0)),
        out_shape=jax.ShapeDtypeStruct((bsz, s, NSA_DIM), BF16),
        scratch_shapes=[pltpu.VMEM((2 * NSA_GROUPS, 1, NSA_HPG * Q_BLOCK), F32),
                        pltpu.VMEM((2 * NSA_GROUPS, VT_ROWS, NSA_HPG * Q_BLOCK), F32),
                        pltpu.VMEM((NSA_GROUPS, SEL_TILES * Q_BLOCK, NSA_HPG * Q_BLOCK), F32),
                        pltpu.VMEM((NSA_GROUPS, SEL_TILES * Q_BLOCK, NSA_HPG * Q_BLOCK), F32),
                        pltpu.VMEM((NSA_GROUPS, 2, NSA_HPG * Q_BLOCK), F32),
                        pltpu.VMEM((NSA_GROUPS, 2, NSA_HPG * Q_BLOCK), F32)],
        compiler_params=_params(("parallel", "parallel")),
        name="nsa_sel_win",
    )(q, k4, v4, member, sel, tz, gate, oc)


XATTN_TM = 1024


def _mem_kv_kernel(mem_ref, wk_ref, wv_ref, k_ref, v_ref):
    mb = mem_ref[0].astype(BF16)
    k_ref[0] = _dot(mb, wk_ref[...]).astype(BF16)
    v_ref[0] = _dot(mb, wv_ref[...]).astype(BF16)


def _mem_kv(mem, wk, wv):
    bsz, m, d = mem.shape
    shape = jax.ShapeDtypeStruct((bsz, m, d), BF16)
    spec = pl.BlockSpec((1, m, d), lambda b: (b, 0, 0))
    return pl.pallas_call(
        _mem_kv_kernel,
        grid=(bsz,),
        in_specs=[spec, _const_spec((d, d)), _const_spec((d, d))],
        out_specs=(spec, spec),
        out_shape=(shape, shape),
        compiler_params=_params(("parallel",)),
        name="xattn_mem_kv",
    )(mem, wk.astype(BF16), wv.astype(BF16))


def _mix_xattn_kernel(x_ref, yr_ref, yn_ref, wm_ref, g2_ref, b2_ref, k_ref, v_ref, wq_ref, wo_ref, g_ref, b_ref, o_ref):
    mixed = _dot(yr_ref[0], wm_ref[0:RWKV_DIM, :]) + _dot(yn_ref[0], wm_ref[RWKV_DIM:, :])
    x = _layer_norm(ALPHA * x_ref[0] + mixed, g2_ref[...], b2_ref[...])
    q = (_dot(x.astype(BF16), wq_ref[...]) * (XATTN_HEAD_DIM ** -0.5)).astype(BF16)
    heads = []
    for h in range(XATTN_HEADS):
        cols = slice(h * XATTN_HEAD_DIM, (h + 1) * XATTN_HEAD_DIM)
        s = _dot_nt(q[:, cols], k_ref[0, :, cols])
        e = jnp.exp(s - s.max(axis=1, keepdims=True))
        p = e * (1.0 / e.sum(axis=1, keepdims=True))
        heads.append(_dot(p.astype(BF16), v_ref[0, :, cols]).astype(BF16))
    o = _dot(jnp.concatenate(heads, axis=1), wo_ref[...])
    o_ref[0] = _layer_norm(ALPHA * x + o, g_ref[...], b_ref[...])


def _mix_cross_attention_ln(x3d, yr, yn, w_out, g2, b2, k, v, wq, wo, g, b):
    bsz, s, d = x3d.shape
    m = k.shape[1]
    tm = min(XATTN_TM, s)
    rows = lambda w: pl.BlockSpec((1, tm, w), lambda bb, i: (bb, i, 0))
    mem = pl.BlockSpec((1, m, d), lambda bb, i: (bb, 0, 0))
    vec = _const_spec((1, d))
    return pl.pallas_call(
        _mix_xattn_kernel,
        grid=(bsz, s // tm),
        in_specs=[rows(d), rows(RWKV_DIM), rows(NSA_DIM), _const_spec(w_out.shape), vec, vec,
                  mem, mem, _const_spec((d, d)), _const_spec((d, d)), vec, vec],
        out_specs=rows(d),
        out_shape=jax.ShapeDtypeStruct((bsz, s, d), F32),
        compiler_params=_params(("parallel", "parallel")),
        name="mix_xattn_ln",
    )(x3d, yr, yn, w_out.astype(BF16), g2.reshape(1, d), b2.reshape(1, d), k, v, wq.astype(BF16), wo.astype(BF16),
      g.reshape(1, d), b.reshape(1, d))


def _nsa_group(rel_bias, q, kcvc, k4, v4, gate, pe_k, w1_k, w2_k, pe_v, w1_v, w2_v):
    tz = _bias_tiles(rel_bias)
    kc, vc = _compress(kcvc, pe_k, w1_k, w2_k, pe_v, w1_v, w2_v)
    oc, sel = _cmp_select(q, kc, vc, tz, gate)
    return _sel_win(q, k4, v4, sel, tz, gate, oc)


def kernel(x, mem, ffn1_w_gate, ffn1_w_up, ffn1_w_down, ln1_g, ln1_b, mix_w_in, rwkv_mu, rwkv_w0, rwkv_w_up, rwkv_a0, rwkv_a_up, rwkv_g_up, rwkv_k_k, rwkv_k_a, rwkv_r_k, rwkv_gn_g, rwkv_gn_b, nsa_pe_k, nsa_w1_k, nsa_w2_k, nsa_pe_v, nsa_w1_v, nsa_w2_v, mix_w_out, ln2_g, ln2_b, xattn_wq, xattn_wk, xattn_wv, xattn_wo, ln3_g, ln3_b, ffn2_w_gate, ffn2_w_up, ffn2_w_down, ln4_g, ln4_b, rel_bias):
    bsz, s, d = x.shape
    rows = bsz * s
    for l in range(DEPTH):
        x1 = _ffn_ln(x.reshape(rows, d), ffn1_w_gate[l], ffn1_w_up[l], ffn1_w_down[l], ln1_g[l], ln1_b[l])
        zr, q, kcvc, k4, v4, gate = _in_projection(x1.reshape(bsz, s, d), mix_w_in[l], rel_bias)
        y_rwkv = _rwkv_group(zr, rwkv_mu[l], rwkv_w0[l], rwkv_w_up[l], rwkv_a0[l], rwkv_a_up[l], rwkv_g_up[l],
                             rwkv_k_k[l], rwkv_k_a[l], rwkv_r_k[l], rwkv_gn_g[l], rwkv_gn_b[l])
        y_nsa = _nsa_group(rel_bias, q, kcvc, k4, v4, gate, nsa_pe_k[l], nsa_w1_k[l], nsa_w2_k[l],
                           nsa_pe_v[l], nsa_w1_v[l], nsa_w2_v[l])
        mk, mv = _mem_kv(mem, xattn_wk[l], xattn_wv[l])
        x3 = _mix_cross_attention_ln(x1.reshape(bsz, s, d), y_rwkv, y_nsa, mix_w_out[l], ln2_g[l], ln2_b[l],
                                     mk, mv, xattn_wq[l], xattn_wo[l], ln3_g[l], ln3_b[l])
        x = _ffn_ln(x3.reshape(rows, d), ffn2_w_gate[l], ffn2_w_up[l], ffn2_w_down[l], ln4_g[l], ln4_b[l])
        x = x.reshape(bsz, s, d)
    return x
```

```python
import functools
import math

import numpy as np
import jax
import jax.numpy as jnp
from jax import lax
from jax.experimental import pallas as pl
from jax.experimental.pallas import tpu as pltpu

F32 = jnp.float32
BF16 = jnp.bfloat16
HIGHEST = lax.Precision.HIGHEST

D_MODEL = 1024
DEPTH = 1
RWKV_HEADS = 8
HEAD_DIM = 64
RWKV_DIM = RWKV_HEADS * HEAD_DIM
DECAY_LORA = 64
AAA_LORA = 64
GATE_LORA = 128
RWKV_IN = 3 * RWKV_DIM + DECAY_LORA + AAA_LORA + GATE_LORA
RWKV_GN_EPS = 64e-5
NSA_HEADS = 8
NSA_GROUPS = 2
NSA_HPG = NSA_HEADS // NSA_GROUPS
NSA_DIM = NSA_HEADS * HEAD_DIM
CMP_LEN = 32
CMP_STRIDE = 16
CMP_HIDDEN = 128
SLC_LEN = 64
N_SEL = 16
WINDOW = 512
Q_BLOCK = 128
FORCED_SCORE = 1e4
NUM_BUCKETS = 32
MAX_DISTANCE = 128
XATTN_HEADS = 4
XATTN_HEAD_DIM = D_MODEL // XATTN_HEADS
D_FF = 2816
LN_EPS = 1e-5
ALPHA = (2.0 * DEPTH) ** 0.25
NEG = -1e30

V7X_LANES = 128
V7X_SUBLANES = 8
V7X_VMEM_LIMIT_BYTES = 56 * 1024 * 1024

CHUNK = 64
CMP_PAD = 120
VT_ROWS = 80


def _dot(a, b, prec=None):
    return jnp.dot(a, b, preferred_element_type=F32, precision=prec)


def _dot_nt(a, b, prec=None):
    return lax.dot_general(a, b, (((1,), (1,)), ((), ())), preferred_element_type=F32, precision=prec)


def _dot_tn(a, b, prec=None):
    return lax.dot_general(a, b, (((0,), (0,)), ((), ())), preferred_element_type=F32, precision=prec)


def _sigmoid(x):
    return 1.0 / (1.0 + jnp.exp(-x))


def _layer_norm(y, g, b):
    mu = jnp.mean(y, axis=-1, keepdims=True)
    yc = y - mu
    var = jnp.mean(yc * yc, axis=-1, keepdims=True)
    return yc * lax.rsqrt(var + LN_EPS) * g + b


def _params(sem):
    return pltpu.CompilerParams(dimension_semantics=sem, vmem_limit_bytes=V7X_VMEM_LIMIT_BYTES)


def _const_spec(shape, single_buffer=False):
    nd = len(shape)
    if single_buffer:
        return pl.BlockSpec(shape, lambda *_: (0,) * nd, pipeline_mode=pl.Buffered(1))
    return pl.BlockSpec(shape, lambda *_: (0,) * nd)


def _bucket_thresholds():
    n = np.arange(0, 4 * MAX_DISTANCE)
    max_exact = NUM_BUCKETS // 2
    nf = np.maximum(n, max_exact).astype(np.float32)
    large = max_exact + (np.log(nf / np.float32(max_exact)) / np.float32(math.log(MAX_DISTANCE / max_exact))
                         * np.float32(NUM_BUCKETS - max_exact)).astype(np.int32)
    large = np.minimum(large, NUM_BUCKETS - 1)
    bucket = np.where(n < max_exact, n, large)
    return [int(np.argmax(bucket >= b)) for b in range(1, NUM_BUCKETS)]


_BUCKET_THR = _bucket_thresholds()


TZ_CMP_NEAR, TZ_DIAG, TZ_SUB, TZ_ZERO, TZ_MASKED, TZ_WIN_OLD, TZ_KINDS = 0, 1, 2, 3, 4, 5, 6


def _bias_tiles_kernel(rb_ref, o_ref):
    h = pl.program_id(0)
    r = lax.broadcasted_iota(jnp.int32, (Q_BLOCK, Q_BLOCK), 0)
    c = lax.broadcasted_iota(jnp.int32, (Q_BLOCK, Q_BLOCK), 1)
    far = jnp.full((Q_BLOCK, Q_BLOCK), rb_ref[NUM_BUCKETS - 1, h], F32)
    far_hi = far.astype(BF16).astype(F32)
    far_added = far_hi + (far - far_hi).astype(BF16).astype(F32)
    dists = (c - CMP_STRIDE * (r - CMP_PAD) - (CMP_LEN - 1), c - r, Q_BLOCK + c - r)
    for kind, dist in enumerate(dists):
        val = jnp.full((Q_BLOCK, Q_BLOCK), rb_ref[0, h], F32)
        for b in range(1, NUM_BUCKETS):
            val = jnp.where(dist >= _BUCKET_THR[b - 1], rb_ref[b, h], val)
        o_ref[0, kind] = jnp.where(dist >= 0, val - far_added, NEG)
    o_ref[0, TZ_ZERO] = far - far_added
    o_ref[0, TZ_MASKED] = jnp.full((Q_BLOCK, Q_BLOCK), NEG, F32)
    o_ref[0, TZ_WIN_OLD] = jnp.where(r > c, far - far_added, NEG)


def _bias_tiles(rel_bias):
    return pl.pallas_call(
        _bias_tiles_kernel,
        grid=(NSA_HEADS,),
        in_specs=[pl.BlockSpec(memory_space=pltpu.SMEM)],
        out_specs=pl.BlockSpec((1, TZ_KINDS, Q_BLOCK, Q_BLOCK), lambda h: (h, 0, 0, 0)),
        out_shape=jax.ShapeDtypeStruct((NSA_HEADS, TZ_KINDS, Q_BLOCK, Q_BLOCK), F32),
        compiler_params=_params(("arbitrary",)),
        name="bias_tiles",
    )(rel_bias)


FFN_TM = 1024
FFN_TF = 256


def _ffn_kernel(x_ref, wg_ref, wu_ref, wd_ref, g_ref, b_ref, o_ref, acc_ref):
    x = x_ref[...]
    xb = x.astype(BF16)
    for c in range(wg_ref.shape[1] // FFN_TF):
        cols = slice(c * FFN_TF, (c + 1) * FFN_TF)
        hg = _dot(xb, wg_ref[:, cols])
        hu = _dot(xb, wu_ref[:, cols])
        h = hg * _sigmoid(hg) * hu
        part = _dot(h.astype(BF16), wd_ref[cols, :])
        if c == 0:
            acc_ref[...] = part
        else:
            acc_ref[...] += part
    o_ref[...] = _layer_norm(ALPHA * x + 0.5 * acc_ref[...], g_ref[...], b_ref[...])


def _ffn_ln(x2d, wg, wu, wd, g, b):
    rows, d = x2d.shape
    f = wg.shape[1]
    assert f % FFN_TF == 0
    tm = min(FFN_TM, rows)
    return pl.pallas_call(
        _ffn_kernel,
        grid=(rows // tm,),
        in_specs=[
            pl.BlockSpec((tm, d), lambda i: (i, 0)),
            _const_spec((d, f), single_buffer=True),
            _const_spec((d, f), single_buffer=True),
            _const_spec((f, d), single_buffer=True),
            _const_spec((1, d)),
            _const_spec((1, d)),
        ],
        out_specs=pl.BlockSpec((tm, d), lambda i: (i, 0)),
        out_shape=jax.ShapeDtypeStruct((rows, d), F32),
        scratch_shapes=[pltpu.VMEM((tm, d), F32)],
        compiler_params=_params(("parallel",)),
        name="ffn_ln",
    )(x2d, wg.astype(BF16), wu.astype(BF16), wd.astype(BF16), g.reshape(1, d), b.reshape(1, d))


PROJ_TM = 512
_C_RWKV = 0
_C_Q = _C_RWKV + RWKV_IN
_C_KCVC = _C_Q + NSA_HEADS * 128
_C_K = _C_KCVC + 256
_C_V = _C_K + 4 * 128
_C_GATE = _C_V + 4 * 128
_C_END = _C_GATE + 128
BIAS_LANES = (HEAD_DIM, HEAD_DIM + 1)
MASK_LANE = HEAD_DIM + 2


def _inproj_kernel(x_ref, w_ref, qx_ref, zr_ref, q_ref, kcvc_ref, k_ref, v_ref, gate_ref):
    xb = x_ref[0].astype(BF16)
    lane = lax.broadcasted_iota(jnp.int32, (1, V7X_LANES), 1)
    zr_ref[0] = _dot(xb, w_ref[:, _C_RWKV:_C_Q])
    zq = _dot(xb, w_ref[:, _C_Q:_C_KCVC]) * (HEAD_DIM ** -0.5)
    for h in range(NSA_HEADS):
        q_ref[0, h] = (zq[:, h * 128:(h + 1) * 128] + qx_ref[h:h + 1, :]).astype(BF16)
    zc = _dot(xb, w_ref[:, _C_KCVC:_C_K])
    kcvc_ref[0, 0] = zc[:, 0:128]
    kcvc_ref[0, 1] = zc[:, 128:256]
    bias_ones = ((lane == BIAS_LANES[0]) | (lane == BIAS_LANES[1])).astype(F32)
    zk = _dot(xb, w_ref[:, _C_K:_C_V])
    for j in range(4):
        k_ref[0, j] = (zk[:, j * 128:(j + 1) * 128] + bias_ones).astype(BF16)
    one_lane = (lane == HEAD_DIM).astype(F32)
    zvg = _dot(xb, w_ref[:, _C_V:_C_END])
    for j in range(4):
        zv = zvg[:, j * 128:(j + 1) * 128] + one_lane
        for i in range(zv.shape[0] // Q_BLOCK):
            v_ref[0, j, i] = zv[i * Q_BLOCK:(i + 1) * Q_BLOCK].T[0:VT_ROWS].astype(BF16)
    gate_ref[0] = zvg[:, 4 * 128:5 * 128]


def _pack_w_in(w_in):
    d = w_in.shape[0]
    o = RWKV_IN
    q = w_in[:, o:o + 512]
    kc = w_in[:, o + 512:o + 640]
    vc = w_in[:, o + 640:o + 768]
    ks = w_in[:, o + 768:o + 896]
    vs = w_in[:, o + 896:o + 1024]
    kw = w_in[:, o + 1024:o + 1152]
    vw = w_in[:, o + 1152:o + 1280]
    gate = w_in[:, o + 1280:o + 1304]
    z64 = jnp.zeros((d, 64), w_in.dtype)
    pad = lambda m, n: [jnp.concatenate([m[:, i * 64:(i + 1) * 64], z64], axis=1) for i in range(n)]
    gate_pad = jnp.concatenate([gate, jnp.zeros((d, 128 - gate.shape[1]), w_in.dtype)], axis=1)
    cols = ([w_in[:, :o]] + pad(q, NSA_HEADS) + [kc, vc] + pad(ks, NSA_GROUPS) + pad(kw, NSA_GROUPS)
            + pad(vs, NSA_GROUPS) + pad(vw, NSA_GROUPS) + [gate_pad])
    return jnp.concatenate(cols, axis=1).astype(BF16)


def _far_bias_lanes(rel_bias):
    far = rel_bias[NUM_BUCKETS - 1, :]
    hi = far.astype(BF16).astype(F32)
    lane = jnp.arange(V7X_LANES)[None, :]
    return jnp.where(lane == BIAS_LANES[0], hi[:, None],
                     jnp.where(lane == BIAS_LANES[1], (far - hi)[:, None], jnp.where(lane == MASK_LANE, 1.0, 0.0)))


def _in_projection(x3d, w_in, rel_bias):
    bsz, s, d = x3d.shape
    tm = min(PROJ_TM, s)
    wp = _pack_w_in(w_in)
    out_shape = (
        jax.ShapeDtypeStruct((bsz, s, RWKV_IN), F32),
        jax.ShapeDtypeStruct((bsz, NSA_HEADS, s, 128), BF16),
        jax.ShapeDtypeStruct((bsz, 2, s, 128), F32),
        jax.ShapeDtypeStruct((bsz, 4, s, 128), BF16),
        jax.ShapeDtypeStruct((bsz, 4, s // Q_BLOCK, VT_ROWS, Q_BLOCK), BF16),
        jax.ShapeDtypeStruct((bsz, s, 128), F32),
    )
    return pl.pallas_call(
        _inproj_kernel,
        grid=(bsz, s // tm),
        in_specs=[pl.BlockSpec((1, tm, d), lambda b, i: (b, i, 0)), _const_spec((d, _C_END)),
                  _const_spec((NSA_HEADS, 128))],
        out_specs=(
            pl.BlockSpec((1, tm, RWKV_IN), lambda b, i: (b, i, 0)),
            pl.BlockSpec((1, NSA_HEADS, tm, 128), lambda b, i: (b, 0, i, 0)),
            pl.BlockSpec((1, 2, tm, 128), lambda b, i: (b, 0, i, 0)),
            pl.BlockSpec((1, 4, tm, 128), lambda b, i: (b, 0, i, 0)),
            pl.BlockSpec((1, 4, tm // Q_BLOCK, VT_ROWS, Q_BLOCK), lambda b, i: (b, 0, i, 0, 0)),
            pl.BlockSpec((1, tm, 128), lambda b, i: (b, i, 0)),
        ),
        out_shape=out_shape,
        compiler_params=_params(("parallel", "parallel")),
        name="in_projection",
    )(x3d, wp, _far_bias_lanes(rel_bias))


RWKV_T = 512
RWKV_CHUNKS_PER_STEP = 4


def _rwkv_consts():
    r = lax.broadcasted_iota(jnp.int32, (128, 128), 0)
    c = lax.broadcasted_iota(jnp.int32, (128, 128), 1)
    same = (r >= CHUNK) == (c >= CHUNK)
    mask_sl = (same & (r > c)).astype(F32)
    mask_l = (same & (r >= c)).astype(F32)
    eye = (r == c).astype(F32)
    head_ones = same.astype(F32)
    rt = lax.broadcasted_iota(jnp.int32, (CHUNK, CHUNK), 0)
    ct = lax.broadcasted_iota(jnp.int32, (CHUNK, CHUNK), 1)
    tri = (rt >= ct).astype(F32)
    lane = lax.broadcasted_iota(jnp.int32, (1, 128), 1)
    m0 = (lane < CHUNK).astype(F32)
    m1 = 1.0 - m0
    return mask_sl, mask_l, eye, head_ones, tri, m0, m1


def _split2(x):
    hi = x.astype(BF16)
    return hi, (x - hi.astype(F32)).astype(BF16)


def _mm(a, b, mode, dot=_dot):
    if mode == "bf16":
        return dot(a.astype(BF16), b.astype(BF16))
    if mode == "bf16x3":
        ah, al = _split2(a)
        bh, bl = _split2(b)
        return dot(ah, bh) + (dot(ah, bl) + dot(al, bh))
    if mode in ("exact_a", "exact_b"):
        x = b if mode == "exact_a" else a
        hi, rest = x.astype(BF16), None
        rest = x - hi.astype(F32)
        mid = rest.astype(BF16)
        lo = (rest - mid.astype(F32)).astype(BF16)
        if mode == "exact_a":
            ab = a.astype(BF16)
            return dot(ab, hi) + (dot(ab, mid) + dot(ab, lo))
        bb = b.astype(BF16)
        return dot(hi, bb) + (dot(mid, bb) + dot(lo, bb))
    raise ValueError(mode)


RWKV_MODES = dict(p="bf16", inv="bf16", av="bf16", wu="bf16", qy="bf16", mn="bf16", y="bf16", h="bf16")


def _rwkv_chunk(rs, lws, ks, vs, kks, as_, hs, consts):
    mask_sl, mask_l, eye, _, tri, m0, m1 = consts
    md = RWKV_MODES
    n = len(rs)
    each = range(n)

    def sm(x):
        return jnp.concatenate([x * m0, x * m1], axis=0)

    def dup(x):
        return jnp.concatenate([x, x], axis=0)

    cums = [_mm(tri, lws[i], "exact_a") for i in each]
    a_sm, r_sm, v_sm, kb, kbh, g_c = [], [], [], [], [], []
    for i in each:
        cum, lw, kk, k = cums[i], lws[i], kks[i], ks[i]
        cl = cum[CHUNK - 1:CHUNK, :]
        ka = kk * as_[i]
        g_tail = jnp.exp(cl - cum)
        g_inv = jnp.exp(-cum)
        a_sm.append(sm(-(kk * jnp.exp(cum - lw))))
        r_sm.append(sm(rs[i] * jnp.exp(cum)))
        v_sm.append(sm(vs[i]))
        kb.append(jnp.concatenate([dup(k * g_inv), dup(ka * g_inv)], axis=0))
        kbh.append(jnp.concatenate([sm(k * g_tail), sm(ka * g_tail)], axis=0))
        g_c.append(jnp.exp(cl))
    pm = [_mm(jnp.concatenate([a_sm[i], r_sm[i]], axis=0), kb[i], md["p"], _dot_nt) for i in each]
    a_ak = [pm[i][0:128, 0:128] * mask_sl for i in each]
    a_rr = [jnp.concatenate([pm[i][128:256, 0:128] * mask_l, pm[i][128:256, 128:256] * mask_l], axis=1) for i in each]
    x = [pm[i][0:128, 128:256] * mask_sl for i in each]
    t_inv = [eye + x[i] for i in each]
    for _ in range(5):
        x = [_mm(x[i], x[i], md["inv"]) for i in each]
        t_inv = [t_inv[i] + _mm(t_inv[i], x[i], md["inv"]) for i in each]
    av = [_mm(a_ak[i], v_sm[i], md["av"]) for i in each]
    wu = [_mm(t_inv[i], jnp.concatenate([a_sm[i], av[i]], axis=1), md["wu"]) for i in each]
    z = [jnp.concatenate([jnp.concatenate([jnp.zeros_like(v_sm[i]), v_sm[i]], axis=1), wu[i]], axis=0) for i in each]
    qy = [_mm(a_rr[i], z[i], md["qy"]) for i in each]
    mn = [_mm(kbh[i], z[i], md["mn"], _dot_tn) for i in each]
    n_pairs = len(hs)
    ys = []
    for c0 in range(0, n, n_pairs):
        idx = range(c0, c0 + n_pairs)
        y_sm = [_mm(r_sm[i] + qy[i][:, 0:128], hs[i - c0], md["y"]) + qy[i][:, 128:256] for i in idx]
        hs = [_mm(mn[i][:, 0:128] + eye * g_c[i], hs[i - c0], md["h"]) + mn[i][:, 128:256] for i in idx]
        ys += [y[0:CHUNK] + y[CHUNK:2 * CHUNK] for y in y_sm]
    return ys, hs


def _rwkv_kernel(z_ref, mu_ref, w0_ref, wup_ref, a0_ref, aup_ref, gup_ref, kk_ref, ka_ref, rk_ref,
                 gng_ref, gnb_ref, o_ref, prev_ref, h_ref, r_s, lw_s, k_s, v_s, kk_s, a_s, y_s, g_s, *, t):
    ti = pl.program_id(1)

    @pl.when(ti == 0)
    def _():
        prev_ref[...] = jnp.zeros_like(prev_ref)
        h_ref[...] = jnp.zeros_like(h_ref)

    consts = _rwkv_consts()
    head_ones = consts[3]

    z = z_ref[0]
    row = lax.broadcasted_iota(jnp.int32, (t, 1), 0)
    prev = jnp.where(row == 0, prev_ref[0:1, :], pltpu.roll(z, 1, 0))
    prev_ref[0:1, :] = z[t - 1:t, :]
    zs = z + (prev - z) * mu_ref[...]

    r = zs[:, 0:512]
    k = zs[:, 512:1024]
    v = zs[:, 1024:1536]
    wa = zs[:, 1536:1664]
    gl = zs[:, 1664:1792]
    u = w0_ref[...] + _dot(jnp.tanh(wa).astype(BF16), wup_ref[...])
    lw = (-math.exp(-0.5)) * _sigmoid(u)
    a = _sigmoid(a0_ref[...] + _dot(wa.astype(BF16), aup_ref[...]))
    g_s[...] = _dot(_sigmoid(gl).astype(BF16), gup_ref[...])
    kkr = k * kk_ref[...]
    k2 = k * (1.0 + (a - 1.0) * ka_ref[...])
    r_s[...] = r
    lw_s[...] = lw
    k_s[...] = k2
    v_s[...] = v
    a_s[...] = a
    for pr in range(4):
        sl = slice(pr * 128, (pr + 1) * 128)
        kp = kkr[:, sl]
        ss = _mm(kp * kp, head_ones, "exact_b")
        kk_s[:, sl] = kp * lax.rsqrt(jnp.maximum(ss, 1e-24))

    pairs = [slice(pr * 128, (pr + 1) * 128) for pr in range(4)]

    def chunk_body(c, carry):
        rows = [pl.ds(pl.multiple_of((c * RWKV_CHUNKS_PER_STEP + i) * CHUNK, CHUNK), CHUNK)
                for i in range(RWKV_CHUNKS_PER_STEP)]
        ys, hs = _rwkv_chunk(*[[ref[rw, sl] for rw in rows for sl in pairs]
                               for ref in (r_s, lw_s, k_s, v_s, kk_s, a_s)],
                             [h_ref[pr] for pr in range(4)], consts)
        for i, rw in enumerate(rows):
            for pr, sl in enumerate(pairs):
                y_s[rw, sl] = ys[i * 4 + pr]
        for pr in range(4):
            h_ref[pr] = hs[pr]
        return carry

    lax.fori_loop(0, t // (CHUNK * RWKV_CHUNKS_PER_STEP), chunk_body, 0)

    for pr in range(4):
        sl = slice(pr * 128, (pr + 1) * 128)
        y = y_s[:, sl]
        mean = _mm(y, head_ones, "exact_b") * (1.0 / HEAD_DIM)
        yc = y - mean
        var = _mm(yc * yc, head_ones, "exact_b") * (1.0 / HEAD_DIM)
        yn = yc * lax.rsqrt(var + RWKV_GN_EPS) * gng_ref[:, sl] + gnb_ref[:, sl]
        rp = r_s[:, sl]
        bonus = _mm(rp * k_s[:, sl] * rk_ref[:, sl], head_ones, "exact_b") * v_s[:, sl]
        o_ref[0, :, sl] = ((yn + bonus) * g_s[:, sl]).astype(o_ref.dtype)


def _rwkv_group(zr, mu, w0, w_up, a0, a_up, g_up, k_k, k_a, r_k, gn_g, gn_b):
    bsz, s, _ = zr.shape
    t = min(RWKV_T, s)
    wup_pad = jnp.concatenate([w_up, jnp.zeros_like(a_up)], axis=0).astype(BF16)
    aup_pad = jnp.concatenate([jnp.zeros_like(w_up), a_up], axis=0).astype(BF16)
    row = lambda p: p.reshape(1, -1)
    tile = pltpu.VMEM((t, RWKV_DIM), F32)
    return pl.pallas_call(
        functools.partial(_rwkv_kernel, t=t),
        grid=(bsz, s // t),
        in_specs=[
            pl.BlockSpec((1, t, RWKV_IN), lambda b, i: (b, i, 0)),
            _const_spec((1, RWKV_IN)), _const_spec((1, RWKV_DIM)), _const_spec((128, RWKV_DIM)),
            _const_spec((1, RWKV_DIM)), _const_spec((128, RWKV_DIM)), _const_spec((GATE_LORA, RWKV_DIM)),
            _const_spec((1, RWKV_DIM)), _const_spec((1, RWKV_DIM)), _const_spec((1, RWKV_DIM)),
            _const_spec((1, RWKV_DIM)), _const_spec((1, RWKV_DIM)),
        ],
        out_specs=pl.BlockSpec((1, t, RWKV_DIM), lambda b, i: (b, i, 0)),
        out_shape=jax.ShapeDtypeStruct((bsz, s, RWKV_DIM), BF16),
        scratch_shapes=[pltpu.VMEM((V7X_SUBLANES, RWKV_IN), F32), pltpu.VMEM((4, 128, 128), F32),
                        tile, tile, tile, tile, tile, tile, tile, tile],
        compiler_params=_params(("parallel", "arbitrary")),
        name="rwkv7",
    )(zr, row(mu), row(w0), wup_pad, row(a0), aup_pad, g_up.astype(BF16), row(k_k), row(k_a), row(r_k),
      row(gn_g), row(gn_b))


def _gelu_tanh(x):
    return 0.5 * x * (1.0 + jnp.tanh(math.sqrt(2.0 / math.pi) * (x + 0.044715 * (x * x * x))))


def _compress_kernel(x_ref, pe_ref, w1_ref, w2_ref, kc_ref, vc_ref, *, n_rows):
    half = CMP_LEN // 2
    for kv, o_ref in enumerate((kc_ref, vc_ref)):
        first = jnp.zeros((n_rows, 2 * CMP_HIDDEN), F32)
        second = jnp.zeros((n_rows, 2 * CMP_HIDDEN), F32)
        for p in range(half):
            xa = x_ref[0, kv, pl.ds(p, n_rows, stride=CMP_STRIDE), :]
            first += _dot((xa + pe_ref[kv, p:p + 1, :]).astype(BF16), w1_ref[kv, p])
            second += _dot((xa + pe_ref[kv, half + p:half + p + 1, :]).astype(BF16), w1_ref[kv, half + p])
        pre = first + pltpu.roll(second, n_rows - 1, 0)
        hid = _gelu_tanh(pre).astype(BF16)
        width = o_ref.shape[3]
        if kv == 0:
            lane = lax.broadcasted_iota(jnp.int32, (1, V7X_LANES), 1)
            bias_ones = ((lane == BIAS_LANES[0]) | (lane == BIAS_LANES[1])).astype(F32)
            outs = [_dot(hid, w2_ref[g]) + bias_ones for g in range(NSA_GROUPS)]
        else:
            out = _dot(hid, w2_ref[NSA_GROUPS])
            outs = [out[:, g * HEAD_DIM:(g + 1) * HEAD_DIM] for g in range(NSA_GROUPS)]
        for g in range(NSA_GROUPS):
            o_ref[0, g, 0:CMP_PAD, :] = jnp.zeros((CMP_PAD, width), o_ref.dtype)
            o_ref[0, g, CMP_PAD:CMP_PAD + n_rows, :] = outs[g].astype(o_ref.dtype)
            tail = o_ref.shape[2] - CMP_PAD - n_rows
            o_ref[0, g, CMP_PAD + n_rows:, :] = jnp.zeros((tail, width), o_ref.dtype)


def _pair_diag(w):
    z = jnp.zeros_like(w)
    return jnp.concatenate([jnp.concatenate([w, z], axis=-1), jnp.concatenate([z, w], axis=-1)], axis=-2)


def _compress(kcvc, pe_k, w1_k, w2_k, pe_v, w1_v, w2_v):
    bsz, _, s, _ = kcvc.shape
    n_rows = s // CMP_STRIDE
    rows_out = CMP_PAD + n_rows + V7X_SUBLANES
    pe = jnp.stack([jnp.concatenate([pe_k, pe_k], axis=1), jnp.concatenate([pe_v, pe_v], axis=1)])
    w1 = jnp.stack([_pair_diag(w1_k.reshape(CMP_LEN, HEAD_DIM, CMP_HIDDEN)),
                    _pair_diag(w1_v.reshape(CMP_LEN, HEAD_DIM, CMP_HIDDEN))]).astype(BF16)
    zk = jnp.zeros_like(w2_k)
    w2_pad = lambda g: jnp.concatenate([jnp.concatenate([w2_k if i == g else zk, zk], axis=1)
                                        for i in range(NSA_GROUPS)], axis=0)
    w2 = jnp.stack([w2_pad(0), w2_pad(1), _pair_diag(w2_v)]).astype(BF16)
    shapes = tuple(jax.ShapeDtypeStruct((bsz, NSA_GROUPS, rows_out, w), BF16) for w in (128, HEAD_DIM))
    specs = tuple(pl.BlockSpec((1, NSA_GROUPS, rows_out, w), lambda b: (b, 0, 0, 0)) for w in (128, HEAD_DIM))
    return pl.pallas_call(
        functools.partial(_compress_kernel, n_rows=n_rows),
        grid=(bsz,),
        in_specs=[pl.BlockSpec((1, 2, s, 128), lambda b: (b, 0, 0, 0)), _const_spec(pe.shape), _const_spec(w1.shape),
                  _const_spec(w2.shape)],
        out_specs=specs,
        out_shape=shapes,
        compiler_params=_params(("parallel",)),
        name="nsa_compress",
    )(kcvc, pe, w1, w2)


def _cmp_select_kernel(q_ref, kc_ref, vc_ref, tz_ref, gate_ref, oc_ref, sel_ref, *, n_far):
    qi = pl.program_id(1)
    qb = Q_BLOCK
    rowi = lax.broadcasted_iota(jnp.int32, (qb, 128), 0)
    lane = lax.broadcasted_iota(jnp.int32, (qb, 128), 1)
    row_f = rowi.astype(F32)
    sg = _sigmoid(gate_ref[0])
    near0 = pl.multiple_of(qi * 8, 8)
    n_first = qi * 8 - CMP_PAD
    for g in range(NSA_GROUPS):
        q4 = q_ref[0, g * NSA_HPG:(g + 1) * NSA_HPG].reshape(NSA_HPG * qb, 128)
        heads = [g * NSA_HPG + p for p in range(NSA_HPG)]
        cols = [slice(p * qb, (p + 1) * qb) for p in range(NSA_HPG)]

        def masked_keys(kt, ok):
            return jnp.where((lane == MASK_LANE) & jnp.logical_not(ok), jnp.asarray(NEG, BF16), kt)

        tiles, values, n_of_lane = [], [], []
        for j in range(n_far):
            rows = slice(CMP_PAD + 128 * j, CMP_PAD + 128 * (j + 1))
            tiles.append(_dot_nt(masked_keys(kc_ref[0, g, rows, :], (128 * j + rowi) < n_first), q4))
            values.append(vc_ref[0, g, rows, :])
            n_of_lane.append(128 * j + lane)
        s = _dot_nt(masked_keys(kc_ref[0, g, pl.ds(near0, 128), :], (n_first + rowi) >= 0), q4)
        tiles.append(jnp.concatenate([s[:, cols[p]] + tz_ref[heads[p], TZ_CMP_NEAR] for p in range(NSA_HPG)], axis=1))
        values.append(vc_ref[0, g, pl.ds(near0, 128), :])
        n_of_lane.append(n_first + lane)

        m = tiles[0].max(axis=0, keepdims=True)
        for tl in tiles[1:]:
            m = jnp.maximum(m, tl.max(axis=0, keepdims=True))
        es = [jnp.exp(tl - m) for tl in tiles]
        den = es[0].sum(axis=0, keepdims=True)
        for e in es[1:]:
            den = den + e.sum(axis=0, keepdims=True)
        inv = jnp.where(m > 0.5 * NEG, 1.0 / jnp.maximum(den, 1e-30), 0.0)
        o_t = jnp.zeros((HEAD_DIM, NSA_HPG * qb), F32)
        imp = jnp.zeros((qb, 128), F32)
        for j, e in enumerate(es):
            pc = e * inv
            o_t = o_t + _dot_tn(values[j], pc.astype(BF16))
            psum = pc[:, cols[0]] + pc[:, cols[1]] + pc[:, cols[2]] + pc[:, cols[3]]
            ov = ((n_of_lane[j] >= 4 * rowi - 1) & (n_of_lane[j] <= 4 * rowi + 3)).astype(F32)
            imp = imp + _mm(ov, psum, "exact_a")
        for p, h in enumerate(heads):
            oc_ref[0, :, h * HEAD_DIM:(h + 1) * HEAD_DIM] = o_t[:, cols[p]].T * sg[:, 3 * h:3 * h + 1]
        cur = 2 * qi + (lane >= SLC_LEN).astype(jnp.int32)
        forced = (rowi == 0) | (rowi == cur) | (rowi == cur - 1)
        score = jnp.where(rowi <= cur, jnp.where(forced, -3e38, imp), -1.0)
        sel = jnp.where(forced & (rowi <= cur), 1.0, 0.0)
        for _ in range(N_SEL - 3):
            mx = score.max(axis=0, keepdims=True)
            idx = jnp.where(score == mx, row_f, 128.0).min(axis=0, keepdims=True)
            pick = row_f == idx
            sel = jnp.where(pick, 1.0, sel)
            score = jnp.where(pick, -3e38, score)
        sel_ref[0, g] = jnp.where(sel.T > 0.5, 0.0, NEG).astype(sel_ref.dtype)


def _cmp_select(q, kc, vc, tz, gate):
    bsz, _, s, _ = q.shape
    nq = s // Q_BLOCK
    n_far = max(0, -(-(8 * (nq - 1) - CMP_PAD) // 128))
    rows_c = kc.shape[2]
    return pl.pallas_call(
        functools.partial(_cmp_select_kernel, n_far=n_far),
        grid=(bsz, nq),
        in_specs=[
            pl.BlockSpec((1, NSA_HEADS, Q_BLOCK, 128), lambda b, i: (b, 0, i, 0)),
            pl.BlockSpec((1, NSA_GROUPS, rows_c, 128), lambda b, i: (b, 0, 0, 0)),
            pl.BlockSpec((1, NSA_GROUPS, rows_c, HEAD_DIM), lambda b, i: (b, 0, 0, 0)),
            _const_spec(tz.shape),
            pl.BlockSpec((1, Q_BLOCK, 128), lambda b, i: (b, i, 0)),
        ],
        out_specs=(
            pl.BlockSpec((1, Q_BLOCK, NSA_DIM), lambda b, i: (b, i, 0)),
            pl.BlockSpec((1, NSA_GROUPS, Q_BLOCK, 128), lambda b, i: (b, 0, i, 0)),
        ),
        out_shape=(jax.ShapeDtypeStruct((bsz, s, NSA_DIM), F32),
                   jax.ShapeDtypeStruct((bsz, NSA_GROUPS, s, 128), BF16)),
        compiler_params=_params(("parallel", "parallel")),
        name="nsa_cmp_select",
    )(q, kc, vc, tz, gate)


SEL_QBLOCKS = 2
SEL_TILES = 4


def _attn_scores(qa, ka, bias_fns, m_prev):
    qb = Q_BLOCK
    s = _dot_nt(ka, qa)
    if bias_fns is not None:
        s = jnp.concatenate(
            [jnp.concatenate([s[i * qb:(i + 1) * qb, p * qb:(p + 1) * qb] + fn(p) for p in range(NSA_HPG)], axis=1)
             for i, fn in enumerate(bias_fns)], axis=0)
    return s, jnp.maximum(m_prev, s.max(axis=0, keepdims=True))


def _attn_accumulate(s, vts, m_prev, m_next, acc_ref):
    pexp = jnp.exp(s - m_next)
    acc_ref[...] = jnp.exp(m_prev - m_next) * acc_ref[...] + _dot(jnp.concatenate(vts, axis=1), pexp.astype(BF16))


def _sel_win_kernel(q_ref, k_ref, vt_ref, e_ref, sel_ref, tz_ref, gate_ref, oc_ref, o_ref, m_ref, acc_ref,
                    s0_ref, s1_ref, mm0_ref, mm1_ref):
    qb = Q_BLOCK
    win_tiles = WINDOW // qb
    chains = [(sub, g) for sub in range(SEL_QBLOCKS) for g in range(NSA_GROUPS)]
    n_chains = len(chains)
    qis = [pl.program_id(1) * SEL_QBLOCKS + sub for sub in range(SEL_QBLOCKS)]
    m_ref[...] = jnp.full(m_ref.shape, NEG, F32)
    acc_ref[...] = jnp.zeros(acc_ref.shape, F32)
    qrows = [slice(sub * qb, (sub + 1) * qb) for sub in range(SEL_QBLOCKS)]
    q4 = [q_ref[0, g * NSA_HPG:(g + 1) * NSA_HPG, qrows[sub], :].reshape(NSA_HPG * qb, 128) for sub, g in chains]
    selm = [jnp.concatenate([sel_ref[0, g, qrows[sub], :]] * NSA_HPG, axis=0) for sub, g in chains]

    def key_rows(kt, n=1):
        return pl.ds(pl.multiple_of(kt * qb, qb), n * qb)

    def bias_of(g, kind):
        return lambda p: tz_ref[g * NSA_HPG + p, kind]

    def keys_sel(g, kt, n=1):
        return jnp.concatenate([k_ref[0, g, key_rows(kt, n), :], e_ref[key_rows(kt, n), :]], axis=1)

    block = lax.broadcasted_iota(jnp.int32, (NSA_HPG * qb, 128), 1)

    n_far = [(jnp.maximum(qi - 1, 0) + SEL_TILES - 1) // SEL_TILES for qi in qis]
    n_run = n_far[-1] | 1
    last_tile0 = k_ref.shape[2] // qb - SEL_TILES

    def tile0(step):
        return jnp.minimum(step * SEL_TILES, last_tile0)

    def score_far(step, s_ref, mm_ref):
        for c, (sub, g) in enumerate(chains):
            hidden = (block >= 2 * (qis[sub] - 1)) | (step >= n_far[sub])
            qa = jnp.concatenate([q4[c], jnp.where(hidden, jnp.asarray(NEG, BF16), selm[c])], axis=1)
            m_prev = m_ref[c]
            s, m_next = _attn_scores(qa, keys_sel(g, tile0(step), SEL_TILES), None, m_prev)
            s_ref[c] = s
            mm_ref[c, 0:1, :] = m_prev
            mm_ref[c, 1:2, :] = m_next
            m_ref[c] = m_next

    def accumulate_far(step, s_ref, mm_ref):
        for c, (sub, g) in enumerate(chains):
            _attn_accumulate(s_ref[c], [vt_ref[0, g, tile0(step) + i] for i in range(SEL_TILES)],
                             mm_ref[c, 0:1, :], mm_ref[c, 1:2, :], acc_ref.at[c])

    score_far(0, s0_ref, mm0_ref)

    def far_body(i, carry):
        score_far(2 * i + 1, s1_ref, mm1_ref)
        accumulate_far(2 * i, s0_ref, mm0_ref)
        score_far(2 * i + 2, s0_ref, mm0_ref)
        accumulate_far(2 * i + 1, s1_ref, mm1_ref)
        return carry

    lax.fori_loop(0, n_run // 2, far_body, 0)

    scored = []
    for c, (sub, g) in enumerate(chains):
        qi = qis[sub]
        kt_sub = jnp.maximum(qi - 1, 0)
        kinds = (jnp.where(qi >= 1, TZ_SUB, TZ_MASKED), TZ_DIAG)
        m_prev = m_ref[c]
        s, m_next = _attn_scores(jnp.concatenate([q4[c], selm[c]], axis=1),
                                 jnp.concatenate([keys_sel(g, kt_sub), keys_sel(g, qi)], axis=0),
                                 [bias_of(g, kind) for kind in kinds], m_prev)
        scored.append((c, s, [vt_ref[0, g, kt_sub], vt_ref[0, g, qi]], m_prev, m_next))
    for c, (sub, g) in enumerate(chains):
        qi = qis[sub]
        kw = NSA_GROUPS + g
        ks, vts, biases = [], [], []
        for d in range(win_tiles, -1, -1):
            kt = jnp.maximum(qi - d, 0)
            kind = TZ_DIAG if d == 0 else TZ_SUB if d == 1 else TZ_WIN_OLD if d == win_tiles else TZ_ZERO
            if d > 0:
                kind = jnp.where(qi >= d, kind, TZ_MASKED)
            ks.append(k_ref[0, kw, key_rows(kt), :])
            vts.append(vt_ref[0, kw, kt])
            biases.append(bias_of(g, kind))
        m_prev = m_ref[n_chains + c]
        s, m_next = _attn_scores(q4[c], jnp.concatenate(ks, axis=0), biases, m_prev)
        scored.append((n_chains + c, s, vts, m_prev, m_next))
    accumulate_far(n_run - 1, s0_ref, mm0_ref)
    for stream, s, vts, m_prev, m_next in scored:
        _attn_accumulate(s, vts, m_prev, m_next, acc_ref.at[stream])

    for c, (sub, g) in enumerate(chains):
        sg = _sigmoid(gate_ref[0, qrows[sub], :])
        outs = []
        for stream in (c, n_chains + c):
            acc = acc_ref[stream]
            outs.append(acc[0:HEAD_DIM] * (1.0 / acc[HEAD_DIM:HEAD_DIM + 1]))
        for p in range(NSA_HPG):
            h = g * NSA_HPG + p
            qcols = slice(p * qb, (p + 1) * qb)
            cols = slice(h * HEAD_DIM, (h + 1) * HEAD_DIM)
            y = (oc_ref[0, qrows[sub], cols] + sg[:, 3 * h + 1:3 * h + 2] * outs[0][:, qcols].T
                 + sg[:, 3 * h + 2:3 * h + 3] * outs[1][:, qcols].T)
            o_ref[0, qrows[sub], cols] = y.astype(o_ref.dtype)


def _sel_win(q, k4, v4, sel, tz, gate, oc):
    bsz, _, s, _ = q.shape
    nq = s // Q_BLOCK
    assert nq % SEL_TILES == 0 and nq % SEL_QBLOCKS == 0
    member = (jnp.arange(s)[:, None] // SLC_LEN == jnp.arange(128)[None, :]).astype(BF16)
    tq = SEL_QBLOCKS * Q_BLOCK
    n_chains = SEL_QBLOCKS * NSA_GROUPS
    lanes = NSA_HPG * Q_BLOCK
    return pl.pallas_call(
        _sel_win_kernel,
        grid=(bsz, nq // SEL_QBLOCKS),
        in_specs=[
            pl.BlockSpec((1, NSA_HEADS, tq, 128), lambda b, i: (b, 0, i, 0)),
            pl.BlockSpec((1, 4, s, 128), lambda b, i: (b, 0, 0, 0)),
            pl.BlockSpec((1, 4, nq, VT_ROWS, Q_BLOCK), lambda b, i: (b, 0, 0, 0, 0)),
            _const_spec((s, 128)),
            pl.BlockSpec((1, NSA_GROUPS, tq, 128), lambda b, i: (b, 0, i, 0)),
            _const_spec(tz.shape),
            pl.BlockSpec((1, tq, 128), lambda b, i: (b, i, 0)),
            pl.BlockSpec((1, tq, NSA_DIM), lambda b, i: (b, i, 0)),
        ],
        out_specs=pl.BlockSpec((1, tq, NSA_DIM), lambda b, i: (b, i, 0)),
        out_shape=jax.ShapeDtypeStruct((bsz, s, NSA_DIM), BF16),
        scratch_shapes=[pltpu.VMEM((2 * n_chains, 1, lanes), F32),
                        pltpu.VMEM((2 * n_chains, VT_ROWS, lanes), F32),
                        pltpu.VMEM((n_chains, SEL_TILES * Q_BLOCK, lanes), F32),
                        pltpu.VMEM((n_chains, SEL_TILES * Q_BLOCK, lanes), F32),
                        pltpu.VMEM((n_chains, 2, lanes), F32),
                        pltpu.VMEM((n_chains, 2, lanes), F32)],
        compiler_params=_params(("parallel", "parallel")),
        name="nsa_sel_win",
    )(q, k4, v4, member, sel, tz, gate, oc)


XATTN_TM = 1024


def _mem_kv_kernel(mem_ref, wk_ref, wv_ref, k_ref, v_ref):
    mb = mem_ref[0].astype(BF16)
    k_ref[0] = _dot(mb, wk_ref[...]).astype(BF16)
    v_ref[0] = _dot(mb, wv_ref[...]).astype(BF16)


def _mem_kv(mem, wk, wv):
    bsz, m, d = mem.shape
    shape = jax.ShapeDtypeStruct((bsz, m, d), BF16)
    spec = pl.BlockSpec((1, m, d), lambda b: (b, 0, 0))
    return pl.pallas_call(
        _mem_kv_kernel,
        grid=(bsz,),
        in_specs=[spec, _const_spec((d, d)), _const_spec((d, d))],
        out_specs=(spec, spec),
        out_shape=(shape, shape),
        compiler_params=_params(("parallel",)),
        name="xattn_mem_kv",
    )(mem, wk.astype(BF16), wv.astype(BF16))


def _mix_xattn_kernel(x_ref, yr_ref, yn_ref, wm_ref, g2_ref, b2_ref, k_ref, v_ref, wq_ref, wo_ref, g_ref, b_ref, o_ref):
    mixed = _dot(yr_ref[0], wm_ref[0:RWKV_DIM, :]) + _dot(yn_ref[0], wm_ref[RWKV_DIM:, :])
    x = _layer_norm(ALPHA * x_ref[0] + mixed, g2_ref[...], b2_ref[...])
    q = (_dot(x.astype(BF16), wq_ref[...]) * (XATTN_HEAD_DIM ** -0.5)).astype(BF16)
    heads = []
    for h in range(XATTN_HEADS):
        cols = slice(h * XATTN_HEAD_DIM, (h + 1) * XATTN_HEAD_DIM)
        s = _dot_nt(q[:, cols], k_ref[0, :, cols])
        e = jnp.exp(s - s.max(axis=1, keepdims=True))
        p = e * (1.0 / e.sum(axis=1, keepdims=True))
        heads.append(_dot(p.astype(BF16), v_ref[0, :, cols]).astype(BF16))
    o = _dot(jnp.concatenate(heads, axis=1), wo_ref[...])
    o_ref[0] = _layer_norm(ALPHA * x + o, g_ref[...], b_ref[...])


def _mix_cross_attention_ln(x3d, yr, yn, w_out, g2, b2, k, v, wq, wo, g, b):
    bsz, s, d = x3d.shape
    m = k.shape[1]
    tm = min(XATTN_TM, s)
    rows = lambda w: pl.BlockSpec((1, tm, w), lambda bb, i: (bb, i, 0))
    mem = pl.BlockSpec((1, m, d), lambda bb, i: (bb, 0, 0))
    vec = _const_spec((1, d))
    return pl.pallas_call(
        _mix_xattn_kernel,
        grid=(bsz, s // tm),
        in_specs=[rows(d), rows(RWKV_DIM), rows(NSA_DIM), _const_spec(w_out.shape), vec, vec,
                  mem, mem, _const_spec((d, d)), _const_spec((d, d)), vec, vec],
        out_specs=rows(d),
        out_shape=jax.ShapeDtypeStruct((bsz, s, d), F32),
        compiler_params=_params(("parallel", "parallel")),
        name="mix_xattn_ln",
    )(x3d, yr, yn, w_out.astype(BF16), g2.reshape(1, d), b2.reshape(1, d), k, v, wq.astype(BF16), wo.astype(BF16),
      g.reshape(1, d), b.reshape(1, d))


def _nsa_group(rel_bias, q, kcvc, k4, v4, gate, pe_k, w1_k, w2_k, pe_v, w1_v, w2_v):
    tz = _bias_tiles(rel_bias)
    kc, vc = _compress(kcvc, pe_k, w1_k, w2_k, pe_v, w1_v, w2_v)
    oc, sel = _cmp_select(q, kc, vc, tz, gate)
    return _sel_win(q, k4, v4, sel, tz, gate, oc)


def kernel(x, mem, ffn1_w_gate, ffn1_w_up, ffn1_w_down, ln1_g, ln1_b, mix_w_in, rwkv_mu, rwkv_w0, rwkv_w_up, rwkv_a0, rwkv_a_up, rwkv_g_up, rwkv_k_k, rwkv_k_a, rwkv_r_k, rwkv_gn_g, rwkv_gn_b, nsa_pe_k, nsa_w1_k, nsa_w2_k, nsa_pe_v, nsa_w1_v, nsa_w2_v, mix_w_out, ln2_g, ln2_b, xattn_wq, xattn_wk, xattn_wv, xattn_wo, ln3_g, ln3_b, ffn2_w_gate, ffn2_w_up, ffn2_w_down, ln4_g, ln4_b, rel_bias):
    bsz, s, d = x.shape
    rows = bsz * s
    for l in range(DEPTH):
        x1 = _ffn_ln(x.reshape(rows, d), ffn1_w_gate[l], ffn1_w_up[l], ffn1_w_down[l], ln1_g[l], ln1_b[l])
        zr, q, kcvc, k4, v4, gate = _in_projection(x1.reshape(bsz, s, d), mix_w_in[l], rel_bias)
        y_rwkv = _rwkv_group(zr, rwkv_mu[l], rwkv_w0[l], rwkv_w_up[l], rwkv_a0[l], rwkv_a_up[l], rwkv_g_up[l],
                             rwkv_k_k[l], rwkv_k_a[l], rwkv_r_k[l], rwkv_gn_g[l], rwkv_gn_b[l])
        y_nsa = _nsa_group(rel_bias, q, kcvc, k4, v4, gate, nsa_pe_k[l], nsa_w1_k[l], nsa_w2_k[l],
                           nsa_pe_v[l], nsa_w1_v[l], nsa_w2_v[l])
        mk, mv = _mem_kv(mem, xattn_wk[l], xattn_wv[l])
        x3 = _mix_cross_attention_ln(x1.reshape(bsz, s, d), y_rwkv, y_nsa, mix_w_out[l], ln2_g[l], ln2_b[l],
                                     mk, mv, xattn_wq[l], xattn_wo[l], ln3_g[l], ln3_b[l])
        x = _ffn_ln(x3.reshape(rows, d), ffn2_w_gate[l], ffn2_w_up[l], ffn2_w_down[l], ln4_g[l], ln4_b[l])
        x = x.reshape(bsz, s, d)
    return x
```

```python
import functools
import math

import numpy as np
import jax
import jax.numpy as jnp
from jax import lax
from jax.experimental import pallas as pl
from jax.experimental.pallas import tpu as pltpu

F32 = jnp.float32
BF16 = jnp.bfloat16
HIGHEST = lax.Precision.HIGHEST

D_MODEL = 1024
DEPTH = 1
RWKV_HEADS = 8
HEAD_DIM = 64
RWKV_DIM = RWKV_HEADS * HEAD_DIM
DECAY_LORA = 64
AAA_LORA = 64
GATE_LORA = 128
RWKV_IN = 3 * RWKV_DIM + DECAY_LORA + AAA_LORA + GATE_LORA
RWKV_GN_EPS = 64e-5
NSA_HEADS = 8
NSA_GROUPS = 2
NSA_HPG = NSA_HEADS // NSA_GROUPS
NSA_DIM = NSA_HEADS * HEAD_DIM
CMP_LEN = 32
CMP_STRIDE = 16
CMP_HIDDEN = 128
SLC_LEN = 64
N_SEL = 16
WINDOW = 512
Q_BLOCK = 128
FORCED_SCORE = 1e4
NUM_BUCKETS = 32
MAX_DISTANCE = 128
XATTN_HEADS = 4
XATTN_HEAD_DIM = D_MODEL // XATTN_HEADS
D_FF = 2816
LN_EPS = 1e-5
ALPHA = (2.0 * DEPTH) ** 0.25
NEG = -1e30

V7X_LANES = 128
V7X_SUBLANES = 8
V7X_VMEM_LIMIT_BYTES = 56 * 1024 * 1024

CHUNK = 64
CMP_PAD = 120
VT_ROWS = 80


def _dot(a, b, prec=None):
    return jnp.dot(a, b, preferred_element_type=F32, precision=prec)


def _dot_nt(a, b, prec=None):
    return lax.dot_general(a, b, (((1,), (1,)), ((), ())), preferred_element_type=F32, precision=prec)


def _dot_tn(a, b, prec=None):
    return lax.dot_general(a, b, (((0,), (0,)), ((), ())), preferred_element_type=F32, precision=prec)


def _sigmoid(x):
    return 1.0 / (1.0 + jnp.exp(-x))


def _layer_norm(y, g, b):
    mu = jnp.mean(y, axis=-1, keepdims=True)
    yc = y - mu
    var = jnp.mean(yc * yc, axis=-1, keepdims=True)
    return yc * lax.rsqrt(var + LN_EPS) * g + b


def _params(sem):
    return pltpu.CompilerParams(dimension_semantics=sem, vmem_limit_bytes=V7X_VMEM_LIMIT_BYTES)


def _const_spec(shape, single_buffer=False):
    nd = len(shape)
    if single_buffer:
        return pl.BlockSpec(shape, lambda *_: (0,) * nd, pipeline_mode=pl.Buffered(1))
    return pl.BlockSpec(shape, lambda *_: (0,) * nd)


def _bucket_thresholds():
    n = np.arange(0, 4 * MAX_DISTANCE)
    max_exact = NUM_BUCKETS // 2
    nf = np.maximum(n, max_exact).astype(np.float32)
    large = max_exact + (np.log(nf / np.float32(max_exact)) / np.float32(math.log(MAX_DISTANCE / max_exact))
                         * np.float32(NUM_BUCKETS - max_exact)).astype(np.int32)
    large = np.minimum(large, NUM_BUCKETS - 1)
    bucket = np.where(n < max_exact, n, large)
    return [int(np.argmax(bucket >= b)) for b in range(1, NUM_BUCKETS)]


_BUCKET_THR = _bucket_thresholds()


TZ_CMP_NEAR, TZ_DIAG, TZ_SUB, TZ_ZERO, TZ_MASKED, TZ_WIN_OLD, TZ_KINDS = 0, 1, 2, 3, 4, 5, 6


def _bias_tiles_kernel(rb_ref, o_ref):
    h = pl.program_id(0)
    r = lax.broadcasted_iota(jnp.int32, (Q_BLOCK, Q_BLOCK), 0)
    c = lax.broadcasted_iota(jnp.int32, (Q_BLOCK, Q_BLOCK), 1)
    far = jnp.full((Q_BLOCK, Q_BLOCK), rb_ref[NUM_BUCKETS - 1, h], F32)
    far_hi = far.astype(BF16).astype(F32)
    far_added = far_hi + (far - far_hi).astype(BF16).astype(F32)
    dists = (c - CMP_STRIDE * (r - CMP_PAD) - (CMP_LEN - 1), c - r, Q_BLOCK + c - r)
    for kind, dist in enumerate(dists):
        val = jnp.full((Q_BLOCK, Q_BLOCK), rb_ref[0, h], F32)
        for b in range(1, NUM_BUCKETS):
            val = jnp.where(dist >= _BUCKET_THR[b - 1], rb_ref[b, h], val)
        o_ref[0, kind] = jnp.where(dist >= 0, val - far_added, NEG)
    o_ref[0, TZ_ZERO] = far - far_added
    o_ref[0, TZ_MASKED] = jnp.full((Q_BLOCK, Q_BLOCK), NEG, F32)
    o_ref[0, TZ_WIN_OLD] = jnp.where(r > c, far - far_added, NEG)


def _bias_tiles(rel_bias):
    return pl.pallas_call(
        _bias_tiles_kernel,
        grid=(NSA_HEADS,),
        in_specs=[pl.BlockSpec(memory_space=pltpu.SMEM)],
        out_specs=pl.BlockSpec((1, TZ_KINDS, Q_BLOCK, Q_BLOCK), lambda h: (h, 0, 0, 0)),
        out_shape=jax.ShapeDtypeStruct((NSA_HEADS, TZ_KINDS, Q_BLOCK, Q_BLOCK), F32),
        compiler_params=_params(("arbitrary",)),
        name="bias_tiles",
    )(rel_bias)


FFN_TM = 1024
FFN_TF = 256


def _ffn_kernel(x_ref, wg_ref, wu_ref, wd_ref, g_ref, b_ref, o_ref, acc_ref):
    x = x_ref[...]
    xb = x.astype(BF16)
    for c in range(wg_ref.shape[1] // FFN_TF):
        cols = slice(c * FFN_TF, (c + 1) * FFN_TF)
        hg = _dot(xb, wg_ref[:, cols])
        hu = _dot(xb, wu_ref[:, cols])
        h = hg * _sigmoid(hg) * hu
        part = _dot(h.astype(BF16), wd_ref[cols, :])
        if c == 0:
            acc_ref[...] = part
        else:
            acc_ref[...] += part
    o_ref[...] = _layer_norm(ALPHA * x + 0.5 * acc_ref[...], g_ref[...], b_ref[...])


def _ffn_ln(x2d, wg, wu, wd, g, b):
    rows, d = x2d.shape
    f = wg.shape[1]
    assert f % FFN_TF == 0
    tm = min(FFN_TM, rows)
    return pl.pallas_call(
        _ffn_kernel,
        grid=(rows // tm,),
        in_specs=[
            pl.BlockSpec((tm, d), lambda i: (i, 0)),
            _const_spec((d, f), single_buffer=True),
            _const_spec((d, f), single_buffer=True),
            _const_spec((f, d), single_buffer=True),
            _const_spec((1, d)),
            _const_spec((1, d)),
        ],
        out_specs=pl.BlockSpec((tm, d), lambda i: (i, 0)),
        out_shape=jax.ShapeDtypeStruct((rows, d), F32),
        scratch_shapes=[pltpu.VMEM((tm, d), F32)],
        compiler_params=_params(("parallel",)),
        name="ffn_ln",
    )(x2d, wg.astype(BF16), wu.astype(BF16), wd.astype(BF16), g.reshape(1, d), b.reshape(1, d))


PROJ_TM = 512
_C_RWKV = 0
_C_Q = _C_RWKV + RWKV_IN
_C_KCVC = _C_Q + NSA_HEADS * 128
_C_K = _C_KCVC + 256
_C_V = _C_K + 4 * 128
_C_GATE = _C_V + 4 * 128
_C_END = _C_GATE + 128
BIAS_LANES = (HEAD_DIM, HEAD_DIM + 1)
MASK_LANE = HEAD_DIM + 2


def _inproj_kernel(x_ref, w_ref, qx_ref, zr_ref, q_ref, kcvc_ref, k_ref, v_ref, gate_ref):
    xb = x_ref[0].astype(BF16)
    lane = lax.broadcasted_iota(jnp.int32, (1, V7X_LANES), 1)
    zr_ref[0] = _dot(xb, w_ref[:, _C_RWKV:_C_Q])
    zq = _dot(xb, w_ref[:, _C_Q:_C_KCVC]) * (HEAD_DIM ** -0.5)
    for h in range(NSA_HEADS):
        q_ref[0, h] = (zq[:, h * 128:(h + 1) * 128] + qx_ref[h:h + 1, :]).astype(BF16)
    zc = _dot(xb, w_ref[:, _C_KCVC:_C_K])
    kcvc_ref[0, 0] = zc[:, 0:128]
    kcvc_ref[0, 1] = zc[:, 128:256]
    bias_ones = ((lane == BIAS_LANES[0]) | (lane == BIAS_LANES[1])).astype(F32)
    zk = _dot(xb, w_ref[:, _C_K:_C_V])
    for j in range(4):
        k_ref[0, j] = (zk[:, j * 128:(j + 1) * 128] + bias_ones).astype(BF16)
    one_lane = (lane == HEAD_DIM).astype(F32)
    zvg = _dot(xb, w_ref[:, _C_V:_C_END])
    for j in range(4):
        zv = zvg[:, j * 128:(j + 1) * 128] + one_lane
        for i in range(zv.shape[0] // Q_BLOCK):
            v_ref[0, j, i] = zv[i * Q_BLOCK:(i + 1) * Q_BLOCK].T[0:VT_ROWS].astype(BF16)
    gate_ref[0] = zvg[:, 4 * 128:5 * 128]


def _pack_w_in(w_in):
    d = w_in.shape[0]
    o = RWKV_IN
    q = w_in[:, o:o + 512]
    kc = w_in[:, o + 512:o + 640]
    vc = w_in[:, o + 640:o + 768]
    ks = w_in[:, o + 768:o + 896]
    vs = w_in[:, o + 896:o + 1024]
    kw = w_in[:, o + 1024:o + 1152]
    vw = w_in[:, o + 1152:o + 1280]
    gate = w_in[:, o + 1280:o + 1304]
    z64 = jnp.zeros((d, 64), w_in.dtype)
    pad = lambda m, n: [jnp.concatenate([m[:, i * 64:(i + 1) * 64], z64], axis=1) for i in range(n)]
    gate_pad = jnp.concatenate([gate, jnp.zeros((d, 128 - gate.shape[1]), w_in.dtype)], axis=1)
    cols = ([w_in[:, :o]] + pad(q, NSA_HEADS) + [kc, vc] + pad(ks, NSA_GROUPS) + pad(kw, NSA_GROUPS)
            + pad(vs, NSA_GROUPS) + pad(vw, NSA_GROUPS) + [gate_pad])
    return jnp.concatenate(cols, axis=1).astype(BF16)


def _far_bias_lanes(rel_bias):
    far = rel_bias[NUM_BUCKETS - 1, :]
    hi = far.astype(BF16).astype(F32)
    lane = jnp.arange(V7X_LANES)[None, :]
    return jnp.where(lane == BIAS_LANES[0], hi[:, None],
                     jnp.where(lane == BIAS_LANES[1], (far - hi)[:, None], jnp.where(lane == MASK_LANE, 1.0, 0.0)))


def _in_projection(x3d, w_in, rel_bias):
    bsz, s, d = x3d.shape
    tm = min(PROJ_TM, s)
    wp = _pack_w_in(w_in)
    out_shape = (
        jax.ShapeDtypeStruct((bsz, s, RWKV_IN), F32),
        jax.ShapeDtypeStruct((bsz, NSA_HEADS, s, 128), BF16),
        jax.ShapeDtypeStruct((bsz, 2, s, 128), F32),
        jax.ShapeDtypeStruct((bsz, 4, s, 128), BF16),
        jax.ShapeDtypeStruct((bsz, 4, s // Q_BLOCK, VT_ROWS, Q_BLOCK), BF16),
        jax.ShapeDtypeStruct((bsz, s, 128), F32),
    )
    return pl.pallas_call(
        _inproj_kernel,
        grid=(bsz, s // tm),
        in_specs=[pl.BlockSpec((1, tm, d), lambda b, i: (b, i, 0)), _const_spec((d, _C_END)),
                  _const_spec((NSA_HEADS, 128))],
        out_specs=(
            pl.BlockSpec((1, tm, RWKV_IN), lambda b, i: (b, i, 0)),
            pl.BlockSpec((1, NSA_HEADS, tm, 128), lambda b, i: (b, 0, i, 0)),
            pl.BlockSpec((1, 2, tm, 128), lambda b, i: (b, 0, i, 0)),
            pl.BlockSpec((1, 4, tm, 128), lambda b, i: (b, 0, i, 0)),
            pl.BlockSpec((1, 4, tm // Q_BLOCK, VT_ROWS, Q_BLOCK), lambda b, i: (b, 0, i, 0, 0)),
            pl.BlockSpec((1, tm, 128), lambda b, i: (b, i, 0)),
        ),
        out_shape=out_shape,
        compiler_params=_params(("parallel", "parallel")),
        name="in_projection",
    )(x3d, wp, _far_bias_lanes(rel_bias))


RWKV_T = 512
RWKV_CHUNKS_PER_STEP = 4


def _rwkv_consts():
    r = lax.broadcasted_iota(jnp.int32, (128, 128), 0)
    c = lax.broadcasted_iota(jnp.int32, (128, 128), 1)
    same = (r >= CHUNK) == (c >= CHUNK)
    mask_sl = (same & (r > c)).astype(F32)
    mask_l = (same & (r >= c)).astype(F32)
    eye = (r == c).astype(F32)
    head_ones = same.astype(F32)
    rt = lax.broadcasted_iota(jnp.int32, (CHUNK, CHUNK), 0)
    ct = lax.broadcasted_iota(jnp.int32, (CHUNK, CHUNK), 1)
    tri = (rt >= ct).astype(F32)
    lane = lax.broadcasted_iota(jnp.int32, (1, 128), 1)
    m0 = (lane < CHUNK).astype(F32)
    m1 = 1.0 - m0
    return mask_sl, mask_l, eye, head_ones, tri, m0, m1


def _split2(x):
    hi = x.astype(BF16)
    return hi, (x - hi.astype(F32)).astype(BF16)


def _mm(a, b, mode, dot=_dot):
    if mode == "bf16":
        return dot(a.astype(BF16), b.astype(BF16))
    if mode == "bf16x3":
        ah, al = _split2(a)
        bh, bl = _split2(b)
        return dot(ah, bh) + (dot(ah, bl) + dot(al, bh))
    if mode in ("exact_a", "exact_b"):
        x = b if mode == "exact_a" else a
        hi, rest = x.astype(BF16), None
        rest = x - hi.astype(F32)
        mid = rest.astype(BF16)
        lo = (rest - mid.astype(F32)).astype(BF16)
        if mode == "exact_a":
            ab = a.astype(BF16)
            return dot(ab, hi) + (dot(ab, mid) + dot(ab, lo))
        bb = b.astype(BF16)
        return dot(hi, bb) + (dot(mid, bb) + dot(lo, bb))
    raise ValueError(mode)


RWKV_MODES = dict(p="bf16", inv="bf16", av="bf16", wu="bf16", qy="bf16", mn="bf16", y="bf16", h="bf16")


def _rwkv_chunk(rs, lws, ks, vs, kks, as_, hs, consts):
    mask_sl, mask_l, eye, _, tri, m0, m1 = consts
    md = RWKV_MODES
    n = len(rs)
    each = range(n)

    def sm(x):
        return jnp.concatenate([x * m0, x * m1], axis=0)

    def dup(x):
        return jnp.concatenate([x, x], axis=0)

    cums = [_mm(tri, lws[i], "exact_a") for i in each]
    a_sm, r_sm, v_sm, kb, kbh, g_c = [], [], [], [], [], []
    for i in each:
        cum, lw, kk, k = cums[i], lws[i], kks[i], ks[i]
        cl = cum[CHUNK - 1:CHUNK, :]
        ka = kk * as_[i]
        g_tail = jnp.exp(cl - cum)
        g_inv = jnp.exp(-cum)
        a_sm.append(sm(-(kk * jnp.exp(cum - lw))))
        r_sm.append(sm(rs[i] * jnp.exp(cum)))
        v_sm.append(sm(vs[i]))
        kb.append(jnp.concatenate([dup(k * g_inv), dup(ka * g_inv)], axis=0))
        kbh.append(jnp.concatenate([sm(k * g_tail), sm(ka * g_tail)], axis=0))
        g_c.append(jnp.exp(cl))
    pm = [_mm(jnp.concatenate([a_sm[i], r_sm[i]], axis=0), kb[i], md["p"], _dot_nt) for i in each]
    a_ak = [pm[i][0:128, 0:128] * mask_sl for i in each]
    a_rr = [jnp.concatenate([pm[i][128:256, 0:128] * mask_l, pm[i][128:256, 128:256] * mask_l], axis=1) for i in each]
    x = [pm[i][0:128, 128:256] * mask_sl for i in each]
    t_inv = [eye + x[i] for i in each]
    for _ in range(5):
        x = [_mm(x[i], x[i], md["inv"]) for i in each]
        t_inv = [t_inv[i] + _mm(t_inv[i], x[i], md["inv"]) for i in each]
    av = [_mm(a_ak[i], v_sm[i], md["av"]) for i in each]
    wu = [_mm(t_inv[i], jnp.concatenate([a_sm[i], av[i]], axis=1), md["wu"]) for i in each]
    z = [jnp.concatenate([jnp.concatenate([jnp.zeros_like(v_sm[i]), v_sm[i]], axis=1), wu[i]], axis=0) for i in each]
    qy = [_mm(a_rr[i], z[i], md["qy"]) for i in each]
    mn = [_mm(kbh[i], z[i], md["mn"], _dot_tn) for i in each]
    n_pairs = len(hs)
    ys = []
    for c0 in range(0, n, n_pairs):
        idx = range(c0, c0 + n_pairs)
        y_sm = [_mm(r_sm[i] + qy[i][:, 0:128], hs[i - c0], md["y"]) + qy[i][:, 128:256] for i in idx]
        hs = [_mm(mn[i][:, 0:128] + eye * g_c[i], hs[i - c0], md["h"]) + mn[i][:, 128:256] for i in idx]
        ys += [y[0:CHUNK] + y[CHUNK:2 * CHUNK] for y in y_sm]
    return ys, hs


def _rwkv_kernel(z_ref, mu_ref, w0_ref, wup_ref, a0_ref, aup_ref, gup_ref, kk_ref, ka_ref, rk_ref,
                 gng_ref, gnb_ref, o_ref, prev_ref, h_ref, r_s, lw_s, k_s, v_s, kk_s, a_s, y_s, g_s, *, t):
    ti = pl.program_id(1)

    @pl.when(ti == 0)
    def _():
        prev_ref[...] = jnp.zeros_like(prev_ref)
        h_ref[...] = jnp.zeros_like(h_ref)

    consts = _rwkv_consts()
    head_ones = consts[3]

    z = z_ref[0]
    row = lax.broadcasted_iota(jnp.int32, (t, 1), 0)
    prev = jnp.where(row == 0, prev_ref[0:1, :], pltpu.roll(z, 1, 0))
    prev_ref[0:1, :] = z[t - 1:t, :]
    zs = z + (prev - z) * mu_ref[...]

    r = zs[:, 0:512]
    k = zs[:, 512:1024]
    v = zs[:, 1024:1536]
    wa = zs[:, 1536:1664]
    gl = zs[:, 1664:1792]
    u = w0_ref[...] + _dot(jnp.tanh(wa).astype(BF16), wup_ref[...])
    lw = (-math.exp(-0.5)) * _sigmoid(u)
    a = _sigmoid(a0_ref[...] + _dot(wa.astype(BF16), aup_ref[...]))
    g_s[...] = _dot(_sigmoid(gl).astype(BF16), gup_ref[...])
    kkr = k * kk_ref[...]
    k2 = k * (1.0 + (a - 1.0) * ka_ref[...])
    r_s[...] = r
    lw_s[...] = lw
    k_s[...] = k2
    v_s[...] = v
    a_s[...] = a
    for pr in range(4):
        sl = slice(pr * 128, (pr + 1) * 128)
        kp = kkr[:, sl]
        ss = _mm(kp * kp, head_ones, "exact_b")
        kk_s[:, sl] = kp * lax.rsqrt(jnp.maximum(ss, 1e-24))

    pairs = [slice(pr * 128, (pr + 1) * 128) for pr in range(4)]

    def chunk_body(c, carry):
        rows = [pl.ds(pl.multiple_of((c * RWKV_CHUNKS_PER_STEP + i) * CHUNK, CHUNK), CHUNK)
                for i in range(RWKV_CHUNKS_PER_STEP)]
        ys, hs = _rwkv_chunk(*[[ref[rw, sl] for rw in rows for sl in pairs]
                               for ref in (r_s, lw_s, k_s, v_s, kk_s, a_s)],
                             [h_ref[pr] for pr in range(4)], consts)
        for i, rw in enumerate(rows):
            for pr, sl in enumerate(pairs):
                y_s[rw, sl] = ys[i * 4 + pr]
        for pr in range(4):
            h_ref[pr] = hs[pr]
        return carry

    lax.fori_loop(0, t // (CHUNK * RWKV_CHUNKS_PER_STEP), chunk_body, 0)

    for pr in range(4):
        sl = slice(pr * 128, (pr + 1) * 128)
        y = y_s[:, sl]
        mean = _mm(y, head_ones, "exact_b") * (1.0 / HEAD_DIM)
        yc = y - mean
        var = _mm(yc * yc, head_ones, "exact_b") * (1.0 / HEAD_DIM)
        yn = yc * lax.rsqrt(var + RWKV_GN_EPS) * gng_ref[:, sl] + gnb_ref[:, sl]
        rp = r_s[:, sl]
        bonus = _mm(rp * k_s[:, sl] * rk_ref[:, sl], head_ones, "exact_b") * v_s[:, sl]
        o_ref[0, :, sl] = ((yn + bonus) * g_s[:, sl]).astype(o_ref.dtype)


def _rwkv_group(zr, mu, w0, w_up, a0, a_up, g_up, k_k, k_a, r_k, gn_g, gn_b):
    bsz, s, _ = zr.shape
    t = min(RWKV_T, s)
    wup_pad = jnp.concatenate([w_up, jnp.zeros_like(a_up)], axis=0).astype(BF16)
    aup_pad = jnp.concatenate([jnp.zeros_like(w_up), a_up], axis=0).astype(BF16)
    row = lambda p: p.reshape(1, -1)
    tile = pltpu.VMEM((t, RWKV_DIM), F32)
    return pl.pallas_call(
        functools.partial(_rwkv_kernel, t=t),
        grid=(bsz, s // t),
        in_specs=[
            pl.BlockSpec((1, t, RWKV_IN), lambda b, i: (b, i, 0)),
            _const_spec((1, RWKV_IN)), _const_spec((1, RWKV_DIM)), _const_spec((128, RWKV_DIM)),
            _const_spec((1, RWKV_DIM)), _const_spec((128, RWKV_DIM)), _const_spec((GATE_LORA, RWKV_DIM)),
            _const_spec((1, RWKV_DIM)), _const_spec((1, RWKV_DIM)), _const_spec((1, RWKV_DIM)),
            _const_spec((1, RWKV_DIM)), _const_spec((1, RWKV_DIM)),
        ],
        out_specs=pl.BlockSpec((1, t, RWKV_DIM), lambda b, i: (b, i, 0)),
        out_shape=jax.ShapeDtypeStruct((bsz, s, RWKV_DIM), BF16),
        scratch_shapes=[pltpu.VMEM((V7X_SUBLANES, RWKV_IN), F32), pltpu.VMEM((4, 128, 128), F32),
                        tile, tile, tile, tile, tile, tile, tile, tile],
        compiler_params=_params(("parallel", "arbitrary")),
        name="rwkv7",
    )(zr, row(mu), row(w0), wup_pad, row(a0), aup_pad, g_up.astype(BF16), row(k_k), row(k_a), row(r_k),
      row(gn_g), row(gn_b))


def _gelu_tanh(x):
    return 0.5 * x * (1.0 + jnp.tanh(math.sqrt(2.0 / math.pi) * (x + 0.044715 * (x * x * x))))


def _compress_kernel(x_ref, pe_ref, w1_ref, w2_ref, kc_ref, vc_ref, *, n_rows):
    half = CMP_LEN // 2
    for kv, o_ref in enumerate((kc_ref, vc_ref)):
        first = jnp.zeros((n_rows, 2 * CMP_HIDDEN), F32)
        second = jnp.zeros((n_rows, 2 * CMP_HIDDEN), F32)
        for p in range(half):
            xa = x_ref[0, kv, pl.ds(p, n_rows, stride=CMP_STRIDE), :]
            first += _dot((xa + pe_ref[kv, p:p + 1, :]).astype(BF16), w1_ref[kv, p])
            second += _dot((xa + pe_ref[kv, half + p:half + p + 1, :]).astype(BF16), w1_ref[kv, half + p])
        pre = first + pltpu.roll(second, n_rows - 1, 0)
        hid = _gelu_tanh(pre).astype(BF16)
        width = o_ref.shape[3]
        if kv == 0:
            lane = lax.broadcasted_iota(jnp.int32, (1, V7X_LANES), 1)
            bias_ones = ((lane == BIAS_LANES[0]) | (lane == BIAS_LANES[1])).astype(F32)
            outs = [_dot(hid, w2_ref[g]) + bias_ones for g in range(NSA_GROUPS)]
        else:
            out = _dot(hid, w2_ref[NSA_GROUPS])
            outs = [out[:, g * HEAD_DIM:(g + 1) * HEAD_DIM] for g in range(NSA_GROUPS)]
        for g in range(NSA_GROUPS):
            o_ref[0, g, 0:CMP_PAD, :] = jnp.zeros((CMP_PAD, width), o_ref.dtype)
            o_ref[0, g, CMP_PAD:CMP_PAD + n_rows, :] = outs[g].astype(o_ref.dtype)
            tail = o_ref.shape[2] - CMP_PAD - n_rows
            o_ref[0, g, CMP_PAD + n_rows:, :] = jnp.zeros((tail, width), o_ref.dtype)


def _pair_diag(w):
    z = jnp.zeros_like(w)
    return jnp.concatenate([jnp.concatenate([w, z], axis=-1), jnp.concatenate([z, w], axis=-1)], axis=-2)


def _compress(kcvc, pe_k, w1_k, w2_k, pe_v, w1_v, w2_v):
    bsz, _, s, _ = kcvc.shape
    n_rows = s // CMP_STRIDE
    rows_out = CMP_PAD + n_rows + V7X_SUBLANES
    pe = jnp.stack([jnp.concatenate([pe_k, pe_k], axis=1), jnp.concatenate([pe_v, pe_v], axis=1)])
    w1 = jnp.stack([_pair_diag(w1_k.reshape(CMP_LEN, HEAD_DIM, CMP_HIDDEN)),
                    _pair_diag(w1_v.reshape(CMP_LEN, HEAD_DIM, CMP_HIDDEN))]).astype(BF16)
    zk = jnp.zeros_like(w2_k)
    w2_pad = lambda g: jnp.concatenate([jnp.concatenate([w2_k if i == g else zk, zk], axis=1)
                                        for i in range(NSA_GROUPS)], axis=0)
    w2 = jnp.stack([w2_pad(0), w2_pad(1), _pair_diag(w2_v)]).astype(BF16)
    shapes = tuple(jax.ShapeDtypeStruct((bsz, NSA_GROUPS, rows_out, w), BF16) for w in (128, HEAD_DIM))
    specs = tuple(pl.BlockSpec((1, NSA_GROUPS, rows_out, w), lambda b: (b, 0, 0, 0)) for w in (128, HEAD_DIM))
    return pl.pallas_call(
        functools.partial(_compress_kernel, n_rows=n_rows),
        grid=(bsz,),
        in_specs=[pl.BlockSpec((1, 2, s, 128), lambda b: (b, 0, 0, 0)), _const_spec(pe.shape), _const_spec(w1.shape),
                  _const_spec(w2.shape)],
        out_specs=specs,
        out_shape=shapes,
        compiler_params=_params(("parallel",)),
        name="nsa_compress",
    )(kcvc, pe, w1, w2)


def _cmp_select_kernel(q_ref, kc_ref, vc_ref, tz_ref, gate_ref, oc_ref, sel_ref, *, n_far):
    qi = pl.program_id(1)
    qb = Q_BLOCK
    rowi = lax.broadcasted_iota(jnp.int32, (qb, 128), 0)
    lane = lax.broadcasted_iota(jnp.int32, (qb, 128), 1)
    row_f = rowi.astype(F32)
    sg = _sigmoid(gate_ref[0])
    near0 = pl.multiple_of(qi * 8, 8)
    n_first = qi * 8 - CMP_PAD
    for g in range(NSA_GROUPS):
        q4 = q_ref[0, g * NSA_HPG:(g + 1) * NSA_HPG].reshape(NSA_HPG * qb, 128)
        heads = [g * NSA_HPG + p for p in range(NSA_HPG)]
        cols = [slice(p * qb, (p + 1) * qb) for p in range(NSA_HPG)]

        def masked_keys(kt, ok):
            return jnp.where((lane == MASK_LANE) & jnp.logical_not(ok), jnp.asarray(NEG, BF16), kt)

        tiles, values, n_of_lane = [], [], []
        for j in range(n_far):
            rows = slice(CMP_PAD + 128 * j, CMP_PAD + 128 * (j + 1))
            tiles.append(_dot_nt(masked_keys(kc_ref[0, g, rows, :], (128 * j + rowi) < n_first), q4))
            values.append(vc_ref[0, g, rows, :])
            n_of_lane.append(128 * j + lane)
        s = _dot_nt(masked_keys(kc_ref[0, g, pl.ds(near0, 128), :], (n_first + rowi) >= 0), q4)
        tiles.append(jnp.concatenate([s[:, cols[p]] + tz_ref[heads[p], TZ_CMP_NEAR] for p in range(NSA_HPG)], axis=1))
        values.append(vc_ref[0, g, pl.ds(near0, 128), :])
        n_of_lane.append(n_first + lane)

        m = tiles[0].max(axis=0, keepdims=True)
        for tl in tiles[1:]:
            m = jnp.maximum(m, tl.max(axis=0, keepdims=True))
        es = [jnp.exp(tl - m) for tl in tiles]
        den = es[0].sum(axis=0, keepdims=True)
        for e in es[1:]:
            den = den + e.sum(axis=0, keepdims=True)
        inv = jnp.where(m > 0.5 * NEG, 1.0 / jnp.maximum(den, 1e-30), 0.0)
        o_t = jnp.zeros((HEAD_DIM, NSA_HPG * qb), F32)
        imp = jnp.zeros((qb, 128), F32)
        for j, e in enumerate(es):
            pc = e * inv
            o_t = o_t + _dot_tn(values[j], pc.astype(BF16))
            psum = pc[:, cols[0]] + pc[:, cols[1]] + pc[:, cols[2]] + pc[:, cols[3]]
            ov = ((n_of_lane[j] >= 4 * rowi - 1) & (n_of_lane[j] <= 4 * rowi + 3)).astype(F32)
            imp = imp + _mm(ov, psum, "exact_a")
        for p, h in enumerate(heads):
            oc_ref[0, :, h * HEAD_DIM:(h + 1) * HEAD_DIM] = o_t[:, cols[p]].T * sg[:, 3 * h:3 * h + 1]
        cur = 2 * qi + (lane >= SLC_LEN).astype(jnp.int32)
        forced = (rowi == 0) | (rowi == cur) | (rowi == cur - 1)
        score = jnp.where(rowi <= cur, jnp.where(forced, -3e38, imp), -1.0)
        sel = jnp.where(forced & (rowi <= cur), 1.0, 0.0)
        for _ in range(N_SEL - 3):
            mx = score.max(axis=0, keepdims=True)
            idx = jnp.where(score == mx, row_f, 128.0).min(axis=0, keepdims=True)
            pick = row_f == idx
            sel = jnp.where(pick, 1.0, sel)
            score = jnp.where(pick, -3e38, score)
        sel_ref[0, g] = jnp.where(sel.T > 0.5, 0.0, NEG).astype(sel_ref.dtype)


def _cmp_select(q, kc, vc, tz, gate):
    bsz, _, s, _ = q.shape
    nq = s // Q_BLOCK
    n_far = max(0, -(-(8 * (nq - 1) - CMP_PAD) // 128))
    rows_c = kc.shape[2]
    return pl.pallas_call(
        functools.partial(_cmp_select_kernel, n_far=n_far),
        grid=(bsz, nq),
        in_specs=[
            pl.BlockSpec((1, NSA_HEADS, Q_BLOCK, 128), lambda b, i: (b, 0, i, 0)),
            pl.BlockSpec((1, NSA_GROUPS, rows_c, 128), lambda b, i: (b, 0, 0, 0)),
            pl.BlockSpec((1, NSA_GROUPS, rows_c, HEAD_DIM), lambda b, i: (b, 0, 0, 0)),
            _const_spec(tz.shape),
            pl.BlockSpec((1, Q_BLOCK, 128), lambda b, i: (b, i, 0)),
        ],
        out_specs=(
            pl.BlockSpec((1, Q_BLOCK, NSA_DIM), lambda b, i: (b, i, 0)),
            pl.BlockSpec((1, NSA_GROUPS, Q_BLOCK, 128), lambda b, i: (b, 0, i, 0)),
        ),
        out_shape=(jax.ShapeDtypeStruct((bsz, s, NSA_DIM), F32),
                   jax.ShapeDtypeStruct((bsz, NSA_GROUPS, s, 128), BF16)),
        compiler_params=_params(("parallel", "parallel")),
        name="nsa_cmp_select",
    )(q, kc, vc, tz, gate)


SEL_QBLOCKS = 4
SEL_TILES = 2


def _attn_scores(qa, ka, bias_fns, m_prev):
    qb = Q_BLOCK
    s = _dot_nt(ka, qa)
    if bias_fns is not None:
        s = jnp.concatenate(
            [jnp.concatenate([s[i * qb:(i + 1) * qb, p * qb:(p + 1) * qb] + fn(p) for p in range(NSA_HPG)], axis=1)
             for i, fn in enumerate(bias_fns)], axis=0)
    return s, jnp.maximum(m_prev, s.max(axis=0, keepdims=True))


def _attn_accumulate(s, vts, m_prev, m_next, acc_ref):
    pexp = jnp.exp(s - m_next)
    acc_ref[...] = jnp.exp(m_prev - m_next) * acc_ref[...] + _dot(jnp.concatenate(vts, axis=1), pexp.astype(BF16))


def _sel_win_kernel(q_ref, k_ref, vt_ref, e_ref, sel_ref, tz_ref, gate_ref, oc_ref, o_ref, m_ref, acc_ref,
                    s0_ref, s1_ref, mm0_ref, mm1_ref):
    qb = Q_BLOCK
    win_tiles = WINDOW // qb
    chains = [(sub, g) for sub in range(SEL_QBLOCKS) for g in range(NSA_GROUPS)]
    n_chains = len(chains)
    qis = [pl.program_id(1) * SEL_QBLOCKS + sub for sub in range(SEL_QBLOCKS)]
    m_ref[...] = jnp.full(m_ref.shape, NEG, F32)
    acc_ref[...] = jnp.zeros(acc_ref.shape, F32)
    qrows = [slice(sub * qb, (sub + 1) * qb) for sub in range(SEL_QBLOCKS)]
    q4 = [q_ref[0, g * NSA_HPG:(g + 1) * NSA_HPG, qrows[sub], :].reshape(NSA_HPG * qb, 128) for sub, g in chains]
    selm = [jnp.concatenate([sel_ref[0, g, qrows[sub], :]] * NSA_HPG, axis=0) for sub, g in chains]

    def key_rows(kt, n=1):
        return pl.ds(pl.multiple_of(kt * qb, qb), n * qb)

    def bias_of(g, kind):
        return lambda p: tz_ref[g * NSA_HPG + p, kind]

    def keys_sel(g, kt, n=1):
        return jnp.concatenate([k_ref[0, g, key_rows(kt, n), :], e_ref[key_rows(kt, n), :]], axis=1)

    block = lax.broadcasted_iota(jnp.int32, (NSA_HPG * qb, 128), 1)

    n_far = [(jnp.maximum(qi - 1, 0) + SEL_TILES - 1) // SEL_TILES for qi in qis]
    n_run = n_far[-1] | 1
    last_tile0 = k_ref.shape[2] // qb - SEL_TILES

    def tile0(step):
        return jnp.minimum(step * SEL_TILES, last_tile0)

    def score_far(step, s_ref, mm_ref):
        for c, (sub, g) in enumerate(chains):
            hidden = (block >= 2 * (qis[sub] - 1)) | (step >= n_far[sub])
            qa = jnp.concatenate([q4[c], jnp.where(hidden, jnp.asarray(NEG, BF16), selm[c])], axis=1)
            m_prev = m_ref[c]
            s, m_next = _attn_scores(qa, keys_sel(g, tile0(step), SEL_TILES), None, m_prev)
            s_ref[c] = s
            mm_ref[c, 0:1, :] = m_prev
            mm_ref[c, 1:2, :] = m_next
            m_ref[c] = m_next

    def accumulate_far(step, s_ref, mm_ref):
        for c, (sub, g) in enumerate(chains):
            _attn_accumulate(s_ref[c], [vt_ref[0, g, tile0(step) + i] for i in range(SEL_TILES)],
                             mm_ref[c, 0:1, :], mm_ref[c, 1:2, :], acc_ref.at[c])

    score_far(0, s0_ref, mm0_ref)

    def far_body(i, carry):
        score_far(2 * i + 1, s1_ref, mm1_ref)
        accumulate_far(2 * i, s0_ref, mm0_ref)
        score_far(2 * i + 2, s0_ref, mm0_ref)
        accumulate_far(2 * i + 1, s1_ref, mm1_ref)
        return carry

    lax.fori_loop(0, n_run // 2, far_body, 0)

    scored = []
    for c, (sub, g) in enumerate(chains):
        qi = qis[sub]
        kt_sub = jnp.maximum(qi - 1, 0)
        kinds = (jnp.where(qi >= 1, TZ_SUB, TZ_MASKED), TZ_DIAG)
        m_prev = m_ref[c]
        s, m_next = _attn_scores(jnp.concatenate([q4[c], selm[c]], axis=1),
                                 jnp.concatenate([keys_sel(g, kt_sub), keys_sel(g, qi)], axis=0),
                                 [bias_of(g, kind) for kind in kinds], m_prev)
        scored.append((c, s, [vt_ref[0, g, kt_sub], vt_ref[0, g, qi]], m_prev, m_next))
    for c, (sub, g) in enumerate(chains):
        qi = qis[sub]
        kw = NSA_GROUPS + g
        ks, vts, biases = [], [], []
        for d in range(win_tiles, -1, -1):
            kt = jnp.maximum(qi - d, 0)
            kind = TZ_DIAG if d == 0 else TZ_SUB if d == 1 else TZ_WIN_OLD if d == win_tiles else TZ_ZERO
            if d > 0:
                kind = jnp.where(qi >= d, kind, TZ_MASKED)
            ks.append(k_ref[0, kw, key_rows(kt), :])
            vts.append(vt_ref[0, kw, kt])
            biases.append(bias_of(g, kind))
        m_prev = m_ref[n_chains + c]
        s, m_next = _attn_scores(q4[c], jnp.concatenate(ks, axis=0), biases, m_prev)
        scored.append((n_chains + c, s, vts, m_prev, m_next))
    accumulate_far(n_run - 1, s0_ref, mm0_ref)
    for stream, s, vts, m_prev, m_next in scored:
        _attn_accumulate(s, vts, m_prev, m_next, acc_ref.at[stream])

    for c, (sub, g) in enumerate(chains):
        sg = _sigmoid(gate_ref[0, qrows[sub], :])
        outs = []
        for stream in (c, n_chains + c):
            acc = acc_ref[stream]
            outs.append(acc[0:HEAD_DIM] * (1.0 / acc[HEAD_DIM:HEAD_DIM + 1]))
        for p in range(NSA_HPG):
            h = g * NSA_HPG + p
            qcols = slice(p * qb, (p + 1) * qb)
            cols = slice(h * HEAD_DIM, (h + 1) * HEAD_DIM)
            y = (oc_ref[0, qrows[sub], cols] + sg[:, 3 * h + 1:3 * h + 2] * outs[0][:, qcols].T
                 + sg[:, 3 * h + 2:3 * h + 3] * outs[1][:, qcols].T)
            o_ref[0, qrows[sub], cols] = y.astype(o_ref.dtype)


def _sel_win(q, k4, v4, sel, tz, gate, oc):
    bsz, _, s, _ = q.shape
    nq = s // Q_BLOCK
    assert nq % SEL_TILES == 0 and nq % SEL_QBLOCKS == 0
    member = (jnp.arange(s)[:, None] // SLC_LEN == jnp.arange(128)[None, :]).astype(BF16)
    tq = SEL_QBLOCKS * Q_BLOCK
    n_chains = SEL_QBLOCKS * NSA_GROUPS
    lanes = NSA_HPG * Q_BLOCK
    return pl.pallas_call(
        _sel_win_kernel,
        grid=(bsz, nq // SEL_QBLOCKS),
        in_specs=[
            pl.BlockSpec((1, NSA_HEADS, tq, 128), lambda b, i: (b, 0, i, 0)),
            pl.BlockSpec((1, 4, s, 128), lambda b, i: (b, 0, 0, 0), pipeline_mode=pl.Buffered(1)),
            pl.BlockSpec((1, 4, nq, VT_ROWS, Q_BLOCK), lambda b, i: (b, 0, 0, 0, 0), pipeline_mode=pl.Buffered(1)),
            _const_spec((s, 128), single_buffer=True),
            pl.BlockSpec((1, NSA_GROUPS, tq, 128), lambda b, i: (b, 0, i, 0)),
            _const_spec(tz.shape, single_buffer=True),
            pl.BlockSpec((1, tq, 128), lambda b, i: (b, i, 0)),
            pl.BlockSpec((1, tq, NSA_DIM), lambda b, i: (b, i, 0)),
        ],
        out_specs=pl.BlockSpec((1, tq, NSA_DIM), lambda b, i: (b, i, 0)),
        out_shape=jax.ShapeDtypeStruct((bsz, s, NSA_DIM), BF16),
        scratch_shapes=[pltpu.VMEM((2 * n_chains, 1, lanes), F32),
                        pltpu.VMEM((2 * n_chains, VT_ROWS, lanes), F32),
                        pltpu.VMEM((n_chains, SEL_TILES * Q_BLOCK, lanes), F32),
                        pltpu.VMEM((n_chains, SEL_TILES * Q_BLOCK, lanes), F32),
                        pltpu.VMEM((n_chains, 2, lanes), F32),
                        pltpu.VMEM((n_chains, 2, lanes), F32)],
        compiler_params=_params(("parallel", "parallel")),
        name="nsa_sel_win",
    )(q, k4, v4, member, sel, tz, gate, oc)


XATTN_TM = 1024


def _mem_kv_kernel(mem_ref, wk_ref, wv_ref, k_ref, v_ref):
    mb = mem_ref[0].astype(BF16)
    k_ref[0] = _dot(mb, wk_ref[...]).astype(BF16)
    v_ref[0] = _dot(mb, wv_ref[...]).astype(BF16)


def _mem_kv(mem, wk, wv):
    bsz, m, d = mem.shape
    shape = jax.ShapeDtypeStruct((bsz, m, d), BF16)
    spec = pl.BlockSpec((1, m, d), lambda b: (b, 0, 0))
    return pl.pallas_call(
        _mem_kv_kernel,
        grid=(bsz,),
        in_specs=[spec, _const_spec((d, d)), _const_spec((d, d))],
        out_specs=(spec, spec),
        out_shape=(shape, shape),
        compiler_params=_params(("parallel",)),
        name="xattn_mem_kv",
    )(mem, wk.astype(BF16), wv.astype(BF16))


def _mix_xattn_kernel(x_ref, yr_ref, yn_ref, wm_ref, g2_ref, b2_ref, k_ref, v_ref, wq_ref, wo_ref, g_ref, b_ref, o_ref):
    mixed = _dot(yr_ref[0], wm_ref[0:RWKV_DIM, :]) + _dot(yn_ref[0], wm_ref[RWKV_DIM:, :])
    x = _layer_norm(ALPHA * x_ref[0] + mixed, g2_ref[...], b2_ref[...])
    q = (_dot(x.astype(BF16), wq_ref[...]) * (XATTN_HEAD_DIM ** -0.5)).astype(BF16)
    heads = []
    for h in range(XATTN_HEADS):
        cols = slice(h * XATTN_HEAD_DIM, (h + 1) * XATTN_HEAD_DIM)
        s = _dot_nt(q[:, cols], k_ref[0, :, cols])
        e = jnp.exp(s - s.max(axis=1, keepdims=True))
        p = e * (1.0 / e.sum(axis=1, keepdims=True))
        heads.append(_dot(p.astype(BF16), v_ref[0, :, cols]).astype(BF16))
    o = _dot(jnp.concatenate(heads, axis=1), wo_ref[...])
    o_ref[0] = _layer_norm(ALPHA * x + o, g_ref[...], b_ref[...])


def _mix_cross_attention_ln(x3d, yr, yn, w_out, g2, b2, k, v, wq, wo, g, b):
    bsz, s, d = x3d.shape
    m = k.shape[1]
    tm = min(XATTN_TM, s)
    rows = lambda w: pl.BlockSpec((1, tm, w), lambda bb, i: (bb, i, 0))
    mem = pl.BlockSpec((1, m, d), lambda bb, i: (bb, 0, 0))
    vec = _const_spec((1, d))
    return pl.pallas_call(
        _mix_xattn_kernel,
        grid=(bsz, s // tm),
        in_specs=[rows(d), rows(RWKV_DIM), rows(NSA_DIM), _const_spec(w_out.shape), vec, vec,
                  mem, mem, _const_spec((d, d)), _const_spec((d, d)), vec, vec],
        out_specs=rows(d),
        out_shape=jax.ShapeDtypeStruct((bsz, s, d), F32),
        compiler_params=_params(("parallel", "parallel")),
        name="mix_xattn_ln",
    )(x3d, yr, yn, w_out.astype(BF16), g2.reshape(1, d), b2.reshape(1, d), k, v, wq.astype(BF16), wo.astype(BF16),
      g.reshape(1, d), b.reshape(1, d))


def _nsa_group(rel_bias, q, kcvc, k4, v4, gate, pe_k, w1_k, w2_k, pe_v, w1_v, w2_v):
    tz = _bias_tiles(rel_bias)
    kc, vc = _compress(kcvc, pe_k, w1_k, w2_k, pe_v, w1_v, w2_v)
    oc, sel = _cmp_select(q, kc, vc, tz, gate)
    return _sel_win(q, k4, v4, sel, tz, gate, oc)


def kernel(x, mem, ffn1_w_gate, ffn1_w_up, ffn1_w_down, ln1_g, ln1_b, mix_w_in, rwkv_mu, rwkv_w0, rwkv_w_up, rwkv_a0, rwkv_a_up, rwkv_g_up, rwkv_k_k, rwkv_k_a, rwkv_r_k, rwkv_gn_g, rwkv_gn_b, nsa_pe_k, nsa_w1_k, nsa_w2_k, nsa_pe_v, nsa_w1_v, nsa_w2_v, mix_w_out, ln2_g, ln2_b, xattn_wq, xattn_wk, xattn_wv, xattn_wo, ln3_g, ln3_b, ffn2_w_gate, ffn2_w_up, ffn2_w_down, ln4_g, ln4_b, rel_bias):
    bsz, s, d = x.shape
    rows = bsz * s
    for l in range(DEPTH):
        x1 = _ffn_ln(x.reshape(rows, d), ffn1_w_gate[l], ffn1_w_up[l], ffn1_w_down[l], ln1_g[l], ln1_b[l])
        zr, q, kcvc, k4, v4, gate = _in_projection(x1.reshape(bsz, s, d), mix_w_in[l], rel_bias)
        y_rwkv = _rwkv_group(zr, rwkv_mu[l], rwkv_w0[l], rwkv_w_up[l], rwkv_a0[l], rwkv_a_up[l], rwkv_g_up[l],
                             rwkv_k_k[l], rwkv_k_a[l], rwkv_r_k[l], rwkv_gn_g[l], rwkv_gn_b[l])
        y_nsa = _nsa_group(rel_bias, q, kcvc, k4, v4, gate, nsa_pe_k[l], nsa_w1_k[l], nsa_w2_k[l],
                           nsa_pe_v[l], nsa_w1_v[l], nsa_w2_v[l])
        mk, mv = _mem_kv(mem, xattn_wk[l], xattn_wv[l])
        x3 = _mix_cross_attention_ln(x1.reshape(bsz, s, d), y_rwkv, y_nsa, mix_w_out[l], ln2_g[l], ln2_b[l],
                                     mk, mv, xattn_wq[l], xattn_wo[l], ln3_g[l], ln3_b[l])
        x = _ffn_ln(x3.reshape(rows, d), ffn2_w_gate[l], ffn2_w_up[l], ffn2_w_down[l], ln4_g[l], ln4_b[l])
        x = x.reshape(bsz, s, d)
    return x
```

```python
import functools
import math

import numpy as np
import jax
import jax.numpy as jnp
from jax import lax
from jax.experimental import pallas as pl
from jax.experimental.pallas import tpu as pltpu

F32 = jnp.float32
BF16 = jnp.bfloat16
HIGHEST = lax.Precision.HIGHEST

D_MODEL = 1024
DEPTH = 1
RWKV_HEADS = 8
HEAD_DIM = 64
RWKV_DIM = RWKV_HEADS * HEAD_DIM
DECAY_LORA = 64
AAA_LORA = 64
GATE_LORA = 128
RWKV_IN = 3 * RWKV_DIM + DECAY_LORA + AAA_LORA + GATE_LORA
RWKV_GN_EPS = 64e-5
NSA_HEADS = 8
NSA_GROUPS = 2
NSA_HPG = NSA_HEADS // NSA_GROUPS
NSA_DIM = NSA_HEADS * HEAD_DIM
CMP_LEN = 32
CMP_STRIDE = 16
CMP_HIDDEN = 128
SLC_LEN = 64
N_SEL = 16
WINDOW = 512
Q_BLOCK = 128
FORCED_SCORE = 1e4
NUM_BUCKETS = 32
MAX_DISTANCE = 128
XATTN_HEADS = 4
XATTN_HEAD_DIM = D_MODEL // XATTN_HEADS
D_FF = 2816
LN_EPS = 1e-5
ALPHA = (2.0 * DEPTH) ** 0.25
NEG = -1e30

V7X_LANES = 128
V7X_SUBLANES = 8
V7X_VMEM_LIMIT_BYTES = 56 * 1024 * 1024

CHUNK = 64
CMP_PAD = 120
VT_ROWS = 80


def _dot(a, b, prec=None):
    return jnp.dot(a, b, preferred_element_type=F32, precision=prec)


def _dot_nt(a, b, prec=None):
    return lax.dot_general(a, b, (((1,), (1,)), ((), ())), preferred_element_type=F32, precision=prec)


def _dot_tn(a, b, prec=None):
    return lax.dot_general(a, b, (((0,), (0,)), ((), ())), preferred_element_type=F32, precision=prec)


def _sigmoid(x):
    return 1.0 / (1.0 + jnp.exp(-x))


def _layer_norm(y, g, b):
    mu = jnp.mean(y, axis=-1, keepdims=True)
    yc = y - mu
    var = jnp.mean(yc * yc, axis=-1, keepdims=True)
    return yc * lax.rsqrt(var + LN_EPS) * g + b


def _params(sem):
    return pltpu.CompilerParams(dimension_semantics=sem, vmem_limit_bytes=V7X_VMEM_LIMIT_BYTES)


def _const_spec(shape, single_buffer=False):
    nd = len(shape)
    if single_buffer:
        return pl.BlockSpec(shape, lambda *_: (0,) * nd, pipeline_mode=pl.Buffered(1))
    return pl.BlockSpec(shape, lambda *_: (0,) * nd)


def _bucket_thresholds():
    n = np.arange(0, 4 * MAX_DISTANCE)
    max_exact = NUM_BUCKETS // 2
    nf = np.maximum(n, max_exact).astype(np.float32)
    large = max_exact + (np.log(nf / np.float32(max_exact)) / np.float32(math.log(MAX_DISTANCE / max_exact))
                         * np.float32(NUM_BUCKETS - max_exact)).astype(np.int32)
    large = np.minimum(large, NUM_BUCKETS - 1)
    bucket = np.where(n < max_exact, n, large)
    return [int(np.argmax(bucket >= b)) for b in range(1, NUM_BUCKETS)]


_BUCKET_THR = _bucket_thresholds()


TZ_CMP_NEAR, TZ_DIAG, TZ_SUB, TZ_ZERO, TZ_MASKED, TZ_WIN_OLD, TZ_KINDS = 0, 1, 2, 3, 4, 5, 6


def _bias_tiles_kernel(rb_ref, o_ref):
    h = pl.program_id(0)
    r = lax.broadcasted_iota(jnp.int32, (Q_BLOCK, Q_BLOCK), 0)
    c = lax.broadcasted_iota(jnp.int32, (Q_BLOCK, Q_BLOCK), 1)
    far = jnp.full((Q_BLOCK, Q_BLOCK), rb_ref[NUM_BUCKETS - 1, h], F32)
    far_hi = far.astype(BF16).astype(F32)
    far_added = far_hi + (far - far_hi).astype(BF16).astype(F32)
    dists = (c - CMP_STRIDE * (r - CMP_PAD) - (CMP_LEN - 1), c - r, Q_BLOCK + c - r)
    for kind, dist in enumerate(dists):
        val = jnp.full((Q_BLOCK, Q_BLOCK), rb_ref[0, h], F32)
        for b in range(1, NUM_BUCKETS):
            val = jnp.where(dist >= _BUCKET_THR[b - 1], rb_ref[b, h], val)
        o_ref[0, kind] = jnp.where(dist >= 0, val - far_added, NEG)
    o_ref[0, TZ_ZERO] = far - far_added
    o_ref[0, TZ_MASKED] = jnp.full((Q_BLOCK, Q_BLOCK), NEG, F32)
    o_ref[0, TZ_WIN_OLD] = jnp.where(r > c, far - far_added, NEG)


def _bias_tiles(rel_bias):
    return pl.pallas_call(
        _bias_tiles_kernel,
        grid=(NSA_HEADS,),
        in_specs=[pl.BlockSpec(memory_space=pltpu.SMEM)],
        out_specs=pl.BlockSpec((1, TZ_KINDS, Q_BLOCK, Q_BLOCK), lambda h: (h, 0, 0, 0)),
        out_shape=jax.ShapeDtypeStruct((NSA_HEADS, TZ_KINDS, Q_BLOCK, Q_BLOCK), F32),
        compiler_params=_params(("arbitrary",)),
        name="bias_tiles",
    )(rel_bias)


FFN_TM = 1024
FFN_TF = 256


def _ffn_kernel(x_ref, wg_ref, wu_ref, wd_ref, g_ref, b_ref, o_ref, acc_ref):
    x = x_ref[...]
    xb = x.astype(BF16)
    for c in range(wg_ref.shape[1] // FFN_TF):
        cols = slice(c * FFN_TF, (c + 1) * FFN_TF)
        hg = _dot(xb, wg_ref[:, cols])
        hu = _dot(xb, wu_ref[:, cols])
        h = hg * _sigmoid(hg) * hu
        part = _dot(h.astype(BF16), wd_ref[cols, :])
        if c == 0:
            acc_ref[...] = part
        else:
            acc_ref[...] += part
    o_ref[...] = _layer_norm(ALPHA * x + 0.5 * acc_ref[...], g_ref[...], b_ref[...])


def _ffn_ln(x2d, wg, wu, wd, g, b):
    rows, d = x2d.shape
    f = wg.shape[1]
    assert f % FFN_TF == 0
    tm = min(FFN_TM, rows)
    return pl.pallas_call(
        _ffn_kernel,
        grid=(rows // tm,),
        in_specs=[
            pl.BlockSpec((tm, d), lambda i: (i, 0)),
            _const_spec((d, f), single_buffer=True),
            _const_spec((d, f), single_buffer=True),
            _const_spec((f, d), single_buffer=True),
            _const_spec((1, d)),
            _const_spec((1, d)),
        ],
        out_specs=pl.BlockSpec((tm, d), lambda i: (i, 0)),
        out_shape=jax.ShapeDtypeStruct((rows, d), F32),
        scratch_shapes=[pltpu.VMEM((tm, d), F32)],
        compiler_params=_params(("parallel",)),
        name="ffn_ln",
    )(x2d, wg.astype(BF16), wu.astype(BF16), wd.astype(BF16), g.reshape(1, d), b.reshape(1, d))


PROJ_TM = 512
_C_RWKV = 0
_C_Q = _C_RWKV + RWKV_IN
_C_KCVC = _C_Q + NSA_DIM
_C_K = _C_KCVC + 256
_C_V = _C_K + 256
_C_GATE = _C_V + 256
_C_END = _C_GATE + 128
BIAS_LANES = (HEAD_DIM, HEAD_DIM + 1)
MASK_LANE = HEAD_DIM + 2


def _inproj_kernel(x_ref, w_ref, qx_ref, zr_ref, q_ref, kcvc_ref, k_ref, v_ref, gate_ref):
    xb = x_ref[0].astype(BF16)
    lane = lax.broadcasted_iota(jnp.int32, (1, V7X_LANES), 1)
    zr_ref[0] = _dot(xb, w_ref[:, _C_RWKV:_C_Q])
    low = lane < HEAD_DIM

    def padded_pair(tile, pad_a, pad_b):
        return jnp.where(low, tile, pad_a), jnp.where(low, pltpu.roll(tile, HEAD_DIM, 1), pad_b)

    zq = _dot(xb, w_ref[:, _C_Q:_C_KCVC]) * (HEAD_DIM ** -0.5)
    for j in range(NSA_HEADS // 2):
        qa, qb = padded_pair(zq[:, j * 128:(j + 1) * 128], qx_ref[2 * j:2 * j + 1, :], qx_ref[2 * j + 1:2 * j + 2, :])
        q_ref[0, 2 * j] = qa.astype(BF16)
        q_ref[0, 2 * j + 1] = qb.astype(BF16)
    zc = _dot(xb, w_ref[:, _C_KCVC:_C_K])
    kcvc_ref[0, 0] = zc[:, 0:128]
    kcvc_ref[0, 1] = zc[:, 128:256]
    bias_ones = ((lane == BIAS_LANES[0]) | (lane == BIAS_LANES[1])).astype(F32)
    zk = _dot(xb, w_ref[:, _C_K:_C_V])
    for j in range(2):
        ka, kb = padded_pair(zk[:, j * 128:(j + 1) * 128], bias_ones, bias_ones)
        k_ref[0, 2 * j] = ka.astype(BF16)
        k_ref[0, 2 * j + 1] = kb.astype(BF16)
    tail_rows = VT_ROWS - HEAD_DIM
    tail = (lax.broadcasted_iota(jnp.int32, (tail_rows, Q_BLOCK), 0) == 0).astype(F32)
    zvg = _dot(xb, w_ref[:, _C_V:_C_END])
    for j in range(2):
        zv = zvg[:, j * 128:(j + 1) * 128]
        for i in range(zv.shape[0] // Q_BLOCK):
            zt = zv[i * Q_BLOCK:(i + 1) * Q_BLOCK].T
            for g in range(NSA_GROUPS):
                v_ref[0, 2 * j + g, i] = jnp.concatenate([zt[g * HEAD_DIM:(g + 1) * HEAD_DIM], tail], axis=0).astype(BF16)
    gate_ref[0] = zvg[:, 2 * 128:3 * 128]


def _pack_w_in(w_in):
    d = w_in.shape[0]
    o = RWKV_IN
    q = w_in[:, o:o + 512]
    kc = w_in[:, o + 512:o + 640]
    vc = w_in[:, o + 640:o + 768]
    ks = w_in[:, o + 768:o + 896]
    vs = w_in[:, o + 896:o + 1024]
    kw = w_in[:, o + 1024:o + 1152]
    vw = w_in[:, o + 1152:o + 1280]
    gate = w_in[:, o + 1280:o + 1304]
    gate_pad = jnp.concatenate([gate, jnp.zeros((d, 128 - gate.shape[1]), w_in.dtype)], axis=1)
    return jnp.concatenate([w_in[:, :o], q, kc, vc, ks, kw, vs, vw, gate_pad], axis=1).astype(BF16)


def _far_bias_lanes(rel_bias):
    far = rel_bias[NUM_BUCKETS - 1, :]
    hi = far.astype(BF16).astype(F32)
    lane = jnp.arange(V7X_LANES)[None, :]
    return jnp.where(lane == BIAS_LANES[0], hi[:, None],
                     jnp.where(lane == BIAS_LANES[1], (far - hi)[:, None], jnp.where(lane == MASK_LANE, 1.0, 0.0)))


def _in_projection(x3d, w_in, rel_bias):
    bsz, s, d = x3d.shape
    tm = min(PROJ_TM, s)
    wp = _pack_w_in(w_in)
    out_shape = (
        jax.ShapeDtypeStruct((bsz, s, RWKV_IN), F32),
        jax.ShapeDtypeStruct((bsz, NSA_HEADS, s, 128), BF16),
        jax.ShapeDtypeStruct((bsz, 2, s, 128), F32),
        jax.ShapeDtypeStruct((bsz, 4, s, 128), BF16),
        jax.ShapeDtypeStruct((bsz, 4, s // Q_BLOCK, VT_ROWS, Q_BLOCK), BF16),
        jax.ShapeDtypeStruct((bsz, s, 128), F32),
    )
    return pl.pallas_call(
        _inproj_kernel,
        grid=(bsz, s // tm),
        in_specs=[pl.BlockSpec((1, tm, d), lambda b, i: (b, i, 0)), _const_spec((d, _C_END)),
                  _const_spec((NSA_HEADS, 128))],
        out_specs=(
            pl.BlockSpec((1, tm, RWKV_IN), lambda b, i: (b, i, 0)),
            pl.BlockSpec((1, NSA_HEADS, tm, 128), lambda b, i: (b, 0, i, 0)),
            pl.BlockSpec((1, 2, tm, 128), lambda b, i: (b, 0, i, 0)),
            pl.BlockSpec((1, 4, tm, 128), lambda b, i: (b, 0, i, 0)),
            pl.BlockSpec((1, 4, tm // Q_BLOCK, VT_ROWS, Q_BLOCK), lambda b, i: (b, 0, i, 0, 0)),
            pl.BlockSpec((1, tm, 128), lambda b, i: (b, i, 0)),
        ),
        out_shape=out_shape,
        compiler_params=_params(("parallel", "parallel")),
        name="in_projection",
    )(x3d, wp, _far_bias_lanes(rel_bias))


RWKV_T = 512
RWKV_CHUNKS_PER_STEP = 4


def _rwkv_consts():
    r = lax.broadcasted_iota(jnp.int32, (128, 128), 0)
    c = lax.broadcasted_iota(jnp.int32, (128, 128), 1)
    same = (r >= CHUNK) == (c >= CHUNK)
    mask_sl = (same & (r > c)).astype(F32)
    mask_l = (same & (r >= c)).astype(F32)
    eye = (r == c).astype(F32)
    head_ones = same.astype(F32)
    rt = lax.broadcasted_iota(jnp.int32, (CHUNK, CHUNK), 0)
    ct = lax.broadcasted_iota(jnp.int32, (CHUNK, CHUNK), 1)
    tri = (rt >= ct).astype(F32)
    lane = lax.broadcasted_iota(jnp.int32, (1, 128), 1)
    m0 = (lane < CHUNK).astype(F32)
    m1 = 1.0 - m0
    return mask_sl, mask_l, eye, head_ones, tri, m0, m1


def _split2(x):
    hi = x.astype(BF16)
    return hi, (x - hi.astype(F32)).astype(BF16)


def _mm(a, b, mode, dot=_dot):
    if mode == "bf16":
        return dot(a.astype(BF16), b.astype(BF16))
    if mode == "bf16x3":
        ah, al = _split2(a)
        bh, bl = _split2(b)
        return dot(ah, bh) + (dot(ah, bl) + dot(al, bh))
    if mode in ("exact_a", "exact_b"):
        x = b if mode == "exact_a" else a
        hi, rest = x.astype(BF16), None
        rest = x - hi.astype(F32)
        mid = rest.astype(BF16)
        lo = (rest - mid.astype(F32)).astype(BF16)
        if mode == "exact_a":
            ab = a.astype(BF16)
            return dot(ab, hi) + (dot(ab, mid) + dot(ab, lo))
        bb = b.astype(BF16)
        return dot(hi, bb) + (dot(mid, bb) + dot(lo, bb))
    raise ValueError(mode)


RWKV_MODES = dict(p="bf16", inv="bf16", av="bf16", wu="bf16", qy="bf16", mn="bf16", y="bf16", h="bf16")


def _rwkv_chunk(rs, lws, ks, vs, kks, as_, hs, consts):
    mask_sl, mask_l, eye, _, tri, m0, m1 = consts
    md = RWKV_MODES
    n = len(rs)
    each = range(n)

    def sm(x):
        return jnp.concatenate([x * m0, x * m1], axis=0)

    def dup(x):
        return jnp.concatenate([x, x], axis=0)

    cums = [_mm(tri, lws[i], "exact_a") for i in each]
    a_sm, r_sm, v_sm, kb, kbh, g_c = [], [], [], [], [], []
    for i in each:
        cum, lw, kk, k = cums[i], lws[i], kks[i], ks[i]
        cl = cum[CHUNK - 1:CHUNK, :]
        ka = kk * as_[i]
        g_tail = jnp.exp(cl - cum)
        g_inv = jnp.exp(-cum)
        a_sm.append(sm(-(kk * jnp.exp(cum - lw))))
        r_sm.append(sm(rs[i] * jnp.exp(cum)))
        v_sm.append(sm(vs[i]))
        kb.append(jnp.concatenate([dup(k * g_inv), dup(ka * g_inv)], axis=0))
        kbh.append(jnp.concatenate([sm(k * g_tail), sm(ka * g_tail)], axis=0))
        g_c.append(jnp.exp(cl))
    pm = [_mm(jnp.concatenate([a_sm[i], r_sm[i]], axis=0), kb[i], md["p"], _dot_nt) for i in each]
    a_ak = [pm[i][0:128, 0:128] * mask_sl for i in each]
    a_rr = [jnp.concatenate([pm[i][128:256, 0:128] * mask_l, pm[i][128:256, 128:256] * mask_l], axis=1) for i in each]
    x = [pm[i][0:128, 128:256] * mask_sl for i in each]
    t_inv = [eye + x[i] for i in each]
    for _ in range(5):
        x = [_mm(x[i], x[i], md["inv"]) for i in each]
        t_inv = [t_inv[i] + _mm(t_inv[i], x[i], md["inv"]) for i in each]
    av = [_mm(a_ak[i], v_sm[i], md["av"]) for i in each]
    wu = [_mm(t_inv[i], jnp.concatenate([a_sm[i], av[i]], axis=1), md["wu"]) for i in each]
    z = [jnp.concatenate([jnp.concatenate([jnp.zeros_like(v_sm[i]), v_sm[i]], axis=1), wu[i]], axis=0) for i in each]
    qy = [_mm(a_rr[i], z[i], md["qy"]) for i in each]
    mn = [_mm(kbh[i], z[i], md["mn"], _dot_tn) for i in each]
    n_pairs = len(hs)
    ys = []
    for c0 in range(0, n, n_pairs):
        idx = range(c0, c0 + n_pairs)
        y_sm = [_mm(r_sm[i] + qy[i][:, 0:128], hs[i - c0], md["y"]) + qy[i][:, 128:256] for i in idx]
        hs = [_mm(mn[i][:, 0:128] + eye * g_c[i], hs[i - c0], md["h"]) + mn[i][:, 128:256] for i in idx]
        ys += [y[0:CHUNK] + y[CHUNK:2 * CHUNK] for y in y_sm]
    return ys, hs


def _rwkv_kernel(z_ref, mu_ref, w0_ref, wup_ref, a0_ref, aup_ref, gup_ref, kk_ref, ka_ref, rk_ref,
                 gng_ref, gnb_ref, o_ref, prev_ref, h_ref, r_s, lw_s, k_s, v_s, kk_s, a_s, y_s, g_s, *, t):
    ti = pl.program_id(1)

    @pl.when(ti == 0)
    def _():
        prev_ref[...] = jnp.zeros_like(prev_ref)
        h_ref[...] = jnp.zeros_like(h_ref)

    consts = _rwkv_consts()
    head_ones = consts[3]

    z = z_ref[0]
    row = lax.broadcasted_iota(jnp.int32, (t, 1), 0)
    prev = jnp.where(row == 0, prev_ref[0:1, :], pltpu.roll(z, 1, 0))
    prev_ref[0:1, :] = z[t - 1:t, :]
    zs = z + (prev - z) * mu_ref[...]

    r = zs[:, 0:512]
    k = zs[:, 512:1024]
    v = zs[:, 1024:1536]
    wa = zs[:, 1536:1664]
    gl = zs[:, 1664:1792]
    u = w0_ref[...] + _dot(jnp.tanh(wa).astype(BF16), wup_ref[...])
    lw = (-math.exp(-0.5)) * _sigmoid(u)
    a = _sigmoid(a0_ref[...] + _dot(wa.astype(BF16), aup_ref[...]))
    g_s[...] = _dot(_sigmoid(gl).astype(BF16), gup_ref[...])
    kkr = k * kk_ref[...]
    k2 = k * (1.0 + (a - 1.0) * ka_ref[...])
    r_s[...] = r
    lw_s[...] = lw
    k_s[...] = k2
    v_s[...] = v
    a_s[...] = a
    for pr in range(4):
        sl = slice(pr * 128, (pr + 1) * 128)
        kp = kkr[:, sl]
        ss = _mm(kp * kp, head_ones, "exact_b")
        kk_s[:, sl] = kp * lax.rsqrt(jnp.maximum(ss, 1e-24))

    pairs = [slice(pr * 128, (pr + 1) * 128) for pr in range(4)]

    def chunk_body(c, carry):
        rows = [pl.ds(pl.multiple_of((c * RWKV_CHUNKS_PER_STEP + i) * CHUNK, CHUNK), CHUNK)
                for i in range(RWKV_CHUNKS_PER_STEP)]
        ys, hs = _rwkv_chunk(*[[ref[rw, sl] for rw in rows for sl in pairs]
                               for ref in (r_s, lw_s, k_s, v_s, kk_s, a_s)],
                             [h_ref[pr] for pr in range(4)], consts)
        for i, rw in enumerate(rows):
            for pr, sl in enumerate(pairs):
                y_s[rw, sl] = ys[i * 4 + pr]
        for pr in range(4):
            h_ref[pr] = hs[pr]
        return carry

    lax.fori_loop(0, t // (CHUNK * RWKV_CHUNKS_PER_STEP), chunk_body, 0)

    for pr in range(4):
        sl = slice(pr * 128, (pr + 1) * 128)
        y = y_s[:, sl]
        mean = _mm(y, head_ones, "exact_b") * (1.0 / HEAD_DIM)
        yc = y - mean
        var = _mm(yc * yc, head_ones, "exact_b") * (1.0 / HEAD_DIM)
        yn = yc * lax.rsqrt(var + RWKV_GN_EPS) * gng_ref[:, sl] + gnb_ref[:, sl]
        rp = r_s[:, sl]
        bonus = _mm(rp * k_s[:, sl] * rk_ref[:, sl], head_ones, "exact_b") * v_s[:, sl]
        o_ref[0, :, sl] = ((yn + bonus) * g_s[:, sl]).astype(o_ref.dtype)


def _rwkv_group(zr, mu, w0, w_up, a0, a_up, g_up, k_k, k_a, r_k, gn_g, gn_b):
    bsz, s, _ = zr.shape
    t = min(RWKV_T, s)
    wup_pad = jnp.concatenate([w_up, jnp.zeros_like(a_up)], axis=0).astype(BF16)
    aup_pad = jnp.concatenate([jnp.zeros_like(w_up), a_up], axis=0).astype(BF16)
    row = lambda p: p.reshape(1, -1)
    tile = pltpu.VMEM((t, RWKV_DIM), F32)
    return pl.pallas_call(
        functools.partial(_rwkv_kernel, t=t),
        grid=(bsz, s // t),
        in_specs=[
            pl.BlockSpec((1, t, RWKV_IN), lambda b, i: (b, i, 0)),
            _const_spec((1, RWKV_IN)), _const_spec((1, RWKV_DIM)), _const_spec((128, RWKV_DIM)),
            _const_spec((1, RWKV_DIM)), _const_spec((128, RWKV_DIM)), _const_spec((GATE_LORA, RWKV_DIM)),
            _const_spec((1, RWKV_DIM)), _const_spec((1, RWKV_DIM)), _const_spec((1, RWKV_DIM)),
            _const_spec((1, RWKV_DIM)), _const_spec((1, RWKV_DIM)),
        ],
        out_specs=pl.BlockSpec((1, t, RWKV_DIM), lambda b, i: (b, i, 0)),
        out_shape=jax.ShapeDtypeStruct((bsz, s, RWKV_DIM), BF16),
        scratch_shapes=[pltpu.VMEM((V7X_SUBLANES, RWKV_IN), F32), pltpu.VMEM((4, 128, 128), F32),
                        tile, tile, tile, tile, tile, tile, tile, tile],
        compiler_params=_params(("parallel", "arbitrary")),
        name="rwkv7",
    )(zr, row(mu), row(w0), wup_pad, row(a0), aup_pad, g_up.astype(BF16), row(k_k), row(k_a), row(r_k),
      row(gn_g), row(gn_b))


def _gelu_tanh(x):
    return 0.5 * x * (1.0 + jnp.tanh(math.sqrt(2.0 / math.pi) * (x + 0.044715 * (x * x * x))))


def _compress_kernel(x_ref, pe_ref, w1_ref, w2_ref, kc_ref, vc_ref, *, n_rows):
    half = CMP_LEN // 2
    for kv, o_ref in enumerate((kc_ref, vc_ref)):
        first = jnp.zeros((n_rows, 2 * CMP_HIDDEN), F32)
        second = jnp.zeros((n_rows, 2 * CMP_HIDDEN), F32)
        for p in range(half):
            xa = x_ref[0, kv, pl.ds(p, n_rows, stride=CMP_STRIDE), :]
            first += _dot((xa + pe_ref[kv, p:p + 1, :]).astype(BF16), w1_ref[kv, p])
            second += _dot((xa + pe_ref[kv, half + p:half + p + 1, :]).astype(BF16), w1_ref[kv, half + p])
        pre = first + pltpu.roll(second, n_rows - 1, 0)
        hid = _gelu_tanh(pre).astype(BF16)
        width = o_ref.shape[3]
        if kv == 0:
            lane = lax.broadcasted_iota(jnp.int32, (1, V7X_LANES), 1)
            bias_ones = ((lane == BIAS_LANES[0]) | (lane == BIAS_LANES[1])).astype(F32)
            outs = [_dot(hid, w2_ref[g]) + bias_ones for g in range(NSA_GROUPS)]
        else:
            out = _dot(hid, w2_ref[NSA_GROUPS])
            outs = [out[:, g * HEAD_DIM:(g + 1) * HEAD_DIM] for g in range(NSA_GROUPS)]
        for g in range(NSA_GROUPS):
            o_ref[0, g, 0:CMP_PAD, :] = jnp.zeros((CMP_PAD, width), o_ref.dtype)
            o_ref[0, g, CMP_PAD:CMP_PAD + n_rows, :] = outs[g].astype(o_ref.dtype)
            tail = o_ref.shape[2] - CMP_PAD - n_rows
            o_ref[0, g, CMP_PAD + n_rows:, :] = jnp.zeros((tail, width), o_ref.dtype)


def _pair_diag(w):
    z = jnp.zeros_like(w)
    return jnp.concatenate([jnp.concatenate([w, z], axis=-1), jnp.concatenate([z, w], axis=-1)], axis=-2)


def _compress(kcvc, pe_k, w1_k, w2_k, pe_v, w1_v, w2_v):
    bsz, _, s, _ = kcvc.shape
    n_rows = s // CMP_STRIDE
    rows_out = CMP_PAD + n_rows + V7X_SUBLANES
    pe = jnp.stack([jnp.concatenate([pe_k, pe_k], axis=1), jnp.concatenate([pe_v, pe_v], axis=1)])
    w1 = jnp.stack([_pair_diag(w1_k.reshape(CMP_LEN, HEAD_DIM, CMP_HIDDEN)),
                    _pair_diag(w1_v.reshape(CMP_LEN, HEAD_DIM, CMP_HIDDEN))]).astype(BF16)
    zk = jnp.zeros_like(w2_k)
    w2_pad = lambda g: jnp.concatenate([jnp.concatenate([w2_k if i == g else zk, zk], axis=1)
                                        for i in range(NSA_GROUPS)], axis=0)
    w2 = jnp.stack([w2_pad(0), w2_pad(1), _pair_diag(w2_v)]).astype(BF16)
    shapes = tuple(jax.ShapeDtypeStruct((bsz, NSA_GROUPS, rows_out, w), BF16) for w in (128, HEAD_DIM))
    specs = tuple(pl.BlockSpec((1, NSA_GROUPS, rows_out, w), lambda b: (b, 0, 0, 0)) for w in (128, HEAD_DIM))
    return pl.pallas_call(
        functools.partial(_compress_kernel, n_rows=n_rows),
        grid=(bsz,),
        in_specs=[pl.BlockSpec((1, 2, s, 128), lambda b: (b, 0, 0, 0)), _const_spec(pe.shape), _const_spec(w1.shape),
                  _const_spec(w2.shape)],
        out_specs=specs,
        out_shape=shapes,
        compiler_params=_params(("parallel",)),
        name="nsa_compress",
    )(kcvc, pe, w1, w2)


CMP_QBLOCKS = 2


def _cmp_select_kernel(q_ref, kc_ref, vc_ref, tz_ref, gate_ref, oc_ref, sel_ref, *, n_far):
    for sub in range(CMP_QBLOCKS):
        _cmp_select_block(pl.program_id(1) * CMP_QBLOCKS + sub, slice(sub * Q_BLOCK, (sub + 1) * Q_BLOCK),
                          q_ref, kc_ref, vc_ref, tz_ref, gate_ref, oc_ref, sel_ref, n_far)


def _cmp_select_block(qi, qrows, q_ref, kc_ref, vc_ref, tz_ref, gate_ref, oc_ref, sel_ref, n_far):
    qb = Q_BLOCK
    rowi = lax.broadcasted_iota(jnp.int32, (qb, 128), 0)
    lane = lax.broadcasted_iota(jnp.int32, (qb, 128), 1)
    row_f = rowi.astype(F32)
    sg = _sigmoid(gate_ref[0, qrows, :])
    near0 = pl.multiple_of(qi * 8, 8)
    n_first = qi * 8 - CMP_PAD
    for g in range(NSA_GROUPS):
        q4 = q_ref[0, g * NSA_HPG:(g + 1) * NSA_HPG, qrows, :].reshape(NSA_HPG * qb, 128)
        heads = [g * NSA_HPG + p for p in range(NSA_HPG)]
        cols = [slice(p * qb, (p + 1) * qb) for p in range(NSA_HPG)]

        def masked_keys(kt, ok):
            return jnp.where((lane == MASK_LANE) & jnp.logical_not(ok), jnp.asarray(NEG, BF16), kt)

        tiles, values, n_of_lane = [], [], []
        for j in range(n_far):
            rows = slice(CMP_PAD + 128 * j, CMP_PAD + 128 * (j + 1))
            tiles.append(_dot_nt(masked_keys(kc_ref[0, g, rows, :], (128 * j + rowi) < n_first), q4))
            values.append(vc_ref[0, g, rows, :])
            n_of_lane.append(128 * j + lane)
        s = _dot_nt(masked_keys(kc_ref[0, g, pl.ds(near0, 128), :], (n_first + rowi) >= 0), q4)
        tiles.append(jnp.concatenate([s[:, cols[p]] + tz_ref[heads[p], TZ_CMP_NEAR] for p in range(NSA_HPG)], axis=1))
        values.append(vc_ref[0, g, pl.ds(near0, 128), :])
        n_of_lane.append(n_first + lane)

        m = tiles[0].max(axis=0, keepdims=True)
        for tl in tiles[1:]:
            m = jnp.maximum(m, tl.max(axis=0, keepdims=True))
        es = [jnp.exp(tl - m) for tl in tiles]
        den = es[0].sum(axis=0, keepdims=True)
        for e in es[1:]:
            den = den + e.sum(axis=0, keepdims=True)
        inv = jnp.where(m > 0.5 * NEG, 1.0 / jnp.maximum(den, 1e-30), 0.0)
        o_t = jnp.zeros((HEAD_DIM, NSA_HPG * qb), F32)
        imp = jnp.zeros((qb, 128), F32)
        for j, e in enumerate(es):
            pc = e * inv
            o_t = o_t + _dot_tn(values[j], pc.astype(BF16))
            psum = pc[:, cols[0]] + pc[:, cols[1]] + pc[:, cols[2]] + pc[:, cols[3]]
            ov = ((n_of_lane[j] >= 4 * rowi - 1) & (n_of_lane[j] <= 4 * rowi + 3)).astype(F32)
            imp = imp + _mm(ov, psum, "exact_a")
        for p, h in enumerate(heads):
            oc_ref[0, qrows, h * HEAD_DIM:(h + 1) * HEAD_DIM] = o_t[:, cols[p]].T * sg[:, 3 * h:3 * h + 1]
        cur = 2 * qi + (lane >= SLC_LEN).astype(jnp.int32)
        forced = (rowi == 0) | (rowi == cur) | (rowi == cur - 1)
        score = jnp.where(rowi <= cur, jnp.where(forced, -3e38, imp), -1.0)
        sel = jnp.where(forced & (rowi <= cur), 1.0, 0.0)
        for _ in range(N_SEL - 3):
            mx = score.max(axis=0, keepdims=True)
            idx = jnp.where(score == mx, row_f, 128.0).min(axis=0, keepdims=True)
            pick = row_f == idx
            sel = jnp.where(pick, 1.0, sel)
            score = jnp.where(pick, -3e38, score)
        sel_ref[0, g, qrows, :] = jnp.where(sel.T > 0.5, 0.0, NEG).astype(sel_ref.dtype)


def _cmp_select(q, kc, vc, tz, gate):
    bsz, _, s, _ = q.shape
    nq = s // Q_BLOCK
    n_far = max(0, -(-(8 * (nq - 1) - CMP_PAD) // 128))
    rows_c = kc.shape[2]
    assert nq % CMP_QBLOCKS == 0
    tq = CMP_QBLOCKS * Q_BLOCK
    return pl.pallas_call(
        functools.partial(_cmp_select_kernel, n_far=n_far),
        grid=(bsz, nq // CMP_QBLOCKS),
        in_specs=[
            pl.BlockSpec((1, NSA_HEADS, tq, 128), lambda b, i: (b, 0, i, 0)),
            pl.BlockSpec((1, NSA_GROUPS, rows_c, 128), lambda b, i: (b, 0, 0, 0)),
            pl.BlockSpec((1, NSA_GROUPS, rows_c, HEAD_DIM), lambda b, i: (b, 0, 0, 0)),
            _const_spec(tz.shape),
            pl.BlockSpec((1, tq, 128), lambda b, i: (b, i, 0)),
        ],
        out_specs=(
            pl.BlockSpec((1, tq, NSA_DIM), lambda b, i: (b, i, 0)),
            pl.BlockSpec((1, NSA_GROUPS, tq, 128), lambda b, i: (b, 0, i, 0)),
        ),
        out_shape=(jax.ShapeDtypeStruct((bsz, s, NSA_DIM), F32),
                   jax.ShapeDtypeStruct((bsz, NSA_GROUPS, s, 128), BF16)),
        compiler_params=_params(("parallel", "parallel")),
        name="nsa_cmp_select",
    )(q, kc, vc, tz, gate)


SEL_QBLOCKS = 4
SEL_TILES = 2


def _attn_scores(qa, ka, bias_fns, m_prev):
    qb = Q_BLOCK
    s = _dot_nt(ka, qa)
    if bias_fns is not None:
        s = jnp.concatenate(
            [jnp.concatenate([s[i * qb:(i + 1) * qb, p * qb:(p + 1) * qb] + fn(p) for p in range(NSA_HPG)], axis=1)
             for i, fn in enumerate(bias_fns)], axis=0)
    return s, jnp.maximum(m_prev, s.max(axis=0, keepdims=True))


def _attn_accumulate(s, vts, m_prev, m_next, acc_ref):
    pexp = jnp.exp(s - m_next)
    acc_ref[...] = jnp.exp(m_prev - m_next) * acc_ref[...] + _dot(jnp.concatenate(vts, axis=1), pexp.astype(BF16))


def _sel_win_kernel(q_ref, k_ref, vt_ref, e_ref, sel_ref, tz_ref, gate_ref, oc_ref, o_ref, m_ref, acc_ref,
                    s0_ref, s1_ref, mm0_ref, mm1_ref):
    qb = Q_BLOCK
    win_tiles = WINDOW // qb
    chains = [(sub, g) for sub in range(SEL_QBLOCKS) for g in range(NSA_GROUPS)]
    n_chains = len(chains)
    qis = [pl.program_id(1) * SEL_QBLOCKS + sub for sub in range(SEL_QBLOCKS)]
    m_ref[...] = jnp.full(m_ref.shape, NEG, F32)
    acc_ref[...] = jnp.zeros(acc_ref.shape, F32)
    qrows = [slice(sub * qb, (sub + 1) * qb) for sub in range(SEL_QBLOCKS)]
    q4 = [q_ref[0, g * NSA_HPG:(g + 1) * NSA_HPG, qrows[sub], :].reshape(NSA_HPG * qb, 128) for sub, g in chains]
    selm = [jnp.concatenate([sel_ref[0, g, qrows[sub], :]] * NSA_HPG, axis=0) for sub, g in chains]

    def key_rows(kt, n=1):
        return pl.ds(pl.multiple_of(kt * qb, qb), n * qb)

    def bias_of(g, kind):
        return lambda p: tz_ref[g * NSA_HPG + p, kind]

    def keys_sel(g, kt, n=1):
        return jnp.concatenate([k_ref[0, g, key_rows(kt, n), :], e_ref[key_rows(kt, n), :]], axis=1)

    block = lax.broadcasted_iota(jnp.int32, (NSA_HPG * qb, 128), 1)

    n_far = [(jnp.maximum(qi - 1, 0) + SEL_TILES - 1) // SEL_TILES for qi in qis]
    n_run = n_far[-1] | 1
    last_tile0 = k_ref.shape[2] // qb - SEL_TILES

    def tile0(step):
        return jnp.minimum(step * SEL_TILES, last_tile0)

    def score_far(step, s_ref, mm_ref):
        for c, (sub, g) in enumerate(chains):
            hidden = (block >= 2 * (qis[sub] - 1)) | (step >= n_far[sub])
            qa = jnp.concatenate([q4[c], jnp.where(hidden, jnp.asarray(NEG, BF16), selm[c])], axis=1)
            m_prev = m_ref[c]
            s, m_next = _attn_scores(qa, keys_sel(g, tile0(step), SEL_TILES), None, m_prev)
            s_ref[c] = s
            mm_ref[c, 0:1, :] = m_prev
            mm_ref[c, 1:2, :] = m_next
            m_ref[c] = m_next

    def accumulate_far(step, s_ref, mm_ref):
        for c, (sub, g) in enumerate(chains):
            _attn_accumulate(s_ref[c], [vt_ref[0, g, tile0(step) + i] for i in range(SEL_TILES)],
                             mm_ref[c, 0:1, :], mm_ref[c, 1:2, :], acc_ref.at[c])

    score_far(0, s0_ref, mm0_ref)

    def far_body(i, carry):
        score_far(2 * i + 1, s1_ref, mm1_ref)
        accumulate_far(2 * i, s0_ref, mm0_ref)
        score_far(2 * i + 2, s0_ref, mm0_ref)
        accumulate_far(2 * i + 1, s1_ref, mm1_ref)
        return carry

    lax.fori_loop(0, n_run // 2, far_body, 0)

    scored = []
    for c, (sub, g) in enumerate(chains):
        qi = qis[sub]
        kt_sub = jnp.maximum(qi - 1, 0)
        kinds = (jnp.where(qi >= 1, TZ_SUB, TZ_MASKED), TZ_DIAG)
        m_prev = m_ref[c]
        s, m_next = _attn_scores(jnp.concatenate([q4[c], selm[c]], axis=1),
                                 jnp.concatenate([keys_sel(g, kt_sub), keys_sel(g, qi)], axis=0),
                                 [bias_of(g, kind) for kind in kinds], m_prev)
        scored.append((c, s, [vt_ref[0, g, kt_sub], vt_ref[0, g, qi]], m_prev, m_next))
    for c, (sub, g) in enumerate(chains):
        qi = qis[sub]
        kw = NSA_GROUPS + g
        ks, vts, biases = [], [], []
        for d in range(win_tiles, -1, -1):
            kt = jnp.maximum(qi - d, 0)
            kind = TZ_DIAG if d == 0 else TZ_SUB if d == 1 else TZ_WIN_OLD if d == win_tiles else TZ_ZERO
            if d > 0:
                kind = jnp.where(qi >= d, kind, TZ_MASKED)
            ks.append(k_ref[0, kw, key_rows(kt), :])
            vts.append(vt_ref[0, kw, kt])
            biases.append(bias_of(g, kind))
        m_prev = m_ref[n_chains + c]
        s, m_next = _attn_scores(q4[c], jnp.concatenate(ks, axis=0), biases, m_prev)
        scored.append((n_chains + c, s, vts, m_prev, m_next))
    accumulate_far(n_run - 1, s0_ref, mm0_ref)
    for stream, s, vts, m_prev, m_next in scored:
        _attn_accumulate(s, vts, m_prev, m_next, acc_ref.at[stream])

    for c, (sub, g) in enumerate(chains):
        sg = _sigmoid(gate_ref[0, qrows[sub], :])
        outs = []
        for stream in (c, n_chains + c):
            acc = acc_ref[stream]
            outs.append(acc[0:HEAD_DIM] * (1.0 / acc[HEAD_DIM:HEAD_DIM + 1]))
        for p in range(NSA_HPG):
            h = g * NSA_HPG + p
            qcols = slice(p * qb, (p + 1) * qb)
            cols = slice(h * HEAD_DIM, (h + 1) * HEAD_DIM)
            y = (oc_ref[0, qrows[sub], cols] + sg[:, 3 * h + 1:3 * h + 2] * outs[0][:, qcols].T
                 + sg[:, 3 * h + 2:3 * h + 3] * outs[1][:, qcols].T)
            o_ref[0, qrows[sub], cols] = y.astype(o_ref.dtype)


def _sel_win(q, k4, v4, sel, tz, gate, oc):
    bsz, _, s, _ = q.shape
    nq = s // Q_BLOCK
    assert nq % SEL_TILES == 0 and nq % SEL_QBLOCKS == 0
    member = (jnp.arange(s)[:, None] // SLC_LEN == jnp.arange(128)[None, :]).astype(BF16)
    tq = SEL_QBLOCKS * Q_BLOCK
    n_chains = SEL_QBLOCKS * NSA_GROUPS
    lanes = NSA_HPG * Q_BLOCK
    return pl.pallas_call(
        _sel_win_kernel,
        grid=(bsz, nq // SEL_QBLOCKS),
        in_specs=[
            pl.BlockSpec((1, NSA_HEADS, tq, 128), lambda b, i: (b, 0, i, 0)),
            pl.BlockSpec((1, 4, s, 128), lambda b, i: (b, 0, 0, 0), pipeline_mode=pl.Buffered(1)),
            pl.BlockSpec((1, 4, nq, VT_ROWS, Q_BLOCK), lambda b, i: (b, 0, 0, 0, 0), pipeline_mode=pl.Buffered(1)),
            _const_spec((s, 128), single_buffer=True),
            pl.BlockSpec((1, NSA_GROUPS, tq, 128), lambda b, i: (b, 0, i, 0)),
            _const_spec(tz.shape, single_buffer=True),
            pl.BlockSpec((1, tq, 128), lambda b, i: (b, i, 0)),
            pl.BlockSpec((1, tq, NSA_DIM), lambda b, i: (b, i, 0)),
        ],
        out_specs=pl.BlockSpec((1, tq, NSA_DIM), lambda b, i: (b, i, 0)),
        out_shape=jax.ShapeDtypeStruct((bsz, s, NSA_DIM), BF16),
        scratch_shapes=[pltpu.VMEM((2 * n_chains, 1, lanes), F32),
                        pltpu.VMEM((2 * n_chains, VT_ROWS, lanes), F32),
                        pltpu.VMEM((n_chains, SEL_TILES * Q_BLOCK, lanes), F32),
                        pltpu.VMEM((n_chains, SEL_TILES * Q_BLOCK, lanes), F32),
                        pltpu.VMEM((n_chains, 2, lanes), F32),
                        pltpu.VMEM((n_chains, 2, lanes), F32)],
        compiler_params=_params(("parallel", "parallel")),
        name="nsa_sel_win",
    )(q, k4, v4, member, sel, tz, gate, oc)


XATTN_TM = 1024


def _mem_kv_kernel(mem_ref, wk_ref, wv_ref, k_ref, v_ref):
    mb = mem_ref[0].astype(BF16)
    k_ref[0] = _dot(mb, wk_ref[...]).astype(BF16)
    v_ref[0] = _dot(mb, wv_ref[...]).astype(BF16)


def _mem_kv(mem, wk, wv):
    bsz, m, d = mem.shape
    shape = jax.ShapeDtypeStruct((bsz, m, d), BF16)
    spec = pl.BlockSpec((1, m, d), lambda b: (b, 0, 0))
    return pl.pallas_call(
        _mem_kv_kernel,
        grid=(bsz,),
        in_specs=[spec, _const_spec((d, d)), _const_spec((d, d))],
        out_specs=(spec, spec),
        out_shape=(shape, shape),
        compiler_params=_params(("parallel",)),
        name="xattn_mem_kv",
    )(mem, wk.astype(BF16), wv.astype(BF16))


def _mix_xattn_kernel(x_ref, yr_ref, yn_ref, wm_ref, g2_ref, b2_ref, k_ref, v_ref, wq_ref, wo_ref, g_ref, b_ref, o_ref):
    mixed = _dot(yr_ref[0], wm_ref[0:RWKV_DIM, :]) + _dot(yn_ref[0], wm_ref[RWKV_DIM:, :])
    x = _layer_norm(ALPHA * x_ref[0] + mixed, g2_ref[...], b2_ref[...])
    q = (_dot(x.astype(BF16), wq_ref[...]) * (XATTN_HEAD_DIM ** -0.5)).astype(BF16)
    heads = []
    for h in range(XATTN_HEADS):
        cols = slice(h * XATTN_HEAD_DIM, (h + 1) * XATTN_HEAD_DIM)
        s = _dot_nt(q[:, cols], k_ref[0, :, cols])
        e = jnp.exp(s - s.max(axis=1, keepdims=True))
        p = e * (1.0 / e.sum(axis=1, keepdims=True))
        heads.append(_dot(p.astype(BF16), v_ref[0, :, cols]).astype(BF16))
    o = _dot(jnp.concatenate(heads, axis=1), wo_ref[...])
    o_ref[0] = _layer_norm(ALPHA * x + o, g_ref[...], b_ref[...])


def _mix_cross_attention_ln(x3d, yr, yn, w_out, g2, b2, k, v, wq, wo, g, b):
    bsz, s, d = x3d.shape
    m = k.shape[1]
    tm = min(XATTN_TM, s)
    rows = lambda w: pl.BlockSpec((1, tm, w), lambda bb, i: (bb, i, 0))
    mem = pl.BlockSpec((1, m, d), lambda bb, i: (bb, 0, 0))
    vec = _const_spec((1, d))
    return pl.pallas_call(
        _mix_xattn_kernel,
        grid=(bsz, s // tm),
        in_specs=[rows(d), rows(RWKV_DIM), rows(NSA_DIM), _const_spec(w_out.shape), vec, vec,
                  mem, mem, _const_spec((d, d)), _const_spec((d, d)), vec, vec],
        out_specs=rows(d),
        out_shape=jax.ShapeDtypeStruct((bsz, s, d), F32),
        compiler_params=_params(("parallel", "parallel")),
        name="mix_xattn_ln",
    )(x3d, yr, yn, w_out.astype(BF16), g2.reshape(1, d), b2.reshape(1, d), k, v, wq.astype(BF16), wo.astype(BF16),
      g.reshape(1, d), b.reshape(1, d))


def _nsa_group(rel_bias, q, kcvc, k4, v4, gate, pe_k, w1_k, w2_k, pe_v, w1_v, w2_v):
    tz = _bias_tiles(rel_bias)
    kc, vc = _compress(kcvc, pe_k, w1_k, w2_k, pe_v, w1_v, w2_v)
    oc, sel = _cmp_select(q, kc, vc, tz, gate)
    return _sel_win(q, k4, v4, sel, tz, gate, oc)


def kernel(x, mem, ffn1_w_gate, ffn1_w_up, ffn1_w_down, ln1_g, ln1_b, mix_w_in, rwkv_mu, rwkv_w0, rwkv_w_up, rwkv_a0, rwkv_a_up, rwkv_g_up, rwkv_k_k, rwkv_k_a, rwkv_r_k, rwkv_gn_g, rwkv_gn_b, nsa_pe_k, nsa_w1_k, nsa_w2_k, nsa_pe_v, nsa_w1_v, nsa_w2_v, mix_w_out, ln2_g, ln2_b, xattn_wq, xattn_wk, xattn_wv, xattn_wo, ln3_g, ln3_b, ffn2_w_gate, ffn2_w_up, ffn2_w_down, ln4_g, ln4_b, rel_bias):
    bsz, s, d = x.shape
    rows = bsz * s
    for l in range(DEPTH):
        x1 = _ffn_ln(x.reshape(rows, d), ffn1_w_gate[l], ffn1_w_up[l], ffn1_w_down[l], ln1_g[l], ln1_b[l])
        zr, q, kcvc, k4, v4, gate = _in_projection(x1.reshape(bsz, s, d), mix_w_in[l], rel_bias)
        y_rwkv = _rwkv_group(zr, rwkv_mu[l], rwkv_w0[l], rwkv_w_up[l], rwkv_a0[l], rwkv_a_up[l], rwkv_g_up[l],
                             rwkv_k_k[l], rwkv_k_a[l], rwkv_r_k[l], rwkv_gn_g[l], rwkv_gn_b[l])
        y_nsa = _nsa_group(rel_bias, q, kcvc, k4, v4, gate, nsa_pe_k[l], nsa_w1_k[l], nsa_w2_k[l],
                           nsa_pe_v[l], nsa_w1_v[l], nsa_w2_v[l])
        mk, mv = _mem_kv(mem, xattn_wk[l], xattn_wv[l])
        x3 = _mix_cross_attention_ln(x1.reshape(bsz, s, d), y_rwkv, y_nsa, mix_w_out[l], ln2_g[l], ln2_b[l],
                                     mk, mv, xattn_wq[l], xattn_wo[l], ln3_g[l], ln3_b[l])
        x = _ffn_ln(x3.reshape(rows, d), ffn2_w_gate[l], ffn2_w_up[l], ffn2_w_down[l], ln4_g[l], ln4_b[l])
        x = x.reshape(bsz, s, d)
    return x
```

```python
import functools
import math

import numpy as np
import jax
import jax.numpy as jnp
from jax import lax
from jax.experimental import pallas as pl
from jax.experimental.pallas import tpu as pltpu

F32 = jnp.float32
BF16 = jnp.bfloat16
HIGHEST = lax.Precision.HIGHEST

D_MODEL = 1024
DEPTH = 1
RWKV_HEADS = 8
HEAD_DIM = 64
RWKV_DIM = RWKV_HEADS * HEAD_DIM
DECAY_LORA = 64
AAA_LORA = 64
GATE_LORA = 128
RWKV_IN = 3 * RWKV_DIM + DECAY_LORA + AAA_LORA + GATE_LORA
RWKV_GN_EPS = 64e-5
NSA_HEADS = 8
NSA_GROUPS = 2
NSA_HPG = NSA_HEADS // NSA_GROUPS
NSA_DIM = NSA_HEADS * HEAD_DIM
CMP_LEN = 32
CMP_STRIDE = 16
CMP_HIDDEN = 128
SLC_LEN = 64
N_SEL = 16
WINDOW = 512
Q_BLOCK = 128
FORCED_SCORE = 1e4
NUM_BUCKETS = 32
MAX_DISTANCE = 128
XATTN_HEADS = 4
XATTN_HEAD_DIM = D_MODEL // XATTN_HEADS
D_FF = 2816
LN_EPS = 1e-5
ALPHA = (2.0 * DEPTH) ** 0.25
NEG = -1e30

V7X_LANES = 128
V7X_SUBLANES = 8
V7X_VMEM_LIMIT_BYTES = 56 * 1024 * 1024

CHUNK = 64
CMP_PAD = 120
VT_ROWS = 80


def _dot(a, b, prec=None):
    return jnp.dot(a, b, preferred_element_type=F32, precision=prec)


def _dot_nt(a, b, prec=None):
    return lax.dot_general(a, b, (((1,), (1,)), ((), ())), preferred_element_type=F32, precision=prec)


def _dot_tn(a, b, prec=None):
    return lax.dot_general(a, b, (((0,), (0,)), ((), ())), preferred_element_type=F32, precision=prec)


def _sigmoid(x):
    return 1.0 / (1.0 + jnp.exp(-x))


def _layer_norm(y, g, b):
    mu = jnp.mean(y, axis=-1, keepdims=True)
    yc = y - mu
    var = jnp.mean(yc * yc, axis=-1, keepdims=True)
    return yc * lax.rsqrt(var + LN_EPS) * g + b


def _params(sem):
    return pltpu.CompilerParams(dimension_semantics=sem, vmem_limit_bytes=V7X_VMEM_LIMIT_BYTES)


def _const_spec(shape, single_buffer=False):
    nd = len(shape)
    if single_buffer:
        return pl.BlockSpec(shape, lambda *_: (0,) * nd, pipeline_mode=pl.Buffered(1))
    return pl.BlockSpec(shape, lambda *_: (0,) * nd)


def _bucket_thresholds():
    n = np.arange(0, 4 * MAX_DISTANCE)
    max_exact = NUM_BUCKETS // 2
    nf = np.maximum(n, max_exact).astype(np.float32)
    large = max_exact + (np.log(nf / np.float32(max_exact)) / np.float32(math.log(MAX_DISTANCE / max_exact))
                         * np.float32(NUM_BUCKETS - max_exact)).astype(np.int32)
    large = np.minimum(large, NUM_BUCKETS - 1)
    bucket = np.where(n < max_exact, n, large)
    return [int(np.argmax(bucket >= b)) for b in range(1, NUM_BUCKETS)]


_BUCKET_THR = _bucket_thresholds()


TZ_CMP_NEAR, TZ_DIAG, TZ_SUB, TZ_ZERO, TZ_MASKED, TZ_WIN_OLD, TZ_KINDS = 0, 1, 2, 3, 4, 5, 6


def _bias_tiles_kernel(rb_ref, o_ref):
    h = pl.program_id(0)
    r = lax.broadcasted_iota(jnp.int32, (Q_BLOCK, Q_BLOCK), 0)
    c = lax.broadcasted_iota(jnp.int32, (Q_BLOCK, Q_BLOCK), 1)
    far = jnp.full((Q_BLOCK, Q_BLOCK), rb_ref[NUM_BUCKETS - 1, h], F32)
    far_hi = far.astype(BF16).astype(F32)
    far_added = far_hi + (far - far_hi).astype(BF16).astype(F32)
    dists = (c - CMP_STRIDE * (r - CMP_PAD) - (CMP_LEN - 1), c - r, Q_BLOCK + c - r)
    for kind, dist in enumerate(dists):
        val = jnp.full((Q_BLOCK, Q_BLOCK), rb_ref[0, h], F32)
        for b in range(1, NUM_BUCKETS):
            val = jnp.where(dist >= _BUCKET_THR[b - 1], rb_ref[b, h], val)
        o_ref[0, kind] = jnp.where(dist >= 0, val - far_added, NEG)
    o_ref[0, TZ_ZERO] = far - far_added
    o_ref[0, TZ_MASKED] = jnp.full((Q_BLOCK, Q_BLOCK), NEG, F32)
    o_ref[0, TZ_WIN_OLD] = jnp.where(r > c, far - far_added, NEG)


def _bias_tiles(rel_bias):
    return pl.pallas_call(
        _bias_tiles_kernel,
        grid=(NSA_HEADS,),
        in_specs=[pl.BlockSpec(memory_space=pltpu.SMEM)],
        out_specs=pl.BlockSpec((1, TZ_KINDS, Q_BLOCK, Q_BLOCK), lambda h: (h, 0, 0, 0)),
        out_shape=jax.ShapeDtypeStruct((NSA_HEADS, TZ_KINDS, Q_BLOCK, Q_BLOCK), F32),
        compiler_params=_params(("arbitrary",)),
        name="bias_tiles",
    )(rel_bias)


FFN_TM = 1024
FFN_TF = 256


def _ffn_kernel(x_ref, wg_ref, wu_ref, wd_ref, g_ref, b_ref, o_ref, acc_ref):
    x = x_ref[...]
    xb = x.astype(BF16)
    for c in range(wg_ref.shape[1] // FFN_TF):
        cols = slice(c * FFN_TF, (c + 1) * FFN_TF)
        hg = _dot(xb, wg_ref[:, cols])
        hu = _dot(xb, wu_ref[:, cols])
        h = hg * _sigmoid(hg) * hu
        part = _dot(h.astype(BF16), wd_ref[cols, :])
        if c == 0:
            acc_ref[...] = part
        else:
            acc_ref[...] += part
    o_ref[...] = _layer_norm(ALPHA * x + 0.5 * acc_ref[...], g_ref[...], b_ref[...])


def _ffn_ln(x2d, wg, wu, wd, g, b):
    rows, d = x2d.shape
    f = wg.shape[1]
    assert f % FFN_TF == 0
    tm = min(FFN_TM, rows)
    return pl.pallas_call(
        _ffn_kernel,
        grid=(rows // tm,),
        in_specs=[
            pl.BlockSpec((tm, d), lambda i: (i, 0)),
            _const_spec((d, f), single_buffer=True),
            _const_spec((d, f), single_buffer=True),
            _const_spec((f, d), single_buffer=True),
            _const_spec((1, d)),
            _const_spec((1, d)),
        ],
        out_specs=pl.BlockSpec((tm, d), lambda i: (i, 0)),
        out_shape=jax.ShapeDtypeStruct((rows, d), F32),
        scratch_shapes=[pltpu.VMEM((tm, d), F32)],
        compiler_params=_params(("parallel",)),
        name="ffn_ln",
    )(x2d, wg.astype(BF16), wu.astype(BF16), wd.astype(BF16), g.reshape(1, d), b.reshape(1, d))


PROJ_TM = 512
_C_RWKV = 0
_C_Q = _C_RWKV + RWKV_IN
_C_KCVC = _C_Q + NSA_DIM
_C_K = _C_KCVC + 256
_C_V = _C_K + 256
_C_GATE = _C_V + 256
_C_END = _C_GATE + 128
BIAS_LANES = (HEAD_DIM, HEAD_DIM + 1)
MASK_LANE = HEAD_DIM + 2


def _inproj_kernel(x_ref, w_ref, qx_ref, zr_ref, q_ref, kcvc_ref, k_ref, v_ref, gate_ref):
    xb = x_ref[0].astype(BF16)
    lane = lax.broadcasted_iota(jnp.int32, (1, V7X_LANES), 1)
    zr_ref[0] = _dot(xb, w_ref[:, _C_RWKV:_C_Q])
    low = lane < HEAD_DIM

    def padded_pair(tile, pad_a, pad_b):
        return jnp.where(low, tile, pad_a), jnp.where(low, pltpu.roll(tile, HEAD_DIM, 1), pad_b)

    zq = _dot(xb, w_ref[:, _C_Q:_C_KCVC]) * (HEAD_DIM ** -0.5)
    for j in range(NSA_HEADS // 2):
        qa, qb = padded_pair(zq[:, j * 128:(j + 1) * 128], qx_ref[2 * j:2 * j + 1, :], qx_ref[2 * j + 1:2 * j + 2, :])
        q_ref[0, 2 * j] = qa.astype(BF16)
        q_ref[0, 2 * j + 1] = qb.astype(BF16)
    zc = _dot(xb, w_ref[:, _C_KCVC:_C_K])
    kcvc_ref[0, 0] = zc[:, 0:128]
    kcvc_ref[0, 1] = zc[:, 128:256]
    bias_ones = ((lane == BIAS_LANES[0]) | (lane == BIAS_LANES[1])).astype(F32)
    zk = _dot(xb, w_ref[:, _C_K:_C_V])
    for j in range(2):
        ka, kb = padded_pair(zk[:, j * 128:(j + 1) * 128], bias_ones, bias_ones)
        k_ref[0, 2 * j] = ka.astype(BF16)
        k_ref[0, 2 * j + 1] = kb.astype(BF16)
    tail_rows = VT_ROWS - HEAD_DIM
    tail = (lax.broadcasted_iota(jnp.int32, (tail_rows, Q_BLOCK), 0) == 0).astype(F32)
    zvg = _dot(xb, w_ref[:, _C_V:_C_END])
    for j in range(2):
        zv = zvg[:, j * 128:(j + 1) * 128]
        for i in range(zv.shape[0] // Q_BLOCK):
            zt = zv[i * Q_BLOCK:(i + 1) * Q_BLOCK].T
            for g in range(NSA_GROUPS):
                v_ref[0, 2 * j + g, i] = jnp.concatenate([zt[g * HEAD_DIM:(g + 1) * HEAD_DIM], tail], axis=0).astype(BF16)
    gate_ref[0] = zvg[:, 2 * 128:3 * 128]


def _pack_w_in(w_in):
    d = w_in.shape[0]
    o = RWKV_IN
    q = w_in[:, o:o + 512]
    kc = w_in[:, o + 512:o + 640]
    vc = w_in[:, o + 640:o + 768]
    ks = w_in[:, o + 768:o + 896]
    vs = w_in[:, o + 896:o + 1024]
    kw = w_in[:, o + 1024:o + 1152]
    vw = w_in[:, o + 1152:o + 1280]
    gate = w_in[:, o + 1280:o + 1304]
    gate_pad = jnp.concatenate([gate, jnp.zeros((d, 128 - gate.shape[1]), w_in.dtype)], axis=1)
    return jnp.concatenate([w_in[:, :o], q, kc, vc, ks, kw, vs, vw, gate_pad], axis=1).astype(BF16)


def _far_bias_lanes(rel_bias):
    far = rel_bias[NUM_BUCKETS - 1, :]
    hi = far.astype(BF16).astype(F32)
    lane = jnp.arange(V7X_LANES)[None, :]
    return jnp.where(lane == BIAS_LANES[0], hi[:, None],
                     jnp.where(lane == BIAS_LANES[1], (far - hi)[:, None], jnp.where(lane == MASK_LANE, 1.0, 0.0)))


def _in_projection(x3d, w_in, rel_bias):
    bsz, s, d = x3d.shape
    tm = min(PROJ_TM, s)
    wp = _pack_w_in(w_in)
    out_shape = (
        jax.ShapeDtypeStruct((bsz, s, RWKV_IN), F32),
        jax.ShapeDtypeStruct((bsz, NSA_HEADS, s, 128), BF16),
        jax.ShapeDtypeStruct((bsz, 2, s, 128), F32),
        jax.ShapeDtypeStruct((bsz, 4, s, 128), BF16),
        jax.ShapeDtypeStruct((bsz, 4, s // Q_BLOCK, VT_ROWS, Q_BLOCK), BF16),
        jax.ShapeDtypeStruct((bsz, s, 128), F32),
    )
    return pl.pallas_call(
        _inproj_kernel,
        grid=(bsz, s // tm),
        in_specs=[pl.BlockSpec((1, tm, d), lambda b, i: (b, i, 0)), _const_spec((d, _C_END)),
                  _const_spec((NSA_HEADS, 128))],
        out_specs=(
            pl.BlockSpec((1, tm, RWKV_IN), lambda b, i: (b, i, 0)),
            pl.BlockSpec((1, NSA_HEADS, tm, 128), lambda b, i: (b, 0, i, 0)),
            pl.BlockSpec((1, 2, tm, 128), lambda b, i: (b, 0, i, 0)),
            pl.BlockSpec((1, 4, tm, 128), lambda b, i: (b, 0, i, 0)),
            pl.BlockSpec((1, 4, tm // Q_BLOCK, VT_ROWS, Q_BLOCK), lambda b, i: (b, 0, i, 0, 0)),
            pl.BlockSpec((1, tm, 128), lambda b, i: (b, i, 0)),
        ),
        out_shape=out_shape,
        compiler_params=_params(("parallel", "parallel")),
        name="in_projection",
    )(x3d, wp, _far_bias_lanes(rel_bias))


RWKV_T = 512
RWKV_CHUNKS_PER_STEP = 8


def _rwkv_consts():
    r = lax.broadcasted_iota(jnp.int32, (128, 128), 0)
    c = lax.broadcasted_iota(jnp.int32, (128, 128), 1)
    same = (r >= CHUNK) == (c >= CHUNK)
    mask_sl = (same & (r > c)).astype(F32)
    mask_l = (same & (r >= c)).astype(F32)
    eye = (r == c).astype(F32)
    rt = lax.broadcasted_iota(jnp.int32, (CHUNK, CHUNK), 0)
    ct = lax.broadcasted_iota(jnp.int32, (CHUNK, CHUNK), 1)
    tri = (rt >= ct).astype(F32)
    lane = lax.broadcasted_iota(jnp.int32, (1, 128), 1)
    m0 = (lane < CHUNK).astype(F32)
    m1 = 1.0 - m0
    return mask_sl, mask_l, eye, tri, m0, m1


def _split2(x):
    hi = x.astype(BF16)
    return hi, (x - hi.astype(F32)).astype(BF16)


def _mm(a, b, mode, dot=_dot):
    if mode == "bf16":
        return dot(a.astype(BF16), b.astype(BF16))
    if mode == "bf16x3":
        ah, al = _split2(a)
        bh, bl = _split2(b)
        return dot(ah, bh) + (dot(ah, bl) + dot(al, bh))
    if mode == "ones_b":
        ah, al = _split2(a)
        bb = b.astype(BF16)
        return dot(ah, bb) + dot(al, bb)
    if mode in ("exact_a", "exact_b"):
        x = b if mode == "exact_a" else a
        hi = x.astype(BF16)
        rest = x - hi.astype(F32)
        mid = rest.astype(BF16)
        lo = (rest - mid.astype(F32)).astype(BF16)
        if mode == "exact_a":
            ab = a.astype(BF16)
            return dot(ab, hi) + (dot(ab, mid) + dot(ab, lo))
        bb = b.astype(BF16)
        return dot(hi, bb) + (dot(mid, bb) + dot(lo, bb))
    raise ValueError(mode)


RWKV_MODES = dict(p="bf16", inv="bf16", av="bf16", wu="bf16", qy="bf16", mn="bf16", y="bf16", h="bf16")


def _rwkv_chunk(rs, lws, ks, vs, kks, as_, hs, consts):
    mask_sl, mask_l, eye, tri, m0, m1 = consts
    md = RWKV_MODES
    n = len(rs)
    each = range(n)

    def sm(x):
        return jnp.concatenate([x * m0, x * m1], axis=0)

    def dup(x):
        return jnp.concatenate([x, x], axis=0)

    cums = [_mm(tri, lws[i], "exact_a") for i in each]
    a_sm, r_sm, v_sm, kb, kbh, g_c = [], [], [], [], [], []
    for i in each:
        cum, lw, kk, k = cums[i], lws[i], kks[i], ks[i]
        cl = cum[CHUNK - 1:CHUNK, :]
        ka = kk * as_[i]
        g_tail = jnp.exp(cl - cum)
        g_inv = jnp.exp(-cum)
        a_sm.append(sm(-(kk * jnp.exp(cum - lw))))
        r_sm.append(sm(rs[i] * jnp.exp(cum)))
        v_sm.append(sm(vs[i]))
        kb.append(jnp.concatenate([dup(k * g_inv), dup(ka * g_inv)], axis=0))
        kbh.append(jnp.concatenate([sm(k * g_tail), sm(ka * g_tail)], axis=0))
        g_c.append(jnp.exp(cl))
    pm = [_mm(jnp.concatenate([a_sm[i], r_sm[i]], axis=0), kb[i], md["p"], _dot_nt) for i in each]
    a_ak = [pm[i][0:128, 0:128] * mask_sl for i in each]
    a_rr = [jnp.concatenate([pm[i][128:256, 0:128] * mask_l, pm[i][128:256, 128:256] * mask_l], axis=1) for i in each]
    x = [pm[i][0:128, 128:256] * mask_sl for i in each]
    t_inv = [eye + x[i] for i in each]
    for _ in range(5):
        x = [_mm(x[i], x[i], md["inv"]) for i in each]
        t_inv = [t_inv[i] + _mm(t_inv[i], x[i], md["inv"]) for i in each]
    av = [_mm(a_ak[i], v_sm[i], md["av"]) for i in each]
    wu = [_mm(t_inv[i], jnp.concatenate([a_sm[i], av[i]], axis=1), md["wu"]) for i in each]
    z = [jnp.concatenate([jnp.concatenate([jnp.zeros_like(v_sm[i]), v_sm[i]], axis=1), wu[i]], axis=0) for i in each]
    qy = [_mm(a_rr[i], z[i], md["qy"]) for i in each]
    mn = [_mm(kbh[i], z[i], md["mn"], _dot_tn) for i in each]
    n_pairs = len(hs)
    ys = []
    for c0 in range(0, n, n_pairs):
        idx = range(c0, c0 + n_pairs)
        y_sm = [_mm(r_sm[i] + qy[i][:, 0:128], hs[i - c0], md["y"]) + qy[i][:, 128:256] for i in idx]
        hs = [_mm(mn[i][:, 0:128] + eye * g_c[i], hs[i - c0], md["h"]) + mn[i][:, 128:256] for i in idx]
        ys += [y[0:CHUNK] + y[CHUNK:2 * CHUNK] for y in y_sm]
    return ys, hs


def _rwkv_kernel(z_ref, mu_ref, w0_ref, wup_ref, a0_ref, aup_ref, gup_ref, kk_ref, ka_ref, rk_ref,
                 gng_ref, gnb_ref, o_ref, prev_ref, h_ref, r_s, lw_s, k_s, v_s, kk_s, a_s, y_s, g_s, *, t):
    ti = pl.program_id(1)

    @pl.when(ti == 0)
    def _():
        prev_ref[...] = jnp.zeros_like(prev_ref)
        h_ref[...] = jnp.zeros_like(h_ref)

    consts = _rwkv_consts()

    z = z_ref[0]
    row = lax.broadcasted_iota(jnp.int32, (t, 1), 0)
    prev = jnp.where(row == 0, prev_ref[0:1, :], pltpu.roll(z, 1, 0))
    prev_ref[0:1, :] = z[t - 1:t, :]
    zs = z + (prev - z) * mu_ref[...]

    r = zs[:, 0:512]
    k = zs[:, 512:1024]
    v = zs[:, 1024:1536]
    wa = zs[:, 1536:1664]
    gl = zs[:, 1664:1792]
    u = w0_ref[...] + _dot(jnp.tanh(wa).astype(BF16), wup_ref[...])
    lw = (-math.exp(-0.5)) * _sigmoid(u)
    a = _sigmoid(a0_ref[...] + _dot(wa.astype(BF16), aup_ref[...]))
    g_s[...] = _dot(_sigmoid(gl).astype(BF16), gup_ref[...])
    kkr = k * kk_ref[...]
    k2 = k * (1.0 + (a - 1.0) * ka_ref[...])
    r_s[...] = r
    lw_s[...] = lw
    k_s[...] = k2
    v_s[...] = v
    a_s[...] = a
    hr = lax.broadcasted_iota(jnp.int32, (256, 256), 0) // HEAD_DIM
    hc = lax.broadcasted_iota(jnp.int32, (256, 256), 1) // HEAD_DIM
    head_ones = (hr == hc).astype(F32)
    halves = [slice(i * 256, (i + 1) * 256) for i in range(2)]
    for sl in halves:
        kp = kkr[:, sl]
        ss = _mm(kp * kp, head_ones, "ones_b")
        kk_s[:, sl] = kp * lax.rsqrt(jnp.maximum(ss, 1e-24))

    pairs = [slice(pr * 128, (pr + 1) * 128) for pr in range(4)]

    def chunk_body(c, carry):
        rows = [pl.ds(pl.multiple_of((c * RWKV_CHUNKS_PER_STEP + i) * CHUNK, CHUNK), CHUNK)
                for i in range(RWKV_CHUNKS_PER_STEP)]
        ys, hs = _rwkv_chunk(*[[ref[rw, sl] for rw in rows for sl in pairs]
                               for ref in (r_s, lw_s, k_s, v_s, kk_s, a_s)],
                             [h_ref[pr] for pr in range(4)], consts)
        for i, rw in enumerate(rows):
            for pr, sl in enumerate(pairs):
                y_s[rw, sl] = ys[i * 4 + pr]
        for pr in range(4):
            h_ref[pr] = hs[pr]
        return carry

    lax.fori_loop(0, t // (CHUNK * RWKV_CHUNKS_PER_STEP), chunk_body, 0)

    for sl in halves:
        y = y_s[:, sl]
        mean = _mm(y, head_ones, "ones_b") * (1.0 / HEAD_DIM)
        yc = y - mean
        var = _mm(yc * yc, head_ones, "ones_b") * (1.0 / HEAD_DIM)
        yn = yc * lax.rsqrt(var + RWKV_GN_EPS) * gng_ref[:, sl] + gnb_ref[:, sl]
        rp = r_s[:, sl]
        bonus = _mm(rp * k_s[:, sl] * rk_ref[:, sl], head_ones, "ones_b") * v_s[:, sl]
        o_ref[0, :, sl] = ((yn + bonus) * g_s[:, sl]).astype(o_ref.dtype)


def _rwkv_group(zr, mu, w0, w_up, a0, a_up, g_up, k_k, k_a, r_k, gn_g, gn_b):
    bsz, s, _ = zr.shape
    t = min(RWKV_T, s)
    wup_pad = jnp.concatenate([w_up, jnp.zeros_like(a_up)], axis=0).astype(BF16)
    aup_pad = jnp.concatenate([jnp.zeros_like(w_up), a_up], axis=0).astype(BF16)
    row = lambda p: p.reshape(1, -1)
    tile = pltpu.VMEM((t, RWKV_DIM), F32)
    return pl.pallas_call(
        functools.partial(_rwkv_kernel, t=t),
        grid=(bsz, s // t),
        in_specs=[
            pl.BlockSpec((1, t, RWKV_IN), lambda b, i: (b, i, 0)),
            _const_spec((1, RWKV_IN)), _const_spec((1, RWKV_DIM)), _const_spec((128, RWKV_DIM)),
            _const_spec((1, RWKV_DIM)), _const_spec((128, RWKV_DIM)), _const_spec((GATE_LORA, RWKV_DIM)),
            _const_spec((1, RWKV_DIM)), _const_spec((1, RWKV_DIM)), _const_spec((1, RWKV_DIM)),
            _const_spec((1, RWKV_DIM)), _const_spec((1, RWKV_DIM)),
        ],
        out_specs=pl.BlockSpec((1, t, RWKV_DIM), lambda b, i: (b, i, 0)),
        out_shape=jax.ShapeDtypeStruct((bsz, s, RWKV_DIM), BF16),
        scratch_shapes=[pltpu.VMEM((V7X_SUBLANES, RWKV_IN), F32), pltpu.VMEM((4, 128, 128), F32),
                        tile, tile, tile, tile, tile, tile, tile, tile],
        compiler_params=_params(("parallel", "arbitrary")),
        name="rwkv7",
    )(zr, row(mu), row(w0), wup_pad, row(a0), aup_pad, g_up.astype(BF16), row(k_k), row(k_a), row(r_k),
      row(gn_g), row(gn_b))


def _gelu_tanh(x):
    return 0.5 * x * (1.0 + jnp.tanh(math.sqrt(2.0 / math.pi) * (x + 0.044715 * (x * x * x))))


def _compress_kernel(x_ref, pe_ref, w1_ref, w2_ref, kc_ref, vc_ref, *, n_rows):
    half = CMP_LEN // 2
    for kv, o_ref in enumerate((kc_ref, vc_ref)):
        first = jnp.zeros((n_rows, 2 * CMP_HIDDEN), F32)
        second = jnp.zeros((n_rows, 2 * CMP_HIDDEN), F32)
        for p in range(half):
            xa = x_ref[0, kv, pl.ds(p, n_rows, stride=CMP_STRIDE), :]
            first += _dot((xa + pe_ref[kv, p:p + 1, :]).astype(BF16), w1_ref[kv, p])
            second += _dot((xa + pe_ref[kv, half + p:half + p + 1, :]).astype(BF16), w1_ref[kv, half + p])
        pre = first + pltpu.roll(second, n_rows - 1, 0)
        hid = _gelu_tanh(pre).astype(BF16)
        width = o_ref.shape[3]
        if kv == 0:
            lane = lax.broadcasted_iota(jnp.int32, (1, V7X_LANES), 1)
            bias_ones = ((lane == BIAS_LANES[0]) | (lane == BIAS_LANES[1])).astype(F32)
            outs = [_dot(hid, w2_ref[g]) + bias_ones for g in range(NSA_GROUPS)]
        else:
            out = _dot(hid, w2_ref[NSA_GROUPS])
            outs = [out[:, g * HEAD_DIM:(g + 1) * HEAD_DIM] for g in range(NSA_GROUPS)]
        for g in range(NSA_GROUPS):
            o_ref[0, g, 0:CMP_PAD, :] = jnp.zeros((CMP_PAD, width), o_ref.dtype)
            o_ref[0, g, CMP_PAD:CMP_PAD + n_rows, :] = outs[g].astype(o_ref.dtype)
            tail = o_ref.shape[2] - CMP_PAD - n_rows
            o_ref[0, g, CMP_PAD + n_rows:, :] = jnp.zeros((tail, width), o_ref.dtype)


def _pair_diag(w):
    z = jnp.zeros_like(w)
    return jnp.concatenate([jnp.concatenate([w, z], axis=-1), jnp.concatenate([z, w], axis=-1)], axis=-2)


def _compress(kcvc, pe_k, w1_k, w2_k, pe_v, w1_v, w2_v):
    bsz, _, s, _ = kcvc.shape
    n_rows = s // CMP_STRIDE
    rows_out = CMP_PAD + n_rows + V7X_SUBLANES
    pe = jnp.stack([jnp.concatenate([pe_k, pe_k], axis=1), jnp.concatenate([pe_v, pe_v], axis=1)])
    w1 = jnp.stack([_pair_diag(w1_k.reshape(CMP_LEN, HEAD_DIM, CMP_HIDDEN)),
                    _pair_diag(w1_v.reshape(CMP_LEN, HEAD_DIM, CMP_HIDDEN))]).astype(BF16)
    zk = jnp.zeros_like(w2_k)
    w2_pad = lambda g: jnp.concatenate([jnp.concatenate([w2_k if i == g else zk, zk], axis=1)
                                        for i in range(NSA_GROUPS)], axis=0)
    w2 = jnp.stack([w2_pad(0), w2_pad(1), _pair_diag(w2_v)]).astype(BF16)
    shapes = tuple(jax.ShapeDtypeStruct((bsz, NSA_GROUPS, rows_out, w), BF16) for w in (128, HEAD_DIM))
    specs = tuple(pl.BlockSpec((1, NSA_GROUPS, rows_out, w), lambda b: (b, 0, 0, 0)) for w in (128, HEAD_DIM))
    return pl.pallas_call(
        functools.partial(_compress_kernel, n_rows=n_rows),
        grid=(bsz,),
        in_specs=[pl.BlockSpec((1, 2, s, 128), lambda b: (b, 0, 0, 0)), _const_spec(pe.shape), _const_spec(w1.shape),
                  _const_spec(w2.shape)],
        out_specs=specs,
        out_shape=shapes,
        compiler_params=_params(("parallel",)),
        name="nsa_compress",
    )(kcvc, pe, w1, w2)


CMP_QBLOCKS = 4


def _cmp_select_kernel(q_ref, kc_ref, vc_ref, tz_ref, gate_ref, oc_ref, sel_ref, *, n_far):
    for sub in range(CMP_QBLOCKS):
        _cmp_select_block(pl.program_id(1) * CMP_QBLOCKS + sub, slice(sub * Q_BLOCK, (sub + 1) * Q_BLOCK),
                          q_ref, kc_ref, vc_ref, tz_ref, gate_ref, oc_ref, sel_ref, n_far)


def _cmp_select_block(qi, qrows, q_ref, kc_ref, vc_ref, tz_ref, gate_ref, oc_ref, sel_ref, n_far):
    qb = Q_BLOCK
    rowi = lax.broadcasted_iota(jnp.int32, (qb, 128), 0)
    lane = lax.broadcasted_iota(jnp.int32, (qb, 128), 1)
    row_f = rowi.astype(F32)
    sg = _sigmoid(gate_ref[0, qrows, :])
    near0 = pl.multiple_of(qi * 8, 8)
    n_first = qi * 8 - CMP_PAD
    for g in range(NSA_GROUPS):
        q4 = q_ref[0, g * NSA_HPG:(g + 1) * NSA_HPG, qrows, :].reshape(NSA_HPG * qb, 128)
        heads = [g * NSA_HPG + p for p in range(NSA_HPG)]
        cols = [slice(p * qb, (p + 1) * qb) for p in range(NSA_HPG)]

        def masked_keys(kt, ok):
            return jnp.where((lane == MASK_LANE) & jnp.logical_not(ok), jnp.asarray(NEG, BF16), kt)

        tiles, values, n_of_lane = [], [], []
        for j in range(n_far):
            rows = slice(CMP_PAD + 128 * j, CMP_PAD + 128 * (j + 1))
            tiles.append(_dot_nt(masked_keys(kc_ref[0, g, rows, :], (128 * j + rowi) < n_first), q4))
            values.append(vc_ref[0, g, rows, :])
            n_of_lane.append(128 * j + lane)
        s = _dot_nt(masked_keys(kc_ref[0, g, pl.ds(near0, 128), :], (n_first + rowi) >= 0), q4)
        tiles.append(jnp.concatenate([s[:, cols[p]] + tz_ref[heads[p], TZ_CMP_NEAR] for p in range(NSA_HPG)], axis=1))
        values.append(vc_ref[0, g, pl.ds(near0, 128), :])
        n_of_lane.append(n_first + lane)

        m = tiles[0].max(axis=0, keepdims=True)
        for tl in tiles[1:]:
            m = jnp.maximum(m, tl.max(axis=0, keepdims=True))
        es = [jnp.exp(tl - m) for tl in tiles]
        den = es[0].sum(axis=0, keepdims=True)
        for e in es[1:]:
            den = den + e.sum(axis=0, keepdims=True)
        inv = jnp.where(m > 0.5 * NEG, 1.0 / jnp.maximum(den, 1e-30), 0.0)
        o_t = jnp.zeros((HEAD_DIM, NSA_HPG * qb), F32)
        imp = jnp.zeros((qb, 128), F32)
        for j, e in enumerate(es):
            pc = e * inv
            o_t = o_t + _dot_tn(values[j], pc.astype(BF16))
            psum = pc[:, cols[0]] + pc[:, cols[1]] + pc[:, cols[2]] + pc[:, cols[3]]
            ov = ((n_of_lane[j] >= 4 * rowi - 1) & (n_of_lane[j] <= 4 * rowi + 3)).astype(F32)
            imp = imp + _mm(ov, psum, "exact_a")
        for p, h in enumerate(heads):
            oc_ref[0, qrows, h * HEAD_DIM:(h + 1) * HEAD_DIM] = o_t[:, cols[p]].T * sg[:, 3 * h:3 * h + 1]
        cur = 2 * qi + (lane >= SLC_LEN).astype(jnp.int32)
        forced = (rowi == 0) | (rowi == cur) | (rowi == cur - 1)
        score = jnp.where(rowi <= cur, jnp.where(forced, -3e38, imp), -1.0)
        sel = jnp.where(forced & (rowi <= cur), 1.0, 0.0)
        for _ in range(N_SEL - 3):
            mx = score.max(axis=0, keepdims=True)
            idx = jnp.where(score == mx, row_f, 128.0).min(axis=0, keepdims=True)
            pick = row_f == idx
            sel = jnp.where(pick, 1.0, sel)
            score = jnp.where(pick, -3e38, score)
        sel_ref[0, g, qrows, :] = jnp.where(sel.T > 0.5, 0.0, NEG).astype(sel_ref.dtype)


def _cmp_select(q, kc, vc, tz, gate):
    bsz, _, s, _ = q.shape
    nq = s // Q_BLOCK
    n_far = max(0, -(-(8 * (nq - 1) - CMP_PAD) // 128))
    rows_c = kc.shape[2]
    assert nq % CMP_QBLOCKS == 0
    tq = CMP_QBLOCKS * Q_BLOCK
    return pl.pallas_call(
        functools.partial(_cmp_select_kernel, n_far=n_far),
        grid=(bsz, nq // CMP_QBLOCKS),
        in_specs=[
            pl.BlockSpec((1, NSA_HEADS, tq, 128), lambda b, i: (b, 0, i, 0)),
            pl.BlockSpec((1, NSA_GROUPS, rows_c, 128), lambda b, i: (b, 0, 0, 0)),
            pl.BlockSpec((1, NSA_GROUPS, rows_c, HEAD_DIM), lambda b, i: (b, 0, 0, 0)),
            _const_spec(tz.shape),
            pl.BlockSpec((1, tq, 128), lambda b, i: (b, i, 0)),
        ],
        out_specs=(
            pl.BlockSpec((1, tq, NSA_DIM), lambda b, i: (b, i, 0)),
            pl.BlockSpec((1, NSA_GROUPS, tq, 128), lambda b, i: (b, 0, i, 0)),
        ),
        out_shape=(jax.ShapeDtypeStruct((bsz, s, NSA_DIM), F32),
                   jax.ShapeDtypeStruct((bsz, NSA_GROUPS, s, 128), BF16)),
        compiler_params=_params(("parallel", "parallel")),
        name="nsa_cmp_select",
    )(q, kc, vc, tz, gate)


SEL_QBLOCKS = 4
SEL_TILES = 2


def _attn_scores(qa, ka, bias_fns, m_prev):
    qb = Q_BLOCK
    s = _dot_nt(ka, qa)
    if bias_fns is not None:
        s = jnp.concatenate(
            [jnp.concatenate([s[i * qb:(i + 1) * qb, p * qb:(p + 1) * qb] + fn(p) for p in range(NSA_HPG)], axis=1)
             for i, fn in enumerate(bias_fns)], axis=0)
    return s, jnp.maximum(m_prev, s.max(axis=0, keepdims=True))


def _attn_accumulate(s, vts, m_prev, m_next, acc_ref):
    pexp = jnp.exp(s - m_next)
    acc_ref[...] = jnp.exp(m_prev - m_next) * acc_ref[...] + _dot(jnp.concatenate(vts, axis=1), pexp.astype(BF16))


def _sel_win_kernel(q_ref, k_ref, vt_ref, e_ref, sel_ref, tz_ref, gate_ref, oc_ref, o_ref, m_ref, acc_ref,
                    s0_ref, s1_ref, mm0_ref, mm1_ref):
    qb = Q_BLOCK
    win_tiles = WINDOW // qb
    chains = [(sub, g) for sub in range(SEL_QBLOCKS) for g in range(NSA_GROUPS)]
    n_chains = len(chains)
    qis = [pl.program_id(1) * SEL_QBLOCKS + sub for sub in range(SEL_QBLOCKS)]
    m_ref[...] = jnp.full(m_ref.shape, NEG, F32)
    acc_ref[...] = jnp.zeros(acc_ref.shape, F32)
    qrows = [slice(sub * qb, (sub + 1) * qb) for sub in range(SEL_QBLOCKS)]
    q4 = [q_ref[0, g * NSA_HPG:(g + 1) * NSA_HPG, qrows[sub], :].reshape(NSA_HPG * qb, 128) for sub, g in chains]
    selm = [jnp.concatenate([sel_ref[0, g, qrows[sub], :]] * NSA_HPG, axis=0) for sub, g in chains]

    def key_rows(kt, n=1):
        return pl.ds(pl.multiple_of(kt * qb, qb), n * qb)

    def bias_of(g, kind):
        return lambda p: tz_ref[g * NSA_HPG + p, kind]

    def keys_sel(g, kt, n=1):
        return jnp.concatenate([k_ref[0, g, key_rows(kt, n), :], e_ref[key_rows(kt, n), :]], axis=1)

    block = lax.broadcasted_iota(jnp.int32, (NSA_HPG * qb, 128), 1)

    n_far = [(jnp.maximum(qi - 1, 0) + SEL_TILES - 1) // SEL_TILES for qi in qis]
    n_run = n_far[-1] | 1
    last_tile0 = k_ref.shape[2] // qb - SEL_TILES

    def tile0(step):
        return jnp.minimum(step * SEL_TILES, last_tile0)

    def score_far(step, s_ref, mm_ref):
        for c, (sub, g) in enumerate(chains):
            hidden = (block >= 2 * (qis[sub] - 1)) | (step >= n_far[sub])
            qa = jnp.concatenate([q4[c], jnp.where(hidden, jnp.asarray(NEG, BF16), selm[c])], axis=1)
            m_prev = m_ref[c]
            s, m_next = _attn_scores(qa, keys_sel(g, tile0(step), SEL_TILES), None, m_prev)
            s_ref[c] = s
            mm_ref[c, 0:1, :] = m_prev
            mm_ref[c, 1:2, :] = m_next
            m_ref[c] = m_next

    def accumulate_far(step, s_ref, mm_ref):
        for c, (sub, g) in enumerate(chains):
            _attn_accumulate(s_ref[c], [vt_ref[0, g, tile0(step) + i] for i in range(SEL_TILES)],
                             mm_ref[c, 0:1, :], mm_ref[c, 1:2, :], acc_ref.at[c])

    score_far(0, s0_ref, mm0_ref)

    def far_body(i, carry):
        score_far(2 * i + 1, s1_ref, mm1_ref)
        accumulate_far(2 * i, s0_ref, mm0_ref)
        score_far(2 * i + 2, s0_ref, mm0_ref)
        accumulate_far(2 * i + 1, s1_ref, mm1_ref)
        return carry

    lax.fori_loop(0, n_run // 2, far_body, 0)

    scored = []
    for c, (sub, g) in enumerate(chains):
        qi = qis[sub]
        kt_sub = jnp.maximum(qi - 1, 0)
        kinds = (jnp.where(qi >= 1, TZ_SUB, TZ_MASKED), TZ_DIAG)
        m_prev = m_ref[c]
        s, m_next = _attn_scores(jnp.concatenate([q4[c], selm[c]], axis=1),
                                 jnp.concatenate([keys_sel(g, kt_sub), keys_sel(g, qi)], axis=0),
                                 [bias_of(g, kind) for kind in kinds], m_prev)
        scored.append((c, s, [vt_ref[0, g, kt_sub], vt_ref[0, g, qi]], m_prev, m_next))
    for c, (sub, g) in enumerate(chains):
        qi = qis[sub]
        kw = NSA_GROUPS + g
        ks, vts, biases = [], [], []
        for d in range(win_tiles, -1, -1):
            kt = jnp.maximum(qi - d, 0)
            kind = TZ_DIAG if d == 0 else TZ_SUB if d == 1 else TZ_WIN_OLD if d == win_tiles else TZ_ZERO
            if d > 0:
                kind = jnp.where(qi >= d, kind, TZ_MASKED)
            ks.append(k_ref[0, kw, key_rows(kt), :])
            vts.append(vt_ref[0, kw, kt])
            biases.append(bias_of(g, kind))
        m_prev = m_ref[n_chains + c]
        s, m_next = _attn_scores(q4[c], jnp.concatenate(ks, axis=0), biases, m_prev)
        scored.append((n_chains + c, s, vts, m_prev, m_next))
    accumulate_far(n_run - 1, s0_ref, mm0_ref)
    for stream, s, vts, m_prev, m_next in scored:
        _attn_accumulate(s, vts, m_prev, m_next, acc_ref.at[stream])

    for c, (sub, g) in enumerate(chains):
        sg = _sigmoid(gate_ref[0, qrows[sub], :])
        outs = []
        for stream in (c, n_chains + c):
            acc = acc_ref[stream]
            outs.append(acc[0:HEAD_DIM] * (1.0 / acc[HEAD_DIM:HEAD_DIM + 1]))
        for p in range(NSA_HPG):
            h = g * NSA_HPG + p
            qcols = slice(p * qb, (p + 1) * qb)
            cols = slice(h * HEAD_DIM, (h + 1) * HEAD_DIM)
            y = (oc_ref[0, qrows[sub], cols] + sg[:, 3 * h + 1:3 * h + 2] * outs[0][:, qcols].T
                 + sg[:, 3 * h + 2:3 * h + 3] * outs[1][:, qcols].T)
            o_ref[0, qrows[sub], cols] = y.astype(o_ref.dtype)


def _sel_win(q, k4, v4, sel, tz, gate, oc):
    bsz, _, s, _ = q.shape
    nq = s // Q_BLOCK
    assert nq % SEL_TILES == 0 and nq % SEL_QBLOCKS == 0
    member = (jnp.arange(s)[:, None] // SLC_LEN == jnp.arange(128)[None, :]).astype(BF16)
    tq = SEL_QBLOCKS * Q_BLOCK
    n_chains = SEL_QBLOCKS * NSA_GROUPS
    lanes = NSA_HPG * Q_BLOCK
    return pl.pallas_call(
        _sel_win_kernel,
        grid=(bsz, nq // SEL_QBLOCKS),
        in_specs=[
            pl.BlockSpec((1, NSA_HEADS, tq, 128), lambda b, i: (b, 0, i, 0)),
            pl.BlockSpec((1, 4, s, 128), lambda b, i: (b, 0, 0, 0), pipeline_mode=pl.Buffered(1)),
            pl.BlockSpec((1, 4, nq, VT_ROWS, Q_BLOCK), lambda b, i: (b, 0, 0, 0, 0), pipeline_mode=pl.Buffered(1)),
            _const_spec((s, 128), single_buffer=True),
            pl.BlockSpec((1, NSA_GROUPS, tq, 128), lambda b, i: (b, 0, i, 0)),
            _const_spec(tz.shape, single_buffer=True),
            pl.BlockSpec((1, tq, 128), lambda b, i: (b, i, 0)),
            pl.BlockSpec((1, tq, NSA_DIM), lambda b, i: (b, i, 0)),
        ],
        out_specs=pl.BlockSpec((1, tq, NSA_DIM), lambda b, i: (b, i, 0)),
        out_shape=jax.ShapeDtypeStruct((bsz, s, NSA_DIM), BF16),
        scratch_shapes=[pltpu.VMEM((2 * n_chains, 1, lanes), F32),
                        pltpu.VMEM((2 * n_chains, VT_ROWS, lanes), F32),
                        pltpu.VMEM((n_chains, SEL_TILES * Q_BLOCK, lanes), F32),
                        pltpu.VMEM((n_chains, SEL_TILES * Q_BLOCK, lanes), F32),
                        pltpu.VMEM((n_chains, 2, lanes), F32),
                        pltpu.VMEM((n_chains, 2, lanes), F32)],
        compiler_params=_params(("parallel", "parallel")),
        name="nsa_sel_win",
    )(q, k4, v4, member, sel, tz, gate, oc)


XATTN_TM = 1024


def _mem_kv_kernel(mem_ref, wk_ref, wv_ref, k_ref, v_ref):
    mb = mem_ref[0].astype(BF16)
    k_ref[0] = _dot(mb, wk_ref[...]).astype(BF16)
    v_ref[0] = _dot(mb, wv_ref[...]).astype(BF16)


def _mem_kv(mem, wk, wv):
    bsz, m, d = mem.shape
    shape = jax.ShapeDtypeStruct((bsz, m, d), BF16)
    spec = pl.BlockSpec((1, m, d), lambda b: (b, 0, 0))
    return pl.pallas_call(
        _mem_kv_kernel,
        grid=(bsz,),
        in_specs=[spec, _const_spec((d, d)), _const_spec((d, d))],
        out_specs=(spec, spec),
        out_shape=(shape, shape),
        compiler_params=_params(("parallel",)),
        name="xattn_mem_kv",
    )(mem, wk.astype(BF16), wv.astype(BF16))


def _mix_xattn_kernel(x_ref, yr_ref, yn_ref, wm_ref, g2_ref, b2_ref, k_ref, v_ref, wq_ref, wo_ref, g_ref, b_ref, o_ref):
    mixed = _dot(yr_ref[0], wm_ref[0:RWKV_DIM, :]) + _dot(yn_ref[0], wm_ref[RWKV_DIM:, :])
    x = _layer_norm(ALPHA * x_ref[0] + mixed, g2_ref[...], b2_ref[...])
    q = (_dot(x.astype(BF16), wq_ref[...]) * (XATTN_HEAD_DIM ** -0.5)).astype(BF16)
    heads = []
    for h in range(XATTN_HEADS):
        cols = slice(h * XATTN_HEAD_DIM, (h + 1) * XATTN_HEAD_DIM)
        s = _dot_nt(q[:, cols], k_ref[0, :, cols])
        e = jnp.exp(s - s.max(axis=1, keepdims=True))
        p = e * (1.0 / e.sum(axis=1, keepdims=True))
        heads.append(_dot(p.astype(BF16), v_ref[0, :, cols]).astype(BF16))
    o = _dot(jnp.concatenate(heads, axis=1), wo_ref[...])
    o_ref[0] = _layer_norm(ALPHA * x + o, g_ref[...], b_ref[...])


def _mix_cross_attention_ln(x3d, yr, yn, w_out, g2, b2, k, v, wq, wo, g, b):
    bsz, s, d = x3d.shape
    m = k.shape[1]
    tm = min(XATTN_TM, s)
    rows = lambda w: pl.BlockSpec((1, tm, w), lambda bb, i: (bb, i, 0))
    mem = pl.BlockSpec((1, m, d), lambda bb, i: (bb, 0, 0))
    vec = _const_spec((1, d))
    return pl.pallas_call(
        _mix_xattn_kernel,
        grid=(bsz, s // tm),
        in_specs=[rows(d), rows(RWKV_DIM), rows(NSA_DIM), _const_spec(w_out.shape), vec, vec,
                  mem, mem, _const_spec((d, d)), _const_spec((d, d)), vec, vec],
        out_specs=rows(d),
        out_shape=jax.ShapeDtypeStruct((bsz, s, d), F32),
        compiler_params=_params(("parallel", "parallel")),
        name="mix_xattn_ln",
    )(x3d, yr, yn, w_out.astype(BF16), g2.reshape(1, d), b2.reshape(1, d), k, v, wq.astype(BF16), wo.astype(BF16),
      g.reshape(1, d), b.reshape(1, d))


def _nsa_group(rel_bias, q, kcvc, k4, v4, gate, pe_k, w1_k, w2_k, pe_v, w1_v, w2_v):
    tz = _bias_tiles(rel_bias)
    kc, vc = _compress(kcvc, pe_k, w1_k, w2_k, pe_v, w1_v, w2_v)
    oc, sel = _cmp_select(q, kc, vc, tz, gate)
    return _sel_win(q, k4, v4, sel, tz, gate, oc)


def kernel(x, mem, ffn1_w_gate, ffn1_w_up, ffn1_w_down, ln1_g, ln1_b, mix_w_in, rwkv_mu, rwkv_w0, rwkv_w_up, rwkv_a0, rwkv_a_up, rwkv_g_up, rwkv_k_k, rwkv_k_a, rwkv_r_k, rwkv_gn_g, rwkv_gn_b, nsa_pe_k, nsa_w1_k, nsa_w2_k, nsa_pe_v, nsa_w1_v, nsa_w2_v, mix_w_out, ln2_g, ln2_b, xattn_wq, xattn_wk, xattn_wv, xattn_wo, ln3_g, ln3_b, ffn2_w_gate, ffn2_w_up, ffn2_w_down, ln4_g, ln4_b, rel_bias):
    bsz, s, d = x.shape
    rows = bsz * s
    for l in range(DEPTH):
        x1 = _ffn_ln(x.reshape(rows, d), ffn1_w_gate[l], ffn1_w_up[l], ffn1_w_down[l], ln1_g[l], ln1_b[l])
        zr, q, kcvc, k4, v4, gate = _in_projection(x1.reshape(bsz, s, d), mix_w_in[l], rel_bias)
        y_rwkv = _rwkv_group(zr, rwkv_mu[l], rwkv_w0[l], rwkv_w_up[l], rwkv_a0[l], rwkv_a_up[l], rwkv_g_up[l],
                             rwkv_k_k[l], rwkv_k_a[l], rwkv_r_k[l], rwkv_gn_g[l], rwkv_gn_b[l])
        y_nsa = _nsa_group(rel_bias, q, kcvc, k4, v4, gate, nsa_pe_k[l], nsa_w1_k[l], nsa_w2_k[l],
                           nsa_pe_v[l], nsa_w1_v[l], nsa_w2_v[l])
        mk, mv = _mem_kv(mem, xattn_wk[l], xattn_wv[l])
        x3 = _mix_cross_attention_ln(x1.reshape(bsz, s, d), y_rwkv, y_nsa, mix_w_out[l], ln2_g[l], ln2_b[l],
                                     mk, mv, xattn_wq[l], xattn_wo[l], ln3_g[l], ln3_b[l])
        x = _ffn_ln(x3.reshape(rows, d), ffn2_w_gate[l], ffn2_w_up[l], ffn2_w_down[l], ln4_g[l], ln4_b[l])
        x = x.reshape(bsz, s, d)
    return x
```

```python
import functools
import math

import numpy as np
import jax
import jax.numpy as jnp
from jax import lax
from jax.experimental import pallas as pl
from jax.experimental.pallas import tpu as pltpu

F32 = jnp.float32
BF16 = jnp.bfloat16
HIGHEST = lax.Precision.HIGHEST

D_MODEL = 1024
DEPTH = 1
RWKV_HEADS = 8
HEAD_DIM = 64
RWKV_DIM = RWKV_HEADS * HEAD_DIM
DECAY_LORA = 64
AAA_LORA = 64
GATE_LORA = 128
RWKV_IN = 3 * RWKV_DIM + DECAY_LORA + AAA_LORA + GATE_LORA
RWKV_GN_EPS = 64e-5
NSA_HEADS = 8
NSA_GROUPS = 2
NSA_HPG = NSA_HEADS // NSA_GROUPS
NSA_DIM = NSA_HEADS * HEAD_DIM
CMP_LEN = 32
CMP_STRIDE = 16
CMP_HIDDEN = 128
SLC_LEN = 64
N_SEL = 16
WINDOW = 512
Q_BLOCK = 128
FORCED_SCORE = 1e4
NUM_BUCKETS = 32
MAX_DISTANCE = 128
XATTN_HEADS = 4
XATTN_HEAD_DIM = D_MODEL // XATTN_HEADS
D_FF = 2816
LN_EPS = 1e-5
ALPHA = (2.0 * DEPTH) ** 0.25
NEG = -1e30

V7X_LANES = 128
V7X_SUBLANES = 8
V7X_VMEM_LIMIT_BYTES = 56 * 1024 * 1024

CHUNK = 64
CMP_PAD = 120
VT_ROWS = 80


def _dot(a, b, prec=None):
    return jnp.dot(a, b, preferred_element_type=F32, precision=prec)


def _dot_nt(a, b, prec=None):
    return lax.dot_general(a, b, (((1,), (1,)), ((), ())), preferred_element_type=F32, precision=prec)


def _dot_tn(a, b, prec=None):
    return lax.dot_general(a, b, (((0,), (0,)), ((), ())), preferred_element_type=F32, precision=prec)


def _sigmoid(x):
    return 1.0 / (1.0 + jnp.exp(-x))


def _layer_norm(y, g, b):
    mu = jnp.mean(y, axis=-1, keepdims=True)
    yc = y - mu
    var = jnp.mean(yc * yc, axis=-1, keepdims=True)
    return yc * lax.rsqrt(var + LN_EPS) * g + b


def _params(sem):
    return pltpu.CompilerParams(dimension_semantics=sem, vmem_limit_bytes=V7X_VMEM_LIMIT_BYTES)


def _const_spec(shape, single_buffer=False):
    nd = len(shape)
    if single_buffer:
        return pl.BlockSpec(shape, lambda *_: (0,) * nd, pipeline_mode=pl.Buffered(1))
    return pl.BlockSpec(shape, lambda *_: (0,) * nd)


def _bucket_thresholds():
    n = np.arange(0, 4 * MAX_DISTANCE)
    max_exact = NUM_BUCKETS // 2
    nf = np.maximum(n, max_exact).astype(np.float32)
    large = max_exact + (np.log(nf / np.float32(max_exact)) / np.float32(math.log(MAX_DISTANCE / max_exact))
                         * np.float32(NUM_BUCKETS - max_exact)).astype(np.int32)
    large = np.minimum(large, NUM_BUCKETS - 1)
    bucket = np.where(n < max_exact, n, large)
    return [int(np.argmax(bucket >= b)) for b in range(1, NUM_BUCKETS)]


_BUCKET_THR = _bucket_thresholds()


TZ_CMP_NEAR, TZ_DIAG, TZ_SUB, TZ_ZERO, TZ_MASKED, TZ_WIN_OLD, TZ_KINDS = 0, 1, 2, 3, 4, 5, 6


def _bias_tiles_kernel(rb_ref, o_ref):
    h = pl.program_id(0)
    r = lax.broadcasted_iota(jnp.int32, (Q_BLOCK, Q_BLOCK), 0)
    c = lax.broadcasted_iota(jnp.int32, (Q_BLOCK, Q_BLOCK), 1)
    far = jnp.full((Q_BLOCK, Q_BLOCK), rb_ref[NUM_BUCKETS - 1, h], F32)
    far_hi = far.astype(BF16).astype(F32)
    far_added = far_hi + (far - far_hi).astype(BF16).astype(F32)
    dists = (c - CMP_STRIDE * (r - CMP_PAD) - (CMP_LEN - 1), c - r, Q_BLOCK + c - r)
    for kind, dist in enumerate(dists):
        val = jnp.full((Q_BLOCK, Q_BLOCK), rb_ref[0, h], F32)
        for b in range(1, NUM_BUCKETS):
            val = jnp.where(dist >= _BUCKET_THR[b - 1], rb_ref[b, h], val)
        o_ref[0, kind] = jnp.where(dist >= 0, val - far_added, NEG)
    o_ref[0, TZ_ZERO] = far - far_added
    o_ref[0, TZ_MASKED] = jnp.full((Q_BLOCK, Q_BLOCK), NEG, F32)
    o_ref[0, TZ_WIN_OLD] = jnp.where(r > c, far - far_added, NEG)


def _bias_tiles(rel_bias):
    return pl.pallas_call(
        _bias_tiles_kernel,
        grid=(NSA_HEADS,),
        in_specs=[pl.BlockSpec(memory_space=pltpu.SMEM)],
        out_specs=pl.BlockSpec((1, TZ_KINDS, Q_BLOCK, Q_BLOCK), lambda h: (h, 0, 0, 0)),
        out_shape=jax.ShapeDtypeStruct((NSA_HEADS, TZ_KINDS, Q_BLOCK, Q_BLOCK), F32),
        compiler_params=_params(("arbitrary",)),
        name="bias_tiles",
    )(rel_bias)


FFN_TM = 1024
FFN_TF = 256


def _ffn_kernel(x_ref, wg_ref, wu_ref, wd_ref, g_ref, b_ref, o_ref, acc_ref):
    x = x_ref[...]
    xb = x.astype(BF16)
    for c in range(wg_ref.shape[1] // FFN_TF):
        cols = slice(c * FFN_TF, (c + 1) * FFN_TF)
        hg = _dot(xb, wg_ref[:, cols])
        hu = _dot(xb, wu_ref[:, cols])
        h = hg * _sigmoid(hg) * hu
        part = _dot(h.astype(BF16), wd_ref[cols, :])
        if c == 0:
            acc_ref[...] = part
        else:
            acc_ref[...] += part
    o_ref[...] = _layer_norm(ALPHA * x + 0.5 * acc_ref[...], g_ref[...], b_ref[...])


def _ffn_ln(x2d, wg, wu, wd, g, b):
    rows, d = x2d.shape
    f = wg.shape[1]
    assert f % FFN_TF == 0
    tm = min(FFN_TM, rows)
    return pl.pallas_call(
        _ffn_kernel,
        grid=(rows // tm,),
        in_specs=[
            pl.BlockSpec((tm, d), lambda i: (i, 0)),
            _const_spec((d, f), single_buffer=True),
            _const_spec((d, f), single_buffer=True),
            _const_spec((f, d), single_buffer=True),
            _const_spec((1, d)),
            _const_spec((1, d)),
        ],
        out_specs=pl.BlockSpec((tm, d), lambda i: (i, 0)),
        out_shape=jax.ShapeDtypeStruct((rows, d), F32),
        scratch_shapes=[pltpu.VMEM((tm, d), F32)],
        compiler_params=_params(("parallel",)),
        name="ffn_ln",
    )(x2d, wg.astype(BF16), wu.astype(BF16), wd.astype(BF16), g.reshape(1, d), b.reshape(1, d))


PROJ_TM = 512
_C_RWKV = 0
_C_Q = _C_RWKV + RWKV_IN
_C_KCVC = _C_Q + NSA_DIM
_C_K = _C_KCVC + 256
_C_V = _C_K + 256
_C_GATE = _C_V + 256
_C_END = _C_GATE + 128
BIAS_LANES = (HEAD_DIM, HEAD_DIM + 1)
MASK_LANE = HEAD_DIM + 2


def _inproj_kernel(x_ref, w_ref, qx_ref, zr_ref, q_ref, kcvc_ref, k_ref, v_ref, gate_ref):
    xb = x_ref[0].astype(BF16)
    lane = lax.broadcasted_iota(jnp.int32, (1, V7X_LANES), 1)
    zr_ref[0] = _dot(xb, w_ref[:, _C_RWKV:_C_Q])
    low = lane < HEAD_DIM

    def padded_pair(tile, pad_a, pad_b):
        return jnp.where(low, tile, pad_a), jnp.where(low, pltpu.roll(tile, HEAD_DIM, 1), pad_b)

    zq = _dot(xb, w_ref[:, _C_Q:_C_KCVC]) * (HEAD_DIM ** -0.5)
    for j in range(NSA_HEADS // 2):
        qa, qb = padded_pair(zq[:, j * 128:(j + 1) * 128], qx_ref[2 * j:2 * j + 1, :], qx_ref[2 * j + 1:2 * j + 2, :])
        q_ref[0, 2 * j] = qa.astype(BF16)
        q_ref[0, 2 * j + 1] = qb.astype(BF16)
    zc = _dot(xb, w_ref[:, _C_KCVC:_C_K])
    kcvc_ref[0, 0] = zc[:, 0:128]
    kcvc_ref[0, 1] = zc[:, 128:256]
    bias_ones = ((lane == BIAS_LANES[0]) | (lane == BIAS_LANES[1])).astype(F32)
    zk = _dot(xb, w_ref[:, _C_K:_C_V])
    for j in range(2):
        ka, kb = padded_pair(zk[:, j * 128:(j + 1) * 128], bias_ones, bias_ones)
        k_ref[0, 2 * j] = ka.astype(BF16)
        k_ref[0, 2 * j + 1] = kb.astype(BF16)
    tail_rows = VT_ROWS - HEAD_DIM
    tail = (lax.broadcasted_iota(jnp.int32, (tail_rows, Q_BLOCK), 0) == 0).astype(F32)
    zvg = _dot(xb, w_ref[:, _C_V:_C_END])
    for j in range(2):
        zv = zvg[:, j * 128:(j + 1) * 128]
        for i in range(zv.shape[0] // Q_BLOCK):
            zt = zv[i * Q_BLOCK:(i + 1) * Q_BLOCK].T
            for g in range(NSA_GROUPS):
                v_ref[0, 2 * j + g, i] = jnp.concatenate([zt[g * HEAD_DIM:(g + 1) * HEAD_DIM], tail], axis=0).astype(BF16)
    gate_ref[0] = zvg[:, 2 * 128:3 * 128]


def _pack_w_in(w_in):
    d = w_in.shape[0]
    o = RWKV_IN
    q = w_in[:, o:o + 512]
    kc = w_in[:, o + 512:o + 640]
    vc = w_in[:, o + 640:o + 768]
    ks = w_in[:, o + 768:o + 896]
    vs = w_in[:, o + 896:o + 1024]
    kw = w_in[:, o + 1024:o + 1152]
    vw = w_in[:, o + 1152:o + 1280]
    gate = w_in[:, o + 1280:o + 1304]
    gate_pad = jnp.concatenate([gate, jnp.zeros((d, 128 - gate.shape[1]), w_in.dtype)], axis=1)
    return jnp.concatenate([w_in[:, :o], q, kc, vc, ks, kw, vs, vw, gate_pad], axis=1).astype(BF16)


def _far_bias_lanes(rel_bias):
    far = rel_bias[NUM_BUCKETS - 1, :]
    hi = far.astype(BF16).astype(F32)
    lane = jnp.arange(V7X_LANES)[None, :]
    return jnp.where(lane == BIAS_LANES[0], hi[:, None],
                     jnp.where(lane == BIAS_LANES[1], (far - hi)[:, None], jnp.where(lane == MASK_LANE, 1.0, 0.0)))


def _in_projection(x3d, w_in, rel_bias):
    bsz, s, d = x3d.shape
    tm = min(PROJ_TM, s)
    wp = _pack_w_in(w_in)
    out_shape = (
        jax.ShapeDtypeStruct((bsz, s, RWKV_IN), F32),
        jax.ShapeDtypeStruct((bsz, NSA_HEADS, s, 128), BF16),
        jax.ShapeDtypeStruct((bsz, 2, s, 128), F32),
        jax.ShapeDtypeStruct((bsz, 4, s, 128), BF16),
        jax.ShapeDtypeStruct((bsz, 4, s // Q_BLOCK, VT_ROWS, Q_BLOCK), BF16),
        jax.ShapeDtypeStruct((bsz, s, 128), F32),
    )
    return pl.pallas_call(
        _inproj_kernel,
        grid=(bsz, s // tm),
        in_specs=[pl.BlockSpec((1, tm, d), lambda b, i: (b, i, 0)), _const_spec((d, _C_END)),
                  _const_spec((NSA_HEADS, 128))],
        out_specs=(
            pl.BlockSpec((1, tm, RWKV_IN), lambda b, i: (b, i, 0)),
            pl.BlockSpec((1, NSA_HEADS, tm, 128), lambda b, i: (b, 0, i, 0)),
            pl.BlockSpec((1, 2, tm, 128), lambda b, i: (b, 0, i, 0)),
            pl.BlockSpec((1, 4, tm, 128), lambda b, i: (b, 0, i, 0)),
            pl.BlockSpec((1, 4, tm // Q_BLOCK, VT_ROWS, Q_BLOCK), lambda b, i: (b, 0, i, 0, 0)),
            pl.BlockSpec((1, tm, 128), lambda b, i: (b, i, 0)),
        ),
        out_shape=out_shape,
        compiler_params=_params(("parallel", "parallel")),
        name="in_projection",
    )(x3d, wp, _far_bias_lanes(rel_bias))


RWKV_T = 512
RWKV_CHUNKS_PER_STEP = 8


def _rwkv_consts():
    r = lax.broadcasted_iota(jnp.int32, (128, 128), 0)
    c = lax.broadcasted_iota(jnp.int32, (128, 128), 1)
    same = (r >= CHUNK) == (c >= CHUNK)
    mask_sl = (same & (r > c)).astype(F32)
    mask_l = (same & (r >= c)).astype(F32)
    eye = (r == c).astype(F32)
    rt = lax.broadcasted_iota(jnp.int32, (CHUNK, CHUNK), 0)
    ct = lax.broadcasted_iota(jnp.int32, (CHUNK, CHUNK), 1)
    tri = (rt >= ct).astype(F32)
    lane = lax.broadcasted_iota(jnp.int32, (1, 128), 1)
    m0 = (lane < CHUNK).astype(F32)
    m1 = 1.0 - m0
    return mask_sl, mask_l, eye, tri, m0, m1


def _split2(x):
    hi = x.astype(BF16)
    return hi, (x - hi.astype(F32)).astype(BF16)


def _mm(a, b, mode, dot=_dot):
    if mode == "bf16":
        return dot(a.astype(BF16), b.astype(BF16))
    if mode == "bf16x3":
        ah, al = _split2(a)
        bh, bl = _split2(b)
        return dot(ah, bh) + (dot(ah, bl) + dot(al, bh))
    if mode == "ones_b":
        ah, al = _split2(a)
        bb = b.astype(BF16)
        return dot(ah, bb) + dot(al, bb)
    if mode in ("exact_a", "exact_b"):
        x = b if mode == "exact_a" else a
        hi = x.astype(BF16)
        rest = x - hi.astype(F32)
        mid = rest.astype(BF16)
        lo = (rest - mid.astype(F32)).astype(BF16)
        if mode == "exact_a":
            ab = a.astype(BF16)
            return dot(ab, hi) + (dot(ab, mid) + dot(ab, lo))
        bb = b.astype(BF16)
        return dot(hi, bb) + (dot(mid, bb) + dot(lo, bb))
    raise ValueError(mode)


RWKV_MODES = dict(p="bf16", inv="bf16", av="bf16", wu="bf16", qy="bf16", mn="bf16", y="bf16", h="bf16")


def _rwkv_chunk(rs, lws, ks, vs, kks, as_, hs, consts):
    mask_sl, mask_l, eye, tri, m0, m1 = consts
    md = RWKV_MODES
    n = len(rs)
    each = range(n)

    def sm(x):
        return jnp.concatenate([x * m0, x * m1], axis=0)

    def dup(x):
        return jnp.concatenate([x, x], axis=0)

    cums = [_mm(tri, lws[i], "exact_a") for i in each]
    a_sm, r_sm, v_sm, kb, kbh, g_c = [], [], [], [], [], []
    for i in each:
        cum, lw, kk, k = cums[i], lws[i], kks[i], ks[i]
        cl = cum[CHUNK - 1:CHUNK, :]
        ka = kk * as_[i]
        g_tail = jnp.exp(cl - cum)
        g_inv = jnp.exp(-cum)
        a_sm.append(sm(-(kk * jnp.exp(cum - lw))))
        r_sm.append(sm(rs[i] * jnp.exp(cum)))
        v_sm.append(sm(vs[i]))
        kb.append(jnp.concatenate([dup(k * g_inv), dup(ka * g_inv)], axis=0))
        kbh.append(jnp.concatenate([sm(k * g_tail), sm(ka * g_tail)], axis=0))
        g_c.append(jnp.exp(cl))
    pm = [_mm(jnp.concatenate([a_sm[i], r_sm[i]], axis=0), kb[i], md["p"], _dot_nt) for i in each]
    a_ak = [pm[i][0:128, 0:128] * mask_sl for i in each]
    a_rr = [jnp.concatenate([pm[i][128:256, 0:128] * mask_l, pm[i][128:256, 128:256] * mask_l], axis=1) for i in each]
    x = [pm[i][0:128, 128:256] * mask_sl for i in each]
    t_inv = [eye + x[i] for i in each]
    for _ in range(5):
        x = [_mm(x[i], x[i], md["inv"]) for i in each]
        t_inv = [t_inv[i] + _mm(t_inv[i], x[i], md["inv"]) for i in each]
    av = [_mm(a_ak[i], v_sm[i], md["av"]) for i in each]
    wu = [_mm(t_inv[i], jnp.concatenate([a_sm[i], av[i]], axis=1), md["wu"]) for i in each]
    z = [jnp.concatenate([jnp.concatenate([jnp.zeros_like(v_sm[i]), v_sm[i]], axis=1), wu[i]], axis=0) for i in each]
    qy = [_mm(a_rr[i], z[i], md["qy"]) for i in each]
    mn = [_mm(kbh[i], z[i], md["mn"], _dot_tn) for i in each]
    n_pairs = len(hs)
    ys = []
    for c0 in range(0, n, n_pairs):
        idx = range(c0, c0 + n_pairs)
        y_sm = [_mm(r_sm[i] + qy[i][:, 0:128], hs[i - c0], md["y"]) + qy[i][:, 128:256] for i in idx]
        hs = [_mm(mn[i][:, 0:128] + eye * g_c[i], hs[i - c0], md["h"]) + mn[i][:, 128:256] for i in idx]
        ys += [y[0:CHUNK] + y[CHUNK:2 * CHUNK] for y in y_sm]
    return ys, hs


def _rwkv_kernel(z_ref, mu_ref, w0_ref, wup_ref, a0_ref, aup_ref, gup_ref, kk_ref, ka_ref, rk_ref,
                 gng_ref, gnb_ref, o_ref, prev_ref, h_ref, r_s, lw_s, k_s, v_s, kk_s, a_s, y_s, g_s, *, t):
    ti = pl.program_id(1)

    @pl.when(ti == 0)
    def _():
        prev_ref[...] = jnp.zeros_like(prev_ref)
        h_ref[...] = jnp.zeros_like(h_ref)

    consts = _rwkv_consts()

    z = z_ref[0]
    row = lax.broadcasted_iota(jnp.int32, (t, 1), 0)
    prev = jnp.where(row == 0, prev_ref[0:1, :], pltpu.roll(z, 1, 0))
    prev_ref[0:1, :] = z[t - 1:t, :]
    zs = z + (prev - z) * mu_ref[...]

    r = zs[:, 0:512]
    k = zs[:, 512:1024]
    v = zs[:, 1024:1536]
    wa = zs[:, 1536:1664]
    gl = zs[:, 1664:1792]
    u = w0_ref[...] + _dot(jnp.tanh(wa).astype(BF16), wup_ref[...])
    lw = (-math.exp(-0.5)) * _sigmoid(u)
    a = _sigmoid(a0_ref[...] + _dot(wa.astype(BF16), aup_ref[...]))
    g_s[...] = _dot(_sigmoid(gl).astype(BF16), gup_ref[...])
    kkr = k * kk_ref[...]
    k2 = k * (1.0 + (a - 1.0) * ka_ref[...])
    r_s[...] = r
    lw_s[...] = lw
    k_s[...] = k2
    v_s[...] = v
    a_s[...] = a
    hr = lax.broadcasted_iota(jnp.int32, (256, 256), 0) // HEAD_DIM
    hc = lax.broadcasted_iota(jnp.int32, (256, 256), 1) // HEAD_DIM
    head_ones = (hr == hc).astype(F32)
    halves = [slice(i * 256, (i + 1) * 256) for i in range(2)]
    for sl in halves:
        kp = kkr[:, sl]
        ss = _mm(kp * kp, head_ones, "ones_b")
        kk_s[:, sl] = kp * lax.rsqrt(jnp.maximum(ss, 1e-24))

    pairs = [slice(pr * 128, (pr + 1) * 128) for pr in range(4)]

    def chunk_body(c, carry):
        rows = [pl.ds(pl.multiple_of((c * RWKV_CHUNKS_PER_STEP + i) * CHUNK, CHUNK), CHUNK)
                for i in range(RWKV_CHUNKS_PER_STEP)]
        ys, hs = _rwkv_chunk(*[[ref[rw, sl] for rw in rows for sl in pairs]
                               for ref in (r_s, lw_s, k_s, v_s, kk_s, a_s)],
                             [h_ref[pr] for pr in range(4)], consts)
        for i, rw in enumerate(rows):
            for pr, sl in enumerate(pairs):
                y_s[rw, sl] = ys[i * 4 + pr]
        for pr in range(4):
            h_ref[pr] = hs[pr]
        return carry

    lax.fori_loop(0, t // (CHUNK * RWKV_CHUNKS_PER_STEP), chunk_body, 0)

    for sl in halves:
        y = y_s[:, sl]
        mean = _mm(y, head_ones, "ones_b") * (1.0 / HEAD_DIM)
        yc = y - mean
        var = _mm(yc * yc, head_ones, "ones_b") * (1.0 / HEAD_DIM)
        yn = yc * lax.rsqrt(var + RWKV_GN_EPS) * gng_ref[:, sl] + gnb_ref[:, sl]
        rp = r_s[:, sl]
        bonus = _mm(rp * k_s[:, sl] * rk_ref[:, sl], head_ones, "ones_b") * v_s[:, sl]
        o_ref[0, :, sl] = ((yn + bonus) * g_s[:, sl]).astype(o_ref.dtype)


def _rwkv_group(zr, mu, w0, w_up, a0, a_up, g_up, k_k, k_a, r_k, gn_g, gn_b):
    bsz, s, _ = zr.shape
    t = min(RWKV_T, s)
    wup_pad = jnp.concatenate([w_up, jnp.zeros_like(a_up)], axis=0).astype(BF16)
    aup_pad = jnp.concatenate([jnp.zeros_like(w_up), a_up], axis=0).astype(BF16)
    row = lambda p: p.reshape(1, -1)
    tile = pltpu.VMEM((t, RWKV_DIM), F32)
    return pl.pallas_call(
        functools.partial(_rwkv_kernel, t=t),
        grid=(bsz, s // t),
        in_specs=[
            pl.BlockSpec((1, t, RWKV_IN), lambda b, i: (b, i, 0)),
            _const_spec((1, RWKV_IN)), _const_spec((1, RWKV_DIM)), _const_spec((128, RWKV_DIM)),
            _const_spec((1, RWKV_DIM)), _const_spec((128, RWKV_DIM)), _const_spec((GATE_LORA, RWKV_DIM)),
            _const_spec((1, RWKV_DIM)), _const_spec((1, RWKV_DIM)), _const_spec((1, RWKV_DIM)),
            _const_spec((1, RWKV_DIM)), _const_spec((1, RWKV_DIM)),
        ],
        out_specs=pl.BlockSpec((1, t, RWKV_DIM), lambda b, i: (b, i, 0)),
        out_shape=jax.ShapeDtypeStruct((bsz, s, RWKV_DIM), BF16),
        scratch_shapes=[pltpu.VMEM((V7X_SUBLANES, RWKV_IN), F32), pltpu.VMEM((4, 128, 128), F32),
                        tile, tile, tile, tile, tile, tile, tile, tile],
        compiler_params=_params(("parallel", "arbitrary")),
        name="rwkv7",
    )(zr, row(mu), row(w0), wup_pad, row(a0), aup_pad, g_up.astype(BF16), row(k_k), row(k_a), row(r_k),
      row(gn_g), row(gn_b))


def _gelu_tanh(x):
    return 0.5 * x * (1.0 + jnp.tanh(math.sqrt(2.0 / math.pi) * (x + 0.044715 * (x * x * x))))


def _compress_kernel(x_ref, pe_ref, w1_ref, w2_ref, kc_ref, vc_ref, *, n_rows):
    half = CMP_LEN // 2
    for kv, o_ref in enumerate((kc_ref, vc_ref)):
        first = jnp.zeros((n_rows, 2 * CMP_HIDDEN), F32)
        second = jnp.zeros((n_rows, 2 * CMP_HIDDEN), F32)
        for p in range(half):
            xa = x_ref[0, kv, pl.ds(p, n_rows, stride=CMP_STRIDE), :]
            first += _dot((xa + pe_ref[kv, p:p + 1, :]).astype(BF16), w1_ref[kv, p])
            second += _dot((xa + pe_ref[kv, half + p:half + p + 1, :]).astype(BF16), w1_ref[kv, half + p])
        pre = first + pltpu.roll(second, n_rows - 1, 0)
        hid = _gelu_tanh(pre).astype(BF16)
        width = o_ref.shape[3]
        if kv == 0:
            lane = lax.broadcasted_iota(jnp.int32, (1, V7X_LANES), 1)
            bias_ones = ((lane == BIAS_LANES[0]) | (lane == BIAS_LANES[1])).astype(F32)
            outs = [_dot(hid, w2_ref[g]) + bias_ones for g in range(NSA_GROUPS)]
        else:
            out = _dot(hid, w2_ref[NSA_GROUPS])
            outs = [out[:, g * HEAD_DIM:(g + 1) * HEAD_DIM] for g in range(NSA_GROUPS)]
        for g in range(NSA_GROUPS):
            o_ref[0, g, 0:CMP_PAD, :] = jnp.zeros((CMP_PAD, width), o_ref.dtype)
            o_ref[0, g, CMP_PAD:CMP_PAD + n_rows, :] = outs[g].astype(o_ref.dtype)
            tail = o_ref.shape[2] - CMP_PAD - n_rows
            o_ref[0, g, CMP_PAD + n_rows:, :] = jnp.zeros((tail, width), o_ref.dtype)


def _pair_diag(w):
    z = jnp.zeros_like(w)
    return jnp.concatenate([jnp.concatenate([w, z], axis=-1), jnp.concatenate([z, w], axis=-1)], axis=-2)


def _compress(kcvc, pe_k, w1_k, w2_k, pe_v, w1_v, w2_v):
    bsz, _, s, _ = kcvc.shape
    n_rows = s // CMP_STRIDE
    rows_out = CMP_PAD + n_rows + V7X_SUBLANES
    pe = jnp.stack([jnp.concatenate([pe_k, pe_k], axis=1), jnp.concatenate([pe_v, pe_v], axis=1)])
    w1 = jnp.stack([_pair_diag(w1_k.reshape(CMP_LEN, HEAD_DIM, CMP_HIDDEN)),
                    _pair_diag(w1_v.reshape(CMP_LEN, HEAD_DIM, CMP_HIDDEN))]).astype(BF16)
    zk = jnp.zeros_like(w2_k)
    w2_pad = lambda g: jnp.concatenate([jnp.concatenate([w2_k if i == g else zk, zk], axis=1)
                                        for i in range(NSA_GROUPS)], axis=0)
    w2 = jnp.stack([w2_pad(0), w2_pad(1), _pair_diag(w2_v)]).astype(BF16)
    shapes = tuple(jax.ShapeDtypeStruct((bsz, NSA_GROUPS, rows_out, w), BF16) for w in (128, HEAD_DIM))
    specs = tuple(pl.BlockSpec((1, NSA_GROUPS, rows_out, w), lambda b: (b, 0, 0, 0)) for w in (128, HEAD_DIM))
    return pl.pallas_call(
        functools.partial(_compress_kernel, n_rows=n_rows),
        grid=(bsz,),
        in_specs=[pl.BlockSpec((1, 2, s, 128), lambda b: (b, 0, 0, 0)), _const_spec(pe.shape), _const_spec(w1.shape),
                  _const_spec(w2.shape)],
        out_specs=specs,
        out_shape=shapes,
        compiler_params=_params(("parallel",)),
        name="nsa_compress",
    )(kcvc, pe, w1, w2)


CMP_QBLOCKS = 8


def _cmp_select_kernel(q_ref, kc_ref, vc_ref, tz_ref, gate_ref, oc_ref, sel_ref, *, n_far):
    for sub in range(CMP_QBLOCKS):
        _cmp_select_block(pl.program_id(1) * CMP_QBLOCKS + sub, slice(sub * Q_BLOCK, (sub + 1) * Q_BLOCK),
                          q_ref, kc_ref, vc_ref, tz_ref, gate_ref, oc_ref, sel_ref, n_far)


def _cmp_select_block(qi, qrows, q_ref, kc_ref, vc_ref, tz_ref, gate_ref, oc_ref, sel_ref, n_far):
    qb = Q_BLOCK
    rowi = lax.broadcasted_iota(jnp.int32, (qb, 128), 0)
    lane = lax.broadcasted_iota(jnp.int32, (qb, 128), 1)
    row_f = rowi.astype(F32)
    sg = _sigmoid(gate_ref[0, qrows, :])
    near0 = pl.multiple_of(qi * 8, 8)
    n_first = qi * 8 - CMP_PAD
    for g in range(NSA_GROUPS):
        q4 = q_ref[0, g * NSA_HPG:(g + 1) * NSA_HPG, qrows, :].reshape(NSA_HPG * qb, 128)
        heads = [g * NSA_HPG + p for p in range(NSA_HPG)]
        cols = [slice(p * qb, (p + 1) * qb) for p in range(NSA_HPG)]

        def masked_keys(kt, ok):
            return jnp.where((lane == MASK_LANE) & jnp.logical_not(ok), jnp.asarray(NEG, BF16), kt)

        tiles, values, n_of_lane = [], [], []
        for j in range(n_far):
            rows = slice(CMP_PAD + 128 * j, CMP_PAD + 128 * (j + 1))
            tiles.append(_dot_nt(masked_keys(kc_ref[0, g, rows, :], (128 * j + rowi) < n_first), q4))
            values.append(vc_ref[0, g, rows, :])
            n_of_lane.append(128 * j + lane)
        s = _dot_nt(masked_keys(kc_ref[0, g, pl.ds(near0, 128), :], (n_first + rowi) >= 0), q4)
        tiles.append(jnp.concatenate([s[:, cols[p]] + tz_ref[heads[p], TZ_CMP_NEAR] for p in range(NSA_HPG)], axis=1))
        values.append(vc_ref[0, g, pl.ds(near0, 128), :])
        n_of_lane.append(n_first + lane)

        m = tiles[0].max(axis=0, keepdims=True)
        for tl in tiles[1:]:
            m = jnp.maximum(m, tl.max(axis=0, keepdims=True))
        es = [jnp.exp(tl - m) for tl in tiles]
        den = es[0].sum(axis=0, keepdims=True)
        for e in es[1:]:
            den = den + e.sum(axis=0, keepdims=True)
        inv = jnp.where(m > 0.5 * NEG, 1.0 / jnp.maximum(den, 1e-30), 0.0)
        o_t = jnp.zeros((HEAD_DIM, NSA_HPG * qb), F32)
        imp = jnp.zeros((qb, 128), F32)
        for j, e in enumerate(es):
            pc = e * inv
            o_t = o_t + _dot_tn(values[j], pc.astype(BF16))
            psum = pc[:, cols[0]] + pc[:, cols[1]] + pc[:, cols[2]] + pc[:, cols[3]]
            ov = ((n_of_lane[j] >= 4 * rowi - 1) & (n_of_lane[j] <= 4 * rowi + 3)).astype(F32)
            imp = imp + _mm(ov, psum, "exact_a")
        for p, h in enumerate(heads):
            oc_ref[0, qrows, h * HEAD_DIM:(h + 1) * HEAD_DIM] = o_t[:, cols[p]].T * sg[:, 3 * h:3 * h + 1]
        cur = 2 * qi + (lane >= SLC_LEN).astype(jnp.int32)
        forced = (rowi == 0) | (rowi == cur) | (rowi == cur - 1)
        score = jnp.where(rowi <= cur, jnp.where(forced, -3e38, imp), -1.0)
        sel = jnp.where(forced & (rowi <= cur), 1.0, 0.0)
        for _ in range(N_SEL - 3):
            mx = score.max(axis=0, keepdims=True)
            idx = jnp.where(score == mx, row_f, 128.0).min(axis=0, keepdims=True)
            pick = row_f == idx
            sel = jnp.where(pick, 1.0, sel)
            score = jnp.where(pick, -3e38, score)
        sel_ref[0, g, qrows, :] = jnp.where(sel.T > 0.5, 0.0, NEG).astype(sel_ref.dtype)


def _cmp_select(q, kc, vc, tz, gate):
    bsz, _, s, _ = q.shape
    nq = s // Q_BLOCK
    n_far = max(0, -(-(8 * (nq - 1) - CMP_PAD) // 128))
    rows_c = kc.shape[2]
    assert nq % CMP_QBLOCKS == 0
    tq = CMP_QBLOCKS * Q_BLOCK
    return pl.pallas_call(
        functools.partial(_cmp_select_kernel, n_far=n_far),
        grid=(bsz, nq // CMP_QBLOCKS),
        in_specs=[
            pl.BlockSpec((1, NSA_HEADS, tq, 128), lambda b, i: (b, 0, i, 0)),
            pl.BlockSpec((1, NSA_GROUPS, rows_c, 128), lambda b, i: (b, 0, 0, 0)),
            pl.BlockSpec((1, NSA_GROUPS, rows_c, HEAD_DIM), lambda b, i: (b, 0, 0, 0)),
            _const_spec(tz.shape),
            pl.BlockSpec((1, tq, 128), lambda b, i: (b, i, 0)),
        ],
        out_specs=(
            pl.BlockSpec((1, tq, NSA_DIM), lambda b, i: (b, i, 0)),
            pl.BlockSpec((1, NSA_GROUPS, tq, 128), lambda b, i: (b, 0, i, 0)),
        ),
        out_shape=(jax.ShapeDtypeStruct((bsz, s, NSA_DIM), F32),
                   jax.ShapeDtypeStruct((bsz, NSA_GROUPS, s, 128), BF16)),
        compiler_params=_params(("parallel", "parallel")),
        name="nsa_cmp_select",
    )(q, kc, vc, tz, gate)


SEL_QBLOCKS = 4
SEL_TILES = 2


def _attn_scores(qa, ka, bias_fns, m_prev):
    qb = Q_BLOCK
    s = _dot_nt(ka, qa)
    if bias_fns is not None:
        s = jnp.concatenate(
            [jnp.concatenate([s[i * qb:(i + 1) * qb, p * qb:(p + 1) * qb] + fn(p) for p in range(NSA_HPG)], axis=1)
             for i, fn in enumerate(bias_fns)], axis=0)
    return s, jnp.maximum(m_prev, s.max(axis=0, keepdims=True))


def _attn_accumulate(s, vts, m_prev, m_next, acc_ref):
    pexp = jnp.exp(s - m_next)
    acc_ref[...] = jnp.exp(m_prev - m_next) * acc_ref[...] + _dot(jnp.concatenate(vts, axis=1), pexp.astype(BF16))


def _sel_win_kernel(q_ref, k_ref, vt_ref, e_ref, sel_ref, tz_ref, gate_ref, oc_ref, o_ref, m_ref, acc_ref,
                    s0_ref, s1_ref, mm0_ref, mm1_ref):
    qb = Q_BLOCK
    win_tiles = WINDOW // qb
    chains = [(sub, g) for sub in range(SEL_QBLOCKS) for g in range(NSA_GROUPS)]
    n_chains = len(chains)
    qis = [pl.program_id(1) * SEL_QBLOCKS + sub for sub in range(SEL_QBLOCKS)]
    m_ref[...] = jnp.full(m_ref.shape, NEG, F32)
    acc_ref[...] = jnp.zeros(acc_ref.shape, F32)
    qrows = [slice(sub * qb, (sub + 1) * qb) for sub in range(SEL_QBLOCKS)]
    q4 = [q_ref[0, g * NSA_HPG:(g + 1) * NSA_HPG, qrows[sub], :].reshape(NSA_HPG * qb, 128) for sub, g in chains]
    selm = [jnp.concatenate([sel_ref[0, g, qrows[sub], :]] * NSA_HPG, axis=0) for sub, g in chains]

    def key_rows(kt, n=1):
        return pl.ds(pl.multiple_of(kt * qb, qb), n * qb)

    def bias_of(g, kind):
        return lambda p: tz_ref[g * NSA_HPG + p, kind]

    def keys_sel(g, kt, n=1):
        return jnp.concatenate([k_ref[0, g, key_rows(kt, n), :], e_ref[key_rows(kt, n), :]], axis=1)

    block = lax.broadcasted_iota(jnp.int32, (NSA_HPG * qb, 128), 1)

    n_far = [(jnp.maximum(qi - 1, 0) + SEL_TILES - 1) // SEL_TILES for qi in qis]
    n_run = n_far[-1] | 1
    last_tile0 = k_ref.shape[2] // qb - SEL_TILES

    def tile0(step):
        return jnp.minimum(step * SEL_TILES, last_tile0)

    def score_far(step, s_ref, mm_ref):
        for c, (sub, g) in enumerate(chains):
            hidden = (block >= 2 * (qis[sub] - 1)) | (step >= n_far[sub])
            qa = jnp.concatenate([q4[c], jnp.where(hidden, jnp.asarray(NEG, BF16), selm[c])], axis=1)
            m_prev = m_ref[c]
            s, m_next = _attn_scores(qa, keys_sel(g, tile0(step), SEL_TILES), None, m_prev)
            s_ref[c] = s
            mm_ref[c, 0:1, :] = m_prev
            mm_ref[c, 1:2, :] = m_next
            m_ref[c] = m_next

    def accumulate_far(step, s_ref, mm_ref):
        for c, (sub, g) in enumerate(chains):
            _attn_accumulate(s_ref[c], [vt_ref[0, g, tile0(step) + i] for i in range(SEL_TILES)],
                             mm_ref[c, 0:1, :], mm_ref[c, 1:2, :], acc_ref.at[c])

    score_far(0, s0_ref, mm0_ref)

    def far_body(i, carry):
        score_far(2 * i + 1, s1_ref, mm1_ref)
        accumulate_far(2 * i, s0_ref, mm0_ref)
        score_far(2 * i + 2, s0_ref, mm0_ref)
        accumulate_far(2 * i + 1, s1_ref, mm1_ref)
        return carry

    lax.fori_loop(0, n_run // 2, far_body, 0)

    scored = []
    for c, (sub, g) in enumerate(chains):
        qi = qis[sub]
        kt_sub = jnp.maximum(qi - 1, 0)
        kinds = (jnp.where(qi >= 1, TZ_SUB, TZ_MASKED), TZ_DIAG)
        m_prev = m_ref[c]
        s, m_next = _attn_scores(jnp.concatenate([q4[c], selm[c]], axis=1),
                                 jnp.concatenate([keys_sel(g, kt_sub), keys_sel(g, qi)], axis=0),
                                 [bias_of(g, kind) for kind in kinds], m_prev)
        scored.append((c, s, [vt_ref[0, g, kt_sub], vt_ref[0, g, qi]], m_prev, m_next))
    for c, (sub, g) in enumerate(chains):
        qi = qis[sub]
        kw = NSA_GROUPS + g
        ks, vts, biases = [], [], []
        for d in range(win_tiles, -1, -1):
            kt = jnp.maximum(qi - d, 0)
            kind = TZ_DIAG if d == 0 else TZ_SUB if d == 1 else TZ_WIN_OLD if d == win_tiles else TZ_ZERO
            if d > 0:
                kind = jnp.where(qi >= d, kind, TZ_MASKED)
            ks.append(k_ref[0, kw, key_rows(kt), :])
            vts.append(vt_ref[0, kw, kt])
            biases.append(bias_of(g, kind))
        m_prev = m_ref[n_chains + c]
        s, m_next = _attn_scores(q4[c], jnp.concatenate(ks, axis=0), biases, m_prev)
        scored.append((n_chains + c, s, vts, m_prev, m_next))
    accumulate_far(n_run - 1, s0_ref, mm0_ref)
    for stream, s, vts, m_prev, m_next in scored:
        _attn_accumulate(s, vts, m_prev, m_next, acc_ref.at[stream])

    for c, (sub, g) in enumerate(chains):
        sg = _sigmoid(gate_ref[0, qrows[sub], :])
        outs = []
        for stream in (c, n_chains + c):
            acc = acc_ref[stream]
            outs.append(acc[0:HEAD_DIM] * (1.0 / acc[HEAD_DIM:HEAD_DIM + 1]))
        for p in range(NSA_HPG):
            h = g * NSA_HPG + p
            qcols = slice(p * qb, (p + 1) * qb)
            cols = slice(h * HEAD_DIM, (h + 1) * HEAD_DIM)
            y = (oc_ref[0, qrows[sub], cols] + sg[:, 3 * h + 1:3 * h + 2] * outs[0][:, qcols].T
                 + sg[:, 3 * h + 2:3 * h + 3] * outs[1][:, qcols].T)
            o_ref[0, qrows[sub], cols] = y.astype(o_ref.dtype)


def _sel_win(q, k4, v4, sel, tz, gate, oc):
    bsz, _, s, _ = q.shape
    nq = s // Q_BLOCK
    assert nq % SEL_TILES == 0 and nq % SEL_QBLOCKS == 0
    member = (jnp.arange(s)[:, None] // SLC_LEN == jnp.arange(128)[None, :]).astype(BF16)
    tq = SEL_QBLOCKS * Q_BLOCK
    n_chains = SEL_QBLOCKS * NSA_GROUPS
    lanes = NSA_HPG * Q_BLOCK
    return pl.pallas_call(
        _sel_win_kernel,
        grid=(bsz, nq // SEL_QBLOCKS),
        in_specs=[
            pl.BlockSpec((1, NSA_HEADS, tq, 128), lambda b, i: (b, 0, i, 0)),
            pl.BlockSpec((1, 4, s, 128), lambda b, i: (b, 0, 0, 0), pipeline_mode=pl.Buffered(1)),
            pl.BlockSpec((1, 4, nq, VT_ROWS, Q_BLOCK), lambda b, i: (b, 0, 0, 0, 0), pipeline_mode=pl.Buffered(1)),
            _const_spec((s, 128), single_buffer=True),
            pl.BlockSpec((1, NSA_GROUPS, tq, 128), lambda b, i: (b, 0, i, 0)),
            _const_spec(tz.shape, single_buffer=True),
            pl.BlockSpec((1, tq, 128), lambda b, i: (b, i, 0)),
            pl.BlockSpec((1, tq, NSA_DIM), lambda b, i: (b, i, 0)),
        ],
        out_specs=pl.BlockSpec((1, tq, NSA_DIM), lambda b, i: (b, i, 0)),
        out_shape=jax.ShapeDtypeStruct((bsz, s, NSA_DIM), BF16),
        scratch_shapes=[pltpu.VMEM((2 * n_chains, 1, lanes), F32),
                        pltpu.VMEM((2 * n_chains, VT_ROWS, lanes), F32),
                        pltpu.VMEM((n_chains, SEL_TILES * Q_BLOCK, lanes), F32),
                        pltpu.VMEM((n_chains, SEL_TILES * Q_BLOCK, lanes), F32),
                        pltpu.VMEM((n_chains, 2, lanes), F32),
                        pltpu.VMEM((n_chains, 2, lanes), F32)],
        compiler_params=_params(("parallel", "parallel")),
        name="nsa_sel_win",
    )(q, k4, v4, member, sel, tz, gate, oc)


XATTN_TM = 1024


def _mem_kv_kernel(mem_ref, wk_ref, wv_ref, k_ref, v_ref):
    mb = mem_ref[0].astype(BF16)
    k_ref[0] = _dot(mb, wk_ref[...]).astype(BF16)
    v_ref[0] = _dot(mb, wv_ref[...]).astype(BF16)


def _mem_kv(mem, wk, wv):
    bsz, m, d = mem.shape
    shape = jax.ShapeDtypeStruct((bsz, m, d), BF16)
    spec = pl.BlockSpec((1, m, d), lambda b: (b, 0, 0))
    return pl.pallas_call(
        _mem_kv_kernel,
        grid=(bsz,),
        in_specs=[spec, _const_spec((d, d)), _const_spec((d, d))],
        out_specs=(spec, spec),
        out_shape=(shape, shape),
        compiler_params=_params(("parallel",)),
        name="xattn_mem_kv",
    )(mem, wk.astype(BF16), wv.astype(BF16))


def _mix_xattn_kernel(x_ref, yr_ref, yn_ref, wm_ref, g2_ref, b2_ref, k_ref, v_ref, wq_ref, wo_ref, g_ref, b_ref, o_ref):
    mixed = _dot(yr_ref[0], wm_ref[0:RWKV_DIM, :]) + _dot(yn_ref[0], wm_ref[RWKV_DIM:, :])
    x = _layer_norm(ALPHA * x_ref[0] + mixed, g2_ref[...], b2_ref[...])
    q = (_dot(x.astype(BF16), wq_ref[...]) * (XATTN_HEAD_DIM ** -0.5)).astype(BF16)
    heads = []
    for h in range(XATTN_HEADS):
        cols = slice(h * XATTN_HEAD_DIM, (h + 1) * XATTN_HEAD_DIM)
        s = _dot_nt(q[:, cols], k_ref[0, :, cols])
        e = jnp.exp(s - s.max(axis=1, keepdims=True))
        p = e * (1.0 / e.sum(axis=1, keepdims=True))
        heads.append(_dot(p.astype(BF16), v_ref[0, :, cols]).astype(BF16))
    o = _dot(jnp.concatenate(heads, axis=1), wo_ref[...])
    o_ref[0] = _layer_norm(ALPHA * x + o, g_ref[...], b_ref[...])


def _mix_cross_attention_ln(x3d, yr, yn, w_out, g2, b2, k, v, wq, wo, g, b):
    bsz, s, d = x3d.shape
    m = k.shape[1]
    tm = min(XATTN_TM, s)
    rows = lambda w: pl.BlockSpec((1, tm, w), lambda bb, i: (bb, i, 0))
    mem = pl.BlockSpec((1, m, d), lambda bb, i: (bb, 0, 0))
    vec = _const_spec((1, d))
    return pl.pallas_call(
        _mix_xattn_kernel,
        grid=(bsz, s // tm),
        in_specs=[rows(d), rows(RWKV_DIM), rows(NSA_DIM), _const_spec(w_out.shape), vec, vec,
                  mem, mem, _const_spec((d, d)), _const_spec((d, d)), vec, vec],
        out_specs=rows(d),
        out_shape=jax.ShapeDtypeStruct((bsz, s, d), F32),
        compiler_params=_params(("parallel", "parallel")),
        name="mix_xattn_ln",
    )(x3d, yr, yn, w_out.astype(BF16), g2.reshape(1, d), b2.reshape(1, d), k, v, wq.astype(BF16), wo.astype(BF16),
      g.reshape(1, d), b.reshape(1, d))


def _nsa_group(rel_bias, q, kcvc, k4, v4, gate, pe_k, w1_k, w2_k, pe_v, w1_v, w2_v):
    tz = _bias_tiles(rel_bias)
    kc, vc = _compress(kcvc, pe_k, w1_k, w2_k, pe_v, w1_v, w2_v)
    oc, sel = _cmp_select(q, kc, vc, tz, gate)
    return _sel_win(q, k4, v4, sel, tz, gate, oc)


def kernel(x, mem, ffn1_w_gate, ffn1_w_up, ffn1_w_down, ln1_g, ln1_b, mix_w_in, rwkv_mu, rwkv_w0, rwkv_w_up, rwkv_a0, rwkv_a_up, rwkv_g_up, rwkv_k_k, rwkv_k_a, rwkv_r_k, rwkv_gn_g, rwkv_gn_b, nsa_pe_k, nsa_w1_k, nsa_w2_k, nsa_pe_v, nsa_w1_v, nsa_w2_v, mix_w_out, ln2_g, ln2_b, xattn_wq, xattn_wk, xattn_wv, xattn_wo, ln3_g, ln3_b, ffn2_w_gate, ffn2_w_up, ffn2_w_down, ln4_g, ln4_b, rel_bias):
    bsz, s, d = x.shape
    rows = bsz * s
    for l in range(DEPTH):
        x1 = _ffn_ln(x.reshape(rows, d), ffn1_w_gate[l], ffn1_w_up[l], ffn1_w_down[l], ln1_g[l], ln1_b[l])
        zr, q, kcvc, k4, v4, gate = _in_projection(x1.reshape(bsz, s, d), mix_w_in[l], rel_bias)
        y_rwkv = _rwkv_group(zr, rwkv_mu[l], rwkv_w0[l], rwkv_w_up[l], rwkv_a0[l], rwkv_a_up[l], rwkv_g_up[l],
                             rwkv_k_k[l], rwkv_k_a[l], rwkv_r_k[l], rwkv_gn_g[l], rwkv_gn_b[l])
        y_nsa = _nsa_group(rel_bias, q, kcvc, k4, v4, gate, nsa_pe_k[l], nsa_w1_k[l], nsa_w2_k[l],
                           nsa_pe_v[l], nsa_w1_v[l], nsa_w2_v[l])
        mk, mv = _mem_kv(mem, xattn_wk[l], xattn_wv[l])
        x3 = _mix_cross_attention_ln(x1.reshape(bsz, s, d), y_rwkv, y_nsa, mix_w_out[l], ln2_g[l], ln2_b[l],
                                     mk, mv, xattn_wq[l], xattn_wo[l], ln3_g[l], ln3_b[l])
        x = _ffn_ln(x3.reshape(rows, d), ffn2_w_gate[l], ffn2_w_up[l], ffn2_w_down[l], ln4_g[l], ln4_b[l])
        x = x.reshape(bsz, s, d)
    return x
```
